```python
import jax, jax.numpy as jnp
from jax import lax
import numpy as np

D_MODEL = 1024
BATCH = 8
SEQ = 2048
DEPTH = 4

N_HEADS = 8
HEAD_DIM = 64
ATTN_WIDTH = N_HEADS * HEAD_DIM
CONV_CH = 512
CONV_K = 31
D_FF = 2816
D_PLE = 256
Q_BLOCK = 128
EPS = 1e-6
FFN_RES = 0.5

Q0 = 0
K0 = Q0 + ATTN_WIDTH
V0 = K0 + ATTN_WIDTH
F0 = V0 + ATTN_WIDTH
C0 = F0 + N_HEADS
GA0 = C0 + 2 * CONV_CH
GC0 = GA0 + D_MODEL
IN_COLS = GC0 + D_MODEL

kernel_name = "macaron_fox_conformer_hybrid"


def rmsnorm(x, g):
    xf = x.astype(jnp.float32)
    y = xf * lax.rsqrt(jnp.mean(xf * xf, axis=-1, keepdims=True) + EPS)
    return (y * g.astype(jnp.float32)).astype(x.dtype)


def swiglu(x, w_in, w_out):
    a, b = jnp.split(x @ w_in, 2, axis=-1)
    return (jax.nn.silu(a) * b) @ w_out


def forgetting_attention(q, k, v, f_logit):
    b, s, h, dh = q.shape
    scale = 1.0 / float(np.sqrt(dh))
    c = jnp.cumsum(jax.nn.log_sigmoid(f_logit.astype(jnp.float32)), axis=1)
    c = jnp.transpose(c, (0, 2, 1))
    qh = jnp.transpose(q, (0, 2, 1, 3))
    kh = jnp.transpose(k, (0, 2, 1, 3))
    vh = jnp.transpose(v, (0, 2, 1, 3))
    outs = []
    for blk in range(s // Q_BLOCK):
        qs, qe = blk * Q_BLOCK, (blk + 1) * Q_BLOCK
        sc = jnp.einsum('bhqd,bhkd->bhqk', qh[:, :, qs:qe], kh[:, :, :qe]).astype(jnp.float32) * scale
        sc = sc + (c[:, :, qs:qe, None] - c[:, :, None, :qe])
        causal = jnp.arange(qs, qe)[:, None] >= jnp.arange(qe)[None, :]
        sc = jnp.where(causal[None, None], sc, -jnp.inf)
        pr = jax.nn.softmax(sc, axis=-1).astype(vh.dtype)
        outs.append(jnp.einsum('bhqk,bhkd->bhqd', pr, vh[:, :, :qe]))
    o = jnp.concatenate(outs, axis=2)
    return jnp.transpose(o, (0, 2, 1, 3)).reshape(b, s, h * dh)


def conformer_conv(glu_in, conv_w, conv_b, g_conv):
    a = glu_in[..., :CONV_CH] * jax.nn.sigmoid(glu_in[..., CONV_CH:])
    y = lax.conv_general_dilated(
        a, conv_w[:, None, :].astype(a.dtype), window_strides=(1,),
        padding=[(CONV_K - 1, 0)], dimension_numbers=('NWC', 'WIO', 'NWC'),
        feature_group_count=CONV_CH) + conv_b
    return jax.nn.silu(rmsnorm(y, g_conv))


def hybrid_mixer(u, w_in, b_f, w_attn_out, conv_w, conv_b, g_conv, w_conv_out, w_out):
    b, s, _ = u.shape
    z = u @ w_in
    q = z[..., Q0:K0].reshape(b, s, N_HEADS, HEAD_DIM)
    k = z[..., K0:V0].reshape(b, s, N_HEADS, HEAD_DIM)
    v = z[..., V0:F0].reshape(b, s, N_HEADS, HEAD_DIM)
    f_logit = z[..., F0:C0] + b_f
    y_attn = forgetting_attention(q, k, v, f_logit) @ w_attn_out
    y_conv = conformer_conv(z[..., C0:GA0], conv_w, conv_b, g_conv) @ w_conv_out
    merged = jax.nn.sigmoid(z[..., GA0:GC0]) * y_attn + jax.nn.sigmoid(z[..., GC0:]) * y_conv
    return merged @ w_out


def _fwd_setup_inputs(seed: int = 0) -> dict:
    key = jax.random.key(seed)
    ks = jax.random.split(key, 24)
    L, D, F = DEPTH, D_MODEL, D_FF
    f32 = jnp.float32

    def w(k, shape, fan_in):
        return jax.random.normal(k, shape, f32) * (fan_in ** -0.5)

    def gain(k, shape):
        return 1.0 + 0.05 * jax.random.normal(k, shape, f32)

    return {
        "x": jax.random.normal(ks[0], (BATCH, SEQ, D), f32),
        "p": jax.random.normal(ks[1], (DEPTH, BATCH, SEQ, D_PLE), f32),
        "g_ff1": gain(ks[2], (L, D)),
        "w_ff1_in": w(ks[3], (L, D, 2 * F), D),
        "w_ff1_out": w(ks[4], (L, F, D), F),
        "g_mix": gain(ks[5], (L, D)),
        "w_in": w(ks[6], (L, D, IN_COLS), D),
        "b_f": 2.0 + 0.5 * jax.random.normal(ks[7], (L, N_HEADS), f32),
        "w_attn_out": w(ks[8], (L, ATTN_WIDTH, D), ATTN_WIDTH),
        "conv_w": w(ks[9], (L, CONV_K, CONV_CH), CONV_K),
        "conv_b": 0.02 * jax.random.normal(ks[10], (L, CONV_CH), f32),
        "g_conv": gain(ks[11], (L, CONV_CH)),
        "w_conv_out": w(ks[12], (L, CONV_CH, D), CONV_CH),
        "w_out": w(ks[13], (L, D, D), D),
        "g_ff2": gain(ks[14], (L, D)),
        "w_ff2_in": w(ks[15], (L, D, 2 * F), D),
        "w_ff2_out": w(ks[16], (L, F, D), F),
        "g_ple": gain(ks[17], (L, D)),
        "w_ple_gate": w(ks[18], (L, D, D), D),
        "w_ple_proj": w(ks[19], (L, D_PLE, D), D_PLE),
        "g_final": gain(ks[20], (D,)),
    }


def _fwd_reference(x, p, g_ff1, w_ff1_in, w_ff1_out, g_mix, w_in, b_f, w_attn_out,
              conv_w, conv_b, g_conv, w_conv_out, w_out, g_ff2, w_ff2_in, w_ff2_out,
              g_ple, w_ple_gate, w_ple_proj, g_final):
    h = x
    for i in range(DEPTH):
        h = h + FFN_RES * swiglu(rmsnorm(h, g_ff1[i]), w_ff1_in[i], w_ff1_out[i])
        h = h + hybrid_mixer(rmsnorm(h, g_mix[i]), w_in[i], b_f[i], w_attn_out[i],
                             conv_w[i], conv_b[i], g_conv[i], w_conv_out[i], w_out[i])
        h = h + FFN_RES * swiglu(rmsnorm(h, g_ff2[i]), w_ff2_in[i], w_ff2_out[i])
        gate = jax.nn.sigmoid(rmsnorm(h, g_ple[i]) @ w_ple_gate[i])
        h = h + gate * (p[i] @ w_ple_proj[i])
    return rmsnorm(h, g_final)


import jax as _jax
import jax.numpy as _jnp

TWIN_FORMAT = 'train_step'
FWD_PARAMS = ['x', 'p', 'g_ff1', 'w_ff1_in', 'w_ff1_out', 'g_mix', 'w_in', 'b_f', 'w_attn_out', 'conv_w', 'conv_b', 'g_conv', 'w_conv_out', 'w_out', 'g_ff2', 'w_ff2_in', 'w_ff2_out', 'g_ple', 'w_ple_gate', 'w_ple_proj', 'g_final']
TWIN_WEIGHTS = ['g_ff1', 'w_ff1_in', 'w_ff1_out', 'g_mix', 'w_in', 'b_f', 'w_attn_out', 'conv_w', 'conv_b', 'g_conv', 'w_conv_out', 'w_out', 'g_ff2', 'w_ff2_in', 'w_ff2_out', 'g_ple', 'w_ple_gate', 'w_ple_proj', 'g_final']
TWIN_DIFF_INPUT = 'x'
TWIN_INPUTS = ['x', 'p', 'g_ff1', 'w_ff1_in', 'w_ff1_out', 'g_mix', 'w_in', 'b_f', 'w_attn_out', 'conv_w', 'conv_b', 'g_conv', 'w_conv_out', 'w_out', 'g_ff2', 'w_ff2_in', 'w_ff2_out', 'g_ple', 'w_ple_gate', 'w_ple_proj', 'g_final', 'loss_target', 'm_g_ff1', 'm_w_ff1_in', 'm_w_ff1_out', 'm_g_mix', 'm_w_in', 'm_b_f', 'm_w_attn_out', 'm_conv_w', 'm_conv_b', 'm_g_conv', 'm_w_conv_out', 'm_w_out', 'm_g_ff2', 'm_w_ff2_in', 'm_w_ff2_out', 'm_g_ple', 'm_w_ple_gate', 'm_w_ple_proj', 'm_g_final', 'v_g_ff1', 'v_w_ff1_in', 'v_w_ff1_out', 'v_g_mix', 'v_w_in', 'v_b_f', 'v_w_attn_out', 'v_conv_w', 'v_conv_b', 'v_g_conv', 'v_w_conv_out', 'v_w_out', 'v_g_ff2', 'v_w_ff2_in', 'v_w_ff2_out', 'v_g_ple', 'v_w_ple_gate', 'v_w_ple_proj', 'v_g_final']
TWIN_OUTPUTS = ['loss', 'grad_x', 'grad_g_ff1', 'grad_w_ff1_in', 'grad_w_ff1_out', 'grad_g_mix', 'grad_w_in', 'grad_b_f', 'grad_w_attn_out', 'grad_conv_w', 'grad_conv_b', 'grad_g_conv', 'grad_w_conv_out', 'grad_w_out', 'grad_g_ff2', 'grad_w_ff2_in', 'grad_w_ff2_out', 'grad_g_ple', 'grad_w_ple_gate', 'grad_w_ple_proj', 'grad_g_final', 'delta_g_ff1', 'delta_w_ff1_in', 'delta_w_ff1_out', 'delta_g_mix', 'delta_w_in', 'delta_b_f', 'delta_w_attn_out', 'delta_conv_w', 'delta_conv_b', 'delta_g_conv', 'delta_w_conv_out', 'delta_w_out', 'delta_g_ff2', 'delta_w_ff2_in', 'delta_w_ff2_out', 'delta_g_ple', 'delta_w_ple_gate', 'delta_w_ple_proj', 'delta_g_final', 'new_m_g_ff1', 'new_m_w_ff1_in', 'new_m_w_ff1_out', 'new_m_g_mix', 'new_m_w_in', 'new_m_b_f', 'new_m_w_attn_out', 'new_m_conv_w', 'new_m_conv_b', 'new_m_g_conv', 'new_m_w_conv_out', 'new_m_w_out', 'new_m_g_ff2', 'new_m_w_ff2_in', 'new_m_w_ff2_out', 'new_m_g_ple', 'new_m_w_ple_gate', 'new_m_w_ple_proj', 'new_m_g_final', 'new_v_g_ff1', 'new_v_w_ff1_in', 'new_v_w_ff1_out', 'new_v_g_mix', 'new_v_w_in', 'new_v_b_f', 'new_v_w_attn_out', 'new_v_conv_w', 'new_v_conv_b', 'new_v_g_conv', 'new_v_w_conv_out', 'new_v_w_out', 'new_v_g_ff2', 'new_v_w_ff2_in', 'new_v_w_ff2_out', 'new_v_g_ple', 'new_v_w_ple_gate', 'new_v_w_ple_proj', 'new_v_g_final']
TWIN_LEAF_KINDS = {'loss': 'loss', 'grad_x': 'grad_x', 'grad_g_ff1': 'grad_w', 'grad_w_ff1_in': 'grad_w', 'grad_w_ff1_out': 'grad_w', 'grad_g_mix': 'grad_w', 'grad_w_in': 'grad_w', 'grad_b_f': 'grad_w', 'grad_w_attn_out': 'grad_w', 'grad_conv_w': 'grad_w', 'grad_conv_b': 'grad_w', 'grad_g_conv': 'grad_w', 'grad_w_conv_out': 'grad_w', 'grad_w_out': 'grad_w', 'grad_g_ff2': 'grad_w', 'grad_w_ff2_in': 'grad_w', 'grad_w_ff2_out': 'grad_w', 'grad_g_ple': 'grad_w', 'grad_w_ple_gate': 'grad_w', 'grad_w_ple_proj': 'grad_w', 'grad_g_final': 'grad_w', 'delta_g_ff1': 'delta_w', 'delta_w_ff1_in': 'delta_w', 'delta_w_ff1_out': 'delta_w', 'delta_g_mix': 'delta_w', 'delta_w_in': 'delta_w', 'delta_b_f': 'delta_w', 'delta_w_attn_out': 'delta_w', 'delta_conv_w': 'delta_w', 'delta_conv_b': 'delta_w', 'delta_g_conv': 'delta_w', 'delta_w_conv_out': 'delta_w', 'delta_w_out': 'delta_w', 'delta_g_ff2': 'delta_w', 'delta_w_ff2_in': 'delta_w', 'delta_w_ff2_out': 'delta_w', 'delta_g_ple': 'delta_w', 'delta_w_ple_gate': 'delta_w', 'delta_w_ple_proj': 'delta_w', 'delta_g_final': 'delta_w', 'new_m_g_ff1': 'new_m', 'new_m_w_ff1_in': 'new_m', 'new_m_w_ff1_out': 'new_m', 'new_m_g_mix': 'new_m', 'new_m_w_in': 'new_m', 'new_m_b_f': 'new_m', 'new_m_w_attn_out': 'new_m', 'new_m_conv_w': 'new_m', 'new_m_conv_b': 'new_m', 'new_m_g_conv': 'new_m', 'new_m_w_conv_out': 'new_m', 'new_m_w_out': 'new_m', 'new_m_g_ff2': 'new_m', 'new_m_w_ff2_in': 'new_m', 'new_m_w_ff2_out': 'new_m', 'new_m_g_ple': 'new_m', 'new_m_w_ple_gate': 'new_m', 'new_m_w_ple_proj': 'new_m', 'new_m_g_final': 'new_m', 'new_v_g_ff1': 'new_v', 'new_v_w_ff1_in': 'new_v', 'new_v_w_ff1_out': 'new_v', 'new_v_g_mix': 'new_v', 'new_v_w_in': 'new_v', 'new_v_b_f': 'new_v', 'new_v_w_attn_out': 'new_v', 'new_v_conv_w': 'new_v', 'new_v_conv_b': 'new_v', 'new_v_g_conv': 'new_v', 'new_v_w_conv_out': 'new_v', 'new_v_w_out': 'new_v', 'new_v_g_ff2': 'new_v', 'new_v_w_ff2_in': 'new_v', 'new_v_w_ff2_out': 'new_v', 'new_v_g_ple': 'new_v', 'new_v_w_ple_gate': 'new_v', 'new_v_w_ple_proj': 'new_v', 'new_v_g_final': 'new_v'}


def _forward(args):
    return _fwd_reference(*[args[k] for k in FWD_PARAMS])


def _output_shape():
    out = _jax.eval_shape(lambda: _forward(_fwd_setup_inputs(0)))
    return out.shape, out.dtype

N_MICROBATCH = 1
ADAM_LR = 0.001
ADAM_B1 = 0.9
ADAM_B2 = 0.999
ADAM_EPS = 1e-08
ADAM_WD = 0.01
ADAM_STEP = 10
PER_EXAMPLE_BATCH_AXIS = {'x': 0, 'p': 1, 'loss_target': 0}
SHARED_INPUTS = []
_WEIGHT_DTYPES = {'g_ff1': _jnp.float32, 'w_ff1_in': _jnp.float32, 'w_ff1_out': _jnp.float32, 'g_mix': _jnp.float32, 'w_in': _jnp.float32, 'b_f': _jnp.float32, 'w_attn_out': _jnp.float32, 'conv_w': _jnp.float32, 'conv_b': _jnp.float32, 'g_conv': _jnp.float32, 'w_conv_out': _jnp.float32, 'w_out': _jnp.float32, 'g_ff2': _jnp.float32, 'w_ff2_in': _jnp.float32, 'w_ff2_out': _jnp.float32, 'g_ple': _jnp.float32, 'w_ple_gate': _jnp.float32, 'w_ple_proj': _jnp.float32, 'g_final': _jnp.float32}
MOMENT_SCALE = {'g_ff1': 4.153864e-02, 'w_ff1_in': 1.774910e-02, 'w_ff1_out': 2.895966e-02, 'g_mix': 4.701875e-02, 'w_in': 2.219725e-02, 'b_f': 2.204813e-01, 'w_attn_out': 2.203260e-02, 'conv_w': 4.242713e-02, 'conv_b': 9.777670e-02, 'g_conv': 5.543121e-02, 'w_conv_out': 3.040773e-02, 'w_out': 3.707307e-02, 'g_ff2': 3.720842e-02, 'w_ff2_in': 1.556399e-02, 'w_ff2_out': 2.541942e-02, 'g_ple': 1.769381e-02, 'w_ple_gate': 1.805550e-02, 'w_ple_proj': 4.557264e-02, 'g_final': 1.606150e+01}


def _to_microbatches(a, axis):
    t = _jnp.moveaxis(a, axis, 0)
    t = t.reshape((N_MICROBATCH, t.shape[0] // N_MICROBATCH) + t.shape[1:])
    return _jnp.moveaxis(t, 1, axis + 1)


def setup_inputs(seed: int = 0) -> dict:
    inp = _fwd_setup_inputs(seed)
    key = _jax.random.fold_in(_jax.random.key(seed), 7919)
    shape, _ = _output_shape()
    out = dict(inp)
    out["loss_target"] = _jax.random.normal(_jax.random.fold_in(key, 0), shape, _jnp.float32)
    for i, name in enumerate(TWIN_WEIGHTS):
        w = inp[name].astype(_jnp.float32)
        if MOMENT_SCALE is None:
            s = _jnp.sqrt(_jnp.mean(_jnp.square(w)) + 1e-30)
        else:
            s = MOMENT_SCALE[name]
        km, kv = _jax.random.split(_jax.random.fold_in(key, i + 1))
        out[name] = w
        out["m_" + name] = s * _jax.random.normal(km, w.shape, _jnp.float32)
        out["v_" + name] = (s * s) * _jax.random.uniform(kv, w.shape, _jnp.float32, 0.5, 1.5)
    if N_MICROBATCH > 1:
        for name, axis in PER_EXAMPLE_BATCH_AXIS.items():
            out[name] = _to_microbatches(out[name], axis)
    return {'x': out['x'], 'p': out['p'], 'g_ff1': out['g_ff1'], 'w_ff1_in': out['w_ff1_in'], 'w_ff1_out': out['w_ff1_out'], 'g_mix': out['g_mix'], 'w_in': out['w_in'], 'b_f': out['b_f'], 'w_attn_out': out['w_attn_out'], 'conv_w': out['conv_w'], 'conv_b': out['conv_b'], 'g_conv': out['g_conv'], 'w_conv_out': out['w_conv_out'], 'w_out': out['w_out'], 'g_ff2': out['g_ff2'], 'w_ff2_in': out['w_ff2_in'], 'w_ff2_out': out['w_ff2_out'], 'g_ple': out['g_ple'], 'w_ple_gate': out['w_ple_gate'], 'w_ple_proj': out['w_ple_proj'], 'g_final': out['g_final'], 'loss_target': out['loss_target'], 'm_g_ff1': out['m_g_ff1'], 'm_w_ff1_in': out['m_w_ff1_in'], 'm_w_ff1_out': out['m_w_ff1_out'], 'm_g_mix': out['m_g_mix'], 'm_w_in': out['m_w_in'], 'm_b_f': out['m_b_f'], 'm_w_attn_out': out['m_w_attn_out'], 'm_conv_w': out['m_conv_w'], 'm_conv_b': out['m_conv_b'], 'm_g_conv': out['m_g_conv'], 'm_w_conv_out': out['m_w_conv_out'], 'm_w_out': out['m_w_out'], 'm_g_ff2': out['m_g_ff2'], 'm_w_ff2_in': out['m_w_ff2_in'], 'm_w_ff2_out': out['m_w_ff2_out'], 'm_g_ple': out['m_g_ple'], 'm_w_ple_gate': out['m_w_ple_gate'], 'm_w_ple_proj': out['m_w_ple_proj'], 'm_g_final': out['m_g_final'], 'v_g_ff1': out['v_g_ff1'], 'v_w_ff1_in': out['v_w_ff1_in'], 'v_w_ff1_out': out['v_w_ff1_out'], 'v_g_mix': out['v_g_mix'], 'v_w_in': out['v_w_in'], 'v_b_f': out['v_b_f'], 'v_w_attn_out': out['v_w_attn_out'], 'v_conv_w': out['v_conv_w'], 'v_conv_b': out['v_conv_b'], 'v_g_conv': out['v_g_conv'], 'v_w_conv_out': out['v_w_conv_out'], 'v_w_out': out['v_w_out'], 'v_g_ff2': out['v_g_ff2'], 'v_w_ff2_in': out['v_w_ff2_in'], 'v_w_ff2_out': out['v_w_ff2_out'], 'v_g_ple': out['v_g_ple'], 'v_w_ple_gate': out['v_w_ple_gate'], 'v_w_ple_proj': out['v_w_ple_proj'], 'v_g_final': out['v_g_final']}


def _loss(weights, diff, rest, loss_target):
    with _jax.named_scope("forward"):
        args = {**rest, TWIN_DIFF_INPUT: diff, **{k: w.astype(_WEIGHT_DTYPES[k]) for k, w in weights.items()}}
        y = _forward(args)
    with _jax.named_scope("loss_head"):
        err = _jnp.square(y.astype(_jnp.float32) - loss_target)
        return 0.5 * _jnp.sum(_jnp.mean(err, axis=-1)) if err.ndim else 0.5 * err


def _adamw(w, g, m, v):
    m = ADAM_B1 * m + (1.0 - ADAM_B1) * g
    v = ADAM_B2 * v + (1.0 - ADAM_B2) * _jnp.square(g)
    m_hat = m / (1.0 - ADAM_B1 ** ADAM_STEP)
    v_hat = v / (1.0 - ADAM_B2 ** ADAM_STEP)
    delta = -ADAM_LR * (m_hat / (_jnp.sqrt(v_hat) + ADAM_EPS) + ADAM_WD * w)
    return delta, m, v


def reference(x, p, g_ff1, w_ff1_in, w_ff1_out, g_mix, w_in, b_f, w_attn_out, conv_w, conv_b, g_conv, w_conv_out, w_out, g_ff2, w_ff2_in, w_ff2_out, g_ple, w_ple_gate, w_ple_proj, g_final, loss_target, m_g_ff1, m_w_ff1_in, m_w_ff1_out, m_g_mix, m_w_in, m_b_f, m_w_attn_out, m_conv_w, m_conv_b, m_g_conv, m_w_conv_out, m_w_out, m_g_ff2, m_w_ff2_in, m_w_ff2_out, m_g_ple, m_w_ple_gate, m_w_ple_proj, m_g_final, v_g_ff1, v_w_ff1_in, v_w_ff1_out, v_g_mix, v_w_in, v_b_f, v_w_attn_out, v_conv_w, v_conv_b, v_g_conv, v_w_conv_out, v_w_out, v_g_ff2, v_w_ff2_in, v_w_ff2_out, v_g_ple, v_w_ple_gate, v_w_ple_proj, v_g_final):
    given = dict(x=x, p=p, g_ff1=g_ff1, w_ff1_in=w_ff1_in, w_ff1_out=w_ff1_out, g_mix=g_mix, w_in=w_in, b_f=b_f, w_attn_out=w_attn_out, conv_w=conv_w, conv_b=conv_b, g_conv=g_conv, w_conv_out=w_conv_out, w_out=w_out, g_ff2=g_ff2, w_ff2_in=w_ff2_in, w_ff2_out=w_ff2_out, g_ple=g_ple, w_ple_gate=w_ple_gate, w_ple_proj=w_ple_proj, g_final=g_final, loss_target=loss_target, m_g_ff1=m_g_ff1, m_w_ff1_in=m_w_ff1_in, m_w_ff1_out=m_w_ff1_out, m_g_mix=m_g_mix, m_w_in=m_w_in, m_b_f=m_b_f, m_w_attn_out=m_w_attn_out, m_conv_w=m_conv_w, m_conv_b=m_conv_b, m_g_conv=m_g_conv, m_w_conv_out=m_w_conv_out, m_w_out=m_w_out, m_g_ff2=m_g_ff2, m_w_ff2_in=m_w_ff2_in, m_w_ff2_out=m_w_ff2_out, m_g_ple=m_g_ple, m_w_ple_gate=m_w_ple_gate, m_w_ple_proj=m_w_ple_proj, m_g_final=m_g_final, v_g_ff1=v_g_ff1, v_w_ff1_in=v_w_ff1_in, v_w_ff1_out=v_w_ff1_out, v_g_mix=v_g_mix, v_w_in=v_w_in, v_b_f=v_b_f, v_w_attn_out=v_w_attn_out, v_conv_w=v_conv_w, v_conv_b=v_conv_b, v_g_conv=v_g_conv, v_w_conv_out=v_w_conv_out, v_w_out=v_w_out, v_g_ff2=v_g_ff2, v_w_ff2_in=v_w_ff2_in, v_w_ff2_out=v_w_ff2_out, v_g_ple=v_g_ple, v_w_ple_gate=v_w_ple_gate, v_w_ple_proj=v_w_ple_proj, v_g_final=v_g_final)
    weights = {n: given[n] for n in TWIN_WEIGHTS}
    shared = {n: given[n] for n in SHARED_INPUTS}
    per_example = {n: given[n] for n in ['x', 'p']}
    grad_fn = _jax.value_and_grad(_loss, argnums=(0, 1))

    def one_microbatch(ex, loss_target):
        ex = dict(ex)
        diff = ex.pop(TWIN_DIFF_INPUT)
        return grad_fn(weights, diff, {**shared, **ex}, loss_target)

    if N_MICROBATCH == 1:
        loss, (grad_w, grad_x) = one_microbatch(per_example, given["loss_target"])
    else:
        def body(carry, xs):
            loss_sum, grad_sum = carry
            l_k, (gw_k, gx_k) = one_microbatch(xs[0], xs[1])
            with _jax.named_scope("update"):
                return (loss_sum + l_k, _jax.tree.map(_jnp.add, grad_sum, gw_k)), gx_k

        init = (_jnp.zeros((), _jnp.float32), _jax.tree.map(_jnp.zeros_like, weights))
        (loss, grad_w), grad_x = _jax.lax.scan(body, init, (per_example, given["loss_target"]))
    with _jax.named_scope("update"):
        delta_w, new_m, new_v = {}, {}, {}
        for n in TWIN_WEIGHTS:
            delta_w[n], new_m[n], new_v[n] = _adamw(weights[n], grad_w[n], given["m_" + n], given["v_" + n])
    return (loss, grad_x, *[grad_w[n] for n in TWIN_WEIGHTS], *[delta_w[n] for n in TWIN_WEIGHTS],
            *[new_m[n] for n in TWIN_WEIGHTS], *[new_v[n] for n in TWIN_WEIGHTS])
```

```python
import functools

import jax
import jax.numpy as jnp
from jax import lax
from jax.experimental import pallas as pl
from jax.experimental.pallas import tpu as pltpu

F32 = jnp.float32
BF16 = jnp.bfloat16
MESH = pl.DeviceIdType.MESH
ANY = pl.BlockSpec(memory_space=pl.ANY)

RMS_EPS = 1e-6
FFN_RES = 0.5
ADAM_LR = 0.001
ADAM_B1 = 0.9
ADAM_B2 = 0.999
ADAM_EPS = 1e-08
ADAM_WD = 0.01
ADAM_STEP = 10

N_CHIPS = 4
N_DEV = 8
LANES = 128
HALO = 32
VMEM_LIMIT = 56 * 1024 * 1024


def _cparams():
    return pltpu.CompilerParams(vmem_limit_bytes=VMEM_LIMIT)


def _pick(n, prefs):
    for p in prefs:
        if p <= n and n % p == 0:
            return p
    return n


def _sig(x):
    return 1.0 / (1.0 + jnp.exp(-x))


def _rowwise(name, fn, ins, outs, consts=(), reds=(), tm=None):
    ins = [a if isinstance(a, tuple) else (a, 0, a.shape[1]) for a in ins]
    m = ins[0][0].shape[0]
    tm = tm or _pick(m, (256, 128, 64, 32, 16, 8))
    n_in, n_c, n_o, n_r = len(ins), len(consts), len(outs), len(reds)

    def body(*refs):
        in_refs = refs[:n_in + n_c]
        o_refs = refs[n_in + n_c:n_in + n_c + n_o]
        r_refs = refs[n_in + n_c + n_o:]
        res = fn(*[r[...] for r in in_refs])
        if not isinstance(res, (tuple, list)):
            res = (res,)
        for r, v in zip(o_refs, res[:n_o]):
            r[...] = v.astype(r.dtype)
        if n_r:
            @pl.when(pl.program_id(0) == 0)
            def _():
                for r in r_refs:
                    r[...] = jnp.zeros(r.shape, r.dtype)
            for r, v in zip(r_refs, res[n_o:]):
                r[...] += v.astype(r.dtype)

    in_specs = [pl.BlockSpec((tm, w), functools.partial(lambda i, cb: (i, cb), cb=cb)) for (_, cb, w) in ins]
    in_specs += [pl.BlockSpec(c.shape, lambda i: (0, 0)) for c in consts]
    out_specs = [pl.BlockSpec((tm, w), lambda i: (i, 0)) for (w, _) in outs]
    out_specs += [pl.BlockSpec(s, lambda i: (0, 0)) for (s, _) in reds]
    out_shape = [jax.ShapeDtypeStruct((m, w), d) for (w, d) in outs]
    out_shape += [jax.ShapeDtypeStruct(s, d) for (s, d) in reds]
    res = pl.pallas_call(
        body, name=name, grid=(m // tm,), in_specs=in_specs, out_specs=out_specs, out_shape=out_shape,
        compiler_params=_cparams(),
    )(*[a for (a, _, _) in ins], *consts)
    return res


def _rms_fwd(name, h, g):
    def fn(x, gg):
        r = lax.rsqrt(jnp.mean(x * x, axis=-1, keepdims=True) + RMS_EPS)
        return x * r * gg
    return _rowwise(name, fn, [h], [(h.shape[1], BF16)], consts=[g])[0]


def _rms_bwd_vals(x, g, dn):
    r = lax.rsqrt(jnp.mean(x * x, axis=-1, keepdims=True) + RMS_EPS)
    xh = x * r
    dxh = dn * g
    dx = r * (dxh - xh * jnp.mean(dxh * xh, axis=-1, keepdims=True))
    dg = jnp.sum(dn * xh, axis=0, keepdims=True)
    return dx, dg


def _rms_bwd_res(name, h, g, dns, dres, oscale):
    n_dn = len(dns)

    def fn(x, *rest):
        dn = rest[0].astype(F32)
        for t in rest[1:n_dn]:
            dn = dn + t.astype(F32)
        dr, gg = rest[n_dn], rest[n_dn + 1]
        dx, dg = _rms_bwd_vals(x, gg, dn)
        dh = dr + dx
        return dh, oscale * dh, dg

    d = h.shape[1]
    return _rowwise(name, fn, [h, *dns, dres], [(d, F32), (d, BF16)], consts=[g], reds=[((1, d), F32)])


_TN_PREFS = (1408, 1024, 768, 512, 256, 128)


def _mm_nn(name, a, w, out_dtype, res=None, scale=1.0):
    m, k = a.shape
    j, _, nb = w.shape
    tm = _pick(m, (512, 256, 128))
    tn = _pick(nb, _TN_PREFS)
    tpb = nb // tn
    has_res = res is not None

    def body(a_ref, w_ref, *rest):
        o_ref = rest[-1]
        acc = jnp.dot(a_ref[...], w_ref[...], preferred_element_type=F32)
        if has_res:
            acc = rest[0][...] + scale * acc
        o_ref[...] = acc.astype(o_ref.dtype)

    in_specs = [pl.BlockSpec((tm, k), lambda n, i: (i, 0)),
                pl.BlockSpec((None, k, tn), lambda n, i: (n // tpb, 0, n % tpb))]
    args = [a, w]
    if has_res:
        in_specs.append(pl.BlockSpec((tm, tn), lambda n, i: (i, n)))
        args.append(res)
    return pl.pallas_call(
        body, name=name, grid=(j * tpb, m // tm), in_specs=in_specs,
        out_specs=pl.BlockSpec((tm, tn), lambda n, i: (i, n)),
        out_shape=jax.ShapeDtypeStruct((m, j * nb), out_dtype), compiler_params=_cparams(),
    )(*args)


def _mm_nt(name, dy, w, out_dtype):
    m, n = dy.shape
    j, k, nb = w.shape
    tm = _pick(m, (512, 256, 128))
    to = _pick(k, _TN_PREFS)
    tc = _pick(nb, (1536, 1408, 1024, 512, 256, 128))
    cpb = nb // tc
    n_red = j * cpb

    def body(dy_ref, w_ref, o_ref, acc_ref):
        r = pl.program_id(2)

        @pl.when(r == 0)
        def _():
            acc_ref[...] = jnp.zeros(acc_ref.shape, F32)

        acc_ref[...] += lax.dot_general(dy_ref[...], w_ref[...], (((1,), (1,)), ((), ())),
                                        preferred_element_type=F32)

        @pl.when(r == n_red - 1)
        def _():
            o_ref[...] = acc_ref[...].astype(o_ref.dtype)

    return pl.pallas_call(
        body, name=name, grid=(k // to, m // tm, n_red),
        in_specs=[pl.BlockSpec((tm, tc), lambda ko, i, r: (i, r)),
                  pl.BlockSpec((None, to, tc), lambda ko, i, r: (r // cpb, ko, r % cpb))],
        out_specs=pl.BlockSpec((tm, to), lambda ko, i, r: (i, ko)),
        out_shape=jax.ShapeDtypeStruct((m, k), out_dtype),
        scratch_shapes=[pltpu.VMEM((tm, to), F32)], compiler_params=_cparams(),
    )(dy, w)


def _mm_tn(name, a, dy, j):
    m, k = a.shape
    n = dy.shape[1]
    nb = n // j
    tk = _pick(k, (512, 256, 128))
    tn = _pick(nb, _TN_PREFS)
    tpb = nb // tn

    def body(a_ref, dy_ref, o_ref):
        o_ref[...] = lax.dot_general(a_ref[...], dy_ref[...], (((0,), (0,)), ((), ())),
                                     preferred_element_type=F32).astype(o_ref.dtype)

    return pl.pallas_call(
        body, name=name, grid=(k // tk, j * tpb),
        in_specs=[pl.BlockSpec((m, tk), lambda kb, nn: (0, kb)),
                  pl.BlockSpec((m, tn), lambda kb, nn: (0, nn))],
        out_specs=pl.BlockSpec((None, tk, tn), lambda kb, nn: (nn // tpb, kb, nn % tpb)),
        out_shape=jax.ShapeDtypeStruct((j, k, nb), BF16), compiler_params=_cparams(),
    )(a, dy)


def _cumsum_rows(x_ref, o_ref, blk, reverse):
    s = x_ref.shape[0]
    nblk = s // blk
    ri = lax.broadcasted_iota(jnp.int32, (blk, blk), 0)
    ci = lax.broadcasted_iota(jnp.int32, (blk, blk), 1)
    tri = jnp.where((ci >= ri) if reverse else (ci <= ri), 1.0, 0.0).astype(F32)
    carry = jnp.zeros((1, x_ref.shape[1]), F32)
    order = range(nblk - 1, -1, -1) if reverse else range(nblk)
    for b in order:
        xb = x_ref[b * blk:(b + 1) * blk, :]
        o_ref[b * blk:(b + 1) * blk, :] = jnp.dot(tri, xb, preferred_element_type=F32,
                                                  precision=lax.Precision.HIGHEST) + carry
        carry = carry + jnp.sum(xb, axis=0, keepdims=True)


def _fgate_fwd(name, zf, bf_row):
    s, w = zf.shape
    blk = _pick(s, (256, 128))

    def body(z_ref, b_ref, c_ref, ls_ref):
        v = z_ref[...] + b_ref[...]
        ls_ref[...] = jnp.minimum(v, 0.0) - jnp.log(1.0 + jnp.exp(-jnp.abs(v)))
        _cumsum_rows(ls_ref, c_ref, blk, reverse=False)

    return pl.pallas_call(
        body, name=name, out_shape=jax.ShapeDtypeStruct((s, w), F32),
        scratch_shapes=[pltpu.VMEM((s, w), F32)], compiler_params=_cparams(),
    )(zf, bf_row)


def _fgate_bwd(name, dc_q, dc_k, zf, bf_row):
    s, w = zf.shape
    blk = _pick(s, (256, 128))

    def body(dcq_ref, dck_ref, z_ref, b_ref, dz_ref, dzb_ref, db_ref, dls_ref, dc_ref):
        dc_ref[...] = dcq_ref[...] + dck_ref[...]
        _cumsum_rows(dc_ref, dls_ref, blk, reverse=True)
        dz = dls_ref[...] * _sig(-(z_ref[...] + b_ref[...]))
        dz_ref[...] = dz
        dzb_ref[...] = dz.astype(BF16)
        db_ref[...] = jnp.sum(dz, axis=0, keepdims=True)

    return pl.pallas_call(
        body, name=name,
        out_shape=[jax.ShapeDtypeStruct((s, w), F32), jax.ShapeDtypeStruct((s, w), BF16),
                   jax.ShapeDtypeStruct((1, w), F32)],
        scratch_shapes=[pltpu.VMEM((s, w), F32), pltpu.VMEM((s, w), F32)], compiler_params=_cparams(),
    )(dc_q, dc_k, zf, bf_row)


def _scores(q, k, cq_col, ck_row, scale, row0, col0):
    s = lax.dot_general(q, k, (((1,), (1,)), ((), ())), preferred_element_type=F32) * scale
    s = s + (cq_col - ck_row)
    rows = row0 + lax.broadcasted_iota(jnp.int32, s.shape, 0)
    cols = col0 + lax.broadcasted_iota(jnp.int32, s.shape, 1)
    return jnp.where(cols <= rows, s, -jnp.inf)


def _attn_fwd(name, q, k, v, c_col, c_row, tb):
    h, s, hd = q.shape
    nb = s // tb
    scale = 1.0 / float(hd) ** 0.5

    def body(q_ref, k_ref, v_ref, cq_ref, ck_ref, o_ref, lse_ref):
        i = pl.program_id(1)
        qv = q_ref[...]
        cq = cq_ref[...]

        def step(j, carry):
            m_i, l_i, acc = carry
            k0 = pl.multiple_of(j * tb, tb)
            sc = _scores(qv, k_ref[pl.ds(k0, tb), :], cq, ck_ref[j], scale, i * tb, j * tb)
            m_new = jnp.maximum(m_i, jnp.max(sc, axis=-1, keepdims=True))
            alpha = jnp.exp(m_i - m_new)
            p = jnp.exp(sc - m_new)
            l_new = alpha * l_i + jnp.sum(p, axis=-1, keepdims=True)
            acc = alpha * acc + jnp.dot(p.astype(BF16), v_ref[pl.ds(k0, tb), :], preferred_element_type=F32)
            return m_new, l_new, acc

        init = (jnp.full((tb, 1), -jnp.inf, F32), jnp.zeros((tb, 1), F32), jnp.zeros((tb, hd), F32))
        m_i, l_i, acc = lax.fori_loop(0, i + 1, step, init)
        o_ref[...] = (acc / l_i).astype(o_ref.dtype)
        lse_ref[...] = m_i + jnp.log(l_i)

    return pl.pallas_call(
        body, name=name, grid=(h, nb),
        in_specs=[pl.BlockSpec((None, tb, hd), lambda hh, i: (hh, i, 0)),
                  pl.BlockSpec((None, s, hd), lambda hh, i: (hh, 0, 0)),
                  pl.BlockSpec((None, s, hd), lambda hh, i: (hh, 0, 0)),
                  pl.BlockSpec((None, tb, 1), lambda hh, i: (hh, i, 0)),
                  pl.BlockSpec((None, nb, 1, tb), lambda hh, i: (hh, 0, 0, 0))],
        out_specs=[pl.BlockSpec((None, tb, hd), lambda hh, i: (hh, i, 0)),
                   pl.BlockSpec((None, tb, 1), lambda hh, i: (hh, i, 0))],
        out_shape=[jax.ShapeDtypeStruct((h, s, hd), F32), jax.ShapeDtypeStruct((h, s, 1), F32)],
        compiler_params=_cparams(),
    )(q, k, v, c_col, c_row)


def _attn_bwd_q(name, q, k, v, o, do, lse, c_col, c_row, tb):
    h, s, hd = q.shape
    nb = s // tb
    scale = 1.0 / float(hd) ** 0.5

    def body(q_ref, k_ref, v_ref, o_ref, do_ref, lse_ref, cq_ref, ck_ref, dq_ref, dl_ref, dcq_ref):
        i = pl.program_id(1)
        qv = q_ref[...]
        dov = do_ref[...]
        cq = cq_ref[...]
        lse_v = lse_ref[...]
        delta = jnp.sum(dov.astype(F32) * o_ref[...], axis=-1, keepdims=True)

        def step(j, carry):
            dq, dcq = carry
            k0 = pl.multiple_of(j * tb, tb)
            kj = k_ref[pl.ds(k0, tb), :]
            p = jnp.exp(_scores(qv, kj, cq, ck_ref[j], scale, i * tb, j * tb) - lse_v)
            dp = lax.dot_general(dov, v_ref[pl.ds(k0, tb), :], (((1,), (1,)), ((), ())), preferred_element_type=F32)
            ds = p * (dp - delta)
            return (dq + jnp.dot(ds.astype(BF16), kj, preferred_element_type=F32),
                    dcq + jnp.sum(ds, axis=-1, keepdims=True))

        dq, dcq = lax.fori_loop(0, i + 1, step, (jnp.zeros((tb, hd), F32), jnp.zeros((tb, 1), F32)))
        dq_ref[...] = (dq * scale).astype(dq_ref.dtype)
        dl_ref[...] = delta
        dcq_ref[...] = dcq

    blk = pl.BlockSpec((None, tb, hd), lambda hh, i: (hh, i, 0))
    full = pl.BlockSpec((None, s, hd), lambda hh, i: (hh, 0, 0))
    col = pl.BlockSpec((None, tb, 1), lambda hh, i: (hh, i, 0))
    return pl.pallas_call(
        body, name=name, grid=(h, nb),
        in_specs=[blk, full, full, blk, blk, col, col,
                  pl.BlockSpec((None, nb, 1, tb), lambda hh, i: (hh, 0, 0, 0))],
        out_specs=[blk, col, col],
        out_shape=[jax.ShapeDtypeStruct((h, s, hd), BF16), jax.ShapeDtypeStruct((h, s, 1), F32),
                   jax.ShapeDtypeStruct((h, s, 1), F32)],
        compiler_params=_cparams(),
    )(q, k, v, o, do, lse, c_col, c_row)


def _attn_bwd_kv(name, q, k, v, do, lse_row, delta_row, c_col, c_row, tb):
    h, s, hd = q.shape
    nb = s // tb
    scale = 1.0 / float(hd) ** 0.5

    def body(q_ref, k_ref, v_ref, do_ref, lse_ref, dl_ref, ck_ref, cq_ref, dk_ref, dv_ref, dc_ref):
        j = pl.program_id(1)
        kv = k_ref[...]
        vv = v_ref[...]
        ck = ck_ref[...]

        def step(i, carry):
            dk, dv, dc = carry
            q0 = pl.multiple_of(i * tb, tb)
            qi = q_ref[pl.ds(q0, tb), :]
            doi = do_ref[pl.ds(q0, tb), :]
            st = lax.dot_general(kv, qi, (((1,), (1,)), ((), ())), preferred_element_type=F32) * scale
            st = st + (cq_ref[i] - ck)
            krow = j * tb + lax.broadcasted_iota(jnp.int32, st.shape, 0)
            qcol = i * tb + lax.broadcasted_iota(jnp.int32, st.shape, 1)
            pt = jnp.exp(jnp.where(krow <= qcol, st, -jnp.inf) - lse_ref[i])
            dv = dv + jnp.dot(pt.astype(BF16), doi, preferred_element_type=F32)
            dpt = lax.dot_general(vv, doi, (((1,), (1,)), ((), ())), preferred_element_type=F32)
            dst = pt * (dpt - dl_ref[i])
            dk = dk + jnp.dot(dst.astype(BF16), qi, preferred_element_type=F32)
            dc = dc - jnp.sum(dst, axis=-1, keepdims=True)
            return dk, dv, dc

        init = (jnp.zeros((tb, hd), F32), jnp.zeros((tb, hd), F32), jnp.zeros((tb, 1), F32))
        dk, dv, dc = lax.fori_loop(j, nb, step, init)
        dk_ref[...] = (dk * scale).astype(dk_ref.dtype)
        dv_ref[...] = dv.astype(dv_ref.dtype)
        dc_ref[...] = dc

    blk = pl.BlockSpec((None, tb, hd), lambda hh, jj: (hh, jj, 0))
    full = pl.BlockSpec((None, s, hd), lambda hh, jj: (hh, 0, 0))
    col = pl.BlockSpec((None, tb, 1), lambda hh, jj: (hh, jj, 0))
    rows = pl.BlockSpec((None, nb, 1, tb), lambda hh, jj: (hh, 0, 0, 0))
    return pl.pallas_call(
        body, name=name, grid=(h, nb),
        in_specs=[full, blk, blk, full, rows, rows, col, rows],
        out_specs=[blk, blk, col],
        out_shape=[jax.ShapeDtypeStruct((h, s, hd), BF16), jax.ShapeDtypeStruct((h, s, hd), BF16),
                   jax.ShapeDtypeStruct((h, s, 1), F32)],
        compiler_params=_cparams(),
    )(q, k, v, do, lse_row, delta_row, c_col, c_row)


def _glu(cv, cc):
    c1 = cv[:, :cc].astype(F32)
    c2 = cv[:, cc:].astype(F32)
    return c1 * _sig(c2)


def _conv_fwd(name, zm, w_pad, b_row, g_row, cc, taps):
    s = zm.shape[0]
    tr = _pick(s, (256, 128))
    hpb = tr // HALO
    off = HALO - (taps - 1)

    def body(cur_ref, halo_ref, w_ref, b_ref, g_ref, y_ref, cs_ref, apad):
        i = pl.program_id(0)
        apad[0:HALO, :] = _glu(halo_ref[...], cc) * jnp.where(i > 0, 1.0, 0.0)
        apad[HALO:, :] = _glu(cur_ref[...], cc)
        acc = jnp.zeros((tr, cc), F32) + b_ref[...]
        for t in range(taps):
            acc = acc + w_ref[t:t + 1, :] * apad[off + t:off + t + tr, :]
        y_ref[...] = acc
        r = lax.rsqrt(jnp.mean(acc * acc, axis=-1, keepdims=True) + RMS_EPS)
        n = acc * r * g_ref[...]
        cs_ref[...] = (n * _sig(n)).astype(cs_ref.dtype)

    return pl.pallas_call(
        body, name=name, grid=(s // tr,),
        in_specs=[pl.BlockSpec((tr, 2 * cc), lambda i: (i, 0)),
                  pl.BlockSpec((HALO, 2 * cc), lambda i: (jnp.maximum(i * hpb - 1, 0), 0)),
                  pl.BlockSpec(w_pad.shape, lambda i: (0, 0)),
                  pl.BlockSpec(b_row.shape, lambda i: (0, 0)),
                  pl.BlockSpec(g_row.shape, lambda i: (0, 0))],
        out_specs=[pl.BlockSpec((tr, cc), lambda i: (i, 0)), pl.BlockSpec((tr, cc), lambda i: (i, 0))],
        out_shape=[jax.ShapeDtypeStruct((s, cc), F32), jax.ShapeDtypeStruct((s, cc), BF16)],
        scratch_shapes=[pltpu.VMEM((HALO + tr, cc), F32)], compiler_params=_cparams(),
    )(zm, zm, w_pad, b_row, g_row)


def _conv_bwd(name, zm, y, dcs, w_pad, g_row, cc, taps):
    s = zm.shape[0]
    tr = _pick(s, (256, 128))
    hpb = tr // HALO
    nblk = s // tr
    off = HALO - (taps - 1)

    def dy_of(yv, dcsv, g):
        r = lax.rsqrt(jnp.mean(yv * yv, axis=-1, keepdims=True) + RMS_EPS)
        xh = yv * r
        n = xh * g
        sg = _sig(n)
        dn = dcsv.astype(F32) * (sg * (1.0 + n * (1.0 - sg)))
        dxh = dn * g
        dy = r * (dxh - xh * jnp.mean(dxh * xh, axis=-1, keepdims=True))
        return dy, dn * xh

    def body(cur_ref, halo_ref, y_ref, yn_ref, dcs_ref, dcsn_ref, w_ref, g_ref,
             dcv_ref, dw_ref, db_ref, dg_ref, apad, dypad):
        i = pl.program_id(0)

        @pl.when(i == 0)
        def _():
            dw_ref[...] = jnp.zeros(dw_ref.shape, F32)
            db_ref[...] = jnp.zeros(db_ref.shape, F32)
            dg_ref[...] = jnp.zeros(dg_ref.shape, F32)

        g = g_ref[...]
        apad[0:HALO, :] = _glu(halo_ref[...], cc) * jnp.where(i > 0, 1.0, 0.0)
        apad[HALO:, :] = _glu(cur_ref[...], cc)
        dy, dgt = dy_of(y_ref[...], dcs_ref[...], g)
        dyn, _ = dy_of(yn_ref[...], dcsn_ref[...], g)
        dypad[0:tr, :] = dy
        dypad[tr:, :] = dyn * jnp.where(i < nblk - 1, 1.0, 0.0)
        db_ref[...] += jnp.sum(dy, axis=0, keepdims=True)
        dg_ref[...] += jnp.sum(dgt, axis=0, keepdims=True)
        da = jnp.zeros((tr, cc), F32)
        for t in range(taps):
            da = da + w_ref[t:t + 1, :] * dypad[taps - 1 - t:taps - 1 - t + tr, :]
            dw_ref[t:t + 1, :] += jnp.sum(dy * apad[off + t:off + t + tr, :], axis=0, keepdims=True)
        cv = cur_ref[...]
        c1 = cv[:, :cc].astype(F32)
        sg = _sig(cv[:, cc:].astype(F32))
        dcv_ref[:, :cc] = (da * sg).astype(dcv_ref.dtype)
        dcv_ref[:, cc:] = (da * c1 * sg * (1.0 - sg)).astype(dcv_ref.dtype)

    nxt = lambda i: (jnp.minimum((i + 1) * hpb, s // HALO - 1), 0)
    return pl.pallas_call(
        body, name=name, grid=(nblk,),
        in_specs=[pl.BlockSpec((tr, 2 * cc), lambda i: (i, 0)),
                  pl.BlockSpec((HALO, 2 * cc), lambda i: (jnp.maximum(i * hpb - 1, 0), 0)),
                  pl.BlockSpec((tr, cc), lambda i: (i, 0)), pl.BlockSpec((HALO, cc), nxt),
                  pl.BlockSpec((tr, cc), lambda i: (i, 0)), pl.BlockSpec((HALO, cc), nxt),
                  pl.BlockSpec(w_pad.shape, lambda i: (0, 0)), pl.BlockSpec(g_row.shape, lambda i: (0, 0))],
        out_specs=[pl.BlockSpec((tr, 2 * cc), lambda i: (i, 0)),
                   pl.BlockSpec(w_pad.shape, lambda i: (0, 0)),
                   pl.BlockSpec((1, cc), lambda i: (0, 0)), pl.BlockSpec((1, cc), lambda i: (0, 0))],
        out_shape=[jax.ShapeDtypeStruct((s, 2 * cc), BF16), jax.ShapeDtypeStruct(w_pad.shape, F32),
                   jax.ShapeDtypeStruct((1, cc), F32), jax.ShapeDtypeStruct((1, cc), F32)],
        scratch_shapes=[pltpu.VMEM((HALO + tr, cc), F32), pltpu.VMEM((tr + HALO, cc), F32)],
        compiler_params=_cparams(),
    )(zm, zm, y, y, dcs, dcs, w_pad, g_row)


def _place():
    x, y, c = lax.axis_index("x"), lax.axis_index("y"), lax.axis_index("c")
    chips = [(1 - x, y), (x, 1 - y), (1 - x, 1 - y)]
    return x, y, c, chips


def _half(c, rows):
    rh = rows // 2
    return pl.ds(pl.multiple_of(c * rh, 16), rh)


def _remote(src, dst, send, recv, dev):
    return pltpu.make_async_remote_copy(src_ref=src, dst_ref=dst, send_sem=send, recv_sem=recv,
                                        device_id=dev, device_id_type=MESH)


def _gather_shards(name, shards):
    nt = len(shards)

    def body(*refs):
        s_refs, g_refs = refs[:nt], refs[nt:2 * nt]
        send, recv, loc = refs[2 * nt:]
        x, y, c, chips = _place()
        me = 2 * x + y
        sib = (x, y, 1 - c)

        def part(t, slot, cc):
            return g_refs[t].at[slot, _half(cc, shards[t].shape[0]), :]

        mine = [pltpu.make_async_copy(s_refs[t], g_refs[t].at[me], loc.at[t]) for t in range(nt)]
        for cp in mine:
            cp.start()
        first = []
        for t in range(nt):
            for j, (qx, qy) in enumerate(chips):
                src = s_refs[t].at[_half(c, shards[t].shape[0]), :]
                first.append(_remote(src, part(t, me, c), send.at[t, j], recv.at[t, j], (qx, qy, c)))
        for cp in first:
            cp.start()
        passed = []
        for t in range(nt):
            for j, (qx, qy) in enumerate(chips):
                q = 2 * qx + qy
                _remote(part(t, q, c), part(t, q, c), send.at[t, j], recv.at[t, j], sib).wait_recv()
                fw = _remote(part(t, q, c), part(t, q, c), send.at[t, 3 + j], recv.at[t, 3 + j], sib)
                fw.start()
                passed.append(fw)
        for t in range(nt):
            for j, (qx, qy) in enumerate(chips):
                q = 2 * qx + qy
                _remote(part(t, q, 1 - c), part(t, q, 1 - c), send.at[t, 3 + j], recv.at[t, 3 + j], sib).wait_recv()
        for cp in first + passed:
            cp.wait_send()
        for cp in mine:
            cp.wait()

    return pl.pallas_call(
        body, name=name, in_specs=[ANY] * nt, out_specs=[ANY] * nt,
        out_shape=[jax.ShapeDtypeStruct((N_CHIPS,) + s.shape, s.dtype) for s in shards],
        scratch_shapes=[pltpu.SemaphoreType.DMA((nt, 6)), pltpu.SemaphoreType.DMA((nt, 6)),
                        pltpu.SemaphoreType.DMA((nt,))],
        compiler_params=pltpu.CompilerParams(has_side_effects=True),
    )(*shards)


def _swap_halves(name, grads):
    nt = len(grads)

    def body(*refs):
        g_refs, r_refs = refs[:nt], refs[nt:2 * nt]
        send, recv = refs[2 * nt:]
        x, y, c, _ = _place()
        cps = [_remote(g_refs[t].at[:, _half(1 - c, grads[t].shape[1]), :], r_refs[t], send.at[t], recv.at[t],
                       (x, y, 1 - c)) for t in range(nt)]
        for cp in cps:
            cp.start()
        for cp in cps:
            cp.wait()

    return pl.pallas_call(
        body, name=name, in_specs=[ANY] * nt, out_specs=[ANY] * nt,
        out_shape=[jax.ShapeDtypeStruct((N_CHIPS, g.shape[1] // 2, g.shape[2]), g.dtype) for g in grads],
        scratch_shapes=[pltpu.SemaphoreType.DMA((nt,)), pltpu.SemaphoreType.DMA((nt,))],
        compiler_params=pltpu.CompilerParams(has_side_effects=True),
    )(*grads)


def _scatter_chips(name, parts):
    nt = len(parts)

    def body(*refs):
        p_refs, r_refs = refs[:nt], refs[nt:2 * nt]
        send, recv = refs[2 * nt:]
        x, y, c, chips = _place()
        cps = []
        for t in range(nt):
            for j, (qx, qy) in enumerate(chips):
                cps.append(_remote(p_refs[t].at[2 * qx + qy], r_refs[t].at[j], send.at[t, j], recv.at[t, j],
                                   (qx, qy, c)))
        for cp in cps:
            cp.start()
        for cp in cps:
            cp.wait()

    return pl.pallas_call(
        body, name=name, in_specs=[ANY] * nt, out_specs=[ANY] * nt,
        out_shape=[jax.ShapeDtypeStruct((3,) + p.shape[1:], p.dtype) for p in parts],
        scratch_shapes=[pltpu.SemaphoreType.DMA((nt, 3)), pltpu.SemaphoreType.DMA((nt, 3))],
        compiler_params=pltpu.CompilerParams(has_side_effects=True),
    )(*parts)


def _join_halves(name, halves):
    nt = len(halves)

    def body(*refs):
        h_refs, o_refs = refs[:nt], refs[nt:2 * nt]
        send, recv, loc = refs[2 * nt:]
        x, y, c, _ = _place()
        cps, mine = [], []
        for t in range(nt):
            dst = o_refs[t].at[_half(c, 2 * halves[t].shape[0]), :]
            mine.append(pltpu.make_async_copy(h_refs[t], dst, loc.at[t]))
            cps.append(_remote(h_refs[t], dst, send.at[t], recv.at[t], (x, y, 1 - c)))
        for cp in mine + cps:
            cp.start()
        for cp in cps:
            cp.wait()
        for cp in mine:
            cp.wait()

    return pl.pallas_call(
        body, name=name, in_specs=[ANY] * nt, out_specs=[ANY] * nt,
        out_shape=[jax.ShapeDtypeStruct((2 * hh.shape[0], hh.shape[1]), hh.dtype) for hh in halves],
        scratch_shapes=[pltpu.SemaphoreType.DMA((nt,)), pltpu.SemaphoreType.DMA((nt,)),
                        pltpu.SemaphoreType.DMA((nt,))],
        compiler_params=pltpu.CompilerParams(has_side_effects=True),
    )(*halves)


def _add_own_half(name, grad, recv_half, c_arr):
    _, r, cdim = grad.shape
    rh = r // 2
    tr = _pick(rh, (256, 128, 64, 32, 16))
    nrb = rh // tr

    def body(c_ref, g_ref, r_ref, o_ref):
        o_ref[...] = (g_ref[...].astype(F32) + r_ref[...].astype(F32)).astype(o_ref.dtype)

    return pl.pallas_call(
        body, name=name,
        grid_spec=pltpu.PrefetchScalarGridSpec(
            num_scalar_prefetch=1, grid=(N_CHIPS, nrb),
            in_specs=[pl.BlockSpec((None, tr, cdim), lambda q, i, cr: (q, cr[0] * nrb + i, 0)),
                      pl.BlockSpec((None, tr, cdim), lambda q, i, cr: (q, i, 0))],
            out_specs=pl.BlockSpec((None, tr, cdim), lambda q, i, cr: (q, i, 0))),
        out_shape=jax.ShapeDtypeStruct((N_CHIPS, rh, cdim), BF16), compiler_params=_cparams(),
    )(c_arr, grad, recv_half)


def _add_chips(name, part, came, chip_arr):
    _, rh, cdim = part.shape
    tr = _pick(rh, (256, 128, 64, 32, 16))

    def body(q_ref, p_ref, r_ref, o_ref):
        acc = p_ref[...].astype(F32)
        for j in range(3):
            acc = acc + r_ref[j].astype(F32)
        o_ref[...] = acc

    return pl.pallas_call(
        body, name=name,
        grid_spec=pltpu.PrefetchScalarGridSpec(
            num_scalar_prefetch=1, grid=(rh // tr,),
            in_specs=[pl.BlockSpec((None, tr, cdim), lambda i, qr: (qr[0], i, 0)),
                      pl.BlockSpec((3, tr, cdim), lambda i, qr: (0, i, 0))],
            out_specs=pl.BlockSpec((tr, cdim), lambda i, qr: (i, 0))),
        out_shape=jax.ShapeDtypeStruct((rh, cdim), F32), compiler_params=_cparams(),
    )(chip_arr, part, came)


def _reduce_scatter(tag, grads, c_arr, chip_arr):
    nt = len(grads)
    got = _swap_halves(f"rs_swap_{tag}", grads)
    parts = [_add_own_half(f"rs_add2_{tag}_{t}", grads[t], got[t], c_arr) for t in range(nt)]
    came = _scatter_chips(f"rs_scatter_{tag}", parts)
    halves = [_add_chips(f"rs_add4_{tag}_{t}", parts[t], came[t], chip_arr) for t in range(nt)]
    return _join_halves(f"rs_join_{tag}", halves)


def _allgather_blocks(name, blk):
    m_per, n = blk.shape

    def body(x_ref, out_ref, send_sems, recv_sems, local_sem):
        x, y, c, chips = _place()
        me, sibling = (x, y, c), (x, y, 1 - c)

        def rows(px, py, pc):
            return out_ref.at[pl.ds(pl.multiple_of((4 * px + 2 * py + pc) * m_per, 8), m_per), :]

        def copy(k, block, to, src=None):
            return _remote(rows(*block) if src is None else src, rows(*block), send_sems.at[k], recv_sems.at[k], to)

        mine = pltpu.make_async_copy(x_ref, rows(*me), local_sem)
        mine.start()
        first = [copy(0, me, sibling, src=x_ref)]
        first += [copy(1 + j, me, (*chip, c), src=x_ref) for j, chip in enumerate(chips)]
        for cp in first:
            cp.start()
        passed = [copy(4 + j, (*chip, c), sibling) for j, chip in enumerate(chips)]
        for j, chip in enumerate(chips):
            copy(1 + j, (*chip, c), me).wait_recv()
            passed[j].start()
        copy(0, sibling, me).wait_recv()
        for j, chip in enumerate(chips):
            copy(4 + j, (*chip, 1 - c), me).wait_recv()
        for cp in first + passed:
            cp.wait_send()
        mine.wait()

    return pl.pallas_call(
        body, name=name, out_shape=jax.ShapeDtypeStruct((N_DEV * m_per, n), blk.dtype),
        in_specs=[pl.BlockSpec(memory_space=pltpu.VMEM)], out_specs=pl.BlockSpec(memory_space=pltpu.VMEM),
        scratch_shapes=[pltpu.SemaphoreType.DMA((7,)), pltpu.SemaphoreType.DMA((7,)), pltpu.SemaphoreType.DMA],
        compiler_params=_cparams(),
    )(blk)


def _sum_blocks(name, stacked):
    nd, m, n = stacked.shape
    tr = _pick(m, (336, 256, 168, 128, 64, 32, 16, 8))

    def body(s_ref, o_ref):
        acc = s_ref[0]
        for d in range(1, nd):
            acc = acc + s_ref[d]
        o_ref[...] = acc

    return pl.pallas_call(
        body, name=name, grid=(m // tr,),
        in_specs=[pl.BlockSpec((nd, tr, n), lambda i: (0, i, 0))],
        out_specs=pl.BlockSpec((tr, n), lambda i: (i, 0)),
        out_shape=jax.ShapeDtypeStruct((m, n), F32), compiler_params=_cparams(),
    )(stacked)


def _adamw_vals(w, g, m, v):
    m2 = ADAM_B1 * m + (1.0 - ADAM_B1) * g
    v2 = ADAM_B2 * v + (1.0 - ADAM_B2) * (g * g)
    m_hat = m2 / (1.0 - ADAM_B1 ** ADAM_STEP)
    v_hat = v2 / (1.0 - ADAM_B2 ** ADAM_STEP)
    delta = -ADAM_LR * (m_hat / (jnp.sqrt(v_hat) + ADAM_EPS) + ADAM_WD * w)
    return delta, m2, v2


def _adamw_layers(name, w, m, v, grads):
    nl, r, cdim = w.shape
    tr = _pick(r, (128, 64, 32, 16, 8))

    def body(w_ref, m_ref, v_ref, *rest):
        g_refs = rest[:nl]
        go_ref, d_ref, mo_ref, vo_ref = rest[nl:]
        layer = pl.program_id(0)
        g = g_refs[0][...]
        for j in range(1, nl):
            g = jnp.where(layer == j, g_refs[j][...], g)
        delta, m2, v2 = _adamw_vals(w_ref[...], g, m_ref[...], v_ref[...])
        go_ref[...] = g
        d_ref[...] = delta
        mo_ref[...] = m2
        vo_ref[...] = v2

    big = pl.BlockSpec((None, tr, cdim), lambda l, i: (l, i, 0))
    g_specs = [pl.BlockSpec((tr, cdim), functools.partial(lambda l, i, j: (jnp.where(l == j, i, 0), 0), j=j))
               for j in range(nl)]
    return pl.pallas_call(
        body, name=name, grid=(nl, r // tr), in_specs=[big, big, big] + g_specs,
        out_specs=[big, big, big, big],
        out_shape=[jax.ShapeDtypeStruct(w.shape, F32)] * 4, compiler_params=_cparams(),
    )(w, m, v, *grads)


def _adamw_small(name, w, g, m, v):
    def body(w_ref, g_ref, m_ref, v_ref, d_ref, mo_ref, vo_ref):
        delta, m2, v2 = _adamw_vals(w_ref[...], g_ref[...], m_ref[...], v_ref[...])
        d_ref[...] = delta
        mo_ref[...] = m2
        vo_ref[...] = v2

    return pl.pallas_call(body, name=name, out_shape=[jax.ShapeDtypeStruct(w.shape, F32)] * 3,
                          compiler_params=_cparams())(w, g, m, v)


def _swiglu_fwd(name, z, f):
    def fn(zz):
        a = zz[:, :f].astype(F32)
        b = zz[:, f:].astype(F32)
        return a * _sig(a) * b
    return _rowwise(name, fn, [z], [(f, BF16)])[0]


def _swiglu_bwd(name, z, ds, f):
    def fn(zz, dd):
        a = zz[:, :f].astype(F32)
        b = zz[:, f:].astype(F32)
        d = dd.astype(F32)
        sg = _sig(a)
        da = d * b * (sg * (1.0 + a * (1.0 - sg)))
        db = d * a * sg
        return jnp.concatenate([da, db], axis=1)
    return _rowwise(name, fn, [z, ds], [(2 * f, BF16)])[0]


def _heads(a, nh):
    s, w = a.shape
    return jnp.transpose(a.reshape(s, nh, w // nh), (1, 0, 2))


def _unheads(a):
    nh, s, hd = a.shape
    return jnp.transpose(a, (1, 0, 2)).reshape(s, nh * hd)


def _as_rows(col, tb):
    nh, s, _ = col.shape
    return col.reshape(nh, s // tb, 1, tb)


class _Dims:
    def __init__(self, d, f, aw, nh, cc, taps, dp, s):
        self.d, self.f, self.aw, self.nh, self.cc, self.taps, self.dp, self.s = d, f, aw, nh, cc, taps, dp, s
        self.o_cv, self.o_ga, self.o_gc = 0, 2 * cc, 2 * cc + d
        self.o_q = 2 * cc + 2 * d
        self.n_main = self.o_q + 3 * aw
        self.tb = _pick(s, (256, 128))
        assert self.o_ga % d == 0 and self.o_q % aw == 0 and nh <= LANES


def _ffn_fwd(tag, h, g_row, w_in, w_out, dm):
    n = _rms_fwd(f"{tag}_rms", h, g_row)
    z = _mm_nn(f"{tag}_in", n, w_in, BF16)
    s = _swiglu_fwd(f"{tag}_act", z, dm.f)
    h2 = _mm_nn(f"{tag}_out", s, w_out, F32, res=h, scale=FFN_RES)
    return h2, (h, n, z, s)


def _ffn_bwd(tag, dh, dh_half_b, saved, g_row, w_in, w_out, dm, oscale):
    h, n, z, s = saved
    ds = _mm_nt(f"{tag}_dout", dh_half_b, w_out, BF16)
    dz = _swiglu_bwd(f"{tag}_dact", z, ds, dm.f)
    dw_out = _mm_tn(f"{tag}_wout", s, dh_half_b, 1)
    dw_in = _mm_tn(f"{tag}_win", n, dz, N_CHIPS)
    dn = _mm_nt(f"{tag}_din", dz, w_in, BF16)
    dh0, dh0_b, dg = _rms_bwd_res(f"{tag}_drms", h, g_row, [dn], dh, oscale)
    return dh0, dh0_b, dg, dw_in, dw_out


def _mixer_fwd(tag, h, sm, wt, dm):
    d, cc, aw, nh, tb = dm.d, dm.cc, dm.aw, dm.nh, dm.tb
    u = _rms_fwd(f"{tag}_rms", h, sm["g_mix"])
    zm = _mm_nn(f"{tag}_in", u, wt["w_main"], BF16)
    zf = _mm_nn(f"{tag}_inf", u, wt["w_f"], F32)
    c = _fgate_fwd(f"{tag}_fgate", zf, sm["b_f"])
    q = _heads(zm[:, dm.o_q:dm.o_q + aw], nh)
    k = _heads(zm[:, dm.o_q + aw:dm.o_q + 2 * aw], nh)
    v = _heads(zm[:, dm.o_q + 2 * aw:dm.o_q + 3 * aw], nh)
    c_col = jnp.transpose(c[:, :nh], (1, 0))[:, :, None]
    c_row = _as_rows(c_col, tb)
    o_h, lse = _attn_fwd(f"{tag}_attn", q, k, v, c_col, c_row, tb)
    o = _unheads(o_h).astype(BF16)
    ya =_mm_nn(f"{tag}_aout", o, wt["w_attn_out"], BF16)
    y, cs = _conv_fwd(f"{tag}_conv", zm, sm["conv_w"], sm["conv_b"], sm["g_conv"], cc, dm.taps)
    yc = _mm_nn(f"{tag}_cout", cs, wt["w_conv_out"], BF16)

    def merge(ga, gc, a, b):
        return _sig(ga.astype(F32)) * a.astype(F32) + _sig(gc.astype(F32)) * b.astype(F32)

    mg = _rowwise(f"{tag}_merge", merge, [(zm, dm.o_ga // d, d), (zm, dm.o_gc // d, d), ya, yc], [(d, BF16)])[0]
    h2 = _mm_nn(f"{tag}_out", mg, wt["w_out"], F32, res=h, scale=1.0)
    return h2, (h, u, zm, zf, q, k, v, c_col, c_row, o_h, lse, o, ya, y, cs, yc, mg)


def _mixer_bwd(tag, dh, dh_b, saved, sm, wt, dm):
    d, cc, aw, nh, tb = dm.d, dm.cc, dm.aw, dm.nh, dm.tb
    h, u, zm, zf, q, k, v, c_col, c_row, o_h, lse, o, ya, y, cs, yc, mg = saved
    dmg = _mm_nt(f"{tag}_dout", dh_b, wt["w_out"], BF16)
    dw_out = _mm_tn(f"{tag}_wout", mg, dh_b, 1)

    def unmerge(dd, ga, gc, a, b):
        dd, a, b = dd.astype(F32), a.astype(F32), b.astype(F32)
        sa, sc = _sig(ga.astype(F32)), _sig(gc.astype(F32))
        return dd * sa, dd * sc, dd * a * sa * (1.0 - sa), dd * b * sc * (1.0 - sc)

    dya, dyc, dga, dgc = _rowwise(f"{tag}_dmerge", unmerge,
                                  [dmg, (zm, dm.o_ga // d, d), (zm, dm.o_gc // d, d), ya, yc], [(d, BF16)] * 4)
    dw_a = _mm_tn(f"{tag}_waout", o, dya, N_CHIPS)
    do = _mm_nt(f"{tag}_daout", dya, wt["w_attn_out"], BF16)
    dw_c = _mm_tn(f"{tag}_wcout", cs, dyc, N_CHIPS)
    dcs = _mm_nt(f"{tag}_dcout", dyc, wt["w_conv_out"], BF16)
    dcv, dconv_w, dconv_b, dg_conv = _conv_bwd(f"{tag}_dconv", zm, y, dcs, sm["conv_w"], sm["g_conv"], cc, dm.taps)
    do_h = _heads(do, nh)
    dq_h, delta, dcq_h = _attn_bwd_q(f"{tag}_dattn_q", q, k, v, o_h, do_h, lse, c_col, c_row, tb)
    dk_h, dv_h, dck_h = _attn_bwd_kv(f"{tag}_dattn_kv", q, k, v, do_h, _as_rows(lse, tb), _as_rows(delta, tb),
                                     c_col, c_row, tb)
    dc = jnp.pad(jnp.transpose(dcq_h[:, :, 0], (1, 0)), ((0, 0), (0, zf.shape[1] - nh)))
    dc_k = jnp.pad(jnp.transpose(dck_h[:, :, 0], (1, 0)), ((0, 0), (0, zf.shape[1] - nh)))
    dzf, dzf_b, db_f = _fgate_bwd(f"{tag}_dfgate", dc, dc_k, zf, sm["b_f"])
    dzm = jnp.concatenate([dcv, dga, dgc, _unheads(dq_h), _unheads(dk_h), _unheads(dv_h)], axis=1)
    dw_main = _mm_tn(f"{tag}_win", u, dzm, 1)
    dw_f = _mm_tn(f"{tag}_winf", u, dzf_b, 1)
    du = _mm_nt(f"{tag}_din", dzm, wt["w_main"], BF16)
    du_f = _mm_nt(f"{tag}_dinf", dzf_b, wt["w_f"], BF16)
    dh0, dh0_b, dg_mix = _rms_bwd_res(f"{tag}_drms", h, sm["g_mix"], [du, du_f], dh, FFN_RES)
    small = dict(g_mix=dg_mix, b_f=db_f[:, :nh], conv_w=dconv_w[:dm.taps], conv_b=dconv_b, g_conv=dg_conv)
    return dh0, dh0_b, small, dw_main, dw_f, dw_a, dw_c, dw_out


def _ple_fwd(tag, h, p_b, sm, wt, dm):
    n = _rms_fwd(f"{tag}_rms", h, sm["g_ple"])
    gp = _mm_nn(f"{tag}_gate", n, wt["w_ple_gate"], BF16)
    pp = _mm_nn(f"{tag}_proj", p_b, wt["w_ple_proj"], BF16)

    def fn(hh, a, b):
        return hh + _sig(a.astype(F32)) * b.astype(F32)

    h2 = _rowwise(f"{tag}_mix", fn, [h, gp, pp], [(dm.d, F32)])[0]
    return h2, (h, n, gp, pp)


def _ple_bwd(tag, dh, saved, p_b, sm, wt, dm):
    h, n, gp, pp = saved

    def fn(dd, a, b):
        gate = _sig(a.astype(F32))
        b = b.astype(F32)
        return dd * b * gate * (1.0 - gate), dd * gate

    dgp, dpp = _rowwise(f"{tag}_dmix", fn, [dh, gp, pp], [(dm.d, BF16)] * 2)
    dw_proj = _mm_tn(f"{tag}_wproj", p_b, dpp, N_CHIPS)
    dw_gate = _mm_tn(f"{tag}_wgate", n, dgp, 1)
    dn = _mm_nt(f"{tag}_dgate", dgp, wt["w_ple_gate"], BF16)
    dh0, dh0_b, dg = _rms_bwd_res(f"{tag}_drms", h, sm["g_ple"], [dn], dh, FFN_RES)
    return dh0, dh0_b, dg, dw_gate, dw_proj


def _loss_head(name, h, g_row, target):
    d = h.shape[1]

    def fn(x, tg, g):
        r = lax.rsqrt(jnp.mean(x * x, axis=-1, keepdims=True) + RMS_EPS)
        out = x * r * g
        e = out - tg
        per_row = jnp.sum(e * e, axis=-1, keepdims=True) * (0.5 / d)
        loss = jnp.zeros((1, LANES), F32) + jnp.sum(per_row, axis=0, keepdims=True)
        dx, dg = _rms_bwd_vals(x, g, e * (1.0 / d))
        return dx, loss, dg

    return _rowwise(name, fn, [h, target], [(d, F32)], consts=[g_row], reds=[((1, LANES), F32), ((1, d), F32)])


_BIG = ("w_ff1_in", "w_ff1_out", "w_in", "w_attn_out", "w_conv_out", "w_out", "w_ff2_in", "w_ff2_out",
        "w_ple_gate", "w_ple_proj")
_SMALL = ("g_ff1", "g_mix", "b_f", "conv_w", "conv_b", "g_conv", "g_ff2", "g_ple")
_ORDER = ("g_ff1", "w_ff1_in", "w_ff1_out", "g_mix", "w_in", "b_f", "w_attn_out", "conv_w", "conv_b", "g_conv",
          "w_conv_out", "w_out", "g_ff2", "w_ff2_in", "w_ff2_out", "g_ple", "w_ple_gate", "w_ple_proj", "g_final")


def _round_up(n, k):
    return (n + k - 1) // k * k


def _unpack_layer(gathered, dm):
    g = dict(zip(_BIG, gathered))
    d = dm.d
    w_in = jnp.transpose(g["w_in"], (1, 0, 2)).reshape(d, -1)
    o_f = 3 * dm.aw
    o_c = o_f + dm.nh
    w_main = jnp.concatenate([w_in[:, o_c:], w_in[:, :o_f]], axis=1)
    w_f = jnp.pad(w_in[:, o_f:o_c], ((0, 0), (0, LANES - dm.nh)))
    flat = lambda a: a.reshape(1, a.shape[0] * a.shape[1], a.shape[2])
    return dict(w_ff1_in=g["w_ff1_in"], w_ff1_out=flat(g["w_ff1_out"]), w_main=w_main[None], w_f=w_f[None],
                w_attn_out=g["w_attn_out"], w_conv_out=g["w_conv_out"], w_out=flat(g["w_out"]),
                w_ff2_in=g["w_ff2_in"], w_ff2_out=flat(g["w_ff2_out"]), w_ple_gate=flat(g["w_ple_gate"]),
                w_ple_proj=g["w_ple_proj"])


def _pack_grads(gw, dm):
    d = dm.d
    o_f = 3 * dm.aw
    main, wf = gw["w_main"][0], gw["w_f"][0]
    n_rest = dm.n_main - o_f
    w_in = jnp.concatenate([main[:, n_rest:], wf[:, :dm.nh], main[:, :n_rest]], axis=1)
    w_in = jnp.transpose(w_in.reshape(d, N_CHIPS, -1), (1, 0, 2))
    split = lambda a: a.reshape(N_CHIPS, a.shape[1] // N_CHIPS, a.shape[2])
    out = dict(w_ff1_in=gw["w_ff1_in"], w_ff1_out=split(gw["w_ff1_out"]), w_in=w_in, w_attn_out=gw["w_attn_out"],
               w_conv_out=gw["w_conv_out"], w_out=split(gw["w_out"]), w_ff2_in=gw["w_ff2_in"],
               w_ff2_out=split(gw["w_ff2_out"]), w_ple_gate=split(gw["w_ple_gate"]), w_ple_proj=gw["w_ple_proj"])
    return [out[n] for n in _BIG]


def kernel(x, p, g_ff1, w_ff1_in, w_ff1_out, g_mix, w_in, b_f, w_attn_out, conv_w, conv_b, g_conv, w_conv_out, w_out, g_ff2, w_ff2_in, w_ff2_out, g_ple, w_ple_gate, w_ple_proj, g_final, loss_target, m_g_ff1, m_w_ff1_in, m_w_ff1_out, m_g_mix, m_w_in, m_b_f, m_w_attn_out, m_conv_w, m_conv_b, m_g_conv, m_w_conv_out, m_w_out, m_g_ff2, m_w_ff2_in, m_w_ff2_out, m_g_ple, m_w_ple_gate, m_w_ple_proj, m_g_final, v_g_ff1, v_w_ff1_in, v_w_ff1_out, v_g_mix, v_w_in, v_b_f, v_w_attn_out, v_conv_w, v_conv_b, v_g_conv, v_w_conv_out, v_w_out, v_g_ff2, v_w_ff2_in, v_w_ff2_out, v_g_ple, v_w_ple_gate, v_w_ple_proj, v_g_final):
    args = dict(locals())
    wts = {n: args[n] for n in _ORDER}
    mom = {n: args["m_" + n] for n in _ORDER}
    var = {n: args["v_" + n] for n in _ORDER}

    nl = g_ff1.shape[0]
    s, d = x.shape[1], x.shape[2]
    nh = b_f.shape[1]
    taps = conv_w.shape[1]
    cc = conv_b.shape[1]
    dm = _Dims(d=d, f=w_ff1_out.shape[1] * N_CHIPS, aw=w_attn_out.shape[1], nh=nh, cc=cc, taps=taps,
               dp=w_ple_proj.shape[1], s=s)
    assert taps - 1 <= HALO

    xi = lax.axis_index("x")
    yi = lax.axis_index("y")
    ci = lax.axis_index("c")
    c_arr = jnp.reshape(ci, (1,)).astype(jnp.int32)
    chip_arr = jnp.reshape(2 * xi + yi, (1,)).astype(jnp.int32)

    h = x[0]
    target = loss_target[0]
    p_b = p[:, 0].astype(BF16)

    cw_rows = _round_up(nl * taps, 8)
    cw_blk = jnp.pad(conv_w.reshape(nl * taps, -1), ((0, cw_rows - nl * taps), (0, 0)))
    cw_all = _allgather_blocks("gather_conv_w", cw_blk).reshape(N_CHIPS, 2, cw_rows, -1)[:, 0, :nl * taps]
    conv_w_full = jnp.transpose(cw_all.reshape(N_CHIPS, nl, taps, -1), (1, 2, 0, 3)).reshape(nl, taps, cc)
    taps_pad = _round_up(taps, 8)

    def small_of(i):
        row = lambda a: a[i][None, :]
        return dict(g_ff1=row(g_ff1), g_mix=row(g_mix), g_conv=row(g_conv), conv_b=row(conv_b), g_ff2=row(g_ff2),
                    g_ple=row(g_ple), b_f=jnp.pad(b_f[i][None, :], ((0, 0), (0, LANES - nh))),
                    conv_w=jnp.pad(conv_w_full[i], ((0, taps_pad - taps), (0, 0))))

    layer_w, saved = [], []
    for i in range(nl):
        shards = [wts[n][i].astype(BF16) for n in _BIG]
        wt = _unpack_layer(_gather_shards(f"gather_l{i}", shards), dm)
        sm = small_of(i)
        h, sv1 = _ffn_fwd(f"l{i}_ff1", h, sm["g_ff1"], wt["w_ff1_in"], wt["w_ff1_out"], dm)
        h, sv2 = _mixer_fwd(f"l{i}_mix", h, sm, wt, dm)
        h, sv3 = _ffn_fwd(f"l{i}_ff2", h, sm["g_ff2"], wt["w_ff2_in"], wt["w_ff2_out"], dm)
        h, sv4 = _ple_fwd(f"l{i}_ple", h, p_b[i], sm, wt, dm)
        layer_w.append((wt, sm))
        saved.append((sv1, sv2, sv3, sv4))

    dh, loss_row, dg_final = _loss_head("loss_head", h, g_final[None, :], target)

    big_grads = [None] * nl
    small_grads = [None] * nl
    for i in range(nl - 1, -1, -1):
        wt, sm = layer_w[i]
        sv1, sv2, sv3, sv4 = saved[i]
        dh, dh_b, dg_ple, dw_gate, dw_proj = _ple_bwd(f"l{i}_ple", dh, sv4, p_b[i], sm, wt, dm)
        dh, dh_b, dg_ff2, dw_in2, dw_out2 = _ffn_bwd(f"l{i}_ff2", dh, dh_b, sv3, sm["g_ff2"], wt["w_ff2_in"],
                                                     wt["w_ff2_out"], dm, 1.0)
        dh, dh_b, sg, dw_main, dw_f, dw_a, dw_c, dw_o = _mixer_bwd(f"l{i}_mix", dh, dh_b, sv2, sm, wt, dm)
        dh, dh_b, dg_ff1, dw_in1, dw_out1 = _ffn_bwd(f"l{i}_ff1", dh, dh_b, sv1, sm["g_ff1"], wt["w_ff1_in"],
                                                     wt["w_ff1_out"], dm, 1.0)
        gw = dict(w_ff1_in=dw_in1, w_ff1_out=dw_out1, w_main=dw_main, w_f=dw_f, w_attn_out=dw_a, w_conv_out=dw_c,
                  w_out=dw_o, w_ff2_in=dw_in2, w_ff2_out=dw_out2, w_ple_gate=dw_gate, w_ple_proj=dw_proj)
        big_grads[i] = _reduce_scatter(f"l{i}", _pack_grads(gw, dm), c_arr, chip_arr)
        sg.update(g_ff1=dg_ff1, g_ff2=dg_ff2, g_ple=dg_ple)
        small_grads[i] = sg
    grad_x = dh[None]

    pieces = [small_grads[i][n].reshape(-1) for i in range(nl) for n in _SMALL]
    pieces += [dg_final.reshape(-1), loss_row[0, :1]]
    flat = jnp.concatenate(pieces)
    n_flat = flat.shape[0]
    rows = _round_up(_round_up(n_flat, LANES) // LANES, 8)
    blk = jnp.pad(flat, (0, rows * LANES - n_flat)).reshape(rows, LANES)
    total = _sum_blocks("sum_small", _allgather_blocks("gather_small", blk).reshape(N_DEV, rows, LANES)).reshape(-1)
    small_tot = {n: [] for n in _SMALL}
    pos = 0
    for i in range(nl):
        for n in _SMALL:
            shp = small_grads[i][n].shape
            size = shp[0] * shp[1]
            small_tot[n].append(total[pos:pos + size].reshape(shp))
            pos += size
    g_final_tot = total[pos:pos + d]
    loss = total[pos + d]

    grads, deltas, new_m, new_v = {}, {}, {}, {}
    for t, n in enumerate(_BIG):
        grads[n], deltas[n], new_m[n], new_v[n] = _adamw_layers(
            f"adamw_{n}", wts[n], mom[n], var[n], [big_grads[i][t] for i in range(nl)])
    chip = 2 * xi + yi
    for n in _SMALL:
        g = jnp.concatenate(small_tot[n], axis=0)
        if n == "conv_w":
            cpc = cc // N_CHIPS
            g = lax.dynamic_slice_in_dim(g.reshape(nl * taps, cc), chip * cpc, cpc, axis=1)
            shape2 = (nl * taps, cpc)
        else:
            shape2 = g.shape
        dl, mm, vv = _adamw_small(f"adamw_{n}", wts[n].reshape(shape2), g, mom[n].reshape(shape2),
                                  var[n].reshape(shape2))
        grads[n] = g.reshape(wts[n].shape)
        deltas[n], new_m[n], new_v[n] = (a.reshape(wts[n].shape) for a in (dl, mm, vv))
    g2 = g_final_tot[None, :]
    dl, mm, vv = _adamw_small("adamw_g_final", g_final[None, :], g2, m_g_final[None, :], v_g_final[None, :])
    grads["g_final"] = g_final_tot
    deltas["g_final"], new_m["g_final"], new_v["g_final"] = dl[0], mm[0], vv[0]

    return (loss, grad_x, *[grads[n] for n in _ORDER], *[deltas[n] for n in _ORDER],
            *[new_m[n] for n in _ORDER], *[new_v[n] for n in _ORDER])
```

```python
import functools

import jax
import jax.numpy as jnp
from jax import lax
from jax.experimental import pallas as pl
from jax.experimental.pallas import tpu as pltpu

F32 = jnp.float32
BF16 = jnp.bfloat16
MESH = pl.DeviceIdType.MESH
ANY = pl.BlockSpec(memory_space=pl.ANY)
HBM_SPEC = pl.BlockSpec(memory_space=pltpu.HBM)
SEM_SPEC = pl.BlockSpec(memory_space=pltpu.SEMAPHORE)
EFFECT = pltpu.SideEffectType.DATAFLOW_SIDE_EFFECTING

RMS_EPS = 1e-6
FFN_RES = 0.5
ADAM_LR = 0.001
ADAM_B1 = 0.9
ADAM_B2 = 0.999
ADAM_EPS = 1e-08
ADAM_WD = 0.01
ADAM_STEP = 10

N_CHIPS = 4
N_DEV = 8
LANES = 128
HALO = 32
VMEM_LIMIT = 56 * 1024 * 1024


def _cparams():
    return pltpu.CompilerParams(vmem_limit_bytes=VMEM_LIMIT)


def _pick(n, prefs):
    for p in prefs:
        if p <= n and n % p == 0:
            return p
    return n


def _sig(x):
    return 1.0 / (1.0 + jnp.exp(-x))


def _rowwise(name, fn, ins, outs, consts=(), reds=(), tm=None, deps=()):
    ins = [a if isinstance(a, tuple) else (a, 0, a.shape[1]) for a in ins]
    m = ins[0][0].shape[0]
    tm = tm or _pick(m, (256, 128, 64, 32, 16, 8))
    n_in, n_c, n_o, n_r = len(ins), len(consts), len(outs), len(reds)
    n_d = len(deps)
    consts = list(consts) + list(deps)

    def body(*refs):
        in_refs = refs[:n_in + n_c]
        o_refs = refs[n_in + n_c + n_d:n_in + n_c + n_d + n_o]
        r_refs = refs[n_in + n_c + n_d + n_o:]
        res = fn(*[r[...] for r in in_refs])
        if not isinstance(res, (tuple, list)):
            res = (res,)
        for r, v in zip(o_refs, res[:n_o]):
            r[...] = v.astype(r.dtype)
        if n_r:
            @pl.when(pl.program_id(0) == 0)
            def _():
                for r in r_refs:
                    r[...] = jnp.zeros(r.shape, r.dtype)
            for r, v in zip(r_refs, res[n_o:]):
                r[...] += v.astype(r.dtype)

    in_specs = [pl.BlockSpec((tm, w), functools.partial(lambda i, cb: (i, cb), cb=cb)) for (_, cb, w) in ins]
    in_specs += [pl.BlockSpec(c.shape, lambda i: (0, 0)) for c in consts]
    out_specs = [pl.BlockSpec((tm, w), lambda i: (i, 0)) for (w, _) in outs]
    out_specs += [pl.BlockSpec(s, lambda i: (0, 0)) for (s, _) in reds]
    out_shape = [jax.ShapeDtypeStruct((m, w), d) for (w, d) in outs]
    out_shape += [jax.ShapeDtypeStruct(s, d) for (s, d) in reds]
    res = pl.pallas_call(
        body, name=name, grid=(m // tm,), in_specs=in_specs, out_specs=out_specs, out_shape=out_shape,
        compiler_params=_cparams(),
    )(*[a for (a, _, _) in ins], *consts)
    return res


def _rms_fwd(name, h, g, deps=()):
    def fn(x, gg):
        r = lax.rsqrt(jnp.mean(x * x, axis=-1, keepdims=True) + RMS_EPS)
        return x * r * gg
    return _rowwise(name, fn, [h], [(h.shape[1], BF16)], consts=[g], deps=deps)[0]


def _rms_bwd_vals(x, g, dn):
    r = lax.rsqrt(jnp.mean(x * x, axis=-1, keepdims=True) + RMS_EPS)
    xh = x * r
    dxh = dn * g
    dx = r * (dxh - xh * jnp.mean(dxh * xh, axis=-1, keepdims=True))
    dg = jnp.sum(dn * xh, axis=0, keepdims=True)
    return dx, dg


def _rms_bwd_res(name, h, g, dns, dres, oscale):
    n_dn = len(dns)

    def fn(x, *rest):
        dn = rest[0].astype(F32)
        for t in rest[1:n_dn]:
            dn = dn + t.astype(F32)
        dr, gg = rest[n_dn], rest[n_dn + 1]
        dx, dg = _rms_bwd_vals(x, gg, dn)
        dh = dr + dx
        return dh, oscale * dh, dg

    d = h.shape[1]
    return _rowwise(name, fn, [h, *dns, dres], [(d, F32), (d, BF16)], consts=[g], reds=[((1, d), F32)])


_TN_PREFS = (1408, 1024, 768, 512, 256, 128)


def _mm_nn(name, a, w, out_dtype, res=None, scale=1.0):
    m, k = a.shape
    j, _, nb = w.shape
    tm = _pick(m, (512, 256, 128))
    tn = _pick(nb, _TN_PREFS)
    tpb = nb // tn
    has_res = res is not None

    def body(a_ref, w_ref, *rest):
        o_ref = rest[-1]
        acc = jnp.dot(a_ref[...], w_ref[...], preferred_element_type=F32)
        if has_res:
            acc = rest[0][...] + scale * acc
        o_ref[...] = acc.astype(o_ref.dtype)

    in_specs = [pl.BlockSpec((tm, k), lambda n, i: (i, 0)),
                pl.BlockSpec((None, k, tn), lambda n, i: (n // tpb, 0, n % tpb))]
    args = [a, w]
    if has_res:
        in_specs.append(pl.BlockSpec((tm, tn), lambda n, i: (i, n)))
        args.append(res)
    return pl.pallas_call(
        body, name=name, grid=(j * tpb, m // tm), in_specs=in_specs,
        out_specs=pl.BlockSpec((tm, tn), lambda n, i: (i, n)),
        out_shape=jax.ShapeDtypeStruct((m, j * nb), out_dtype), compiler_params=_cparams(),
    )(*args)


def _mm_nt(name, dy, w, out_dtype):
    m, n = dy.shape
    j, k, nb = w.shape
    tm = _pick(m, (512, 256, 128))
    to = _pick(k, _TN_PREFS)
    tc = _pick(nb, (1536, 1408, 1024, 512, 256, 128))
    cpb = nb // tc
    n_red = j * cpb

    def body(dy_ref, w_ref, o_ref, acc_ref):
        r = pl.program_id(2)

        @pl.when(r == 0)
        def _():
            acc_ref[...] = jnp.zeros(acc_ref.shape, F32)

        acc_ref[...] += lax.dot_general(dy_ref[...], w_ref[...], (((1,), (1,)), ((), ())),
                                        preferred_element_type=F32)

        @pl.when(r == n_red - 1)
        def _():
            o_ref[...] = acc_ref[...].astype(o_ref.dtype)

    return pl.pallas_call(
        body, name=name, grid=(k // to, m // tm, n_red),
        in_specs=[pl.BlockSpec((tm, tc), lambda ko, i, r: (i, r)),
                  pl.BlockSpec((None, to, tc), lambda ko, i, r: (r // cpb, ko, r % cpb))],
        out_specs=pl.BlockSpec((tm, to), lambda ko, i, r: (i, ko)),
        out_shape=jax.ShapeDtypeStruct((m, k), out_dtype),
        scratch_shapes=[pltpu.VMEM((tm, to), F32)], compiler_params=_cparams(),
    )(dy, w)


def _mm_tn(name, a, dy, j):
    m, k = a.shape
    n = dy.shape[1]
    nb = n // j
    tk = _pick(k, (512, 256, 128))
    tn = _pick(nb, _TN_PREFS)
    tpb = nb // tn

    def body(a_ref, dy_ref, o_ref):
        o_ref[...] = lax.dot_general(a_ref[...], dy_ref[...], (((0,), (0,)), ((), ())),
                                     preferred_element_type=F32).astype(o_ref.dtype)

    return pl.pallas_call(
        body, name=name, grid=(k // tk, j * tpb),
        in_specs=[pl.BlockSpec((m, tk), lambda kb, nn: (0, kb)),
                  pl.BlockSpec((m, tn), lambda kb, nn: (0, nn))],
        out_specs=pl.BlockSpec((None, tk, tn), lambda kb, nn: (nn // tpb, kb, nn % tpb)),
        out_shape=jax.ShapeDtypeStruct((j, k, nb), BF16), compiler_params=_cparams(),
    )(a, dy)


def _cumsum_rows(x_ref, o_ref, blk, reverse):
    s = x_ref.shape[0]
    nblk = s // blk
    ri = lax.broadcasted_iota(jnp.int32, (blk, blk), 0)
    ci = lax.broadcasted_iota(jnp.int32, (blk, blk), 1)
    tri = jnp.where((ci >= ri) if reverse else (ci <= ri), 1.0, 0.0).astype(F32)
    carry = jnp.zeros((1, x_ref.shape[1]), F32)
    order = range(nblk - 1, -1, -1) if reverse else range(nblk)
    for b in order:
        xb = x_ref[b * blk:(b + 1) * blk, :]
        o_ref[b * blk:(b + 1) * blk, :] = jnp.dot(tri, xb, preferred_element_type=F32,
                                                  precision=lax.Precision.HIGHEST) + carry
        carry = carry + jnp.sum(xb, axis=0, keepdims=True)


def _fgate_fwd(name, zf, bf_row):
    s, w = zf.shape
    blk = _pick(s, (256, 128))

    def body(z_ref, b_ref, c_ref, ls_ref):
        v = z_ref[...] + b_ref[...]
        ls_ref[...] = jnp.minimum(v, 0.0) - jnp.log(1.0 + jnp.exp(-jnp.abs(v)))
        _cumsum_rows(ls_ref, c_ref, blk, reverse=False)

    return pl.pallas_call(
        body, name=name, out_shape=jax.ShapeDtypeStruct((s, w), F32),
        scratch_shapes=[pltpu.VMEM((s, w), F32)], compiler_params=_cparams(),
    )(zf, bf_row)


def _fgate_bwd(name, dc_q, dc_k, zf, bf_row):
    s, w = zf.shape
    blk = _pick(s, (256, 128))

    def body(dcq_ref, dck_ref, z_ref, b_ref, dz_ref, dzb_ref, db_ref, dls_ref, dc_ref):
        dc_ref[...] = dcq_ref[...] + dck_ref[...]
        _cumsum_rows(dc_ref, dls_ref, blk, reverse=True)
        dz = dls_ref[...] * _sig(-(z_ref[...] + b_ref[...]))
        dz_ref[...] = dz
        dzb_ref[...] = dz.astype(BF16)
        db_ref[...] = jnp.sum(dz, axis=0, keepdims=True)

    return pl.pallas_call(
        body, name=name,
        out_shape=[jax.ShapeDtypeStruct((s, w), F32), jax.ShapeDtypeStruct((s, w), BF16),
                   jax.ShapeDtypeStruct((1, w), F32)],
        scratch_shapes=[pltpu.VMEM((s, w), F32), pltpu.VMEM((s, w), F32)], compiler_params=_cparams(),
    )(dc_q, dc_k, zf, bf_row)


def _scores(q, k, cq_col, ck_row, scale, row0, col0):
    s = lax.dot_general(q, k, (((1,), (1,)), ((), ())), preferred_element_type=F32) * scale
    s = s + (cq_col - ck_row)
    rows = row0 + lax.broadcasted_iota(jnp.int32, s.shape, 0)
    cols = col0 + lax.broadcasted_iota(jnp.int32, s.shape, 1)
    return jnp.where(cols <= rows, s, -jnp.inf)


def _attn_fwd(name, q, k, v, c_col, c_row, tb):
    h, s, hd = q.shape
    nb = s // tb
    scale = 1.0 / float(hd) ** 0.5

    def body(q_ref, k_ref, v_ref, cq_ref, ck_ref, o_ref, lse_ref):
        i = pl.program_id(1)
        qv = q_ref[...]
        cq = cq_ref[...]

        def step(j, carry):
            m_i, l_i, acc = carry
            k0 = pl.multiple_of(j * tb, tb)
            sc = _scores(qv, k_ref[pl.ds(k0, tb), :], cq, ck_ref[j], scale, i * tb, j * tb)
            m_new = jnp.maximum(m_i, jnp.max(sc, axis=-1, keepdims=True))
            alpha = jnp.exp(m_i - m_new)
            p = jnp.exp(sc - m_new)
            l_new = alpha * l_i + jnp.sum(p, axis=-1, keepdims=True)
            acc = alpha * acc + jnp.dot(p.astype(BF16), v_ref[pl.ds(k0, tb), :], preferred_element_type=F32)
            return m_new, l_new, acc

        init = (jnp.full((tb, 1), -jnp.inf, F32), jnp.zeros((tb, 1), F32), jnp.zeros((tb, hd), F32))
        m_i, l_i, acc = lax.fori_loop(0, i + 1, step, init)
        o_ref[...] = (acc / l_i).astype(o_ref.dtype)
        lse_ref[...] = m_i + jnp.log(l_i)

    return pl.pallas_call(
        body, name=name, grid=(h, nb),
        in_specs=[pl.BlockSpec((None, tb, hd), lambda hh, i: (hh, i, 0)),
                  pl.BlockSpec((None, s, hd), lambda hh, i: (hh, 0, 0)),
                  pl.BlockSpec((None, s, hd), lambda hh, i: (hh, 0, 0)),
                  pl.BlockSpec((None, tb, 1), lambda hh, i: (hh, i, 0)),
                  pl.BlockSpec((None, nb, 1, tb), lambda hh, i: (hh, 0, 0, 0))],
        out_specs=[pl.BlockSpec((None, tb, hd), lambda hh, i: (hh, i, 0)),
                   pl.BlockSpec((None, tb, 1), lambda hh, i: (hh, i, 0))],
        out_shape=[jax.ShapeDtypeStruct((h, s, hd), F32), jax.ShapeDtypeStruct((h, s, 1), F32)],
        compiler_params=_cparams(),
    )(q, k, v, c_col, c_row)


def _attn_bwd_q(name, q, k, v, o, do, lse, c_col, c_row, tb):
    h, s, hd = q.shape
    nb = s // tb
    scale = 1.0 / float(hd) ** 0.5

    def body(q_ref, k_ref, v_ref, o_ref, do_ref, lse_ref, cq_ref, ck_ref, dq_ref, dl_ref, dcq_ref):
        i = pl.program_id(1)
        qv = q_ref[...]
        dov = do_ref[...]
        cq = cq_ref[...]
        lse_v = lse_ref[...]
        delta = jnp.sum(dov.astype(F32) * o_ref[...], axis=-1, keepdims=True)

        def step(j, carry):
            dq, dcq = carry
            k0 = pl.multiple_of(j * tb, tb)
            kj = k_ref[pl.ds(k0, tb), :]
            p = jnp.exp(_scores(qv, kj, cq, ck_ref[j], scale, i * tb, j * tb) - lse_v)
            dp = lax.dot_general(dov, v_ref[pl.ds(k0, tb), :], (((1,), (1,)), ((), ())), preferred_element_type=F32)
            ds = p * (dp - delta)
            return (dq + jnp.dot(ds.astype(BF16), kj, preferred_element_type=F32),
                    dcq + jnp.sum(ds, axis=-1, keepdims=True))

        dq, dcq = lax.fori_loop(0, i + 1, step, (jnp.zeros((tb, hd), F32), jnp.zeros((tb, 1), F32)))
        dq_ref[...] = (dq * scale).astype(dq_ref.dtype)
        dl_ref[...] = delta
        dcq_ref[...] = dcq

    blk = pl.BlockSpec((None, tb, hd), lambda hh, i: (hh, i, 0))
    full = pl.BlockSpec((None, s, hd), lambda hh, i: (hh, 0, 0))
    col = pl.BlockSpec((None, tb, 1), lambda hh, i: (hh, i, 0))
    return pl.pallas_call(
        body, name=name, grid=(h, nb),
        in_specs=[blk, full, full, blk, blk, col, col,
                  pl.BlockSpec((None, nb, 1, tb), lambda hh, i: (hh, 0, 0, 0))],
        out_specs=[blk, col, col],
        out_shape=[jax.ShapeDtypeStruct((h, s, hd), BF16), jax.ShapeDtypeStruct((h, s, 1), F32),
                   jax.ShapeDtypeStruct((h, s, 1), F32)],
        compiler_params=_cparams(),
    )(q, k, v, o, do, lse, c_col, c_row)


def _attn_bwd_kv(name, q, k, v, do, lse_row, delta_row, c_col, c_row, tb):
    h, s, hd = q.shape
    nb = s // tb
    scale = 1.0 / float(hd) ** 0.5

    def body(q_ref, k_ref, v_ref, do_ref, lse_ref, dl_ref, ck_ref, cq_ref, dk_ref, dv_ref, dc_ref):
        j = pl.program_id(1)
        kv = k_ref[...]
        vv = v_ref[...]
        ck = ck_ref[...]

        def step(i, carry):
            dk, dv, dc = carry
            q0 = pl.multiple_of(i * tb, tb)
            qi = q_ref[pl.ds(q0, tb), :]
            doi = do_ref[pl.ds(q0, tb), :]
            st = lax.dot_general(kv, qi, (((1,), (1,)), ((), ())), preferred_element_type=F32) * scale
            st = st + (cq_ref[i] - ck)
            krow = j * tb + lax.broadcasted_iota(jnp.int32, st.shape, 0)
            qcol = i * tb + lax.broadcasted_iota(jnp.int32, st.shape, 1)
            pt = jnp.exp(jnp.where(krow <= qcol, st, -jnp.inf) - lse_ref[i])
            dv = dv + jnp.dot(pt.astype(BF16), doi, preferred_element_type=F32)
            dpt = lax.dot_general(vv, doi, (((1,), (1,)), ((), ())), preferred_element_type=F32)
            dst = pt * (dpt - dl_ref[i])
            dk = dk + jnp.dot(dst.astype(BF16), qi, preferred_element_type=F32)
            dc = dc - jnp.sum(dst, axis=-1, keepdims=True)
            return dk, dv, dc

        init = (jnp.zeros((tb, hd), F32), jnp.zeros((tb, hd), F32), jnp.zeros((tb, 1), F32))
        dk, dv, dc = lax.fori_loop(j, nb, step, init)
        dk_ref[...] = (dk * scale).astype(dk_ref.dtype)
        dv_ref[...] = dv.astype(dv_ref.dtype)
        dc_ref[...] = dc

    blk = pl.BlockSpec((None, tb, hd), lambda hh, jj: (hh, jj, 0))
    full = pl.BlockSpec((None, s, hd), lambda hh, jj: (hh, 0, 0))
    col = pl.BlockSpec((None, tb, 1), lambda hh, jj: (hh, jj, 0))
    rows = pl.BlockSpec((None, nb, 1, tb), lambda hh, jj: (hh, 0, 0, 0))
    return pl.pallas_call(
        body, name=name, grid=(h, nb),
        in_specs=[full, blk, blk, full, rows, rows, col, rows],
        out_specs=[blk, blk, col],
        out_shape=[jax.ShapeDtypeStruct((h, s, hd), BF16), jax.ShapeDtypeStruct((h, s, hd), BF16),
                   jax.ShapeDtypeStruct((h, s, 1), F32)],
        compiler_params=_cparams(),
    )(q, k, v, do, lse_row, delta_row, c_col, c_row)


def _glu(cv, cc):
    c1 = cv[:, :cc].astype(F32)
    c2 = cv[:, cc:].astype(F32)
    return c1 * _sig(c2)


def _conv_fwd(name, zm, w_pad, b_row, g_row, cc, taps):
    s = zm.shape[0]
    tr = _pick(s, (256, 128))
    hpb = tr // HALO
    off = HALO - (taps - 1)

    def body(cur_ref, halo_ref, w_ref, b_ref, g_ref, y_ref, cs_ref, apad):
        i = pl.program_id(0)
        apad[0:HALO, :] = _glu(halo_ref[...], cc) * jnp.where(i > 0, 1.0, 0.0)
        apad[HALO:, :] = _glu(cur_ref[...], cc)
        acc = jnp.zeros((tr, cc), F32) + b_ref[...]
        for t in range(taps):
            acc = acc + w_ref[t:t + 1, :] * apad[off + t:off + t + tr, :]
        y_ref[...] = acc
        r = lax.rsqrt(jnp.mean(acc * acc, axis=-1, keepdims=True) + RMS_EPS)
        n = acc * r * g_ref[...]
        cs_ref[...] = (n * _sig(n)).astype(cs_ref.dtype)

    return pl.pallas_call(
        body, name=name, grid=(s // tr,),
        in_specs=[pl.BlockSpec((tr, 2 * cc), lambda i: (i, 0)),
                  pl.BlockSpec((HALO, 2 * cc), lambda i: (jnp.maximum(i * hpb - 1, 0), 0)),
                  pl.BlockSpec(w_pad.shape, lambda i: (0, 0)),
                  pl.BlockSpec(b_row.shape, lambda i: (0, 0)),
                  pl.BlockSpec(g_row.shape, lambda i: (0, 0))],
        out_specs=[pl.BlockSpec((tr, cc), lambda i: (i, 0)), pl.BlockSpec((tr, cc), lambda i: (i, 0))],
        out_shape=[jax.ShapeDtypeStruct((s, cc), F32), jax.ShapeDtypeStruct((s, cc), BF16)],
        scratch_shapes=[pltpu.VMEM((HALO + tr, cc), F32)], compiler_params=_cparams(),
    )(zm, zm, w_pad, b_row, g_row)


def _conv_bwd(name, zm, y, dcs, w_pad, g_row, cc, taps):
    s = zm.shape[0]
    tr = _pick(s, (256, 128))
    hpb = tr // HALO
    nblk = s // tr
    off = HALO - (taps - 1)

    def dy_of(yv, dcsv, g):
        r = lax.rsqrt(jnp.mean(yv * yv, axis=-1, keepdims=True) + RMS_EPS)
        xh = yv * r
        n = xh * g
        sg = _sig(n)
        dn = dcsv.astype(F32) * (sg * (1.0 + n * (1.0 - sg)))
        dxh = dn * g
        dy = r * (dxh - xh * jnp.mean(dxh * xh, axis=-1, keepdims=True))
        return dy, dn * xh

    def body(cur_ref, halo_ref, y_ref, yn_ref, dcs_ref, dcsn_ref, w_ref, g_ref,
             dcv_ref, dw_ref, db_ref, dg_ref, apad, dypad):
        i = pl.program_id(0)

        @pl.when(i == 0)
        def _():
            dw_ref[...] = jnp.zeros(dw_ref.shape, F32)
            db_ref[...] = jnp.zeros(db_ref.shape, F32)
            dg_ref[...] = jnp.zeros(dg_ref.shape, F32)

        g = g_ref[...]
        apad[0:HALO, :] = _glu(halo_ref[...], cc) * jnp.where(i > 0, 1.0, 0.0)
        apad[HALO:, :] = _glu(cur_ref[...], cc)
        dy, dgt = dy_of(y_ref[...], dcs_ref[...], g)
        dyn, _ = dy_of(yn_ref[...], dcsn_ref[...], g)
        dypad[0:tr, :] = dy
        dypad[tr:, :] = dyn * jnp.where(i < nblk - 1, 1.0, 0.0)
        db_ref[...] += jnp.sum(dy, axis=0, keepdims=True)
        dg_ref[...] += jnp.sum(dgt, axis=0, keepdims=True)
        da = jnp.zeros((tr, cc), F32)
        for t in range(taps):
            da = da + w_ref[t:t + 1, :] * dypad[taps - 1 - t:taps - 1 - t + tr, :]
            dw_ref[t:t + 1, :] += jnp.sum(dy * apad[off + t:off + t + tr, :], axis=0, keepdims=True)
        cv = cur_ref[...]
        c1 = cv[:, :cc].astype(F32)
        sg = _sig(cv[:, cc:].astype(F32))
        dcv_ref[:, :cc] = (da * sg).astype(dcv_ref.dtype)
        dcv_ref[:, cc:] = (da * c1 * sg * (1.0 - sg)).astype(dcv_ref.dtype)

    nxt = lambda i: (jnp.minimum((i + 1) * hpb, s // HALO - 1), 0)
    return pl.pallas_call(
        body, name=name, grid=(nblk,),
        in_specs=[pl.BlockSpec((tr, 2 * cc), lambda i: (i, 0)),
                  pl.BlockSpec((HALO, 2 * cc), lambda i: (jnp.maximum(i * hpb - 1, 0), 0)),
                  pl.BlockSpec((tr, cc), lambda i: (i, 0)), pl.BlockSpec((HALO, cc), nxt),
                  pl.BlockSpec((tr, cc), lambda i: (i, 0)), pl.BlockSpec((HALO, cc), nxt),
                  pl.BlockSpec(w_pad.shape, lambda i: (0, 0)), pl.BlockSpec(g_row.shape, lambda i: (0, 0))],
        out_specs=[pl.BlockSpec((tr, 2 * cc), lambda i: (i, 0)),
                   pl.BlockSpec(w_pad.shape, lambda i: (0, 0)),
                   pl.BlockSpec((1, cc), lambda i: (0, 0)), pl.BlockSpec((1, cc), lambda i: (0, 0))],
        out_shape=[jax.ShapeDtypeStruct((s, 2 * cc), BF16), jax.ShapeDtypeStruct(w_pad.shape, F32),
                   jax.ShapeDtypeStruct((1, cc), F32), jax.ShapeDtypeStruct((1, cc), F32)],
        scratch_shapes=[pltpu.VMEM((HALO + tr, cc), F32), pltpu.VMEM((tr + HALO, cc), F32)],
        compiler_params=_cparams(),
    )(zm, zm, y, y, dcs, dcs, w_pad, g_row)


def _place():
    x, y, c = lax.axis_index("x"), lax.axis_index("y"), lax.axis_index("c")
    chips = [(1 - x, y), (x, 1 - y), (1 - x, 1 - y)]
    return x, y, c, chips


def _half(c, rows):
    rh = rows // 2
    return pl.ds(pl.multiple_of(c * rh, 16), rh)


def _remote(src, dst, send, recv, dev):
    return pltpu.make_async_remote_copy(src_ref=src, dst_ref=dst, send_sem=send, recv_sem=recv,
                                        device_id=dev, device_id_type=MESH)


def _cast_slots(name, w, chip_arr):
    nl, r, cdim = w.shape
    tr = _pick(r, (256, 128, 64, 32, 16))
    n = r // tr

    def body(q_ref, w_ref, *o_refs):
        layer = pl.program_id(0)
        for j, o_ref in enumerate(o_refs):
            @pl.when(layer == j)
            def _(o_ref=o_ref):
                o_ref[...] = w_ref[...].astype(o_ref.dtype)

    def out_map(j):
        return lambda l, i, q: (q[0], jnp.where(l < j, 0, jnp.where(l == j, i, n - 1)), 0)

    return pl.pallas_call(
        body, name=name,
        grid_spec=pltpu.PrefetchScalarGridSpec(
            num_scalar_prefetch=1, grid=(nl, n),
            in_specs=[pl.BlockSpec((None, tr, cdim), lambda l, i, q: (l, i, 0))],
            out_specs=[pl.BlockSpec((None, tr, cdim), out_map(j)) for j in range(nl)]),
        out_shape=[jax.ShapeDtypeStruct((N_CHIPS, r, cdim), BF16)] * nl, compiler_params=_cparams(),
    )(chip_arr, w)


def _split_start(name, n_sems, bufs, issue):
    nb = len(bufs)

    def body(*refs):
        issue(refs[:nb], refs[nb], refs[nb + 1])
        refs[-1][...] = jnp.zeros(refs[-1].shape, F32)

    outs = pl.pallas_call(
        body, name=name, in_specs=[HBM_SPEC] * nb,
        out_shape=(pltpu.SemaphoreType.DMA(n_sems), pltpu.SemaphoreType.DMA(n_sems),
                   *[pltpu.HBM(b.shape, b.dtype) for b in bufs], jax.ShapeDtypeStruct((8, LANES), F32)),
        out_specs=(SEM_SPEC, SEM_SPEC, *[HBM_SPEC] * nb, pl.BlockSpec(memory_space=pltpu.VMEM)),
        input_output_aliases={t: t + 2 for t in range(nb)},
        compiler_params=pltpu.CompilerParams(has_side_effects=EFFECT),
    )(*[pltpu.with_memory_space_constraint(b, pltpu.HBM) for b in bufs])
    return outs[0], outs[1], list(outs[2:2 + nb]), outs[-1]


def _split_wait(name, bufs, send, recv, after, drain):
    nb = len(bufs)

    def body(*refs):
        drain(refs[:nb], refs[nb], refs[nb + 1])

    return list(pl.pallas_call(
        body, name=name, in_specs=[HBM_SPEC] * nb + [SEM_SPEC, SEM_SPEC, ANY],
        out_shape=tuple(pltpu.HBM(b.shape, b.dtype) for b in bufs), out_specs=tuple([HBM_SPEC] * nb),
        input_output_aliases={t: t for t in range(nb)},
        compiler_params=pltpu.CompilerParams(has_side_effects=EFFECT),
    )(*bufs, send, recv, after))


def _gather_start(name, bufs):
    nt = len(bufs)

    def issue(g, send, recv):
        x, y, c, chips = _place()
        me = 2 * x + y
        for t in range(nt):
            part = g[t].at[me, _half(c, bufs[t].shape[1]), :]
            for j, (qx, qy) in enumerate(chips):
                _remote(part, part, send.at[3 * t + j], recv.at[3 * t + j], (qx, qy, c)).start()

    return _split_start(name, (3 * nt,), bufs, issue)


def _gather_wait(name, bufs, send, recv, after):
    nt = len(bufs)

    def drain(g, send, recv):
        x, y, c, _ = _place()
        for t in range(nt):
            part = g[t].at[0, pl.ds(0, bufs[t].shape[1] // 2), :]
            for j in range(3):
                cp = _remote(part, part, send.at[3 * t + j], recv.at[3 * t + j], (x, y, c))
                cp.wait_send()
                cp.wait_recv()

    return _split_wait(name, bufs, send, recv, after, drain)


def _gather_forward(name, bufs):
    nt = len(bufs)

    def body(*refs):
        g = refs[nt:2 * nt]
        send, recv = refs[2 * nt:]
        x, y, c, chips = _place()
        cps = []
        for t in range(nt):
            for j, (qx, qy) in enumerate(chips):
                part = g[t].at[2 * qx + qy, _half(c, bufs[t].shape[1]), :]
                cps.append(_remote(part, part, send.at[t, j], recv.at[t, j], (x, y, 1 - c)))
        for cp in cps:
            cp.start()
        for t in range(nt):
            for j, (qx, qy) in enumerate(chips):
                theirs = g[t].at[2 * qx + qy, _half(1 - c, bufs[t].shape[1]), :]
                _remote(theirs, theirs, send.at[t, j], recv.at[t, j], (x, y, 1 - c)).wait_recv()
        for cp in cps:
            cp.wait_send()

    return list(pl.pallas_call(
        body, name=name, in_specs=[ANY] * nt, out_specs=[ANY] * nt,
        out_shape=[jax.ShapeDtypeStruct(b.shape, b.dtype) for b in bufs],
        input_output_aliases={t: t for t in range(nt)},
        scratch_shapes=[pltpu.SemaphoreType.DMA((nt, 3)), pltpu.SemaphoreType.DMA((nt, 3))],
        compiler_params=pltpu.CompilerParams(has_side_effects=True),
    )(*bufs))


def _swap_halves(name, grads):
    nt = len(grads)

    def body(*refs):
        g_refs, r_refs = refs[:nt], refs[nt:2 * nt]
        send, recv = refs[2 * nt:]
        x, y, c, _ = _place()
        cps = [_remote(g_refs[t].at[:, _half(1 - c, grads[t].shape[1]), :], r_refs[t], send.at[t], recv.at[t],
                       (x, y, 1 - c)) for t in range(nt)]
        for cp in cps:
            cp.start()
        for cp in cps:
            cp.wait()

    return pl.pallas_call(
        body, name=name, in_specs=[ANY] * nt, out_specs=[ANY] * nt,
        out_shape=[jax.ShapeDtypeStruct((N_CHIPS, g.shape[1] // 2, g.shape[2]), g.dtype) for g in grads],
        scratch_shapes=[pltpu.SemaphoreType.DMA((nt,)), pltpu.SemaphoreType.DMA((nt,))],
        compiler_params=pltpu.CompilerParams(has_side_effects=True),
    )(*grads)


def _scatter_start(name, parts):
    nt = len(parts)
    lands = [lax.empty((3,) + p.shape[1:], p.dtype) for p in parts]

    def issue(refs, send, recv):
        x, y, c, chips = _place()
        for t in range(nt):
            for j, (qx, qy) in enumerate(chips):
                _remote(refs[t].at[2 * qx + qy], refs[nt + t].at[j], send.at[3 * t + j], recv.at[3 * t + j],
                        (qx, qy, c)).start()

    return _split_start(name, (3 * nt,), list(parts) + lands, issue)


def _scatter_wait(name, bufs, send, recv, after):
    nt = len(bufs) // 2

    def drain(refs, send, recv):
        x, y, c, _ = _place()
        for t in range(nt):
            for j in range(3):
                cp = _remote(refs[t].at[0], refs[nt + t].at[j], send.at[3 * t + j], recv.at[3 * t + j], (x, y, c))
                cp.wait_send()
                cp.wait_recv()

    return _split_wait(name, bufs, send, recv, after, drain)


def _join_halves(name, fulls):
    nt = len(fulls)

    def body(*refs):
        o_refs = refs[nt:2 * nt]
        send, recv = refs[2 * nt:]
        x, y, c, _ = _place()
        cps = []
        for t in range(nt):
            half = o_refs[t].at[_half(c, fulls[t].shape[0]), :]
            cps.append(_remote(half, half, send.at[t], recv.at[t], (x, y, 1 - c)))
        for cp in cps:
            cp.start()
        for t in range(nt):
            theirs = o_refs[t].at[_half(1 - c, fulls[t].shape[0]), :]
            _remote(theirs, theirs, send.at[t], recv.at[t], (x, y, 1 - c)).wait_recv()
        for cp in cps:
            cp.wait_send()

    return list(pl.pallas_call(
        body, name=name, in_specs=[ANY] * nt, out_specs=[ANY] * nt,
        out_shape=[jax.ShapeDtypeStruct(a.shape, a.dtype) for a in fulls],
        input_output_aliases={t: t for t in range(nt)},
        scratch_shapes=[pltpu.SemaphoreType.DMA((nt,)), pltpu.SemaphoreType.DMA((nt,))],
        compiler_params=pltpu.CompilerParams(has_side_effects=True),
    )(*fulls))


def _add_own_half(name, grad, recv_half, c_arr):
    _, r, cdim = grad.shape
    rh = r // 2
    tr = _pick(rh, (256, 128, 64, 32, 16))
    nrb = rh // tr

    def body(c_ref, g_ref, r_ref, o_ref):
        o_ref[...] = (g_ref[...].astype(F32) + r_ref[...].astype(F32)).astype(o_ref.dtype)

    return pl.pallas_call(
        body, name=name,
        grid_spec=pltpu.PrefetchScalarGridSpec(
            num_scalar_prefetch=1, grid=(N_CHIPS, nrb),
            in_specs=[pl.BlockSpec((None, tr, cdim), lambda q, i, cr: (q, cr[0] * nrb + i, 0)),
                      pl.BlockSpec((None, tr, cdim), lambda q, i, cr: (q, i, 0))],
            out_specs=pl.BlockSpec((None, tr, cdim), lambda q, i, cr: (q, i, 0))),
        out_shape=jax.ShapeDtypeStruct((N_CHIPS, rh, cdim), BF16), compiler_params=_cparams(),
    )(c_arr, grad, recv_half)


def _add_chips(name, part, came, place_arr):
    _, rh, cdim = part.shape
    tr = _pick(rh, (256, 128, 64, 32, 16))
    nrb = rh // tr

    def body(q_ref, p_ref, r_ref, o_ref):
        acc = p_ref[...].astype(F32)
        for j in range(3):
            acc = acc + r_ref[j].astype(F32)
        o_ref[...] = acc

    return pl.pallas_call(
        body, name=name,
        grid_spec=pltpu.PrefetchScalarGridSpec(
            num_scalar_prefetch=1, grid=(nrb,),
            in_specs=[pl.BlockSpec((None, tr, cdim), lambda i, qr: (qr[0], i, 0)),
                      pl.BlockSpec((3, tr, cdim), lambda i, qr: (0, i, 0))],
            out_specs=pl.BlockSpec((tr, cdim), lambda i, qr: (qr[1] * nrb + i, 0))),
        out_shape=jax.ShapeDtypeStruct((2 * rh, cdim), F32), compiler_params=_cparams(),
    )(place_arr, part, came)


def _reduce_scatter_begin(tag, grads, c_arr):
    nt = len(grads)
    got = _swap_halves(f"rs_swap_{tag}", grads)
    parts = [_add_own_half(f"rs_add2_{tag}_{t}", grads[t], got[t], c_arr) for t in range(nt)]
    return _scatter_start(f"rs_scatter_start_{tag}", parts)


def _reduce_scatter_end(tag, state, after, place_arr):
    send, recv, bufs, _ = state
    nt = len(bufs) // 2
    bufs = _scatter_wait(f"rs_scatter_wait_{tag}", bufs, send, recv, after)
    fulls = [_add_chips(f"rs_add4_{tag}_{t}", bufs[t], bufs[nt + t], place_arr) for t in range(nt)]
    return _join_halves(f"rs_join_{tag}", fulls)


def _allgather_blocks(name, blk):
    m_per, n = blk.shape

    def body(x_ref, out_ref, send_sems, recv_sems, local_sem):
        x, y, c, chips = _place()
        me, sibling = (x, y, c), (x, y, 1 - c)

        def rows(px, py, pc):
            return out_ref.at[pl.ds(pl.multiple_of((4 * px + 2 * py + pc) * m_per, 8), m_per), :]

        def copy(k, block, to, src=None):
            return _remote(rows(*block) if src is None else src, rows(*block), send_sems.at[k], recv_sems.at[k], to)

        mine = pltpu.make_async_copy(x_ref, rows(*me), local_sem)
        mine.start()
        first = [copy(0, me, sibling, src=x_ref)]
        first += [copy(1 + j, me, (*chip, c), src=x_ref) for j, chip in enumerate(chips)]
        for cp in first:
            cp.start()
        passed = [copy(4 + j, (*chip, c), sibling) for j, chip in enumerate(chips)]
        for j, chip in enumerate(chips):
            copy(1 + j, (*chip, c), me).wait_recv()
            passed[j].start()
        copy(0, sibling, me).wait_recv()
        for j, chip in enumerate(chips):
            copy(4 + j, (*chip, 1 - c), me).wait_recv()
        for cp in first + passed:
            cp.wait_send()
        mine.wait()

    return pl.pallas_call(
        body, name=name, out_shape=jax.ShapeDtypeStruct((N_DEV * m_per, n), blk.dtype),
        in_specs=[pl.BlockSpec(memory_space=pltpu.VMEM)], out_specs=pl.BlockSpec(memory_space=pltpu.VMEM),
        scratch_shapes=[pltpu.SemaphoreType.DMA((7,)), pltpu.SemaphoreType.DMA((7,)), pltpu.SemaphoreType.DMA],
        compiler_params=_cparams(),
    )(blk)


def _sum_blocks(name, stacked):
    nd, m, n = stacked.shape
    tr = _pick(m, (336, 256, 168, 128, 64, 32, 16, 8))

    def body(s_ref, o_ref):
        acc = s_ref[0]
        for d in range(1, nd):
            acc = acc + s_ref[d]
        o_ref[...] = acc

    return pl.pallas_call(
        body, name=name, grid=(m // tr,),
        in_specs=[pl.BlockSpec((nd, tr, n), lambda i: (0, i, 0))],
        out_specs=pl.BlockSpec((tr, n), lambda i: (i, 0)),
        out_shape=jax.ShapeDtypeStruct((m, n), F32), compiler_params=_cparams(),
    )(stacked)


def _adamw_vals(w, g, m, v):
    m2 = ADAM_B1 * m + (1.0 - ADAM_B1) * g
    v2 = ADAM_B2 * v + (1.0 - ADAM_B2) * (g * g)
    m_hat = m2 / (1.0 - ADAM_B1 ** ADAM_STEP)
    v_hat = v2 / (1.0 - ADAM_B2 ** ADAM_STEP)
    delta = -ADAM_LR * (m_hat / (jnp.sqrt(v_hat) + ADAM_EPS) + ADAM_WD * w)
    return delta, m2, v2


def _adamw_layers(name, w, m, v, grads):
    nl, r, cdim = w.shape
    tr = _pick(r, (128, 64, 32, 16, 8))

    def body(w_ref, m_ref, v_ref, *rest):
        g_refs = rest[:nl]
        go_ref, d_ref, mo_ref, vo_ref = rest[nl:]
        layer = pl.program_id(0)
        g = g_refs[0][...]
        for j in range(1, nl):
            g = jnp.where(layer == j, g_refs[j][...], g)
        delta, m2, v2 = _adamw_vals(w_ref[...], g, m_ref[...], v_ref[...])
        go_ref[...] = g
        d_ref[...] = delta
        mo_ref[...] = m2
        vo_ref[...] = v2

    big = pl.BlockSpec((None, tr, cdim), lambda l, i: (l, i, 0))
    g_specs = [pl.BlockSpec((tr, cdim), functools.partial(lambda l, i, j: (jnp.where(l == j, i, 0), 0), j=j))
               for j in range(nl)]
    return pl.pallas_call(
        body, name=name, grid=(nl, r // tr), in_specs=[big, big, big] + g_specs,
        out_specs=[big, big, big, big],
        out_shape=[jax.ShapeDtypeStruct(w.shape, F32)] * 4, compiler_params=_cparams(),
    )(w, m, v, *grads)


def _adamw_small(name, w, g, m, v):
    def body(w_ref, g_ref, m_ref, v_ref, d_ref, mo_ref, vo_ref):
        delta, m2, v2 = _adamw_vals(w_ref[...], g_ref[...], m_ref[...], v_ref[...])
        d_ref[...] = delta
        mo_ref[...] = m2
        vo_ref[...] = v2

    return pl.pallas_call(body, name=name, out_shape=[jax.ShapeDtypeStruct(w.shape, F32)] * 3,
                          compiler_params=_cparams())(w, g, m, v)


def _swiglu_fwd(name, z, f):
    def fn(zz):
        a = zz[:, :f].astype(F32)
        b = zz[:, f:].astype(F32)
        return a * _sig(a) * b
    return _rowwise(name, fn, [z], [(f, BF16)])[0]


def _swiglu_bwd(name, z, ds, f):
    def fn(zz, dd):
        a = zz[:, :f].astype(F32)
        b = zz[:, f:].astype(F32)
        d = dd.astype(F32)
        sg = _sig(a)
        da = d * b * (sg * (1.0 + a * (1.0 - sg)))
        db = d * a * sg
        return jnp.concatenate([da, db], axis=1)
    return _rowwise(name, fn, [z, ds], [(2 * f, BF16)])[0]


def _heads(a, nh):
    s, w = a.shape
    return jnp.transpose(a.reshape(s, nh, w // nh), (1, 0, 2))


def _unheads(a):
    nh, s, hd = a.shape
    return jnp.transpose(a, (1, 0, 2)).reshape(s, nh * hd)


def _as_rows(col, tb):
    nh, s, _ = col.shape
    return col.reshape(nh, s // tb, 1, tb)


class _Dims:
    def __init__(self, d, f, aw, nh, cc, taps, dp, s):
        self.d, self.f, self.aw, self.nh, self.cc, self.taps, self.dp, self.s = d, f, aw, nh, cc, taps, dp, s
        self.o_cv, self.o_ga, self.o_gc = 0, 2 * cc, 2 * cc + d
        self.o_q = 2 * cc + 2 * d
        self.n_main = self.o_q + 3 * aw
        self.tb = _pick(s, (256, 128))
        assert self.o_ga % d == 0 and self.o_q % aw == 0 and nh <= LANES


def _ffn_fwd(tag, h, g_row, w_in, w_out, dm, deps=()):
    n = _rms_fwd(f"{tag}_rms", h, g_row, deps)
    z = _mm_nn(f"{tag}_in", n, w_in, BF16)
    s = _swiglu_fwd(f"{tag}_act", z, dm.f)
    h2 = _mm_nn(f"{tag}_out", s, w_out, F32, res=h, scale=FFN_RES)
    return h2, (h, n, z, s)


def _ffn_bwd(tag, dh, dh_half_b, saved, g_row, w_in, w_out, dm, oscale):
    h, n, z, s = saved
    ds = _mm_nt(f"{tag}_dout", dh_half_b, w_out, BF16)
    dz = _swiglu_bwd(f"{tag}_dact", z, ds, dm.f)
    dw_out = _mm_tn(f"{tag}_wout", s, dh_half_b, 1)
    dw_in = _mm_tn(f"{tag}_win", n, dz, N_CHIPS)
    dn = _mm_nt(f"{tag}_din", dz, w_in, BF16)
    dh0, dh0_b, dg = _rms_bwd_res(f"{tag}_drms", h, g_row, [dn], dh, oscale)
    return dh0, dh0_b, dg, dw_in, dw_out


def _mixer_fwd(tag, h, sm, wt, dm):
    d, cc, aw, nh, tb = dm.d, dm.cc, dm.aw, dm.nh, dm.tb
    u = _rms_fwd(f"{tag}_rms", h, sm["g_mix"])
    zm = _mm_nn(f"{tag}_in", u, wt["w_main"], BF16)
    zf = _mm_nn(f"{tag}_inf", u, wt["w_f"], F32)
    c = _fgate_fwd(f"{tag}_fgate", zf, sm["b_f"])
    q = _heads(zm[:, dm.o_q:dm.o_q + aw], nh)
    k = _heads(zm[:, dm.o_q + aw:dm.o_q + 2 * aw], nh)
    v = _heads(zm[:, dm.o_q + 2 * aw:dm.o_q + 3 * aw], nh)
    c_col = jnp.transpose(c[:, :nh], (1, 0))[:, :, None]
    c_row = _as_rows(c_col, tb)
    o_h, lse = _attn_fwd(f"{tag}_attn", q, k, v, c_col, c_row, tb)
    o = _unheads(o_h).astype(BF16)
    ya =_mm_nn(f"{tag}_aout", o, wt["w_attn_out"], BF16)
    y, cs = _conv_fwd(f"{tag}_conv", zm, sm["conv_w"], sm["conv_b"], sm["g_conv"], cc, dm.taps)
    yc = _mm_nn(f"{tag}_cout", cs, wt["w_conv_out"], BF16)

    def merge(ga, gc, a, b):
        return _sig(ga.astype(F32)) * a.astype(F32) + _sig(gc.astype(F32)) * b.astype(F32)

    mg = _rowwise(f"{tag}_merge", merge, [(zm, dm.o_ga // d, d), (zm, dm.o_gc // d, d), ya, yc], [(d, BF16)])[0]
    h2 = _mm_nn(f"{tag}_out", mg, wt["w_out"], F32, res=h, scale=1.0)
    return h2, (h, u, zm, zf, q, k, v, c_col, c_row, o_h, lse, o, ya, y, cs, yc, mg)


def _mixer_bwd(tag, dh, dh_b, saved, sm, wt, dm):
    d, cc, aw, nh, tb = dm.d, dm.cc, dm.aw, dm.nh, dm.tb
    h, u, zm, zf, q, k, v, c_col, c_row, o_h, lse, o, ya, y, cs, yc, mg = saved
    dmg = _mm_nt(f"{tag}_dout", dh_b, wt["w_out"], BF16)
    dw_out = _mm_tn(f"{tag}_wout", mg, dh_b, 1)

    def unmerge(dd, ga, gc, a, b):
        dd, a, b = dd.astype(F32), a.astype(F32), b.astype(F32)
        sa, sc = _sig(ga.astype(F32)), _sig(gc.astype(F32))
        return dd * sa, dd * sc, dd * a * sa * (1.0 - sa), dd * b * sc * (1.0 - sc)

    dya, dyc, dga, dgc = _rowwise(f"{tag}_dmerge", unmerge,
                                  [dmg, (zm, dm.o_ga // d, d), (zm, dm.o_gc // d, d), ya, yc], [(d, BF16)] * 4)
    dw_a = _mm_tn(f"{tag}_waout", o, dya, N_CHIPS)
    do = _mm_nt(f"{tag}_daout", dya, wt["w_attn_out"], BF16)
    dw_c = _mm_tn(f"{tag}_wcout", cs, dyc, N_CHIPS)
    dcs = _mm_nt(f"{tag}_dcout", dyc, wt["w_conv_out"], BF16)
    dcv, dconv_w, dconv_b, dg_conv = _conv_bwd(f"{tag}_dconv", zm, y, dcs, sm["conv_w"], sm["g_conv"], cc, dm.taps)
    do_h = _heads(do, nh)
    dq_h, delta, dcq_h = _attn_bwd_q(f"{tag}_dattn_q", q, k, v, o_h, do_h, lse, c_col, c_row, tb)
    dk_h, dv_h, dck_h = _attn_bwd_kv(f"{tag}_dattn_kv", q, k, v, do_h, _as_rows(lse, tb), _as_rows(delta, tb),
                                     c_col, c_row, tb)
    dc = jnp.pad(jnp.transpose(dcq_h[:, :, 0], (1, 0)), ((0, 0), (0, zf.shape[1] - nh)))
    dc_k = jnp.pad(jnp.transpose(dck_h[:, :, 0], (1, 0)), ((0, 0), (0, zf.shape[1] - nh)))
    dzf, dzf_b, db_f = _fgate_bwd(f"{tag}_dfgate", dc, dc_k, zf, sm["b_f"])
    dzm = jnp.concatenate([dcv, dga, dgc, _unheads(dq_h), _unheads(dk_h), _unheads(dv_h)], axis=1)
    dw_main = _mm_tn(f"{tag}_win", u, dzm, 1)
    dw_f = _mm_tn(f"{tag}_winf", u, dzf_b, 1)
    du = _mm_nt(f"{tag}_din", dzm, wt["w_main"], BF16)
    du_f = _mm_nt(f"{tag}_dinf", dzf_b, wt["w_f"], BF16)
    dh0, dh0_b, dg_mix = _rms_bwd_res(f"{tag}_drms", h, sm["g_mix"], [du, du_f], dh, FFN_RES)
    small = dict(g_mix=dg_mix, b_f=db_f[:, :nh], conv_w=dconv_w[:dm.taps], conv_b=dconv_b, g_conv=dg_conv)
    return dh0, dh0_b, small, dw_main, dw_f, dw_a, dw_c, dw_out


def _ple_fwd(tag, h, p_b, sm, wt, dm):
    n = _rms_fwd(f"{tag}_rms", h, sm["g_ple"])
    gp = _mm_nn(f"{tag}_gate", n, wt["w_ple_gate"], BF16)
    pp = _mm_nn(f"{tag}_proj", p_b, wt["w_ple_proj"], BF16)

    def fn(hh, a, b):
        return hh + _sig(a.astype(F32)) * b.astype(F32)

    h2 = _rowwise(f"{tag}_mix", fn, [h, gp, pp], [(dm.d, F32)])[0]
    return h2, (h, n, gp, pp)


def _ple_bwd(tag, dh, saved, p_b, sm, wt, dm, deps=()):
    h, n, gp, pp = saved

    def fn(dd, a, b):
        gate = _sig(a.astype(F32))
        b = b.astype(F32)
        return dd * b * gate * (1.0 - gate), dd * gate

    dgp, dpp = _rowwise(f"{tag}_dmix", fn, [dh, gp, pp], [(dm.d, BF16)] * 2, deps=deps)
    dw_proj = _mm_tn(f"{tag}_wproj", p_b, dpp, N_CHIPS)
    dw_gate = _mm_tn(f"{tag}_wgate", n, dgp, 1)
    dn = _mm_nt(f"{tag}_dgate", dgp, wt["w_ple_gate"], BF16)
    dh0, dh0_b, dg = _rms_bwd_res(f"{tag}_drms", h, sm["g_ple"], [dn], dh, FFN_RES)
    return dh0, dh0_b, dg, dw_gate, dw_proj


def _loss_head(name, h, g_row, target):
    d = h.shape[1]

    def fn(x, tg, g):
        r = lax.rsqrt(jnp.mean(x * x, axis=-1, keepdims=True) + RMS_EPS)
        out = x * r * g
        e = out - tg
        per_row = jnp.sum(e * e, axis=-1, keepdims=True) * (0.5 / d)
        loss = jnp.zeros((1, LANES), F32) + jnp.sum(per_row, axis=0, keepdims=True)
        dx, dg = _rms_bwd_vals(x, g, e * (1.0 / d))
        return dx, loss, dg

    return _rowwise(name, fn, [h, target], [(d, F32)], consts=[g_row], reds=[((1, LANES), F32), ((1, d), F32)])


_BIG = ("w_ff1_in", "w_ff1_out", "w_in", "w_attn_out", "w_conv_out", "w_out", "w_ff2_in", "w_ff2_out",
        "w_ple_gate", "w_ple_proj")
_SMALL = ("g_ff1", "g_mix", "b_f", "conv_w", "conv_b", "g_conv", "g_ff2", "g_ple")
_ORDER = ("g_ff1", "w_ff1_in", "w_ff1_out", "g_mix", "w_in", "b_f", "w_attn_out", "conv_w", "conv_b", "g_conv",
          "w_conv_out", "w_out", "g_ff2", "w_ff2_in", "w_ff2_out", "g_ple", "w_ple_gate", "w_ple_proj", "g_final")


def _round_up(n, k):
    return (n + k - 1) // k * k


def _unpack_layer(gathered, dm):
    g = dict(zip(_BIG, gathered))
    d = dm.d
    w_in = jnp.transpose(g["w_in"], (1, 0, 2)).reshape(d, -1)
    o_f = 3 * dm.aw
    o_c = o_f + dm.nh
    w_main = jnp.concatenate([w_in[:, o_c:], w_in[:, :o_f]], axis=1)
    w_f = jnp.pad(w_in[:, o_f:o_c], ((0, 0), (0, LANES - dm.nh)))
    flat = lambda a: a.reshape(1, a.shape[0] * a.shape[1], a.shape[2])
    return dict(w_ff1_in=g["w_ff1_in"], w_ff1_out=flat(g["w_ff1_out"]), w_main=w_main[None], w_f=w_f[None],
                w_attn_out=g["w_attn_out"], w_conv_out=g["w_conv_out"], w_out=flat(g["w_out"]),
                w_ff2_in=g["w_ff2_in"], w_ff2_out=flat(g["w_ff2_out"]), w_ple_gate=flat(g["w_ple_gate"]),
                w_ple_proj=g["w_ple_proj"])


def _pack_grads(gw, dm):
    d = dm.d
    o_f = 3 * dm.aw
    main, wf = gw["w_main"][0], gw["w_f"][0]
    n_rest = dm.n_main - o_f
    w_in = jnp.concatenate([main[:, n_rest:], wf[:, :dm.nh], main[:, :n_rest]], axis=1)
    w_in = jnp.transpose(w_in.reshape(d, N_CHIPS, -1), (1, 0, 2))
    split = lambda a: a.reshape(N_CHIPS, a.shape[1] // N_CHIPS, a.shape[2])
    out = dict(w_ff1_in=gw["w_ff1_in"], w_ff1_out=split(gw["w_ff1_out"]), w_in=w_in, w_attn_out=gw["w_attn_out"],
               w_conv_out=gw["w_conv_out"], w_out=split(gw["w_out"]), w_ff2_in=gw["w_ff2_in"],
               w_ff2_out=split(gw["w_ff2_out"]), w_ple_gate=split(gw["w_ple_gate"]), w_ple_proj=gw["w_ple_proj"])
    return [out[n] for n in _BIG]


def kernel(x, p, g_ff1, w_ff1_in, w_ff1_out, g_mix, w_in, b_f, w_attn_out, conv_w, conv_b, g_conv, w_conv_out, w_out, g_ff2, w_ff2_in, w_ff2_out, g_ple, w_ple_gate, w_ple_proj, g_final, loss_target, m_g_ff1, m_w_ff1_in, m_w_ff1_out, m_g_mix, m_w_in, m_b_f, m_w_attn_out, m_conv_w, m_conv_b, m_g_conv, m_w_conv_out, m_w_out, m_g_ff2, m_w_ff2_in, m_w_ff2_out, m_g_ple, m_w_ple_gate, m_w_ple_proj, m_g_final, v_g_ff1, v_w_ff1_in, v_w_ff1_out, v_g_mix, v_w_in, v_b_f, v_w_attn_out, v_conv_w, v_conv_b, v_g_conv, v_w_conv_out, v_w_out, v_g_ff2, v_w_ff2_in, v_w_ff2_out, v_g_ple, v_w_ple_gate, v_w_ple_proj, v_g_final):
    args = dict(locals())
    wts = {n: args[n] for n in _ORDER}
    mom = {n: args["m_" + n] for n in _ORDER}
    var = {n: args["v_" + n] for n in _ORDER}

    nl = g_ff1.shape[0]
    s, d = x.shape[1], x.shape[2]
    nh = b_f.shape[1]
    taps = conv_w.shape[1]
    cc = conv_b.shape[1]
    dm = _Dims(d=d, f=w_ff1_out.shape[1] * N_CHIPS, aw=w_attn_out.shape[1], nh=nh, cc=cc, taps=taps,
               dp=w_ple_proj.shape[1], s=s)
    assert taps - 1 <= HALO

    xi = lax.axis_index("x")
    yi = lax.axis_index("y")
    ci = lax.axis_index("c")
    c_arr = jnp.reshape(ci, (1,)).astype(jnp.int32)
    chip_arr = jnp.reshape(2 * xi + yi, (1,)).astype(jnp.int32)
    place_arr = jnp.stack([2 * xi + yi, ci]).astype(jnp.int32)

    h = x[0]
    target = loss_target[0]
    p_b = p[:, 0].astype(BF16)

    cw_rows = _round_up(nl * taps, 8)
    cw_blk = jnp.pad(conv_w.reshape(nl * taps, -1), ((0, cw_rows - nl * taps), (0, 0)))
    cw_all = _allgather_blocks("gather_conv_w", cw_blk).reshape(N_CHIPS, 2, cw_rows, -1)[:, 0, :nl * taps]
    conv_w_full = jnp.transpose(cw_all.reshape(N_CHIPS, nl, taps, -1), (1, 2, 0, 3)).reshape(nl, taps, cc)
    taps_pad = _round_up(taps, 8)

    def small_of(i):
        row = lambda a: a[i][None, :]
        return dict(g_ff1=row(g_ff1), g_mix=row(g_mix), g_conv=row(g_conv), conv_b=row(conv_b), g_ff2=row(g_ff2),
                    g_ple=row(g_ple), b_f=jnp.pad(b_f[i][None, :], ((0, 0), (0, LANES - nh))),
                    conv_w=jnp.pad(conv_w_full[i], ((0, taps_pad - taps), (0, 0))))

    slots = [_cast_slots(f"cast_{n}", wts[n], chip_arr) for n in _BIG]
    started = {}

    def start_gather(i):
        if i < nl:
            started[i] = _gather_start(f"gather_start_l{i}", [slots[t][i] for t in range(len(_BIG))])

    start_gather(0)
    start_gather(1)
    layer_w, saved = [], []
    for i in range(nl):
        send, recv, bufs, _ = started.pop(i)
        bufs = _gather_wait(f"gather_wait_l{i}", bufs, send, recv, h)
        wt = _unpack_layer(_gather_forward(f"gather_fwd_l{i}", bufs), dm)
        start_gather(i + 2)
        in_flight = [st[3] for st in started.values()]
        sm = small_of(i)
        h, sv1 = _ffn_fwd(f"l{i}_ff1", h, sm["g_ff1"], wt["w_ff1_in"], wt["w_ff1_out"], dm, deps=in_flight)
        h, sv2 = _mixer_fwd(f"l{i}_mix", h, sm, wt, dm)
        h, sv3 = _ffn_fwd(f"l{i}_ff2", h, sm["g_ff2"], wt["w_ff2_in"], wt["w_ff2_out"], dm)
        h, sv4 = _ple_fwd(f"l{i}_ple", h, p_b[i], sm, wt, dm)
        layer_w.append((wt, sm))
        saved.append((sv1, sv2, sv3, sv4))

    dh, loss_row, dg_final = _loss_head("loss_head", h, g_final[None, :], target)

    big_grads = [None] * nl
    small_grads = [None] * nl
    leaving = None
    for i in range(nl - 1, -1, -1):
        wt, sm = layer_w[i]
        sv1, sv2, sv3, sv4 = saved[i]
        deps = [leaving[3]] if leaving is not None else []
        dh, dh_b, dg_ple, dw_gate, dw_proj = _ple_bwd(f"l{i}_ple", dh, sv4, p_b[i], sm, wt, dm, deps=deps)
        dh, dh_b, dg_ff2, dw_in2, dw_out2 = _ffn_bwd(f"l{i}_ff2", dh, dh_b, sv3, sm["g_ff2"], wt["w_ff2_in"],
                                                     wt["w_ff2_out"], dm, 1.0)
        dh, dh_b, sg, dw_main, dw_f, dw_a, dw_c, dw_o = _mixer_bwd(f"l{i}_mix", dh, dh_b, sv2, sm, wt, dm)
        dh, dh_b, dg_ff1, dw_in1, dw_out1 = _ffn_bwd(f"l{i}_ff1", dh, dh_b, sv1, sm["g_ff1"], wt["w_ff1_in"],
                                                     wt["w_ff1_out"], dm, 1.0)
        gw = dict(w_ff1_in=dw_in1, w_ff1_out=dw_out1, w_main=dw_main, w_f=dw_f, w_attn_out=dw_a, w_conv_out=dw_c,
                  w_out=dw_o, w_ff2_in=dw_in2, w_ff2_out=dw_out2, w_ple_gate=dw_gate, w_ple_proj=dw_proj)
        if leaving is not None:
            big_grads[i + 1] = _reduce_scatter_end(f"l{i + 1}", leaving, dh, place_arr)
        leaving = _reduce_scatter_begin(f"l{i}", _pack_grads(gw, dm), c_arr)
        sg.update(g_ff1=dg_ff1, g_ff2=dg_ff2, g_ple=dg_ple)
        small_grads[i] = sg
    big_grads[0] = _reduce_scatter_end("l0", leaving, dh, place_arr)
    grad_x = dh[None]

    pieces = [small_grads[i][n].reshape(-1) for i in range(nl) for n in _SMALL]
    pieces += [dg_final.reshape(-1), loss_row[0, :1]]
    flat = jnp.concatenate(pieces)
    n_flat = flat.shape[0]
    rows = _round_up(_round_up(n_flat, LANES) // LANES, 8)
    blk = jnp.pad(flat, (0, rows * LANES - n_flat)).reshape(rows, LANES)
    total = _sum_blocks("sum_small", _allgather_blocks("gather_small", blk).reshape(N_DEV, rows, LANES)).reshape(-1)
    small_tot = {n: [] for n in _SMALL}
    pos = 0
    for i in range(nl):
        for n in _SMALL:
            shp = small_grads[i][n].shape
            size = shp[0] * shp[1]
            small_tot[n].append(total[pos:pos + size].reshape(shp))
            pos += size
    g_final_tot = total[pos:pos + d]
    loss = total[pos + d]

    grads, deltas, new_m, new_v = {}, {}, {}, {}
    for t, n in enumerate(_BIG):
        grads[n], deltas[n], new_m[n], new_v[n] = _adamw_layers(
            f"adamw_{n}", wts[n], mom[n], var[n], [big_grads[i][t] for i in range(nl)])
    chip = 2 * xi + yi
    for n in _SMALL:
        g = jnp.concatenate(small_tot[n], axis=0)
        if n == "conv_w":
            cpc = cc // N_CHIPS
            g = lax.dynamic_slice_in_dim(g.reshape(nl * taps, cc), chip * cpc, cpc, axis=1)
            shape2 = (nl * taps, cpc)
        else:
            shape2 = g.shape
        dl, mm, vv = _adamw_small(f"adamw_{n}", wts[n].reshape(shape2), g, mom[n].reshape(shape2),
                                  var[n].reshape(shape2))
        grads[n] = g.reshape(wts[n].shape)
        deltas[n], new_m[n], new_v[n] = (a.reshape(wts[n].shape) for a in (dl, mm, vv))
    g2 = g_final_tot[None, :]
    dl, mm, vv = _adamw_small("adamw_g_final", g_final[None, :], g2, m_g_final[None, :], v_g_final[None, :])
    grads["g_final"] = g_final_tot
    deltas["g_final"], new_m["g_final"], new_v["g_final"] = dl[0], mm[0], vv[0]

    return (loss, grad_x, *[grads[n] for n in _ORDER], *[deltas[n] for n in _ORDER],
            *[new_m[n] for n in _ORDER], *[new_v[n] for n in _ORDER])
```

```python
import functools

import jax
import jax.numpy as jnp
from jax import lax
from jax.experimental import pallas as pl
from jax.experimental.pallas import tpu as pltpu

F32 = jnp.float32
BF16 = jnp.bfloat16
MESH = pl.DeviceIdType.MESH
ANY = pl.BlockSpec(memory_space=pl.ANY)
HBM_SPEC = pl.BlockSpec(memory_space=pltpu.HBM)
SEM_SPEC = pl.BlockSpec(memory_space=pltpu.SEMAPHORE)
EFFECT = pltpu.SideEffectType.DATAFLOW_SIDE_EFFECTING

RMS_EPS = 1e-6
FFN_RES = 0.5
ADAM_LR = 0.001
ADAM_B1 = 0.9
ADAM_B2 = 0.999
ADAM_EPS = 1e-08
ADAM_WD = 0.01
ADAM_STEP = 10

N_CHIPS = 4
N_DEV = 8
LANES = 128
HALO = 32
VMEM_LIMIT = 56 * 1024 * 1024


def _cparams():
    return pltpu.CompilerParams(vmem_limit_bytes=VMEM_LIMIT)


def _pick(n, prefs):
    for p in prefs:
        if p <= n and n % p == 0:
            return p
    return n


def _sig(x):
    return 1.0 / (1.0 + jnp.exp(-x))


def _rowwise(name, fn, ins, outs, consts=(), reds=(), tm=None, deps=()):
    ins = [a if isinstance(a, tuple) else (a, 0, a.shape[1]) for a in ins]
    m = ins[0][0].shape[0]
    tm = tm or _pick(m, (256, 128, 64, 32, 16, 8))
    n_in, n_c, n_o, n_r = len(ins), len(consts), len(outs), len(reds)
    n_d = len(deps)
    consts = list(consts) + list(deps)

    def body(*refs):
        in_refs = refs[:n_in + n_c]
        o_refs = refs[n_in + n_c + n_d:n_in + n_c + n_d + n_o]
        r_refs = refs[n_in + n_c + n_d + n_o:]
        res = fn(*[r[...] for r in in_refs])
        if not isinstance(res, (tuple, list)):
            res = (res,)
        for r, v in zip(o_refs, res[:n_o]):
            r[...] = v.astype(r.dtype)
        if n_r:
            @pl.when(pl.program_id(0) == 0)
            def _():
                for r in r_refs:
                    r[...] = jnp.zeros(r.shape, r.dtype)
            for r, v in zip(r_refs, res[n_o:]):
                r[...] += v.astype(r.dtype)

    in_specs = [pl.BlockSpec((tm, w), functools.partial(lambda i, cb: (i, cb), cb=cb)) for (_, cb, w) in ins]
    in_specs += [pl.BlockSpec(c.shape, lambda i: (0, 0)) for c in consts]
    out_specs = [pl.BlockSpec((tm, w), lambda i: (i, 0)) for (w, _) in outs]
    out_specs += [pl.BlockSpec(s, lambda i: (0, 0)) for (s, _) in reds]
    out_shape = [jax.ShapeDtypeStruct((m, w), d) for (w, d) in outs]
    out_shape += [jax.ShapeDtypeStruct(s, d) for (s, d) in reds]
    res = pl.pallas_call(
        body, name=name, grid=(m // tm,), in_specs=in_specs, out_specs=out_specs, out_shape=out_shape,
        compiler_params=_cparams(),
    )(*[a for (a, _, _) in ins], *consts)
    return res


def _rms_fwd(name, h, g, deps=()):
    def fn(x, gg):
        r = lax.rsqrt(jnp.mean(x * x, axis=-1, keepdims=True) + RMS_EPS)
        return x * r * gg
    return _rowwise(name, fn, [h], [(h.shape[1], BF16)], consts=[g], deps=deps)[0]


def _rms_bwd_vals(x, g, dn):
    r = lax.rsqrt(jnp.mean(x * x, axis=-1, keepdims=True) + RMS_EPS)
    xh = x * r
    dxh = dn * g
    dx = r * (dxh - xh * jnp.mean(dxh * xh, axis=-1, keepdims=True))
    dg = jnp.sum(dn * xh, axis=0, keepdims=True)
    return dx, dg


def _rms_bwd_res(name, h, g, dns, dres, oscale):
    n_dn = len(dns)

    def fn(x, *rest):
        dn = rest[0].astype(F32)
        for t in rest[1:n_dn]:
            dn = dn + t.astype(F32)
        dr, gg = rest[n_dn], rest[n_dn + 1]
        dx, dg = _rms_bwd_vals(x, gg, dn)
        dh = dr + dx
        return dh, oscale * dh, dg

    d = h.shape[1]
    return _rowwise(name, fn, [h, *dns, dres], [(d, F32), (d, BF16)], consts=[g], reds=[((1, d), F32)])


_TN_PREFS = (1408, 1024, 768, 512, 256, 128)


def _mm_nn(name, a, w, out_dtype, res=None, scale=1.0):
    m, k = a.shape
    j, _, nb = w.shape
    tm = _pick(m, (512, 256, 128))
    tn = _pick(nb, _TN_PREFS)
    tpb = nb // tn
    has_res = res is not None

    def body(a_ref, w_ref, *rest):
        o_ref = rest[-1]
        acc = jnp.dot(a_ref[...], w_ref[...], preferred_element_type=F32)
        if has_res:
            acc = rest[0][...] + scale * acc
        o_ref[...] = acc.astype(o_ref.dtype)

    in_specs = [pl.BlockSpec((tm, k), lambda n, i: (i, 0)),
                pl.BlockSpec((None, k, tn), lambda n, i: (n // tpb, 0, n % tpb))]
    args = [a, w]
    if has_res:
        in_specs.append(pl.BlockSpec((tm, tn), lambda n, i: (i, n)))
        args.append(res)
    return pl.pallas_call(
        body, name=name, grid=(j * tpb, m // tm), in_specs=in_specs,
        out_specs=pl.BlockSpec((tm, tn), lambda n, i: (i, n)),
        out_shape=jax.ShapeDtypeStruct((m, j * nb), out_dtype), compiler_params=_cparams(),
    )(*args)


def _mm_nt(name, dy, w, out_dtype):
    m, n = dy.shape
    j, k, nb = w.shape
    tm = _pick(m, (512, 256, 128))
    to = _pick(k, _TN_PREFS)
    tc = _pick(nb, (1536, 1408, 1024, 512, 256, 128))
    cpb = nb // tc
    n_red = j * cpb

    def body(dy_ref, w_ref, o_ref, acc_ref):
        r = pl.program_id(2)

        @pl.when(r == 0)
        def _():
            acc_ref[...] = jnp.zeros(acc_ref.shape, F32)

        acc_ref[...] += lax.dot_general(dy_ref[...], w_ref[...], (((1,), (1,)), ((), ())),
                                        preferred_element_type=F32)

        @pl.when(r == n_red - 1)
        def _():
            o_ref[...] = acc_ref[...].astype(o_ref.dtype)

    return pl.pallas_call(
        body, name=name, grid=(k // to, m // tm, n_red),
        in_specs=[pl.BlockSpec((tm, tc), lambda ko, i, r: (i, r)),
                  pl.BlockSpec((None, to, tc), lambda ko, i, r: (r // cpb, ko, r % cpb))],
        out_specs=pl.BlockSpec((tm, to), lambda ko, i, r: (i, ko)),
        out_shape=jax.ShapeDtypeStruct((m, k), out_dtype),
        scratch_shapes=[pltpu.VMEM((tm, to), F32)], compiler_params=_cparams(),
    )(dy, w)


def _mm_tn(name, a, dy, j):
    m, k = a.shape
    n = dy.shape[1]
    nb = n // j
    tk = _pick(k, (512, 256, 128))
    tn = _pick(nb, _TN_PREFS)
    tpb = nb // tn

    def body(a_ref, dy_ref, o_ref):
        o_ref[...] = lax.dot_general(a_ref[...], dy_ref[...], (((0,), (0,)), ((), ())),
                                     preferred_element_type=F32).astype(o_ref.dtype)

    return pl.pallas_call(
        body, name=name, grid=(k // tk, j * tpb),
        in_specs=[pl.BlockSpec((m, tk), lambda kb, nn: (0, kb)),
                  pl.BlockSpec((m, tn), lambda kb, nn: (0, nn))],
        out_specs=pl.BlockSpec((None, tk, tn), lambda kb, nn: (nn // tpb, kb, nn % tpb)),
        out_shape=jax.ShapeDtypeStruct((j, k, nb), BF16), compiler_params=_cparams(),
    )(a, dy)


def _cumsum_rows(x_ref, o_ref, blk, reverse):
    s = x_ref.shape[0]
    nblk = s // blk
    ri = lax.broadcasted_iota(jnp.int32, (blk, blk), 0)
    ci = lax.broadcasted_iota(jnp.int32, (blk, blk), 1)
    tri = jnp.where((ci >= ri) if reverse else (ci <= ri), 1.0, 0.0).astype(F32)
    carry = jnp.zeros((1, x_ref.shape[1]), F32)
    order = range(nblk - 1, -1, -1) if reverse else range(nblk)
    for b in order:
        xb = x_ref[b * blk:(b + 1) * blk, :]
        o_ref[b * blk:(b + 1) * blk, :] = jnp.dot(tri, xb, preferred_element_type=F32,
                                                  precision=lax.Precision.HIGHEST) + carry
        carry = carry + jnp.sum(xb, axis=0, keepdims=True)


def _fgate_fwd(name, zf, bf_row):
    s, w = zf.shape
    blk = _pick(s, (256, 128))

    def body(z_ref, b_ref, c_ref, ls_ref):
        v = z_ref[...] + b_ref[...]
        ls_ref[...] = jnp.minimum(v, 0.0) - jnp.log(1.0 + jnp.exp(-jnp.abs(v)))
        _cumsum_rows(ls_ref, c_ref, blk, reverse=False)

    return pl.pallas_call(
        body, name=name, out_shape=jax.ShapeDtypeStruct((s, w), F32),
        scratch_shapes=[pltpu.VMEM((s, w), F32)], compiler_params=_cparams(),
    )(zf, bf_row)


def _fgate_bwd(name, dc_q, dc_k, zf, bf_row):
    s, w = zf.shape
    blk = _pick(s, (256, 128))

    def body(dcq_ref, dck_ref, z_ref, b_ref, dz_ref, dzb_ref, db_ref, dls_ref, dc_ref):
        dc_ref[...] = dcq_ref[...] + dck_ref[...]
        _cumsum_rows(dc_ref, dls_ref, blk, reverse=True)
        dz = dls_ref[...] * _sig(-(z_ref[...] + b_ref[...]))
        dz_ref[...] = dz
        dzb_ref[...] = dz.astype(BF16)
        db_ref[...] = jnp.sum(dz, axis=0, keepdims=True)

    return pl.pallas_call(
        body, name=name,
        out_shape=[jax.ShapeDtypeStruct((s, w), F32), jax.ShapeDtypeStruct((s, w), BF16),
                   jax.ShapeDtypeStruct((1, w), F32)],
        scratch_shapes=[pltpu.VMEM((s, w), F32), pltpu.VMEM((s, w), F32)], compiler_params=_cparams(),
    )(dc_q, dc_k, zf, bf_row)


def _scores(q, k, cq_col, ck_row, scale, row0, col0):
    s = lax.dot_general(q, k, (((1,), (1,)), ((), ())), preferred_element_type=F32) * scale
    s = s + (cq_col - ck_row)
    rows = row0 + lax.broadcasted_iota(jnp.int32, s.shape, 0)
    cols = col0 + lax.broadcasted_iota(jnp.int32, s.shape, 1)
    return jnp.where(cols <= rows, s, -jnp.inf)


def _attn_fwd(name, q, k, v, c_col, c_row, tb):
    h, s, hd = q.shape
    nb = s // tb
    scale = 1.0 / float(hd) ** 0.5

    def body(q_ref, k_ref, v_ref, cq_ref, ck_ref, o_ref, lse_ref):
        i = pl.program_id(1)
        qv = q_ref[...]
        cq = cq_ref[...]

        def step(j, carry):
            m_i, l_i, acc = carry
            k0 = pl.multiple_of(j * tb, tb)
            sc = _scores(qv, k_ref[pl.ds(k0, tb), :], cq, ck_ref[j], scale, i * tb, j * tb)
            m_new = jnp.maximum(m_i, jnp.max(sc, axis=-1, keepdims=True))
            alpha = jnp.exp(m_i - m_new)
            p = jnp.exp(sc - m_new)
            l_new = alpha * l_i + jnp.sum(p, axis=-1, keepdims=True)
            acc = alpha * acc + jnp.dot(p.astype(BF16), v_ref[pl.ds(k0, tb), :], preferred_element_type=F32)
            return m_new, l_new, acc

        init = (jnp.full((tb, 1), -jnp.inf, F32), jnp.zeros((tb, 1), F32), jnp.zeros((tb, hd), F32))
        m_i, l_i, acc = lax.fori_loop(0, i + 1, step, init)
        o_ref[...] = (acc / l_i).astype(o_ref.dtype)
        lse_ref[...] = m_i + jnp.log(l_i)

    return pl.pallas_call(
        body, name=name, grid=(h, nb),
        in_specs=[pl.BlockSpec((None, tb, hd), lambda hh, i: (hh, i, 0)),
                  pl.BlockSpec((None, s, hd), lambda hh, i: (hh, 0, 0)),
                  pl.BlockSpec((None, s, hd), lambda hh, i: (hh, 0, 0)),
                  pl.BlockSpec((None, tb, 1), lambda hh, i: (hh, i, 0)),
                  pl.BlockSpec((None, nb, 1, tb), lambda hh, i: (hh, 0, 0, 0))],
        out_specs=[pl.BlockSpec((None, tb, hd), lambda hh, i: (hh, i, 0)),
                   pl.BlockSpec((None, tb, 1), lambda hh, i: (hh, i, 0))],
        out_shape=[jax.ShapeDtypeStruct((h, s, hd), F32), jax.ShapeDtypeStruct((h, s, 1), F32)],
        compiler_params=_cparams(),
    )(q, k, v, c_col, c_row)


def _attn_bwd_q(name, q, k, v, o, do, lse, c_col, c_row, tb):
    h, s, hd = q.shape
    nb = s // tb
    scale = 1.0 / float(hd) ** 0.5

    def body(q_ref, k_ref, v_ref, o_ref, do_ref, lse_ref, cq_ref, ck_ref, dq_ref, dl_ref, dcq_ref):
        i = pl.program_id(1)
        qv = q_ref[...]
        dov = do_ref[...]
        cq = cq_ref[...]
        lse_v = lse_ref[...]
        delta = jnp.sum(dov.astype(F32) * o_ref[...], axis=-1, keepdims=True)

        def step(j, carry):
            dq, dcq = carry
            k0 = pl.multiple_of(j * tb, tb)
            kj = k_ref[pl.ds(k0, tb), :]
            p = jnp.exp(_scores(qv, kj, cq, ck_ref[j], scale, i * tb, j * tb) - lse_v)
            dp = lax.dot_general(dov, v_ref[pl.ds(k0, tb), :], (((1,), (1,)), ((), ())), preferred_element_type=F32)
            ds = p * (dp - delta)
            return (dq + jnp.dot(ds.astype(BF16), kj, preferred_element_type=F32),
                    dcq + jnp.sum(ds, axis=-1, keepdims=True))

        dq, dcq = lax.fori_loop(0, i + 1, step, (jnp.zeros((tb, hd), F32), jnp.zeros((tb, 1), F32)))
        dq_ref[...] = (dq * scale).astype(dq_ref.dtype)
        dl_ref[...] = delta
        dcq_ref[...] = dcq

    blk = pl.BlockSpec((None, tb, hd), lambda hh, i: (hh, i, 0))
    full = pl.BlockSpec((None, s, hd), lambda hh, i: (hh, 0, 0))
    col = pl.BlockSpec((None, tb, 1), lambda hh, i: (hh, i, 0))
    return pl.pallas_call(
        body, name=name, grid=(h, nb),
        in_specs=[blk, full, full, blk, blk, col, col,
                  pl.BlockSpec((None, nb, 1, tb), lambda hh, i: (hh, 0, 0, 0))],
        out_specs=[blk, col, col],
        out_shape=[jax.ShapeDtypeStruct((h, s, hd), BF16), jax.ShapeDtypeStruct((h, s, 1), F32),
                   jax.ShapeDtypeStruct((h, s, 1), F32)],
        compiler_params=_cparams(),
    )(q, k, v, o, do, lse, c_col, c_row)


def _attn_bwd_kv(name, q, k, v, do, lse_row, delta_row, c_col, c_row, tb):
    h, s, hd = q.shape
    nb = s // tb
    scale = 1.0 / float(hd) ** 0.5

    def body(q_ref, k_ref, v_ref, do_ref, lse_ref, dl_ref, ck_ref, cq_ref, dk_ref, dv_ref, dc_ref):
        j = pl.program_id(1)
        kv = k_ref[...]
        vv = v_ref[...]
        ck = ck_ref[...]

        def step(i, carry):
            dk, dv, dc = carry
            q0 = pl.multiple_of(i * tb, tb)
            qi = q_ref[pl.ds(q0, tb), :]
            doi = do_ref[pl.ds(q0, tb), :]
            st = lax.dot_general(kv, qi, (((1,), (1,)), ((), ())), preferred_element_type=F32) * scale
            st = st + (cq_ref[i] - ck)
            krow = j * tb + lax.broadcasted_iota(jnp.int32, st.shape, 0)
            qcol = i * tb + lax.broadcasted_iota(jnp.int32, st.shape, 1)
            pt = jnp.exp(jnp.where(krow <= qcol, st, -jnp.inf) - lse_ref[i])
            dv = dv + jnp.dot(pt.astype(BF16), doi, preferred_element_type=F32)
            dpt = lax.dot_general(vv, doi, (((1,), (1,)), ((), ())), preferred_element_type=F32)
            dst = pt * (dpt - dl_ref[i])
            dk = dk + jnp.dot(dst.astype(BF16), qi, preferred_element_type=F32)
            dc = dc - jnp.sum(dst, axis=-1, keepdims=True)
            return dk, dv, dc

        init = (jnp.zeros((tb, hd), F32), jnp.zeros((tb, hd), F32), jnp.zeros((tb, 1), F32))
        dk, dv, dc = lax.fori_loop(j, nb, step, init)
        dk_ref[...] = (dk * scale).astype(dk_ref.dtype)
        dv_ref[...] = dv.astype(dv_ref.dtype)
        dc_ref[...] = dc

    blk = pl.BlockSpec((None, tb, hd), lambda hh, jj: (hh, jj, 0))
    full = pl.BlockSpec((None, s, hd), lambda hh, jj: (hh, 0, 0))
    col = pl.BlockSpec((None, tb, 1), lambda hh, jj: (hh, jj, 0))
    rows = pl.BlockSpec((None, nb, 1, tb), lambda hh, jj: (hh, 0, 0, 0))
    return pl.pallas_call(
        body, name=name, grid=(h, nb),
        in_specs=[full, blk, blk, full, rows, rows, col, rows],
        out_specs=[blk, blk, col],
        out_shape=[jax.ShapeDtypeStruct((h, s, hd), BF16), jax.ShapeDtypeStruct((h, s, hd), BF16),
                   jax.ShapeDtypeStruct((h, s, 1), F32)],
        compiler_params=_cparams(),
    )(q, k, v, do, lse_row, delta_row, c_col, c_row)


def _pair_masks(hd):
    lane = lax.broadcasted_iota(jnp.int32, (1, 2 * hd), 1)
    return lane < hd, lane >= hd


def _only(mask, a):
    return jnp.where(mask, a, jnp.zeros_like(a))


def _nt(a, b):
    return lax.dot_general(a, b, (((1,), (1,)), ((), ())), preferred_element_type=F32)


def _causal(sc, rows_are_queries):
    r = lax.broadcasted_iota(jnp.int32, sc.shape, 0)
    c = lax.broadcasted_iota(jnp.int32, sc.shape, 1)
    return jnp.where((c <= r) if rows_are_queries else (r <= c), sc, -jnp.inf)


def _pair_specs(dm, s):
    hd = dm.aw // dm.nh
    pw = 2 * hd
    assert pw == LANES and dm.o_q % pw == 0 and dm.aw % pw == 0
    return hd, pw, dm.o_q // pw, (dm.o_q + dm.aw) // pw, (dm.o_q + 2 * dm.aw) // pw


def _attn2_fwd(name, zm, c_col, c_row, dm):
    s, tb = zm.shape[0], dm.tb
    nb = s // tb
    hd, pw, qb, kb, vb = _pair_specs(dm, s)
    scale = 1.0 / float(hd) ** 0.5

    def body(q_ref, k_ref, v_ref, cq_ref, cr_ref, o_ref, ob_ref, lse_ref):
        i = pl.program_id(1)
        masks = _pair_masks(hd)
        q2 = q_ref[...]
        qe = [_only(m, q2) for m in masks]
        cq = [cq_ref[0], cq_ref[1]]

        def block(j, carry, diag):
            k0 = pl.multiple_of(j * tb, tb)
            kj = k_ref[pl.ds(k0, tb), :]
            vj = v_ref[pl.ds(k0, tb), :]
            out = []
            for e in range(2):
                m_i, l_i, acc = carry[e]
                sc = _nt(qe[e], kj) * scale + (cq[e] - cr_ref[e, j])
                if diag:
                    sc = _causal(sc, True)
                m_new = jnp.maximum(m_i, jnp.max(sc, axis=-1, keepdims=True))
                alpha = jnp.exp(m_i - m_new)
                p = jnp.exp(sc - m_new)
                l_new = alpha * l_i + jnp.sum(p, axis=-1, keepdims=True)
                acc = alpha * acc + jnp.dot(p.astype(BF16), vj, preferred_element_type=F32)
                out.append((m_new, l_new, acc))
            return tuple(out)

        one = (jnp.full((tb, 1), -jnp.inf, F32), jnp.zeros((tb, 1), F32), jnp.zeros((tb, pw), F32))
        carry = lax.fori_loop(0, i, lambda j, c: block(j, c, False), (one, one))
        (m0, l0, a0), (m1, l1, a1) = block(i, carry, True)
        o = jnp.where(masks[0], a0 / l0, a1 / l1)
        o_ref[...] = o
        ob_ref[...] = o.astype(BF16)
        lse_ref[0] = m0 + jnp.log(l0)
        lse_ref[1] = m1 + jnp.log(l1)

    blk = lambda cb: pl.BlockSpec((tb, pw), functools.partial(lambda hp, i, cb: (i, cb + hp), cb=cb))
    full = lambda cb: pl.BlockSpec((s, pw), functools.partial(lambda hp, i, cb: (0, cb + hp), cb=cb))
    col = pl.BlockSpec((2, tb, 1), lambda hp, i: (hp, i, 0))
    rows = pl.BlockSpec((2, nb, 1, tb), lambda hp, i: (hp, 0, 0, 0))
    return pl.pallas_call(
        body, name=name, grid=(dm.nh // 2, nb),
        in_specs=[blk(qb), full(kb), full(vb), col, rows],
        out_specs=[blk(0), blk(0), col],
        out_shape=[jax.ShapeDtypeStruct((s, dm.aw), F32), jax.ShapeDtypeStruct((s, dm.aw), BF16),
                   jax.ShapeDtypeStruct((dm.nh, s, 1), F32)],
        compiler_params=_cparams(),
    )(zm, zm, zm, c_col, c_row)


def _attn2_bwd_q(name, zm, o, do, lse, c_col, c_row, dm):
    s, tb = zm.shape[0], dm.tb
    nb = s // tb
    hd, pw, qb, kb, vb = _pair_specs(dm, s)
    scale = 1.0 / float(hd) ** 0.5

    def body(q_ref, k_ref, v_ref, o_ref, do_ref, lse_ref, cq_ref, cr_ref, dq_ref, dl_ref, dcq_ref):
        i = pl.program_id(1)
        masks = _pair_masks(hd)
        q2 = q_ref[...]
        do2 = do_ref[...]
        prod = do2.astype(F32) * o_ref[...]
        qe = [_only(m, q2) for m in masks]
        doe = [_only(m, do2) for m in masks]
        delta = [jnp.sum(_only(m, prod), axis=-1, keepdims=True) for m in masks]
        cq = [cq_ref[0], cq_ref[1]]
        lse_v = [lse_ref[0], lse_ref[1]]

        def block(j, carry, diag):
            k0 = pl.multiple_of(j * tb, tb)
            kj = k_ref[pl.ds(k0, tb), :]
            vj = v_ref[pl.ds(k0, tb), :]
            out = []
            for e in range(2):
                dq, dcq = carry[e]
                sc = _nt(qe[e], kj) * scale + (cq[e] - cr_ref[e, j])
                if diag:
                    sc = _causal(sc, True)
                p = jnp.exp(sc - lse_v[e])
                ds = p * (_nt(doe[e], vj) - delta[e])
                out.append((dq + jnp.dot(ds.astype(BF16), kj, preferred_element_type=F32),
                            dcq + jnp.sum(ds, axis=-1, keepdims=True)))
            return tuple(out)

        one = (jnp.zeros((tb, pw), F32), jnp.zeros((tb, 1), F32))
        carry = lax.fori_loop(0, i, lambda j, c: block(j, c, False), (one, one))
        (dq0, dc0), (dq1, dc1) = block(i, carry, True)
        dq_ref[...] = (jnp.where(masks[0], dq0, dq1) * scale).astype(dq_ref.dtype)
        dl_ref[0] = delta[0]
        dl_ref[1] = delta[1]
        dcq_ref[0] = dc0
        dcq_ref[1] = dc1

    blk = lambda cb: pl.BlockSpec((tb, pw), functools.partial(lambda hp, i, cb: (i, cb + hp), cb=cb))
    full = lambda cb: pl.BlockSpec((s, pw), functools.partial(lambda hp, i, cb: (0, cb + hp), cb=cb))
    col = pl.BlockSpec((2, tb, 1), lambda hp, i: (hp, i, 0))
    rows = pl.BlockSpec((2, nb, 1, tb), lambda hp, i: (hp, 0, 0, 0))
    return pl.pallas_call(
        body, name=name, grid=(dm.nh // 2, nb),
        in_specs=[blk(qb), full(kb), full(vb), blk(0), blk(0), col, col, rows],
        out_specs=[blk(0), col, col],
        out_shape=[jax.ShapeDtypeStruct((s, dm.aw), BF16), jax.ShapeDtypeStruct((dm.nh, s, 1), F32),
                   jax.ShapeDtypeStruct((dm.nh, s, 1), F32)],
        compiler_params=_cparams(),
    )(zm, zm, zm, o, do, lse, c_col, c_row)


def _attn2_bwd_kv(name, zm, do, lse_row, delta_row, c_col, c_row, dm):
    s, tb = zm.shape[0], dm.tb
    nb = s // tb
    hd, pw, qb, kb, vb = _pair_specs(dm, s)
    scale = 1.0 / float(hd) ** 0.5

    def body(q_ref, k_ref, v_ref, do_ref, lse_ref, dl_ref, ck_ref, cr_ref, dk_ref, dv_ref, dc_ref):
        j = pl.program_id(1)
        masks = _pair_masks(hd)
        k2 = k_ref[...]
        v2 = v_ref[...]
        ke = [_only(m, k2) for m in masks]
        ve = [_only(m, v2) for m in masks]
        ck = [ck_ref[0], ck_ref[1]]

        def block(i, carry, diag):
            q0 = pl.multiple_of(i * tb, tb)
            qi = q_ref[pl.ds(q0, tb), :]
            doi = do_ref[pl.ds(q0, tb), :]
            out = []
            for e in range(2):
                dk, dv, dc = carry[e]
                st = _nt(ke[e], qi) * scale + (cr_ref[e, i] - ck[e])
                if diag:
                    st = _causal(st, False)
                pt = jnp.exp(st - lse_ref[e, i])
                dv = dv + jnp.dot(pt.astype(BF16), doi, preferred_element_type=F32)
                dst = pt * (_nt(ve[e], doi) - dl_ref[e, i])
                dk = dk + jnp.dot(dst.astype(BF16), qi, preferred_element_type=F32)
                out.append((dk, dv, dc - jnp.sum(dst, axis=-1, keepdims=True)))
            return tuple(out)

        one = (jnp.zeros((tb, pw), F32), jnp.zeros((tb, pw), F32), jnp.zeros((tb, 1), F32))
        carry = block(j, (one, one), True)
        (dk0, dv0, dc0), (dk1, dv1, dc1) = lax.fori_loop(j + 1, nb, lambda i, c: block(i, c, False), carry)
        dk_ref[...] = (jnp.where(masks[0], dk0, dk1) * scale).astype(dk_ref.dtype)
        dv_ref[...] = jnp.where(masks[0], dv0, dv1).astype(dv_ref.dtype)
        dc_ref[0] = dc0
        dc_ref[1] = dc1

    blk = lambda cb: pl.BlockSpec((tb, pw), functools.partial(lambda hp, jj, cb: (jj, cb + hp), cb=cb))
    full = lambda cb: pl.BlockSpec((s, pw), functools.partial(lambda hp, jj, cb: (0, cb + hp), cb=cb))
    col = pl.BlockSpec((2, tb, 1), lambda hp, jj: (hp, jj, 0))
    rows = pl.BlockSpec((2, nb, 1, tb), lambda hp, jj: (hp, 0, 0, 0))
    return pl.pallas_call(
        body, name=name, grid=(dm.nh // 2, nb),
        in_specs=[full(qb), blk(kb), blk(vb), full(0), rows, rows, col, rows],
        out_specs=[blk(0), blk(0), col],
        out_shape=[jax.ShapeDtypeStruct((s, dm.aw), BF16), jax.ShapeDtypeStruct((s, dm.aw), BF16),
                   jax.ShapeDtypeStruct((dm.nh, s, 1), F32)],
        compiler_params=_cparams(),
    )(zm, zm, zm, do, lse_row, delta_row, c_col, c_row)


def _glu(cv, cc):
    c1 = cv[:, :cc].astype(F32)
    c2 = cv[:, cc:].astype(F32)
    return c1 * _sig(c2)


def _conv_fwd(name, zm, w_pad, b_row, g_row, cc, taps):
    s = zm.shape[0]
    tr = _pick(s, (256, 128))
    hpb = tr // HALO
    off = HALO - (taps - 1)

    def body(cur_ref, halo_ref, w_ref, b_ref, g_ref, y_ref, cs_ref, apad):
        i = pl.program_id(0)
        apad[0:HALO, :] = _glu(halo_ref[...], cc) * jnp.where(i > 0, 1.0, 0.0)
        apad[HALO:, :] = _glu(cur_ref[...], cc)
        acc = jnp.zeros((tr, cc), F32) + b_ref[...]
        for t in range(taps):
            acc = acc + w_ref[t:t + 1, :] * apad[off + t:off + t + tr, :]
        y_ref[...] = acc
        r = lax.rsqrt(jnp.mean(acc * acc, axis=-1, keepdims=True) + RMS_EPS)
        n = acc * r * g_ref[...]
        cs_ref[...] = (n * _sig(n)).astype(cs_ref.dtype)

    return pl.pallas_call(
        body, name=name, grid=(s // tr,),
        in_specs=[pl.BlockSpec((tr, 2 * cc), lambda i: (i, 0)),
                  pl.BlockSpec((HALO, 2 * cc), lambda i: (jnp.maximum(i * hpb - 1, 0), 0)),
                  pl.BlockSpec(w_pad.shape, lambda i: (0, 0)),
                  pl.BlockSpec(b_row.shape, lambda i: (0, 0)),
                  pl.BlockSpec(g_row.shape, lambda i: (0, 0))],
        out_specs=[pl.BlockSpec((tr, cc), lambda i: (i, 0)), pl.BlockSpec((tr, cc), lambda i: (i, 0))],
        out_shape=[jax.ShapeDtypeStruct((s, cc), F32), jax.ShapeDtypeStruct((s, cc), BF16)],
        scratch_shapes=[pltpu.VMEM((HALO + tr, cc), F32)], compiler_params=_cparams(),
    )(zm, zm, w_pad, b_row, g_row)


def _conv_bwd(name, zm, y, dcs, w_pad, g_row, cc, taps):
    s = zm.shape[0]
    tr = _pick(s, (256, 128))
    hpb = tr // HALO
    nblk = s // tr
    off = HALO - (taps - 1)

    def dy_of(yv, dcsv, g):
        r = lax.rsqrt(jnp.mean(yv * yv, axis=-1, keepdims=True) + RMS_EPS)
        xh = yv * r
        n = xh * g
        sg = _sig(n)
        dn = dcsv.astype(F32) * (sg * (1.0 + n * (1.0 - sg)))
        dxh = dn * g
        dy = r * (dxh - xh * jnp.mean(dxh * xh, axis=-1, keepdims=True))
        return dy, dn * xh

    def body(cur_ref, halo_ref, y_ref, yn_ref, dcs_ref, dcsn_ref, w_ref, g_ref,
             dcv_ref, dw_ref, db_ref, dg_ref, apad, dypad):
        i = pl.program_id(0)

        @pl.when(i == 0)
        def _():
            dw_ref[...] = jnp.zeros(dw_ref.shape, F32)
            db_ref[...] = jnp.zeros(db_ref.shape, F32)
            dg_ref[...] = jnp.zeros(dg_ref.shape, F32)

        g = g_ref[...]
        apad[0:HALO, :] = _glu(halo_ref[...], cc) * jnp.where(i > 0, 1.0, 0.0)
        apad[HALO:, :] = _glu(cur_ref[...], cc)
        dy, dgt = dy_of(y_ref[...], dcs_ref[...], g)
        dyn, _ = dy_of(yn_ref[...], dcsn_ref[...], g)
        dypad[0:tr, :] = dy
        dypad[tr:, :] = dyn * jnp.where(i < nblk - 1, 1.0, 0.0)
        db_ref[...] += jnp.sum(dy, axis=0, keepdims=True)
        dg_ref[...] += jnp.sum(dgt, axis=0, keepdims=True)
        da = jnp.zeros((tr, cc), F32)
        for t in range(taps):
            da = da + w_ref[t:t + 1, :] * dypad[taps - 1 - t:taps - 1 - t + tr, :]
            dw_ref[t:t + 1, :] += jnp.sum(dy * apad[off + t:off + t + tr, :], axis=0, keepdims=True)
        cv = cur_ref[...]
        c1 = cv[:, :cc].astype(F32)
        sg = _sig(cv[:, cc:].astype(F32))
        dcv_ref[:, :cc] = (da * sg).astype(dcv_ref.dtype)
        dcv_ref[:, cc:] = (da * c1 * sg * (1.0 - sg)).astype(dcv_ref.dtype)

    nxt = lambda i: (jnp.minimum((i + 1) * hpb, s // HALO - 1), 0)
    return pl.pallas_call(
        body, name=name, grid=(nblk,),
        in_specs=[pl.BlockSpec((tr, 2 * cc), lambda i: (i, 0)),
                  pl.BlockSpec((HALO, 2 * cc), lambda i: (jnp.maximum(i * hpb - 1, 0), 0)),
                  pl.BlockSpec((tr, cc), lambda i: (i, 0)), pl.BlockSpec((HALO, cc), nxt),
                  pl.BlockSpec((tr, cc), lambda i: (i, 0)), pl.BlockSpec((HALO, cc), nxt),
                  pl.BlockSpec(w_pad.shape, lambda i: (0, 0)), pl.BlockSpec(g_row.shape, lambda i: (0, 0))],
        out_specs=[pl.BlockSpec((tr, 2 * cc), lambda i: (i, 0)),
                   pl.BlockSpec(w_pad.shape, lambda i: (0, 0)),
                   pl.BlockSpec((1, cc), lambda i: (0, 0)), pl.BlockSpec((1, cc), lambda i: (0, 0))],
        out_shape=[jax.ShapeDtypeStruct((s, 2 * cc), BF16), jax.ShapeDtypeStruct(w_pad.shape, F32),
                   jax.ShapeDtypeStruct((1, cc), F32), jax.ShapeDtypeStruct((1, cc), F32)],
        scratch_shapes=[pltpu.VMEM((HALO + tr, cc), F32), pltpu.VMEM((tr + HALO, cc), F32)],
        compiler_params=_cparams(),
    )(zm, zm, y, y, dcs, dcs, w_pad, g_row)


def _place():
    x, y, c = lax.axis_index("x"), lax.axis_index("y"), lax.axis_index("c")
    chips = [(1 - x, y), (x, 1 - y), (1 - x, 1 - y)]
    return x, y, c, chips


def _half(c, rows):
    rh = rows // 2
    return pl.ds(pl.multiple_of(c * rh, 16), rh)


def _remote(src, dst, send, recv, dev):
    return pltpu.make_async_remote_copy(src_ref=src, dst_ref=dst, send_sem=send, recv_sem=recv,
                                        device_id=dev, device_id_type=MESH)


def _cast_slots(name, w, chip_arr):
    nl, r, cdim = w.shape
    tr = _pick(r, (256, 128, 64, 32, 16))
    n = r // tr

    def body(q_ref, w_ref, *o_refs):
        layer = pl.program_id(0)
        for j, o_ref in enumerate(o_refs):
            @pl.when(layer == j)
            def _(o_ref=o_ref):
                o_ref[...] = w_ref[...].astype(o_ref.dtype)

    def out_map(j):
        return lambda l, i, q: (q[0], jnp.where(l < j, 0, jnp.where(l == j, i, n - 1)), 0)

    return pl.pallas_call(
        body, name=name,
        grid_spec=pltpu.PrefetchScalarGridSpec(
            num_scalar_prefetch=1, grid=(nl, n),
            in_specs=[pl.BlockSpec((None, tr, cdim), lambda l, i, q: (l, i, 0))],
            out_specs=[pl.BlockSpec((None, tr, cdim), out_map(j)) for j in range(nl)]),
        out_shape=[jax.ShapeDtypeStruct((N_CHIPS, r, cdim), BF16)] * nl, compiler_params=_cparams(),
    )(chip_arr, w)


def _split_start(name, n_sems, bufs, issue, after=()):
    nb = len(bufs)
    n_in = nb + len(after)

    def body(*refs):
        issue(refs[:nb], refs[n_in], refs[n_in + 1])
        refs[-1][...] = jnp.zeros(refs[-1].shape, F32)

    outs = pl.pallas_call(
        body, name=name, in_specs=[HBM_SPEC] * nb + [ANY] * len(after),
        out_shape=(pltpu.SemaphoreType.DMA(n_sems), pltpu.SemaphoreType.DMA(n_sems),
                   *[pltpu.HBM(b.shape, b.dtype) for b in bufs], jax.ShapeDtypeStruct((8, LANES), F32)),
        out_specs=(SEM_SPEC, SEM_SPEC, *[HBM_SPEC] * nb, pl.BlockSpec(memory_space=pltpu.VMEM)),
        input_output_aliases={t: t + 2 for t in range(nb)},
        compiler_params=pltpu.CompilerParams(has_side_effects=EFFECT),
    )(*[pltpu.with_memory_space_constraint(b, pltpu.HBM) for b in bufs], *after)
    return outs[0], outs[1], list(outs[2:2 + nb]), outs[-1]


def _split_wait(name, bufs, send, recv, after, drain):
    nb = len(bufs)

    def body(*refs):
        drain(refs[:nb], refs[nb], refs[nb + 1])

    return list(pl.pallas_call(
        body, name=name, in_specs=[HBM_SPEC] * nb + [SEM_SPEC, SEM_SPEC, ANY],
        out_shape=tuple(pltpu.HBM(b.shape, b.dtype) for b in bufs), out_specs=tuple([HBM_SPEC] * nb),
        input_output_aliases={t: t for t in range(nb)},
        compiler_params=pltpu.CompilerParams(has_side_effects=EFFECT),
    )(*bufs, send, recv, after))


def _gather_start(name, bufs, after=()):
    nt = len(bufs)

    def issue(g, send, recv):
        x, y, c, chips = _place()
        me = 2 * x + y
        for t in range(nt):
            part = g[t].at[me, _half(c, bufs[t].shape[1]), :]
            for j, (qx, qy) in enumerate(chips):
                _remote(part, part, send.at[3 * t + j], recv.at[3 * t + j], (qx, qy, c)).start()

    return _split_start(name, (3 * nt,), bufs, issue, after)


def _gather_wait(name, bufs, send, recv, after):
    nt = len(bufs)

    def drain(g, send, recv):
        x, y, c, _ = _place()
        for t in range(nt):
            part = g[t].at[0, pl.ds(0, bufs[t].shape[1] // 2), :]
            for j in range(3):
                cp = _remote(part, part, send.at[3 * t + j], recv.at[3 * t + j], (x, y, c))
                cp.wait_send()
                cp.wait_recv()

    return _split_wait(name, bufs, send, recv, after, drain)


def _gather_forward(name, bufs):
    nt = len(bufs)

    def body(*refs):
        g = refs[nt:2 * nt]
        send, recv = refs[2 * nt:]
        x, y, c, chips = _place()
        cps = []
        for t in range(nt):
            for j, (qx, qy) in enumerate(chips):
                part = g[t].at[2 * qx + qy, _half(c, bufs[t].shape[1]), :]
                cps.append(_remote(part, part, send.at[t, j], recv.at[t, j], (x, y, 1 - c)))
        for cp in cps:
            cp.start()
        for t in range(nt):
            for j, (qx, qy) in enumerate(chips):
                theirs = g[t].at[2 * qx + qy, _half(1 - c, bufs[t].shape[1]), :]
                _remote(theirs, theirs, send.at[t, j], recv.at[t, j], (x, y, 1 - c)).wait_recv()
        for cp in cps:
            cp.wait_send()

    return list(pl.pallas_call(
        body, name=name, in_specs=[ANY] * nt, out_specs=[ANY] * nt,
        out_shape=[jax.ShapeDtypeStruct(b.shape, b.dtype) for b in bufs],
        input_output_aliases={t: t for t in range(nt)},
        scratch_shapes=[pltpu.SemaphoreType.DMA((nt, 3)), pltpu.SemaphoreType.DMA((nt, 3))],
        compiler_params=pltpu.CompilerParams(has_side_effects=True),
    )(*bufs))


def _swap_halves(name, grads):
    nt = len(grads)

    def body(*refs):
        g_refs, r_refs = refs[:nt], refs[nt:2 * nt]
        send, recv = refs[2 * nt:]
        x, y, c, _ = _place()
        cps = [_remote(g_refs[t].at[:, _half(1 - c, grads[t].shape[1]), :], r_refs[t], send.at[t], recv.at[t],
                       (x, y, 1 - c)) for t in range(nt)]
        for cp in cps:
            cp.start()
        for cp in cps:
            cp.wait()

    return pl.pallas_call(
        body, name=name, in_specs=[ANY] * nt, out_specs=[ANY] * nt,
        out_shape=[jax.ShapeDtypeStruct((N_CHIPS, g.shape[1] // 2, g.shape[2]), g.dtype) for g in grads],
        scratch_shapes=[pltpu.SemaphoreType.DMA((nt,)), pltpu.SemaphoreType.DMA((nt,))],
        compiler_params=pltpu.CompilerParams(has_side_effects=True),
    )(*grads)


def _scatter_start(name, parts):
    nt = len(parts)
    lands = [lax.empty((3,) + p.shape[1:], p.dtype) for p in parts]

    def issue(refs, send, recv):
        x, y, c, chips = _place()
        for t in range(nt):
            for j, (qx, qy) in enumerate(chips):
                _remote(refs[t].at[2 * qx + qy], refs[nt + t].at[j], send.at[3 * t + j], recv.at[3 * t + j],
                        (qx, qy, c)).start()

    return _split_start(name, (3 * nt,), list(parts) + lands, issue)


def _scatter_wait(name, bufs, send, recv, after):
    nt = len(bufs) // 2

    def drain(refs, send, recv):
        x, y, c, _ = _place()
        for t in range(nt):
            for j in range(3):
                cp = _remote(refs[t].at[0], refs[nt + t].at[j], send.at[3 * t + j], recv.at[3 * t + j], (x, y, c))
                cp.wait_send()
                cp.wait_recv()

    return _split_wait(name, bufs, send, recv, after, drain)


def _join_halves(name, fulls):
    nt = len(fulls)

    def body(*refs):
        o_refs = refs[nt:2 * nt]
        send, recv = refs[2 * nt:]
        x, y, c, _ = _place()
        cps = []
        for t in range(nt):
            half = o_refs[t].at[_half(c, fulls[t].shape[0]), :]
            cps.append(_remote(half, half, send.at[t], recv.at[t], (x, y, 1 - c)))
        for cp in cps:
            cp.start()
        for t in range(nt):
            theirs = o_refs[t].at[_half(1 - c, fulls[t].shape[0]), :]
            _remote(theirs, theirs, send.at[t], recv.at[t], (x, y, 1 - c)).wait_recv()
        for cp in cps:
            cp.wait_send()

    return list(pl.pallas_call(
        body, name=name, in_specs=[ANY] * nt, out_specs=[ANY] * nt,
        out_shape=[jax.ShapeDtypeStruct(a.shape, a.dtype) for a in fulls],
        input_output_aliases={t: t for t in range(nt)},
        scratch_shapes=[pltpu.SemaphoreType.DMA((nt,)), pltpu.SemaphoreType.DMA((nt,))],
        compiler_params=pltpu.CompilerParams(has_side_effects=True),
    )(*fulls))


def _add_own_half(name, grad, recv_half, c_arr):
    _, r, cdim = grad.shape
    rh = r // 2
    tr = _pick(rh, (256, 128, 64, 32, 16))
    nrb = rh // tr

    def body(c_ref, g_ref, r_ref, o_ref):
        o_ref[...] = (g_ref[...].astype(F32) + r_ref[...].astype(F32)).astype(o_ref.dtype)

    return pl.pallas_call(
        body, name=name,
        grid_spec=pltpu.PrefetchScalarGridSpec(
            num_scalar_prefetch=1, grid=(N_CHIPS, nrb),
            in_specs=[pl.BlockSpec((None, tr, cdim), lambda q, i, cr: (q, cr[0] * nrb + i, 0)),
                      pl.BlockSpec((None, tr, cdim), lambda q, i, cr: (q, i, 0))],
            out_specs=pl.BlockSpec((None, tr, cdim), lambda q, i, cr: (q, i, 0))),
        out_shape=jax.ShapeDtypeStruct((N_CHIPS, rh, cdim), BF16), compiler_params=_cparams(),
    )(c_arr, grad, recv_half)


def _add_chips(name, part, came, place_arr):
    _, rh, cdim = part.shape
    tr = _pick(rh, (256, 128, 64, 32, 16))
    nrb = rh // tr

    def body(q_ref, p_ref, r_ref, o_ref):
        acc = p_ref[...].astype(F32)
        for j in range(3):
            acc = acc + r_ref[j].astype(F32)
        o_ref[...] = acc

    return pl.pallas_call(
        body, name=name,
        grid_spec=pltpu.PrefetchScalarGridSpec(
            num_scalar_prefetch=1, grid=(nrb,),
            in_specs=[pl.BlockSpec((None, tr, cdim), lambda i, qr: (qr[0], i, 0)),
                      pl.BlockSpec((3, tr, cdim), lambda i, qr: (0, i, 0))],
            out_specs=pl.BlockSpec((tr, cdim), lambda i, qr: (qr[1] * nrb + i, 0))),
        out_shape=jax.ShapeDtypeStruct((2 * rh, cdim), F32), compiler_params=_cparams(),
    )(place_arr, part, came)


def _reduce_scatter_begin(tag, grads, c_arr):
    nt = len(grads)
    got = _swap_halves(f"rs_swap_{tag}", grads)
    parts = [_add_own_half(f"rs_add2_{tag}_{t}", grads[t], got[t], c_arr) for t in range(nt)]
    return _scatter_start(f"rs_scatter_start_{tag}", parts)


def _reduce_scatter_end(tag, state, after, place_arr):
    send, recv, bufs, _ = state
    nt = len(bufs) // 2
    bufs = _scatter_wait(f"rs_scatter_wait_{tag}", bufs, send, recv, after)
    fulls = [_add_chips(f"rs_add4_{tag}_{t}", bufs[t], bufs[nt + t], place_arr) for t in range(nt)]
    return _join_halves(f"rs_join_{tag}", fulls)


def _allgather_blocks(name, blk):
    m_per, n = blk.shape

    def body(x_ref, out_ref, send_sems, recv_sems, local_sem):
        x, y, c, chips = _place()
        me, sibling = (x, y, c), (x, y, 1 - c)

        def rows(px, py, pc):
            return out_ref.at[pl.ds(pl.multiple_of((4 * px + 2 * py + pc) * m_per, 8), m_per), :]

        def copy(k, block, to, src=None):
            return _remote(rows(*block) if src is None else src, rows(*block), send_sems.at[k], recv_sems.at[k], to)

        mine = pltpu.make_async_copy(x_ref, rows(*me), local_sem)
        mine.start()
        first = [copy(0, me, sibling, src=x_ref)]
        first += [copy(1 + j, me, (*chip, c), src=x_ref) for j, chip in enumerate(chips)]
        for cp in first:
            cp.start()
        passed = [copy(4 + j, (*chip, c), sibling) for j, chip in enumerate(chips)]
        for j, chip in enumerate(chips):
            copy(1 + j, (*chip, c), me).wait_recv()
            passed[j].start()
        copy(0, sibling, me).wait_recv()
        for j, chip in enumerate(chips):
            copy(4 + j, (*chip, 1 - c), me).wait_recv()
        for cp in first + passed:
            cp.wait_send()
        mine.wait()

    return pl.pallas_call(
        body, name=name, out_shape=jax.ShapeDtypeStruct((N_DEV * m_per, n), blk.dtype),
        in_specs=[pl.BlockSpec(memory_space=pltpu.VMEM)], out_specs=pl.BlockSpec(memory_space=pltpu.VMEM),
        scratch_shapes=[pltpu.SemaphoreType.DMA((7,)), pltpu.SemaphoreType.DMA((7,)), pltpu.SemaphoreType.DMA],
        compiler_params=_cparams(),
    )(blk)


def _sum_blocks(name, stacked):
    nd, m, n = stacked.shape
    tr = _pick(m, (336, 256, 168, 128, 64, 32, 16, 8))

    def body(s_ref, o_ref):
        acc = s_ref[0]
        for d in range(1, nd):
            acc = acc + s_ref[d]
        o_ref[...] = acc

    return pl.pallas_call(
        body, name=name, grid=(m // tr,),
        in_specs=[pl.BlockSpec((nd, tr, n), lambda i: (0, i, 0))],
        out_specs=pl.BlockSpec((tr, n), lambda i: (i, 0)),
        out_shape=jax.ShapeDtypeStruct((m, n), F32), compiler_params=_cparams(),
    )(stacked)


def _adamw_vals(w, g, m, v):
    m2 = ADAM_B1 * m + (1.0 - ADAM_B1) * g
    v2 = ADAM_B2 * v + (1.0 - ADAM_B2) * (g * g)
    m_hat = m2 / (1.0 - ADAM_B1 ** ADAM_STEP)
    v_hat = v2 / (1.0 - ADAM_B2 ** ADAM_STEP)
    delta = -ADAM_LR * (m_hat / (jnp.sqrt(v_hat) + ADAM_EPS) + ADAM_WD * w)
    return delta, m2, v2


def _adamw_layers(name, w, m, v, grads):
    nl, r, cdim = w.shape
    tr = _pick(r, (128, 64, 32, 16, 8))

    def body(w_ref, m_ref, v_ref, *rest):
        g_refs = rest[:nl]
        go_ref, d_ref, mo_ref, vo_ref = rest[nl:]
        layer = pl.program_id(0)
        g = g_refs[0][...]
        for j in range(1, nl):
            g = jnp.where(layer == j, g_refs[j][...], g)
        delta, m2, v2 = _adamw_vals(w_ref[...], g, m_ref[...], v_ref[...])
        go_ref[...] = g
        d_ref[...] = delta
        mo_ref[...] = m2
        vo_ref[...] = v2

    big = pl.BlockSpec((None, tr, cdim), lambda l, i: (l, i, 0))
    g_specs = [pl.BlockSpec((tr, cdim), functools.partial(lambda l, i, j: (jnp.where(l == j, i, 0), 0), j=j))
               for j in range(nl)]
    return pl.pallas_call(
        body, name=name, grid=(nl, r // tr), in_specs=[big, big, big] + g_specs,
        out_specs=[big, big, big, big],
        out_shape=[jax.ShapeDtypeStruct(w.shape, F32)] * 4, compiler_params=_cparams(),
    )(w, m, v, *grads)


def _adamw_small(name, w, g, m, v):
    def body(w_ref, g_ref, m_ref, v_ref, d_ref, mo_ref, vo_ref):
        delta, m2, v2 = _adamw_vals(w_ref[...], g_ref[...], m_ref[...], v_ref[...])
        d_ref[...] = delta
        mo_ref[...] = m2
        vo_ref[...] = v2

    return pl.pallas_call(body, name=name, out_shape=[jax.ShapeDtypeStruct(w.shape, F32)] * 3,
                          compiler_params=_cparams())(w, g, m, v)


def _swiglu_fwd(name, z, f):
    def fn(zz):
        a = zz[:, :f].astype(F32)
        b = zz[:, f:].astype(F32)
        return a * _sig(a) * b
    return _rowwise(name, fn, [z], [(f, BF16)])[0]


def _swiglu_bwd(name, z, ds, f):
    def fn(zz, dd):
        a = zz[:, :f].astype(F32)
        b = zz[:, f:].astype(F32)
        d = dd.astype(F32)
        sg = _sig(a)
        da = d * b * (sg * (1.0 + a * (1.0 - sg)))
        db = d * a * sg
        return jnp.concatenate([da, db], axis=1)
    return _rowwise(name, fn, [z, ds], [(2 * f, BF16)])[0]


def _heads(a, nh):
    s, w = a.shape
    return jnp.transpose(a.reshape(s, nh, w // nh), (1, 0, 2))


def _unheads(a):
    nh, s, hd = a.shape
    return jnp.transpose(a, (1, 0, 2)).reshape(s, nh * hd)


def _as_rows(col, tb):
    nh, s, _ = col.shape
    return col.reshape(nh, s // tb, 1, tb)


class _Dims:
    def __init__(self, d, f, aw, nh, cc, taps, dp, s):
        self.d, self.f, self.aw, self.nh, self.cc, self.taps, self.dp, self.s = d, f, aw, nh, cc, taps, dp, s
        self.o_cv, self.o_ga, self.o_gc = 0, 2 * cc, 2 * cc + d
        self.o_q = 2 * cc + 2 * d
        self.n_main = self.o_q + 3 * aw
        self.tb = _pick(s, (256, 128))
        assert self.o_ga % d == 0 and self.o_q % aw == 0 and nh <= LANES


def _ffn_fwd(tag, h, g_row, w_in, w_out, dm, deps=()):
    n = _rms_fwd(f"{tag}_rms", h, g_row, deps)
    z = _mm_nn(f"{tag}_in", n, w_in, BF16)
    s = _swiglu_fwd(f"{tag}_act", z, dm.f)
    h2 = _mm_nn(f"{tag}_out", s, w_out, F32, res=h, scale=FFN_RES)
    return h2, (h, n, z, s)


def _ffn_bwd(tag, dh, dh_half_b, saved, g_row, w_in, w_out, dm, oscale):
    h, n, z, s = saved
    ds = _mm_nt(f"{tag}_dout", dh_half_b, w_out, BF16)
    dz = _swiglu_bwd(f"{tag}_dact", z, ds, dm.f)
    dw_out = _mm_tn(f"{tag}_wout", s, dh_half_b, 1)
    dw_in = _mm_tn(f"{tag}_win", n, dz, N_CHIPS)
    dn = _mm_nt(f"{tag}_din", dz, w_in, BF16)
    dh0, dh0_b, dg = _rms_bwd_res(f"{tag}_drms", h, g_row, [dn], dh, oscale)
    return dh0, dh0_b, dg, dw_in, dw_out


def _mixer_fwd(tag, h, sm, wt, dm):
    d, cc, aw, nh, tb = dm.d, dm.cc, dm.aw, dm.nh, dm.tb
    u = _rms_fwd(f"{tag}_rms", h, sm["g_mix"])
    zm = _mm_nn(f"{tag}_in", u, wt["w_main"], BF16)
    zf = _mm_nn(f"{tag}_inf", u, wt["w_f"], F32)
    c = _fgate_fwd(f"{tag}_fgate", zf, sm["b_f"])
    c_col = jnp.transpose(c[:, :nh], (1, 0))[:, :, None]
    c_row = _as_rows(c_col, tb)
    o32, o, lse = _attn2_fwd(f"{tag}_attn", zm, c_col, c_row, dm)
    ya = _mm_nn(f"{tag}_aout", o, wt["w_attn_out"], BF16)
    y, cs = _conv_fwd(f"{tag}_conv", zm, sm["conv_w"], sm["conv_b"], sm["g_conv"], cc, dm.taps)
    yc = _mm_nn(f"{tag}_cout", cs, wt["w_conv_out"], BF16)

    def merge(ga, gc, a, b):
        return _sig(ga.astype(F32)) * a.astype(F32) + _sig(gc.astype(F32)) * b.astype(F32)

    mg = _rowwise(f"{tag}_merge", merge, [(zm, dm.o_ga // d, d), (zm, dm.o_gc // d, d), ya, yc], [(d, BF16)])[0]
    h2 = _mm_nn(f"{tag}_out", mg, wt["w_out"], F32, res=h, scale=1.0)
    return h2, (h, u, zm, zf, c_col, c_row, o32, lse, o, ya, y, cs, yc, mg)


def _mixer_bwd(tag, dh, dh_b, saved, sm, wt, dm):
    d, cc, aw, nh, tb = dm.d, dm.cc, dm.aw, dm.nh, dm.tb
    h, u, zm, zf, c_col, c_row, o32, lse, o, ya, y, cs, yc, mg = saved
    dmg = _mm_nt(f"{tag}_dout", dh_b, wt["w_out"], BF16)
    dw_out = _mm_tn(f"{tag}_wout", mg, dh_b, 1)

    def unmerge(dd, ga, gc, a, b):
        dd, a, b = dd.astype(F32), a.astype(F32), b.astype(F32)
        sa, sc = _sig(ga.astype(F32)), _sig(gc.astype(F32))
        return dd * sa, dd * sc, dd * a * sa * (1.0 - sa), dd * b * sc * (1.0 - sc)

    dya, dyc, dga, dgc = _rowwise(f"{tag}_dmerge", unmerge,
                                  [dmg, (zm, dm.o_ga // d, d), (zm, dm.o_gc // d, d), ya, yc], [(d, BF16)] * 4)
    dw_a = _mm_tn(f"{tag}_waout", o, dya, N_CHIPS)
    do = _mm_nt(f"{tag}_daout", dya, wt["w_attn_out"], BF16)
    dw_c = _mm_tn(f"{tag}_wcout", cs, dyc, N_CHIPS)
    dcs = _mm_nt(f"{tag}_dcout", dyc, wt["w_conv_out"], BF16)
    dcv, dconv_w, dconv_b, dg_conv = _conv_bwd(f"{tag}_dconv", zm, y, dcs, sm["conv_w"], sm["g_conv"], cc, dm.taps)
    dq, delta, dcq_h = _attn2_bwd_q(f"{tag}_dattn_q", zm, o32, do, lse, c_col, c_row, dm)
    dk, dv, dck_h = _attn2_bwd_kv(f"{tag}_dattn_kv", zm, do, _as_rows(lse, tb), _as_rows(delta, tb),
                                  c_col, c_row, dm)
    dc = jnp.pad(jnp.transpose(dcq_h[:, :, 0], (1, 0)), ((0, 0), (0, zf.shape[1] - nh)))
    dc_k = jnp.pad(jnp.transpose(dck_h[:, :, 0], (1, 0)), ((0, 0), (0, zf.shape[1] - nh)))
    dzf, dzf_b, db_f = _fgate_bwd(f"{tag}_dfgate", dc, dc_k, zf, sm["b_f"])
    dzm = jnp.concatenate([dcv, dga, dgc, dq, dk, dv], axis=1)
    dw_main = _mm_tn(f"{tag}_win", u, dzm, 1)
    dw_f = _mm_tn(f"{tag}_winf", u, dzf_b, 1)
    du = _mm_nt(f"{tag}_din", dzm, wt["w_main"], BF16)
    du_f = _mm_nt(f"{tag}_dinf", dzf_b, wt["w_f"], BF16)
    dh0, dh0_b, dg_mix = _rms_bwd_res(f"{tag}_drms", h, sm["g_mix"], [du, du_f], dh, FFN_RES)
    small = dict(g_mix=dg_mix, b_f=db_f[:, :nh], conv_w=dconv_w[:dm.taps], conv_b=dconv_b, g_conv=dg_conv)
    return dh0, dh0_b, small, dw_main, dw_f, dw_a, dw_c, dw_out


def _ple_fwd(tag, h, p_b, sm, wt, dm):
    n = _rms_fwd(f"{tag}_rms", h, sm["g_ple"])
    gp = _mm_nn(f"{tag}_gate", n, wt["w_ple_gate"], BF16)
    pp = _mm_nn(f"{tag}_proj", p_b, wt["w_ple_proj"], BF16)

    def fn(hh, a, b):
        return hh + _sig(a.astype(F32)) * b.astype(F32)

    h2 = _rowwise(f"{tag}_mix", fn, [h, gp, pp], [(dm.d, F32)])[0]
    return h2, (h, n, gp, pp)


def _ple_bwd(tag, dh, saved, p_b, sm, wt, dm, deps=()):
    h, n, gp, pp = saved

    def fn(dd, a, b):
        gate = _sig(a.astype(F32))
        b = b.astype(F32)
        return dd * b * gate * (1.0 - gate), dd * gate

    dgp, dpp = _rowwise(f"{tag}_dmix", fn, [dh, gp, pp], [(dm.d, BF16)] * 2, deps=deps)
    dw_proj = _mm_tn(f"{tag}_wproj", p_b, dpp, N_CHIPS)
    dw_gate = _mm_tn(f"{tag}_wgate", n, dgp, 1)
    dn = _mm_nt(f"{tag}_dgate", dgp, wt["w_ple_gate"], BF16)
    dh0, dh0_b, dg = _rms_bwd_res(f"{tag}_drms", h, sm["g_ple"], [dn], dh, FFN_RES)
    return dh0, dh0_b, dg, dw_gate, dw_proj


def _loss_head(name, h, g_row, target):
    d = h.shape[1]

    def fn(x, tg, g):
        r = lax.rsqrt(jnp.mean(x * x, axis=-1, keepdims=True) + RMS_EPS)
        out = x * r * g
        e = out - tg
        per_row = jnp.sum(e * e, axis=-1, keepdims=True) * (0.5 / d)
        loss = jnp.zeros((1, LANES), F32) + jnp.sum(per_row, axis=0, keepdims=True)
        dx, dg = _rms_bwd_vals(x, g, e * (1.0 / d))
        return dx, loss, dg

    return _rowwise(name, fn, [h, target], [(d, F32)], consts=[g_row], reds=[((1, LANES), F32), ((1, d), F32)])


_BIG = ("w_ff1_in", "w_ff1_out", "w_in", "w_attn_out", "w_conv_out", "w_out", "w_ff2_in", "w_ff2_out",
        "w_ple_gate", "w_ple_proj")
_SMALL = ("g_ff1", "g_mix", "b_f", "conv_w", "conv_b", "g_conv", "g_ff2", "g_ple")
_ORDER = ("g_ff1", "w_ff1_in", "w_ff1_out", "g_mix", "w_in", "b_f", "w_attn_out", "conv_w", "conv_b", "g_conv",
          "w_conv_out", "w_out", "g_ff2", "w_ff2_in", "w_ff2_out", "g_ple", "w_ple_gate", "w_ple_proj", "g_final")


def _round_up(n, k):
    return (n + k - 1) // k * k


def _unpack_layer(gathered, dm):
    g = dict(zip(_BIG, gathered))
    d = dm.d
    w_in = jnp.transpose(g["w_in"], (1, 0, 2)).reshape(d, -1)
    o_f = 3 * dm.aw
    o_c = o_f + dm.nh
    w_main = jnp.concatenate([w_in[:, o_c:], w_in[:, :o_f]], axis=1)
    w_f = jnp.pad(w_in[:, o_f:o_c], ((0, 0), (0, LANES - dm.nh)))
    flat = lambda a: a.reshape(1, a.shape[0] * a.shape[1], a.shape[2])
    return dict(w_ff1_in=g["w_ff1_in"], w_ff1_out=flat(g["w_ff1_out"]), w_main=w_main[None], w_f=w_f[None],
                w_attn_out=g["w_attn_out"], w_conv_out=g["w_conv_out"], w_out=flat(g["w_out"]),
                w_ff2_in=g["w_ff2_in"], w_ff2_out=flat(g["w_ff2_out"]), w_ple_gate=flat(g["w_ple_gate"]),
                w_ple_proj=g["w_ple_proj"])


def _pack_grads(gw, dm):
    d = dm.d
    o_f = 3 * dm.aw
    main, wf = gw["w_main"][0], gw["w_f"][0]
    n_rest = dm.n_main - o_f
    w_in = jnp.concatenate([main[:, n_rest:], wf[:, :dm.nh], main[:, :n_rest]], axis=1)
    w_in = jnp.transpose(w_in.reshape(d, N_CHIPS, -1), (1, 0, 2))
    split = lambda a: a.reshape(N_CHIPS, a.shape[1] // N_CHIPS, a.shape[2])
    out = dict(w_ff1_in=gw["w_ff1_in"], w_ff1_out=split(gw["w_ff1_out"]), w_in=w_in, w_attn_out=gw["w_attn_out"],
               w_conv_out=gw["w_conv_out"], w_out=split(gw["w_out"]), w_ff2_in=gw["w_ff2_in"],
               w_ff2_out=split(gw["w_ff2_out"]), w_ple_gate=split(gw["w_ple_gate"]), w_ple_proj=gw["w_ple_proj"])
    return [out[n] for n in _BIG]


def kernel(x, p, g_ff1, w_ff1_in, w_ff1_out, g_mix, w_in, b_f, w_attn_out, conv_w, conv_b, g_conv, w_conv_out, w_out, g_ff2, w_ff2_in, w_ff2_out, g_ple, w_ple_gate, w_ple_proj, g_final, loss_target, m_g_ff1, m_w_ff1_in, m_w_ff1_out, m_g_mix, m_w_in, m_b_f, m_w_attn_out, m_conv_w, m_conv_b, m_g_conv, m_w_conv_out, m_w_out, m_g_ff2, m_w_ff2_in, m_w_ff2_out, m_g_ple, m_w_ple_gate, m_w_ple_proj, m_g_final, v_g_ff1, v_w_ff1_in, v_w_ff1_out, v_g_mix, v_w_in, v_b_f, v_w_attn_out, v_conv_w, v_conv_b, v_g_conv, v_w_conv_out, v_w_out, v_g_ff2, v_w_ff2_in, v_w_ff2_out, v_g_ple, v_w_ple_gate, v_w_ple_proj, v_g_final):
    args = dict(locals())
    wts = {n: args[n] for n in _ORDER}
    mom = {n: args["m_" + n] for n in _ORDER}
    var = {n: args["v_" + n] for n in _ORDER}

    nl = g_ff1.shape[0]
    s, d = x.shape[1], x.shape[2]
    nh = b_f.shape[1]
    taps = conv_w.shape[1]
    cc = conv_b.shape[1]
    dm = _Dims(d=d, f=w_ff1_out.shape[1] * N_CHIPS, aw=w_attn_out.shape[1], nh=nh, cc=cc, taps=taps,
               dp=w_ple_proj.shape[1], s=s)
    assert taps - 1 <= HALO

    xi = lax.axis_index("x")
    yi = lax.axis_index("y")
    ci = lax.axis_index("c")
    c_arr = jnp.reshape(ci, (1,)).astype(jnp.int32)
    chip_arr = jnp.reshape(2 * xi + yi, (1,)).astype(jnp.int32)
    place_arr = jnp.stack([2 * xi + yi, ci]).astype(jnp.int32)

    h = x[0]
    target = loss_target[0]
    p_b = p[:, 0].astype(BF16)

    cw_rows = _round_up(nl * taps, 8)
    cw_blk = jnp.pad(conv_w.reshape(nl * taps, -1), ((0, cw_rows - nl * taps), (0, 0)))
    cw_all = _allgather_blocks("gather_conv_w", cw_blk).reshape(N_CHIPS, 2, cw_rows, -1)[:, 0, :nl * taps]
    conv_w_full = jnp.transpose(cw_all.reshape(N_CHIPS, nl, taps, -1), (1, 2, 0, 3)).reshape(nl, taps, cc)
    taps_pad = _round_up(taps, 8)

    def small_of(i):
        row = lambda a: a[i][None, :]
        return dict(g_ff1=row(g_ff1), g_mix=row(g_mix), g_conv=row(g_conv), conv_b=row(conv_b), g_ff2=row(g_ff2),
                    g_ple=row(g_ple), b_f=jnp.pad(b_f[i][None, :], ((0, 0), (0, LANES - nh))),
                    conv_w=jnp.pad(conv_w_full[i], ((0, taps_pad - taps), (0, 0))))

    slots = [_cast_slots(f"cast_{n}", wts[n], chip_arr) for n in _BIG]
    started = {}

    def start_gather(i):
        if i < nl:
            started[i] = _gather_start(f"gather_start_l{i}", [slots[t][i] for t in range(len(_BIG))],
                                       after=[conv_w_full])

    start_gather(0)
    start_gather(1)
    layer_w, saved = [], []
    for i in range(nl):
        send, recv, bufs, _ = started.pop(i)
        bufs = _gather_wait(f"gather_wait_l{i}", bufs, send, recv, h)
        wt = _unpack_layer(_gather_forward(f"gather_fwd_l{i}", bufs), dm)
        start_gather(i + 2)
        in_flight = [st[3] for st in started.values()]
        sm = small_of(i)
        h, sv1 = _ffn_fwd(f"l{i}_ff1", h, sm["g_ff1"], wt["w_ff1_in"], wt["w_ff1_out"], dm, deps=in_flight)
        h, sv2 = _mixer_fwd(f"l{i}_mix", h, sm, wt, dm)
        h, sv3 = _ffn_fwd(f"l{i}_ff2", h, sm["g_ff2"], wt["w_ff2_in"], wt["w_ff2_out"], dm)
        h, sv4 = _ple_fwd(f"l{i}_ple", h, p_b[i], sm, wt, dm)
        layer_w.append((wt, sm))
        saved.append((sv1, sv2, sv3, sv4))

    dh, loss_row, dg_final = _loss_head("loss_head", h, g_final[None, :], target)

    big_grads = [None] * nl
    small_grads = [None] * nl
    leaving = None
    for i in range(nl - 1, -1, -1):
        wt, sm = layer_w[i]
        sv1, sv2, sv3, sv4 = saved[i]
        deps = [leaving[3]] if leaving is not None else []
        dh, dh_b, dg_ple, dw_gate, dw_proj = _ple_bwd(f"l{i}_ple", dh, sv4, p_b[i], sm, wt, dm, deps=deps)
        dh, dh_b, dg_ff2, dw_in2, dw_out2 = _ffn_bwd(f"l{i}_ff2", dh, dh_b, sv3, sm["g_ff2"], wt["w_ff2_in"],
                                                     wt["w_ff2_out"], dm, 1.0)
        dh, dh_b, sg, dw_main, dw_f, dw_a, dw_c, dw_o = _mixer_bwd(f"l{i}_mix", dh, dh_b, sv2, sm, wt, dm)
        dh, dh_b, dg_ff1, dw_in1, dw_out1 = _ffn_bwd(f"l{i}_ff1", dh, dh_b, sv1, sm["g_ff1"], wt["w_ff1_in"],
                                                     wt["w_ff1_out"], dm, 1.0)
        gw = dict(w_ff1_in=dw_in1, w_ff1_out=dw_out1, w_main=dw_main, w_f=dw_f, w_attn_out=dw_a, w_conv_out=dw_c,
                  w_out=dw_o, w_ff2_in=dw_in2, w_ff2_out=dw_out2, w_ple_gate=dw_gate, w_ple_proj=dw_proj)
        if leaving is not None:
            big_grads[i + 1] = _reduce_scatter_end(f"l{i + 1}", leaving, dh, place_arr)
        leaving = _reduce_scatter_begin(f"l{i}", _pack_grads(gw, dm), c_arr)
        sg.update(g_ff1=dg_ff1, g_ff2=dg_ff2, g_ple=dg_ple)
        small_grads[i] = sg
    big_grads[0] = _reduce_scatter_end("l0", leaving, dh, place_arr)
    grad_x = dh[None]

    pieces = [small_grads[i][n].reshape(-1) for i in range(nl) for n in _SMALL]
    pieces += [dg_final.reshape(-1), loss_row[0, :1]]
    flat = jnp.concatenate(pieces)
    n_flat = flat.shape[0]
    rows = _round_up(_round_up(n_flat, LANES) // LANES, 8)
    blk = jnp.pad(flat, (0, rows * LANES - n_flat)).reshape(rows, LANES)
    total = _sum_blocks("sum_small", _allgather_blocks("gather_small", blk).reshape(N_DEV, rows, LANES)).reshape(-1)
    small_tot = {n: [] for n in _SMALL}
    pos = 0
    for i in range(nl):
        for n in _SMALL:
            shp = small_grads[i][n].shape
            size = shp[0] * shp[1]
            small_tot[n].append(total[pos:pos + size].reshape(shp))
            pos += size
    g_final_tot = total[pos:pos + d]
    loss = total[pos + d]

    grads, deltas, new_m, new_v = {}, {}, {}, {}
    for t, n in enumerate(_BIG):
        grads[n], deltas[n], new_m[n], new_v[n] = _adamw_layers(
            f"adamw_{n}", wts[n], mom[n], var[n], [big_grads[i][t] for i in range(nl)])
    chip = 2 * xi + yi
    for n in _SMALL:
        g = jnp.concatenate(small_tot[n], axis=0)
        if n == "conv_w":
            cpc = cc // N_CHIPS
            g = lax.dynamic_slice_in_dim(g.reshape(nl * taps, cc), chip * cpc, cpc, axis=1)
            shape2 = (nl * taps, cpc)
        else:
            shape2 = g.shape
        dl, mm, vv = _adamw_small(f"adamw_{n}", wts[n].reshape(shape2), g, mom[n].reshape(shape2),
                                  var[n].reshape(shape2))
        grads[n] = g.reshape(wts[n].shape)
        deltas[n], new_m[n], new_v[n] = (a.reshape(wts[n].shape) for a in (dl, mm, vv))
    g2 = g_final_tot[None, :]
    dl, mm, vv = _adamw_small("adamw_g_final", g_final[None, :], g2, m_g_final[None, :], v_g_final[None, :])
    grads["g_final"] = g_final_tot
    deltas["g_final"], new_m["g_final"], new_v["g_final"] = dl[0], mm[0], vv[0]

    return (loss, grad_x, *[grads[n] for n in _ORDER], *[deltas[n] for n in _ORDER],
            *[new_m[n] for n in _ORDER], *[new_v[n] for n in _ORDER])
```

```python
import functools

import jax
import jax.numpy as jnp
from jax import lax
from jax.experimental import pallas as pl
from jax.experimental.pallas import tpu as pltpu

F32 = jnp.float32
BF16 = jnp.bfloat16
MESH = pl.DeviceIdType.MESH
ANY = pl.BlockSpec(memory_space=pl.ANY)
HBM_SPEC = pl.BlockSpec(memory_space=pltpu.HBM)
SEM_SPEC = pl.BlockSpec(memory_space=pltpu.SEMAPHORE)
EFFECT = pltpu.SideEffectType.DATAFLOW_SIDE_EFFECTING

RMS_EPS = 1e-6
FFN_RES = 0.5
ADAM_LR = 0.001
ADAM_B1 = 0.9
ADAM_B2 = 0.999
ADAM_EPS = 1e-08
ADAM_WD = 0.01
ADAM_STEP = 10

N_CHIPS = 4
N_DEV = 8
LANES = 128
HALO = 32
VMEM_LIMIT = 56 * 1024 * 1024


def _cparams():
    return pltpu.CompilerParams(vmem_limit_bytes=VMEM_LIMIT)


def _pick(n, prefs):
    for p in prefs:
        if p <= n and n % p == 0:
            return p
    return n


def _half_rows(r):
    return r // 2 if r % 32 == 0 else r


def _sig(x):
    return 1.0 / (1.0 + jnp.exp(-x))


def _rowwise(name, fn, ins, outs, consts=(), reds=(), tm=None, deps=()):
    ins = [a if isinstance(a, tuple) else (a, 0, a.shape[1]) for a in ins]
    m = ins[0][0].shape[0]
    tm = tm or _pick(m, (256, 128, 64, 32, 16, 8))
    n_in, n_c, n_o, n_r = len(ins), len(consts), len(outs), len(reds)
    n_d = len(deps)
    consts = list(consts) + list(deps)

    def body(*refs):
        in_refs = refs[:n_in + n_c]
        o_refs = refs[n_in + n_c + n_d:n_in + n_c + n_d + n_o]
        r_refs = refs[n_in + n_c + n_d + n_o:]
        res = fn(*[r[...] for r in in_refs])
        if not isinstance(res, (tuple, list)):
            res = (res,)
        for r, v in zip(o_refs, res[:n_o]):
            r[...] = v.astype(r.dtype)
        if n_r:
            @pl.when(pl.program_id(0) == 0)
            def _():
                for r in r_refs:
                    r[...] = jnp.zeros(r.shape, r.dtype)
            for r, v in zip(r_refs, res[n_o:]):
                r[...] += v.astype(r.dtype)

    in_specs = [pl.BlockSpec((tm, w), functools.partial(lambda i, cb: (i, cb), cb=cb)) for (_, cb, w) in ins]
    in_specs += [pl.BlockSpec(c.shape, lambda i: (0, 0)) for c in consts]
    out_specs = [pl.BlockSpec((tm, w), lambda i: (i, 0)) for (w, _) in outs]
    out_specs += [pl.BlockSpec(s, lambda i: (0, 0)) for (s, _) in reds]
    out_shape = [jax.ShapeDtypeStruct((m, w), d) for (w, d) in outs]
    out_shape += [jax.ShapeDtypeStruct(s, d) for (s, d) in reds]
    res = pl.pallas_call(
        body, name=name, grid=(m // tm,), in_specs=in_specs, out_specs=out_specs, out_shape=out_shape,
        compiler_params=_cparams(),
    )(*[a for (a, _, _) in ins], *consts)
    return res


def _rms_fwd(name, h, g, deps=()):
    def fn(x, gg):
        r = lax.rsqrt(jnp.mean(x * x, axis=-1, keepdims=True) + RMS_EPS)
        return x * r * gg
    return _rowwise(name, fn, [h], [(h.shape[1], BF16)], consts=[g], deps=deps)[0]


def _rms_bwd_vals(x, g, dn):
    r = lax.rsqrt(jnp.mean(x * x, axis=-1, keepdims=True) + RMS_EPS)
    xh = x * r
    dxh = dn * g
    dx = r * (dxh - xh * jnp.mean(dxh * xh, axis=-1, keepdims=True))
    dg = jnp.sum(dn * xh, axis=0, keepdims=True)
    return dx, dg


def _rms_bwd_res(name, h, g, dns, dres, oscale):
    n_dn = len(dns)

    def fn(x, *rest):
        dn = rest[0].astype(F32)
        for t in rest[1:n_dn]:
            dn = dn + t.astype(F32)
        dr, gg = rest[n_dn], rest[n_dn + 1]
        dx, dg = _rms_bwd_vals(x, gg, dn)
        dh = dr + dx
        return dh, oscale * dh, dg

    d = h.shape[1]
    return _rowwise(name, fn, [h, *dns, dres], [(d, F32), (d, BF16)], consts=[g], reds=[((1, d), F32)])


_TN_PREFS = (1408, 1024, 768, 512, 256, 128)


def _mm_nn(name, a, w, out_dtype, res=None, scale=1.0):
    m, k = a.shape
    j, _, nb = w.shape
    tm = _pick(m, (512, 256, 128))
    tn = _pick(nb, _TN_PREFS)
    tpb = nb // tn
    has_res = res is not None

    def body(a_ref, w_ref, *rest):
        o_ref = rest[-1]
        acc = jnp.dot(a_ref[...], w_ref[...], preferred_element_type=F32)
        if has_res:
            acc = rest[0][...] + scale * acc
        o_ref[...] = acc.astype(o_ref.dtype)

    in_specs = [pl.BlockSpec((tm, k), lambda n, i: (i, 0)),
                pl.BlockSpec((None, k, tn), lambda n, i: (n // tpb, 0, n % tpb))]
    args = [a, w]
    if has_res:
        in_specs.append(pl.BlockSpec((tm, tn), lambda n, i: (i, n)))
        args.append(res)
    return pl.pallas_call(
        body, name=name, grid=(j * tpb, m // tm), in_specs=in_specs,
        out_specs=pl.BlockSpec((tm, tn), lambda n, i: (i, n)),
        out_shape=jax.ShapeDtypeStruct((m, j * nb), out_dtype), compiler_params=_cparams(),
    )(*args)


def _mm_nt(name, dy, w, out_dtype):
    m, n = dy.shape
    j, k, nb = w.shape
    tm = _pick(m, (512, 256, 128))
    to = _pick(k, _TN_PREFS)
    tc = _pick(nb, (1536, 1408, 1024, 512, 256, 128))
    cpb = nb // tc
    n_red = j * cpb

    def body(dy_ref, w_ref, o_ref, acc_ref):
        r = pl.program_id(2)

        @pl.when(r == 0)
        def _():
            acc_ref[...] = jnp.zeros(acc_ref.shape, F32)

        acc_ref[...] += lax.dot_general(dy_ref[...], w_ref[...], (((1,), (1,)), ((), ())),
                                        preferred_element_type=F32)

        @pl.when(r == n_red - 1)
        def _():
            o_ref[...] = acc_ref[...].astype(o_ref.dtype)

    return pl.pallas_call(
        body, name=name, grid=(k // to, m // tm, n_red),
        in_specs=[pl.BlockSpec((tm, tc), lambda ko, i, r: (i, r)),
                  pl.BlockSpec((None, to, tc), lambda ko, i, r: (r // cpb, ko, r % cpb))],
        out_specs=pl.BlockSpec((tm, to), lambda ko, i, r: (i, ko)),
        out_shape=jax.ShapeDtypeStruct((m, k), out_dtype),
        scratch_shapes=[pltpu.VMEM((tm, to), F32)], compiler_params=_cparams(),
    )(dy, w)


def _mm_tn(name, a, dy, j):
    m, k = a.shape
    n = dy.shape[1]
    nb = n // j
    tk = _pick(k, (512, 256, 128))
    tn = _pick(nb, _TN_PREFS)
    tpb = nb // tn

    def body(a_ref, dy_ref, o_ref):
        o_ref[...] = lax.dot_general(a_ref[...], dy_ref[...], (((0,), (0,)), ((), ())),
                                     preferred_element_type=F32).astype(o_ref.dtype)

    return pl.pallas_call(
        body, name=name, grid=(k // tk, j * tpb),
        in_specs=[pl.BlockSpec((m, tk), lambda kb, nn: (0, kb)),
                  pl.BlockSpec((m, tn), lambda kb, nn: (0, nn))],
        out_specs=pl.BlockSpec((None, tk, tn), lambda kb, nn: (nn // tpb, kb, nn % tpb)),
        out_shape=jax.ShapeDtypeStruct((j, k, nb), BF16), compiler_params=_cparams(),
    )(a, dy)


def _cumsum_rows(x_ref, o_ref, blk, reverse):
    s = x_ref.shape[0]
    nblk = s // blk
    ri = lax.broadcasted_iota(jnp.int32, (blk, blk), 0)
    ci = lax.broadcasted_iota(jnp.int32, (blk, blk), 1)
    tri = jnp.where((ci >= ri) if reverse else (ci <= ri), 1.0, 0.0).astype(F32)
    carry = jnp.zeros((1, x_ref.shape[1]), F32)
    order = range(nblk - 1, -1, -1) if reverse else range(nblk)
    for b in order:
        xb = x_ref[b * blk:(b + 1) * blk, :]
        o_ref[b * blk:(b + 1) * blk, :] = jnp.dot(tri, xb, preferred_element_type=F32,
                                                  precision=lax.Precision.HIGHEST) + carry
        carry = carry + jnp.sum(xb, axis=0, keepdims=True)


def _fgate_fwd(name, zf, bf_row):
    s, w = zf.shape
    blk = _pick(s, (256, 128))

    def body(z_ref, b_ref, c_ref, ls_ref):
        v = z_ref[...] + b_ref[...]
        ls_ref[...] = jnp.minimum(v, 0.0) - jnp.log(1.0 + jnp.exp(-jnp.abs(v)))
        _cumsum_rows(ls_ref, c_ref, blk, reverse=False)

    return pl.pallas_call(
        body, name=name, out_shape=jax.ShapeDtypeStruct((s, w), F32),
        scratch_shapes=[pltpu.VMEM((s, w), F32)], compiler_params=_cparams(),
    )(zf, bf_row)


def _fgate_bwd(name, dc_q, dc_k, zf, bf_row):
    s, w = zf.shape
    blk = _pick(s, (256, 128))

    def body(dcq_ref, dck_ref, z_ref, b_ref, dz_ref, dzb_ref, db_ref, dls_ref, dc_ref):
        dc_ref[...] = dcq_ref[...] + dck_ref[...]
        _cumsum_rows(dc_ref, dls_ref, blk, reverse=True)
        dz = dls_ref[...] * _sig(-(z_ref[...] + b_ref[...]))
        dz_ref[...] = dz
        dzb_ref[...] = dz.astype(BF16)
        db_ref[...] = jnp.sum(dz, axis=0, keepdims=True)

    return pl.pallas_call(
        body, name=name,
        out_shape=[jax.ShapeDtypeStruct((s, w), F32), jax.ShapeDtypeStruct((s, w), BF16),
                   jax.ShapeDtypeStruct((1, w), F32)],
        scratch_shapes=[pltpu.VMEM((s, w), F32), pltpu.VMEM((s, w), F32)], compiler_params=_cparams(),
    )(dc_q, dc_k, zf, bf_row)


def _scores(q, k, cq_col, ck_row, scale, row0, col0):
    s = lax.dot_general(q, k, (((1,), (1,)), ((), ())), preferred_element_type=F32) * scale
    s = s + (cq_col - ck_row)
    rows = row0 + lax.broadcasted_iota(jnp.int32, s.shape, 0)
    cols = col0 + lax.broadcasted_iota(jnp.int32, s.shape, 1)
    return jnp.where(cols <= rows, s, -jnp.inf)


def _attn_fwd(name, q, k, v, c_col, c_row, tb):
    h, s, hd = q.shape
    nb = s // tb
    scale = 1.0 / float(hd) ** 0.5

    def body(q_ref, k_ref, v_ref, cq_ref, ck_ref, o_ref, lse_ref):
        i = pl.program_id(1)
        qv = q_ref[...]
        cq = cq_ref[...]

        def step(j, carry):
            m_i, l_i, acc = carry
            k0 = pl.multiple_of(j * tb, tb)
            sc = _scores(qv, k_ref[pl.ds(k0, tb), :], cq, ck_ref[j], scale, i * tb, j * tb)
            m_new = jnp.maximum(m_i, jnp.max(sc, axis=-1, keepdims=True))
            alpha = jnp.exp(m_i - m_new)
            p = jnp.exp(sc - m_new)
            l_new = alpha * l_i + jnp.sum(p, axis=-1, keepdims=True)
            acc = alpha * acc + jnp.dot(p.astype(BF16), v_ref[pl.ds(k0, tb), :], preferred_element_type=F32)
            return m_new, l_new, acc

        init = (jnp.full((tb, 1), -jnp.inf, F32), jnp.zeros((tb, 1), F32), jnp.zeros((tb, hd), F32))
        m_i, l_i, acc = lax.fori_loop(0, i + 1, step, init)
        o_ref[...] = (acc / l_i).astype(o_ref.dtype)
        lse_ref[...] = m_i + jnp.log(l_i)

    return pl.pallas_call(
        body, name=name, grid=(h, nb),
        in_specs=[pl.BlockSpec((None, tb, hd), lambda hh, i: (hh, i, 0)),
                  pl.BlockSpec((None, s, hd), lambda hh, i: (hh, 0, 0)),
                  pl.BlockSpec((None, s, hd), lambda hh, i: (hh, 0, 0)),
                  pl.BlockSpec((None, tb, 1), lambda hh, i: (hh, i, 0)),
                  pl.BlockSpec((None, nb, 1, tb), lambda hh, i: (hh, 0, 0, 0))],
        out_specs=[pl.BlockSpec((None, tb, hd), lambda hh, i: (hh, i, 0)),
                   pl.BlockSpec((None, tb, 1), lambda hh, i: (hh, i, 0))],
        out_shape=[jax.ShapeDtypeStruct((h, s, hd), F32), jax.ShapeDtypeStruct((h, s, 1), F32)],
        compiler_params=_cparams(),
    )(q, k, v, c_col, c_row)


def _attn_bwd_q(name, q, k, v, o, do, lse, c_col, c_row, tb):
    h, s, hd = q.shape
    nb = s // tb
    scale = 1.0 / float(hd) ** 0.5

    def body(q_ref, k_ref, v_ref, o_ref, do_ref, lse_ref, cq_ref, ck_ref, dq_ref, dl_ref, dcq_ref):
        i = pl.program_id(1)
        qv = q_ref[...]
        dov = do_ref[...]
        cq = cq_ref[...]
        lse_v = lse_ref[...]
        delta = jnp.sum(dov.astype(F32) * o_ref[...], axis=-1, keepdims=True)

        def step(j, carry):
            dq, dcq = carry
            k0 = pl.multiple_of(j * tb, tb)
            kj = k_ref[pl.ds(k0, tb), :]
            p = jnp.exp(_scores(qv, kj, cq, ck_ref[j], scale, i * tb, j * tb) - lse_v)
            dp = lax.dot_general(dov, v_ref[pl.ds(k0, tb), :], (((1,), (1,)), ((), ())), preferred_element_type=F32)
            ds = p * (dp - delta)
            return (dq + jnp.dot(ds.astype(BF16), kj, preferred_element_type=F32),
                    dcq + jnp.sum(ds, axis=-1, keepdims=True))

        dq, dcq = lax.fori_loop(0, i + 1, step, (jnp.zeros((tb, hd), F32), jnp.zeros((tb, 1), F32)))
        dq_ref[...] = (dq * scale).astype(dq_ref.dtype)
        dl_ref[...] = delta
        dcq_ref[...] = dcq

    blk = pl.BlockSpec((None, tb, hd), lambda hh, i: (hh, i, 0))
    full = pl.BlockSpec((None, s, hd), lambda hh, i: (hh, 0, 0))
    col = pl.BlockSpec((None, tb, 1), lambda hh, i: (hh, i, 0))
    return pl.pallas_call(
        body, name=name, grid=(h, nb),
        in_specs=[blk, full, full, blk, blk, col, col,
                  pl.BlockSpec((None, nb, 1, tb), lambda hh, i: (hh, 0, 0, 0))],
        out_specs=[blk, col, col],
        out_shape=[jax.ShapeDtypeStruct((h, s, hd), BF16), jax.ShapeDtypeStruct((h, s, 1), F32),
                   jax.ShapeDtypeStruct((h, s, 1), F32)],
        compiler_params=_cparams(),
    )(q, k, v, o, do, lse, c_col, c_row)


def _attn_bwd_kv(name, q, k, v, do, lse_row, delta_row, c_col, c_row, tb):
    h, s, hd = q.shape
    nb = s // tb
    scale = 1.0 / float(hd) ** 0.5

    def body(q_ref, k_ref, v_ref, do_ref, lse_ref, dl_ref, ck_ref, cq_ref, dk_ref, dv_ref, dc_ref):
        j = pl.program_id(1)
        kv = k_ref[...]
        vv = v_ref[...]
        ck = ck_ref[...]

        def step(i, carry):
            dk, dv, dc = carry
            q0 = pl.multiple_of(i * tb, tb)
            qi = q_ref[pl.ds(q0, tb), :]
            doi = do_ref[pl.ds(q0, tb), :]
            st = lax.dot_general(kv, qi, (((1,), (1,)), ((), ())), preferred_element_type=F32) * scale
            st = st + (cq_ref[i] - ck)
            krow = j * tb + lax.broadcasted_iota(jnp.int32, st.shape, 0)
            qcol = i * tb + lax.broadcasted_iota(jnp.int32, st.shape, 1)
            pt = jnp.exp(jnp.where(krow <= qcol, st, -jnp.inf) - lse_ref[i])
            dv = dv + jnp.dot(pt.astype(BF16), doi, preferred_element_type=F32)
            dpt = lax.dot_general(vv, doi, (((1,), (1,)), ((), ())), preferred_element_type=F32)
            dst = pt * (dpt - dl_ref[i])
            dk = dk + jnp.dot(dst.astype(BF16), qi, preferred_element_type=F32)
            dc = dc - jnp.sum(dst, axis=-1, keepdims=True)
            return dk, dv, dc

        init = (jnp.zeros((tb, hd), F32), jnp.zeros((tb, hd), F32), jnp.zeros((tb, 1), F32))
        dk, dv, dc = lax.fori_loop(j, nb, step, init)
        dk_ref[...] = (dk * scale).astype(dk_ref.dtype)
        dv_ref[...] = dv.astype(dv_ref.dtype)
        dc_ref[...] = dc

    blk = pl.BlockSpec((None, tb, hd), lambda hh, jj: (hh, jj, 0))
    full = pl.BlockSpec((None, s, hd), lambda hh, jj: (hh, 0, 0))
    col = pl.BlockSpec((None, tb, 1), lambda hh, jj: (hh, jj, 0))
    rows = pl.BlockSpec((None, nb, 1, tb), lambda hh, jj: (hh, 0, 0, 0))
    return pl.pallas_call(
        body, name=name, grid=(h, nb),
        in_specs=[full, blk, blk, full, rows, rows, col, rows],
        out_specs=[blk, blk, col],
        out_shape=[jax.ShapeDtypeStruct((h, s, hd), BF16), jax.ShapeDtypeStruct((h, s, hd), BF16),
                   jax.ShapeDtypeStruct((h, s, 1), F32)],
        compiler_params=_cparams(),
    )(q, k, v, do, lse_row, delta_row, c_col, c_row)


def _pair_masks(hd):
    lane = lax.broadcasted_iota(jnp.int32, (1, 2 * hd), 1)
    return lane < hd, lane >= hd


def _only(mask, a):
    return jnp.where(mask, a, jnp.zeros_like(a))


def _nt(a, b):
    return lax.dot_general(a, b, (((1,), (1,)), ((), ())), preferred_element_type=F32)


def _causal(sc, rows_are_queries):
    r = lax.broadcasted_iota(jnp.int32, sc.shape, 0)
    c = lax.broadcasted_iota(jnp.int32, sc.shape, 1)
    return jnp.where((c <= r) if rows_are_queries else (r <= c), sc, -jnp.inf)


def _pair_specs(dm, s):
    hd = dm.aw // dm.nh
    pw = 2 * hd
    assert pw == LANES and dm.o_q % pw == 0 and dm.aw % pw == 0
    return hd, pw, dm.o_q // pw, (dm.o_q + dm.aw) // pw, (dm.o_q + 2 * dm.aw) // pw


def _attn2_fwd(name, zm, c_col, c_row, dm):
    s, tb = zm.shape[0], dm.tb
    nb = s // tb
    hd, pw, qb, kb, vb = _pair_specs(dm, s)
    scale = 1.0 / float(hd) ** 0.5

    def body(q_ref, k_ref, v_ref, cq_ref, cr_ref, o_ref, ob_ref, lse_ref):
        i = pl.program_id(1)
        masks = _pair_masks(hd)
        q2 = q_ref[...]
        qe = [_only(m, q2) for m in masks]
        cq = [cq_ref[0], cq_ref[1]]

        def block(j, carry, diag):
            k0 = pl.multiple_of(j * tb, tb)
            kj = k_ref[pl.ds(k0, tb), :]
            vj = v_ref[pl.ds(k0, tb), :]
            out = []
            for e in range(2):
                m_i, l_i, acc = carry[e]
                sc = _nt(qe[e], kj) * scale + (cq[e] - cr_ref[e, j])
                if diag:
                    sc = _causal(sc, True)
                m_new = jnp.maximum(m_i, jnp.max(sc, axis=-1, keepdims=True))
                alpha = jnp.exp(m_i - m_new)
                p = jnp.exp(sc - m_new)
                l_new = alpha * l_i + jnp.sum(p, axis=-1, keepdims=True)
                acc = alpha * acc + jnp.dot(p.astype(BF16), vj, preferred_element_type=F32)
                out.append((m_new, l_new, acc))
            return tuple(out)

        one = (jnp.full((tb, 1), -jnp.inf, F32), jnp.zeros((tb, 1), F32), jnp.zeros((tb, pw), F32))
        carry = lax.fori_loop(0, i, lambda j, c: block(j, c, False), (one, one))
        (m0, l0, a0), (m1, l1, a1) = block(i, carry, True)
        o = jnp.where(masks[0], a0 / l0, a1 / l1)
        o_ref[...] = o
        ob_ref[...] = o.astype(BF16)
        lse_ref[0] = m0 + jnp.log(l0)
        lse_ref[1] = m1 + jnp.log(l1)

    blk = lambda cb: pl.BlockSpec((tb, pw), functools.partial(lambda hp, i, cb: (i, cb + hp), cb=cb))
    full = lambda cb: pl.BlockSpec((s, pw), functools.partial(lambda hp, i, cb: (0, cb + hp), cb=cb))
    col = pl.BlockSpec((2, tb, 1), lambda hp, i: (hp, i, 0))
    rows = pl.BlockSpec((2, nb, 1, tb), lambda hp, i: (hp, 0, 0, 0))
    return pl.pallas_call(
        body, name=name, grid=(dm.nh // 2, nb),
        in_specs=[blk(qb), full(kb), full(vb), col, rows],
        out_specs=[blk(0), blk(0), col],
        out_shape=[jax.ShapeDtypeStruct((s, dm.aw), F32), jax.ShapeDtypeStruct((s, dm.aw), BF16),
                   jax.ShapeDtypeStruct((dm.nh, s, 1), F32)],
        compiler_params=_cparams(),
    )(zm, zm, zm, c_col, c_row)


def _attn2_bwd_q(name, zm, o, do, lse, c_col, c_row, dm):
    s, tb = zm.shape[0], dm.tb
    nb = s // tb
    hd, pw, qb, kb, vb = _pair_specs(dm, s)
    scale = 1.0 / float(hd) ** 0.5

    def body(q_ref, k_ref, v_ref, o_ref, do_ref, lse_ref, cq_ref, cr_ref, dq_ref, dl_ref, dcq_ref):
        i = pl.program_id(1)
        masks = _pair_masks(hd)
        q2 = q_ref[...]
        do2 = do_ref[...]
        prod = do2.astype(F32) * o_ref[...]
        qe = [_only(m, q2) for m in masks]
        doe = [_only(m, do2) for m in masks]
        delta = [jnp.sum(_only(m, prod), axis=-1, keepdims=True) for m in masks]
        cq = [cq_ref[0], cq_ref[1]]
        lse_v = [lse_ref[0], lse_ref[1]]

        def block(j, carry, diag):
            k0 = pl.multiple_of(j * tb, tb)
            kj = k_ref[pl.ds(k0, tb), :]
            vj = v_ref[pl.ds(k0, tb), :]
            out = []
            for e in range(2):
                dq, dcq = carry[e]
                sc = _nt(qe[e], kj) * scale + (cq[e] - cr_ref[e, j])
                if diag:
                    sc = _causal(sc, True)
                p = jnp.exp(sc - lse_v[e])
                ds = p * (_nt(doe[e], vj) - delta[e])
                out.append((dq + jnp.dot(ds.astype(BF16), kj, preferred_element_type=F32),
                            dcq + jnp.sum(ds, axis=-1, keepdims=True)))
            return tuple(out)

        one = (jnp.zeros((tb, pw), F32), jnp.zeros((tb, 1), F32))
        carry = lax.fori_loop(0, i, lambda j, c: block(j, c, False), (one, one))
        (dq0, dc0), (dq1, dc1) = block(i, carry, True)
        dq_ref[...] = (jnp.where(masks[0], dq0, dq1) * scale).astype(dq_ref.dtype)
        dl_ref[0] = delta[0]
        dl_ref[1] = delta[1]
        dcq_ref[0] = dc0
        dcq_ref[1] = dc1

    blk = lambda cb: pl.BlockSpec((tb, pw), functools.partial(lambda hp, i, cb: (i, cb + hp), cb=cb))
    full = lambda cb: pl.BlockSpec((s, pw), functools.partial(lambda hp, i, cb: (0, cb + hp), cb=cb))
    col = pl.BlockSpec((2, tb, 1), lambda hp, i: (hp, i, 0))
    rows = pl.BlockSpec((2, nb, 1, tb), lambda hp, i: (hp, 0, 0, 0))
    return pl.pallas_call(
        body, name=name, grid=(dm.nh // 2, nb),
        in_specs=[blk(qb), full(kb), full(vb), blk(0), blk(0), col, col, rows],
        out_specs=[blk(0), col, col],
        out_shape=[jax.ShapeDtypeStruct((s, dm.aw), BF16), jax.ShapeDtypeStruct((dm.nh, s, 1), F32),
                   jax.ShapeDtypeStruct((dm.nh, s, 1), F32)],
        compiler_params=_cparams(),
    )(zm, zm, zm, o, do, lse, c_col, c_row)


def _attn2_bwd_kv(name, zm, do, lse_row, delta_row, c_col, c_row, dm):
    s, tb = zm.shape[0], dm.tb
    nb = s // tb
    hd, pw, qb, kb, vb = _pair_specs(dm, s)
    scale = 1.0 / float(hd) ** 0.5

    def body(q_ref, k_ref, v_ref, do_ref, lse_ref, dl_ref, ck_ref, cr_ref, dk_ref, dv_ref, dc_ref):
        j = pl.program_id(1)
        masks = _pair_masks(hd)
        k2 = k_ref[...]
        v2 = v_ref[...]
        ke = [_only(m, k2) for m in masks]
        ve = [_only(m, v2) for m in masks]
        ck = [ck_ref[0], ck_ref[1]]

        def block(i, carry, diag):
            q0 = pl.multiple_of(i * tb, tb)
            qi = q_ref[pl.ds(q0, tb), :]
            doi = do_ref[pl.ds(q0, tb), :]
            out = []
            for e in range(2):
                dk, dv, dc = carry[e]
                st = _nt(ke[e], qi) * scale + (cr_ref[e, i] - ck[e])
                if diag:
                    st = _causal(st, False)
                pt = jnp.exp(st - lse_ref[e, i])
                dv = dv + jnp.dot(pt.astype(BF16), doi, preferred_element_type=F32)
                dst = pt * (_nt(ve[e], doi) - dl_ref[e, i])
                dk = dk + jnp.dot(dst.astype(BF16), qi, preferred_element_type=F32)
                out.append((dk, dv, dc - jnp.sum(dst, axis=-1, keepdims=True)))
            return tuple(out)

        one = (jnp.zeros((tb, pw), F32), jnp.zeros((tb, pw), F32), jnp.zeros((tb, 1), F32))
        carry = block(j, (one, one), True)
        (dk0, dv0, dc0), (dk1, dv1, dc1) = lax.fori_loop(j + 1, nb, lambda i, c: block(i, c, False), carry)
        dk_ref[...] = (jnp.where(masks[0], dk0, dk1) * scale).astype(dk_ref.dtype)
        dv_ref[...] = jnp.where(masks[0], dv0, dv1).astype(dv_ref.dtype)
        dc_ref[0] = dc0
        dc_ref[1] = dc1

    blk = lambda cb: pl.BlockSpec((tb, pw), functools.partial(lambda hp, jj, cb: (jj, cb + hp), cb=cb))
    full = lambda cb: pl.BlockSpec((s, pw), functools.partial(lambda hp, jj, cb: (0, cb + hp), cb=cb))
    col = pl.BlockSpec((2, tb, 1), lambda hp, jj: (hp, jj, 0))
    rows = pl.BlockSpec((2, nb, 1, tb), lambda hp, jj: (hp, 0, 0, 0))
    return pl.pallas_call(
        body, name=name, grid=(dm.nh // 2, nb),
        in_specs=[full(qb), blk(kb), blk(vb), full(0), rows, rows, col, rows],
        out_specs=[blk(0), blk(0), col],
        out_shape=[jax.ShapeDtypeStruct((s, dm.aw), BF16), jax.ShapeDtypeStruct((s, dm.aw), BF16),
                   jax.ShapeDtypeStruct((dm.nh, s, 1), F32)],
        compiler_params=_cparams(),
    )(zm, zm, zm, do, lse_row, delta_row, c_col, c_row)


def _glu(cv, cc):
    c1 = cv[:, :cc].astype(F32)
    c2 = cv[:, cc:].astype(F32)
    return c1 * _sig(c2)


def _conv_fwd(name, zm, w_pad, b_row, g_row, cc, taps):
    s = zm.shape[0]
    tr = _pick(s, (256, 128))
    hpb = tr // HALO
    off = HALO - (taps - 1)

    def body(cur_ref, halo_ref, w_ref, b_ref, g_ref, y_ref, cs_ref, apad):
        i = pl.program_id(0)
        apad[0:HALO, :] = _glu(halo_ref[...], cc) * jnp.where(i > 0, 1.0, 0.0)
        apad[HALO:, :] = _glu(cur_ref[...], cc)
        acc = jnp.zeros((tr, cc), F32) + b_ref[...]
        for t in range(taps):
            acc = acc + w_ref[t:t + 1, :] * apad[off + t:off + t + tr, :]
        y_ref[...] = acc
        r = lax.rsqrt(jnp.mean(acc * acc, axis=-1, keepdims=True) + RMS_EPS)
        n = acc * r * g_ref[...]
        cs_ref[...] = (n * _sig(n)).astype(cs_ref.dtype)

    return pl.pallas_call(
        body, name=name, grid=(s // tr,),
        in_specs=[pl.BlockSpec((tr, 2 * cc), lambda i: (i, 0)),
                  pl.BlockSpec((HALO, 2 * cc), lambda i: (jnp.maximum(i * hpb - 1, 0), 0)),
                  pl.BlockSpec(w_pad.shape, lambda i: (0, 0)),
                  pl.BlockSpec(b_row.shape, lambda i: (0, 0)),
                  pl.BlockSpec(g_row.shape, lambda i: (0, 0))],
        out_specs=[pl.BlockSpec((tr, cc), lambda i: (i, 0)), pl.BlockSpec((tr, cc), lambda i: (i, 0))],
        out_shape=[jax.ShapeDtypeStruct((s, cc), F32), jax.ShapeDtypeStruct((s, cc), BF16)],
        scratch_shapes=[pltpu.VMEM((HALO + tr, cc), F32)], compiler_params=_cparams(),
    )(zm, zm, w_pad, b_row, g_row)


def _conv_bwd(name, zm, y, dcs, w_pad, g_row, cc, taps):
    s = zm.shape[0]
    tr = _pick(s, (256, 128))
    hpb = tr // HALO
    nblk = s // tr
    off = HALO - (taps - 1)

    def dy_of(yv, dcsv, g):
        r = lax.rsqrt(jnp.mean(yv * yv, axis=-1, keepdims=True) + RMS_EPS)
        xh = yv * r
        n = xh * g
        sg = _sig(n)
        dn = dcsv.astype(F32) * (sg * (1.0 + n * (1.0 - sg)))
        dxh = dn * g
        dy = r * (dxh - xh * jnp.mean(dxh * xh, axis=-1, keepdims=True))
        return dy, dn * xh

    def body(cur_ref, halo_ref, y_ref, yn_ref, dcs_ref, dcsn_ref, w_ref, g_ref,
             dcv_ref, dw_ref, db_ref, dg_ref, apad, dypad):
        i = pl.program_id(0)

        @pl.when(i == 0)
        def _():
            dw_ref[...] = jnp.zeros(dw_ref.shape, F32)
            db_ref[...] = jnp.zeros(db_ref.shape, F32)
            dg_ref[...] = jnp.zeros(dg_ref.shape, F32)

        g = g_ref[...]
        apad[0:HALO, :] = _glu(halo_ref[...], cc) * jnp.where(i > 0, 1.0, 0.0)
        apad[HALO:, :] = _glu(cur_ref[...], cc)
        dy, dgt = dy_of(y_ref[...], dcs_ref[...], g)
        dyn, _ = dy_of(yn_ref[...], dcsn_ref[...], g)
        dypad[0:tr, :] = dy
        dypad[tr:, :] = dyn * jnp.where(i < nblk - 1, 1.0, 0.0)
        db_ref[...] += jnp.sum(dy, axis=0, keepdims=True)
        dg_ref[...] += jnp.sum(dgt, axis=0, keepdims=True)
        da = jnp.zeros((tr, cc), F32)
        for t in range(taps):
            da = da + w_ref[t:t + 1, :] * dypad[taps - 1 - t:taps - 1 - t + tr, :]
            dw_ref[t:t + 1, :] += jnp.sum(dy * apad[off + t:off + t + tr, :], axis=0, keepdims=True)
        cv = cur_ref[...]
        c1 = cv[:, :cc].astype(F32)
        sg = _sig(cv[:, cc:].astype(F32))
        dcv_ref[:, :cc] = (da * sg).astype(dcv_ref.dtype)
        dcv_ref[:, cc:] = (da * c1 * sg * (1.0 - sg)).astype(dcv_ref.dtype)

    nxt = lambda i: (jnp.minimum((i + 1) * hpb, s // HALO - 1), 0)
    return pl.pallas_call(
        body, name=name, grid=(nblk,),
        in_specs=[pl.BlockSpec((tr, 2 * cc), lambda i: (i, 0)),
                  pl.BlockSpec((HALO, 2 * cc), lambda i: (jnp.maximum(i * hpb - 1, 0), 0)),
                  pl.BlockSpec((tr, cc), lambda i: (i, 0)), pl.BlockSpec((HALO, cc), nxt),
                  pl.BlockSpec((tr, cc), lambda i: (i, 0)), pl.BlockSpec((HALO, cc), nxt),
                  pl.BlockSpec(w_pad.shape, lambda i: (0, 0)), pl.BlockSpec(g_row.shape, lambda i: (0, 0))],
        out_specs=[pl.BlockSpec((tr, 2 * cc), lambda i: (i, 0)),
                   pl.BlockSpec(w_pad.shape, lambda i: (0, 0)),
                   pl.BlockSpec((1, cc), lambda i: (0, 0)), pl.BlockSpec((1, cc), lambda i: (0, 0))],
        out_shape=[jax.ShapeDtypeStruct((s, 2 * cc), BF16), jax.ShapeDtypeStruct(w_pad.shape, F32),
                   jax.ShapeDtypeStruct((1, cc), F32), jax.ShapeDtypeStruct((1, cc), F32)],
        scratch_shapes=[pltpu.VMEM((HALO + tr, cc), F32), pltpu.VMEM((tr + HALO, cc), F32)],
        compiler_params=_cparams(),
    )(zm, zm, y, y, dcs, dcs, w_pad, g_row)


def _place():
    x, y, c = lax.axis_index("x"), lax.axis_index("y"), lax.axis_index("c")
    chips = [(1 - x, y), (x, 1 - y), (1 - x, 1 - y)]
    return x, y, c, chips


def _half(c, rows):
    rh = rows // 2
    return pl.ds(pl.multiple_of(c * rh, 16), rh)


def _remote(src, dst, send, recv, dev):
    return pltpu.make_async_remote_copy(src_ref=src, dst_ref=dst, send_sem=send, recv_sem=recv,
                                        device_id=dev, device_id_type=MESH)


def _cast_slots(name, w, chip_arr):
    nl, r, cdim = w.shape
    tr = _half_rows(r)
    n = r // tr

    def body(q_ref, w_ref, *o_refs):
        layer = pl.program_id(0)
        for j, o_ref in enumerate(o_refs):
            @pl.when(layer == j)
            def _(o_ref=o_ref):
                o_ref[...] = w_ref[...].astype(o_ref.dtype)

    def out_map(j):
        return lambda l, i, q: (q[0], jnp.where(l < j, 0, jnp.where(l == j, i, n - 1)), 0)

    return pl.pallas_call(
        body, name=name,
        grid_spec=pltpu.PrefetchScalarGridSpec(
            num_scalar_prefetch=1, grid=(nl, n),
            in_specs=[pl.BlockSpec((None, tr, cdim), lambda l, i, q: (l, i, 0))],
            out_specs=[pl.BlockSpec((None, tr, cdim), out_map(j)) for j in range(nl)]),
        out_shape=[jax.ShapeDtypeStruct((N_CHIPS, r, cdim), BF16)] * nl, compiler_params=_cparams(),
    )(chip_arr, w)


def _split_start(name, n_sems, bufs, issue, after=()):
    nb = len(bufs)
    n_in = nb + len(after)

    def body(*refs):
        issue(refs[:nb], refs[n_in], refs[n_in + 1])
        refs[-1][...] = jnp.zeros(refs[-1].shape, F32)

    outs = pl.pallas_call(
        body, name=name, in_specs=[HBM_SPEC] * nb + [ANY] * len(after),
        out_shape=(pltpu.SemaphoreType.DMA(n_sems), pltpu.SemaphoreType.DMA(n_sems),
                   *[pltpu.HBM(b.shape, b.dtype) for b in bufs], jax.ShapeDtypeStruct((8, LANES), F32)),
        out_specs=(SEM_SPEC, SEM_SPEC, *[HBM_SPEC] * nb, pl.BlockSpec(memory_space=pltpu.VMEM)),
        input_output_aliases={t: t + 2 for t in range(nb)},
        compiler_params=pltpu.CompilerParams(has_side_effects=EFFECT),
    )(*[pltpu.with_memory_space_constraint(b, pltpu.HBM) for b in bufs], *after)
    return outs[0], outs[1], list(outs[2:2 + nb]), outs[-1]


def _split_wait(name, bufs, send, recv, after, drain):
    nb = len(bufs)

    def body(*refs):
        drain(refs[:nb], refs[nb], refs[nb + 1])

    return list(pl.pallas_call(
        body, name=name, in_specs=[HBM_SPEC] * nb + [SEM_SPEC, SEM_SPEC, ANY],
        out_shape=tuple(pltpu.HBM(b.shape, b.dtype) for b in bufs), out_specs=tuple([HBM_SPEC] * nb),
        input_output_aliases={t: t for t in range(nb)},
        compiler_params=pltpu.CompilerParams(has_side_effects=EFFECT),
    )(*bufs, send, recv, after))


def _gather_start(name, bufs, after=()):
    nt = len(bufs)

    def issue(g, send, recv):
        x, y, c, chips = _place()
        me = 2 * x + y
        for t in range(nt):
            part = g[t].at[me, _half(c, bufs[t].shape[1]), :]
            for j, (qx, qy) in enumerate(chips):
                _remote(part, part, send.at[3 * t + j], recv.at[3 * t + j], (qx, qy, c)).start()

    return _split_start(name, (3 * nt,), bufs, issue, after)


def _gather_wait(name, bufs, send, recv, after):
    nt = len(bufs)

    def drain(g, send, recv):
        x, y, c, _ = _place()
        for t in range(nt):
            part = g[t].at[0, pl.ds(0, bufs[t].shape[1] // 2), :]
            for j in range(3):
                cp = _remote(part, part, send.at[3 * t + j], recv.at[3 * t + j], (x, y, c))
                cp.wait_send()
                cp.wait_recv()

    return _split_wait(name, bufs, send, recv, after, drain)


def _gather_forward(name, bufs):
    nt = len(bufs)

    def body(*refs):
        g = refs[nt:2 * nt]
        send, recv = refs[2 * nt:]
        x, y, c, chips = _place()
        cps = []
        for t in range(nt):
            for j, (qx, qy) in enumerate(chips):
                part = g[t].at[2 * qx + qy, _half(c, bufs[t].shape[1]), :]
                cps.append(_remote(part, part, send.at[t, j], recv.at[t, j], (x, y, 1 - c)))
        for cp in cps:
            cp.start()
        for t in range(nt):
            for j, (qx, qy) in enumerate(chips):
                theirs = g[t].at[2 * qx + qy, _half(1 - c, bufs[t].shape[1]), :]
                _remote(theirs, theirs, send.at[t, j], recv.at[t, j], (x, y, 1 - c)).wait_recv()
        for cp in cps:
            cp.wait_send()

    return list(pl.pallas_call(
        body, name=name, in_specs=[ANY] * nt, out_specs=[ANY] * nt,
        out_shape=[jax.ShapeDtypeStruct(b.shape, b.dtype) for b in bufs],
        input_output_aliases={t: t for t in range(nt)},
        scratch_shapes=[pltpu.SemaphoreType.DMA((nt, 3)), pltpu.SemaphoreType.DMA((nt, 3))],
        compiler_params=pltpu.CompilerParams(has_side_effects=True),
    )(*bufs))


def _swap_halves(name, grads):
    nt = len(grads)

    def body(*refs):
        g_refs, r_refs = refs[:nt], refs[nt:2 * nt]
        send, recv = refs[2 * nt:]
        x, y, c, _ = _place()
        cps = [_remote(g_refs[t].at[:, _half(1 - c, grads[t].shape[1]), :], r_refs[t], send.at[t], recv.at[t],
                       (x, y, 1 - c)) for t in range(nt)]
        for cp in cps:
            cp.start()
        for cp in cps:
            cp.wait()

    return pl.pallas_call(
        body, name=name, in_specs=[ANY] * nt, out_specs=[ANY] * nt,
        out_shape=[jax.ShapeDtypeStruct((N_CHIPS, g.shape[1] // 2, g.shape[2]), g.dtype) for g in grads],
        scratch_shapes=[pltpu.SemaphoreType.DMA((nt,)), pltpu.SemaphoreType.DMA((nt,))],
        compiler_params=pltpu.CompilerParams(has_side_effects=True),
    )(*grads)


def _scatter_start(name, parts, after=()):
    nt = len(parts)
    lands = [lax.empty((3,) + p.shape[1:], p.dtype) for p in parts]

    def issue(refs, send, recv):
        x, y, c, chips = _place()
        for t in range(nt):
            for j, (qx, qy) in enumerate(chips):
                _remote(refs[t].at[2 * qx + qy], refs[nt + t].at[j], send.at[3 * t + j], recv.at[3 * t + j],
                        (qx, qy, c)).start()

    return _split_start(name, (3 * nt,), list(parts) + lands, issue, after)


def _scatter_wait(name, bufs, send, recv, after):
    nt = len(bufs) // 2

    def drain(refs, send, recv):
        x, y, c, _ = _place()
        for t in range(nt):
            for j in range(3):
                cp = _remote(refs[t].at[0], refs[nt + t].at[j], send.at[3 * t + j], recv.at[3 * t + j], (x, y, c))
                cp.wait_send()
                cp.wait_recv()

    return _split_wait(name, bufs, send, recv, after, drain)


def _join_halves(name, fulls):
    nt = len(fulls)

    def body(*refs):
        o_refs = refs[nt:2 * nt]
        send, recv = refs[2 * nt:]
        x, y, c, _ = _place()
        cps = []
        for t in range(nt):
            half = o_refs[t].at[_half(c, fulls[t].shape[0]), :]
            cps.append(_remote(half, half, send.at[t], recv.at[t], (x, y, 1 - c)))
        for cp in cps:
            cp.start()
        for t in range(nt):
            theirs = o_refs[t].at[_half(1 - c, fulls[t].shape[0]), :]
            _remote(theirs, theirs, send.at[t], recv.at[t], (x, y, 1 - c)).wait_recv()
        for cp in cps:
            cp.wait_send()

    return list(pl.pallas_call(
        body, name=name, in_specs=[ANY] * nt, out_specs=[ANY] * nt,
        out_shape=[jax.ShapeDtypeStruct(a.shape, a.dtype) for a in fulls],
        input_output_aliases={t: t for t in range(nt)},
        scratch_shapes=[pltpu.SemaphoreType.DMA((nt,)), pltpu.SemaphoreType.DMA((nt,))],
        compiler_params=pltpu.CompilerParams(has_side_effects=True),
    )(*fulls))


def _add_own_half(name, grad, recv_half, c_arr):
    _, r, cdim = grad.shape
    rh = r // 2
    tr = _half_rows(rh)
    nrb = rh // tr

    def body(c_ref, g_ref, r_ref, o_ref):
        o_ref[...] = (g_ref[...].astype(F32) + r_ref[...].astype(F32)).astype(o_ref.dtype)

    return pl.pallas_call(
        body, name=name,
        grid_spec=pltpu.PrefetchScalarGridSpec(
            num_scalar_prefetch=1, grid=(N_CHIPS, nrb),
            in_specs=[pl.BlockSpec((None, tr, cdim), lambda q, i, cr: (q, cr[0] * nrb + i, 0)),
                      pl.BlockSpec((None, tr, cdim), lambda q, i, cr: (q, i, 0))],
            out_specs=pl.BlockSpec((None, tr, cdim), lambda q, i, cr: (q, i, 0))),
        out_shape=jax.ShapeDtypeStruct((N_CHIPS, rh, cdim), BF16), compiler_params=_cparams(),
    )(c_arr, grad, recv_half)


def _add_chips(name, part, came, place_arr):
    _, rh, cdim = part.shape
    tr = _half_rows(rh)
    nrb = rh // tr

    def body(q_ref, p_ref, r_ref, o_ref):
        acc = p_ref[...].astype(F32)
        for j in range(3):
            acc = acc + r_ref[j].astype(F32)
        o_ref[...] = acc

    return pl.pallas_call(
        body, name=name,
        grid_spec=pltpu.PrefetchScalarGridSpec(
            num_scalar_prefetch=1, grid=(nrb,),
            in_specs=[pl.BlockSpec((None, tr, cdim), lambda i, qr: (qr[0], i, 0)),
                      pl.BlockSpec((3, tr, cdim), lambda i, qr: (0, i, 0))],
            out_specs=pl.BlockSpec((tr, cdim), lambda i, qr: (qr[1] * nrb + i, 0))),
        out_shape=jax.ShapeDtypeStruct((2 * rh, cdim), F32), compiler_params=_cparams(),
    )(place_arr, part, came)


def _reduce_scatter_begin(tag, grads, c_arr, after=()):
    nt = len(grads)
    got = _swap_halves(f"rs_swap_{tag}", grads)
    parts = [_add_own_half(f"rs_add2_{tag}_{t}", grads[t], got[t], c_arr) for t in range(nt)]
    return _scatter_start(f"rs_scatter_start_{tag}", parts, after)


def _reduce_scatter_end(tag, state, after, place_arr):
    send, recv, bufs, _ = state
    nt = len(bufs) // 2
    bufs = _scatter_wait(f"rs_scatter_wait_{tag}", bufs, send, recv, after)
    fulls = [_add_chips(f"rs_add4_{tag}_{t}", bufs[t], bufs[nt + t], place_arr) for t in range(nt)]
    return _join_halves(f"rs_join_{tag}", fulls)


def _allgather_blocks(name, blk):
    m_per, n = blk.shape

    def body(x_ref, out_ref, send_sems, recv_sems, local_sem):
        x, y, c, chips = _place()
        me, sibling = (x, y, c), (x, y, 1 - c)

        def rows(px, py, pc):
            return out_ref.at[pl.ds(pl.multiple_of((4 * px + 2 * py + pc) * m_per, 8), m_per), :]

        def copy(k, block, to, src=None):
            return _remote(rows(*block) if src is None else src, rows(*block), send_sems.at[k], recv_sems.at[k], to)

        mine = pltpu.make_async_copy(x_ref, rows(*me), local_sem)
        mine.start()
        first = [copy(0, me, sibling, src=x_ref)]
        first += [copy(1 + j, me, (*chip, c), src=x_ref) for j, chip in enumerate(chips)]
        for cp in first:
            cp.start()
        passed = [copy(4 + j, (*chip, c), sibling) for j, chip in enumerate(chips)]
        for j, chip in enumerate(chips):
            copy(1 + j, (*chip, c), me).wait_recv()
            passed[j].start()
        copy(0, sibling, me).wait_recv()
        for j, chip in enumerate(chips):
            copy(4 + j, (*chip, 1 - c), me).wait_recv()
        for cp in first + passed:
            cp.wait_send()
        mine.wait()

    return pl.pallas_call(
        body, name=name, out_shape=jax.ShapeDtypeStruct((N_DEV * m_per, n), blk.dtype),
        in_specs=[pl.BlockSpec(memory_space=pltpu.VMEM)], out_specs=pl.BlockSpec(memory_space=pltpu.VMEM),
        scratch_shapes=[pltpu.SemaphoreType.DMA((7,)), pltpu.SemaphoreType.DMA((7,)), pltpu.SemaphoreType.DMA],
        compiler_params=_cparams(),
    )(blk)


def _sum_blocks(name, stacked):
    nd, m, n = stacked.shape
    tr = _pick(m, (336, 256, 168, 128, 64, 32, 16, 8))

    def body(s_ref, o_ref):
        acc = s_ref[0]
        for d in range(1, nd):
            acc = acc + s_ref[d]
        o_ref[...] = acc

    return pl.pallas_call(
        body, name=name, grid=(m // tr,),
        in_specs=[pl.BlockSpec((nd, tr, n), lambda i: (0, i, 0))],
        out_specs=pl.BlockSpec((tr, n), lambda i: (i, 0)),
        out_shape=jax.ShapeDtypeStruct((m, n), F32), compiler_params=_cparams(),
    )(stacked)


def _adamw_vals(w, g, m, v):
    m2 = ADAM_B1 * m + (1.0 - ADAM_B1) * g
    v2 = ADAM_B2 * v + (1.0 - ADAM_B2) * (g * g)
    m_hat = m2 / (1.0 - ADAM_B1 ** ADAM_STEP)
    v_hat = v2 / (1.0 - ADAM_B2 ** ADAM_STEP)
    delta = -ADAM_LR * (m_hat / (jnp.sqrt(v_hat) + ADAM_EPS) + ADAM_WD * w)
    return delta, m2, v2


def _adamw_layer(name, layer, w, m, v, g, stacked, deps=()):
    nl, r, cdim = w.shape
    tr = _pick(r, (128, 64, 32, 16, 8))
    n_d = len(deps)

    def body(w_ref, m_ref, v_ref, g_ref, *rest):
        go_ref, d_ref, mo_ref, vo_ref = rest[4 + n_d:]
        gv = g_ref[...]
        delta, m2, v2 = _adamw_vals(w_ref[...], gv, m_ref[...], v_ref[...])
        go_ref[...] = gv
        d_ref[...] = delta
        mo_ref[...] = m2
        vo_ref[...] = v2

    big = pl.BlockSpec((None, tr, cdim), lambda i: (layer, i, 0))
    return pl.pallas_call(
        body, name=name, grid=(r // tr,),
        in_specs=[big, big, big, pl.BlockSpec((tr, cdim), lambda i: (i, 0))] + [ANY] * (4 + n_d),
        out_specs=[big, big, big, big], out_shape=[jax.ShapeDtypeStruct(w.shape, F32)] * 4,
        input_output_aliases={4 + k: k for k in range(4)}, compiler_params=_cparams(),
    )(w, m, v, g, *stacked, *deps)


def _adamw_small(name, w, g, m, v):
    def body(w_ref, g_ref, m_ref, v_ref, d_ref, mo_ref, vo_ref):
        delta, m2, v2 = _adamw_vals(w_ref[...], g_ref[...], m_ref[...], v_ref[...])
        d_ref[...] = delta
        mo_ref[...] = m2
        vo_ref[...] = v2

    return pl.pallas_call(body, name=name, out_shape=[jax.ShapeDtypeStruct(w.shape, F32)] * 3,
                          compiler_params=_cparams())(w, g, m, v)


def _swiglu_fwd(name, z, f):
    def fn(zz):
        a = zz[:, :f].astype(F32)
        b = zz[:, f:].astype(F32)
        return a * _sig(a) * b
    return _rowwise(name, fn, [z], [(f, BF16)])[0]


def _swiglu_bwd(name, z, ds, f):
    def fn(zz, dd):
        a = zz[:, :f].astype(F32)
        b = zz[:, f:].astype(F32)
        d = dd.astype(F32)
        sg = _sig(a)
        da = d * b * (sg * (1.0 + a * (1.0 - sg)))
        db = d * a * sg
        return jnp.concatenate([da, db], axis=1)
    return _rowwise(name, fn, [z, ds], [(2 * f, BF16)])[0]


def _heads(a, nh):
    s, w = a.shape
    return jnp.transpose(a.reshape(s, nh, w // nh), (1, 0, 2))


def _unheads(a):
    nh, s, hd = a.shape
    return jnp.transpose(a, (1, 0, 2)).reshape(s, nh * hd)


def _as_rows(col, tb):
    nh, s, _ = col.shape
    return col.reshape(nh, s // tb, 1, tb)


class _Dims:
    def __init__(self, d, f, aw, nh, cc, taps, dp, s):
        self.d, self.f, self.aw, self.nh, self.cc, self.taps, self.dp, self.s = d, f, aw, nh, cc, taps, dp, s
        self.o_cv, self.o_ga, self.o_gc = 0, 2 * cc, 2 * cc + d
        self.o_q = 2 * cc + 2 * d
        self.n_main = self.o_q + 3 * aw
        self.tb = _pick(s, (256, 128))
        assert self.o_ga % d == 0 and self.o_q % aw == 0 and nh <= LANES


def _ffn_fwd(tag, h, g_row, w_in, w_out, dm, deps=()):
    n = _rms_fwd(f"{tag}_rms", h, g_row, deps)
    z = _mm_nn(f"{tag}_in", n, w_in, BF16)
    s = _swiglu_fwd(f"{tag}_act", z, dm.f)
    h2 = _mm_nn(f"{tag}_out", s, w_out, F32, res=h, scale=FFN_RES)
    return h2, (h, n, z, s)


def _ffn_bwd(tag, dh, dh_half_b, saved, g_row, w_in, w_out, dm, oscale):
    h, n, z, s = saved
    ds = _mm_nt(f"{tag}_dout", dh_half_b, w_out, BF16)
    dz = _swiglu_bwd(f"{tag}_dact", z, ds, dm.f)
    dw_out = _mm_tn(f"{tag}_wout", s, dh_half_b, 1)
    dw_in = _mm_tn(f"{tag}_win", n, dz, N_CHIPS)
    dn = _mm_nt(f"{tag}_din", dz, w_in, BF16)
    dh0, dh0_b, dg = _rms_bwd_res(f"{tag}_drms", h, g_row, [dn], dh, oscale)
    return dh0, dh0_b, dg, dw_in, dw_out


def _mixer_fwd(tag, h, sm, wt, dm):
    d, cc, aw, nh, tb = dm.d, dm.cc, dm.aw, dm.nh, dm.tb
    u = _rms_fwd(f"{tag}_rms", h, sm["g_mix"])
    zm = _mm_nn(f"{tag}_in", u, wt["w_main"], BF16)
    zf = _mm_nn(f"{tag}_inf", u, wt["w_f"], F32)
    c = _fgate_fwd(f"{tag}_fgate", zf, sm["b_f"])
    c_col = jnp.transpose(c[:, :nh], (1, 0))[:, :, None]
    c_row = _as_rows(c_col, tb)
    o32, o, lse = _attn2_fwd(f"{tag}_attn", zm, c_col, c_row, dm)
    ya = _mm_nn(f"{tag}_aout", o, wt["w_attn_out"], BF16)
    y, cs = _conv_fwd(f"{tag}_conv", zm, sm["conv_w"], sm["conv_b"], sm["g_conv"], cc, dm.taps)
    yc = _mm_nn(f"{tag}_cout", cs, wt["w_conv_out"], BF16)

    def merge(ga, gc, a, b):
        return _sig(ga.astype(F32)) * a.astype(F32) + _sig(gc.astype(F32)) * b.astype(F32)

    mg = _rowwise(f"{tag}_merge", merge, [(zm, dm.o_ga // d, d), (zm, dm.o_gc // d, d), ya, yc], [(d, BF16)])[0]
    h2 = _mm_nn(f"{tag}_out", mg, wt["w_out"], F32, res=h, scale=1.0)
    return h2, (h, u, zm, zf, c_col, c_row, o32, lse, o, ya, y, cs, yc, mg)


def _mixer_bwd(tag, dh, dh_b, saved, sm, wt, dm):
    d, cc, aw, nh, tb = dm.d, dm.cc, dm.aw, dm.nh, dm.tb
    h, u, zm, zf, c_col, c_row, o32, lse, o, ya, y, cs, yc, mg = saved
    dmg = _mm_nt(f"{tag}_dout", dh_b, wt["w_out"], BF16)
    dw_out = _mm_tn(f"{tag}_wout", mg, dh_b, 1)

    def unmerge(dd, ga, gc, a, b):
        dd, a, b = dd.astype(F32), a.astype(F32), b.astype(F32)
        sa, sc = _sig(ga.astype(F32)), _sig(gc.astype(F32))
        return dd * sa, dd * sc, dd * a * sa * (1.0 - sa), dd * b * sc * (1.0 - sc)

    dya, dyc, dga, dgc = _rowwise(f"{tag}_dmerge", unmerge,
                                  [dmg, (zm, dm.o_ga // d, d), (zm, dm.o_gc // d, d), ya, yc], [(d, BF16)] * 4)
    dw_a = _mm_tn(f"{tag}_waout", o, dya, N_CHIPS)
    do = _mm_nt(f"{tag}_daout", dya, wt["w_attn_out"], BF16)
    dw_c = _mm_tn(f"{tag}_wcout", cs, dyc, N_CHIPS)
    dcs = _mm_nt(f"{tag}_dcout", dyc, wt["w_conv_out"], BF16)
    dcv, dconv_w, dconv_b, dg_conv = _conv_bwd(f"{tag}_dconv", zm, y, dcs, sm["conv_w"], sm["g_conv"], cc, dm.taps)
    dq, delta, dcq_h = _attn2_bwd_q(f"{tag}_dattn_q", zm, o32, do, lse, c_col, c_row, dm)
    dk, dv, dck_h = _attn2_bwd_kv(f"{tag}_dattn_kv", zm, do, _as_rows(lse, tb), _as_rows(delta, tb),
                                  c_col, c_row, dm)
    dc = jnp.pad(jnp.transpose(dcq_h[:, :, 0], (1, 0)), ((0, 0), (0, zf.shape[1] - nh)))
    dc_k = jnp.pad(jnp.transpose(dck_h[:, :, 0], (1, 0)), ((0, 0), (0, zf.shape[1] - nh)))
    dzf, dzf_b, db_f = _fgate_bwd(f"{tag}_dfgate", dc, dc_k, zf, sm["b_f"])
    dzm = jnp.concatenate([dcv, dga, dgc, dq, dk, dv], axis=1)
    dw_main = _mm_tn(f"{tag}_win", u, dzm, 1)
    dw_f = _mm_tn(f"{tag}_winf", u, dzf_b, 1)
    du = _mm_nt(f"{tag}_din", dzm, wt["w_main"], BF16)
    du_f = _mm_nt(f"{tag}_dinf", dzf_b, wt["w_f"], BF16)
    dh0, dh0_b, dg_mix = _rms_bwd_res(f"{tag}_drms", h, sm["g_mix"], [du, du_f], dh, FFN_RES)
    small = dict(g_mix=dg_mix, b_f=db_f[:, :nh], conv_w=dconv_w[:dm.taps], conv_b=dconv_b, g_conv=dg_conv)
    return dh0, dh0_b, small, dw_main, dw_f, dw_a, dw_c, dw_out


def _ple_fwd(tag, h, p_b, sm, wt, dm):
    n = _rms_fwd(f"{tag}_rms", h, sm["g_ple"])
    gp = _mm_nn(f"{tag}_gate", n, wt["w_ple_gate"], BF16)
    pp = _mm_nn(f"{tag}_proj", p_b, wt["w_ple_proj"], BF16)

    def fn(hh, a, b):
        return hh + _sig(a.astype(F32)) * b.astype(F32)

    h2 = _rowwise(f"{tag}_mix", fn, [h, gp, pp], [(dm.d, F32)])[0]
    return h2, (h, n, gp, pp)


def _ple_bwd(tag, dh, saved, p_b, sm, wt, dm, deps=()):
    h, n, gp, pp = saved

    def fn(dd, a, b):
        gate = _sig(a.astype(F32))
        b = b.astype(F32)
        return dd * b * gate * (1.0 - gate), dd * gate

    dgp, dpp = _rowwise(f"{tag}_dmix", fn, [dh, gp, pp], [(dm.d, BF16)] * 2, deps=deps)
    dw_proj = _mm_tn(f"{tag}_wproj", p_b, dpp, N_CHIPS)
    dw_gate = _mm_tn(f"{tag}_wgate", n, dgp, 1)
    dn = _mm_nt(f"{tag}_dgate", dgp, wt["w_ple_gate"], BF16)
    dh0, dh0_b, dg = _rms_bwd_res(f"{tag}_drms", h, sm["g_ple"], [dn], dh, FFN_RES)
    return dh0, dh0_b, dg, dw_gate, dw_proj


def _loss_head(name, h, g_row, target):
    d = h.shape[1]

    def fn(x, tg, g):
        r = lax.rsqrt(jnp.mean(x * x, axis=-1, keepdims=True) + RMS_EPS)
        out = x * r * g
        e = out - tg
        per_row = jnp.sum(e * e, axis=-1, keepdims=True) * (0.5 / d)
        loss = jnp.zeros((1, LANES), F32) + jnp.sum(per_row, axis=0, keepdims=True)
        dx, dg = _rms_bwd_vals(x, g, e * (1.0 / d))
        return dx, loss, dg

    return _rowwise(name, fn, [h, target], [(d, F32)], consts=[g_row], reds=[((1, LANES), F32), ((1, d), F32)])


_BIG = ("w_ff1_in", "w_ff1_out", "w_in", "w_attn_out", "w_conv_out", "w_out", "w_ff2_in", "w_ff2_out",
        "w_ple_gate", "w_ple_proj")
_SMALL = ("g_ff1", "g_mix", "b_f", "conv_w", "conv_b", "g_conv", "g_ff2", "g_ple")
_ORDER = ("g_ff1", "w_ff1_in", "w_ff1_out", "g_mix", "w_in", "b_f", "w_attn_out", "conv_w", "conv_b", "g_conv",
          "w_conv_out", "w_out", "g_ff2", "w_ff2_in", "w_ff2_out", "g_ple", "w_ple_gate", "w_ple_proj", "g_final")


def _round_up(n, k):
    return (n + k - 1) // k * k


_ROW_SHARDED = ("w_ff1_out", "w_out", "w_ff2_out", "w_ple_gate")
_FIRST_LAYER_GROUPS = ((0, 1), (2, 3, 4, 5), (6, 7, 8, 9))


def _unpack_layer(g, dm):
    out = {}
    for n, a in g.items():
        if n == "w_in":
            w_in = jnp.transpose(a, (1, 0, 2)).reshape(dm.d, -1)
            o_f = 3 * dm.aw
            o_c = o_f + dm.nh
            out["w_main"] = jnp.concatenate([w_in[:, o_c:], w_in[:, :o_f]], axis=1)[None]
            out["w_f"] = jnp.pad(w_in[:, o_f:o_c], ((0, 0), (0, LANES - dm.nh)))[None]
        elif n in _ROW_SHARDED:
            out[n] = a.reshape(1, a.shape[0] * a.shape[1], a.shape[2])
        else:
            out[n] = a
    return out


def _pack_grads(gw, dm):
    d = dm.d
    o_f = 3 * dm.aw
    main, wf = gw["w_main"][0], gw["w_f"][0]
    n_rest = dm.n_main - o_f
    w_in = jnp.concatenate([main[:, n_rest:], wf[:, :dm.nh], main[:, :n_rest]], axis=1)
    w_in = jnp.transpose(w_in.reshape(d, N_CHIPS, -1), (1, 0, 2))
    split = lambda a: a.reshape(N_CHIPS, a.shape[1] // N_CHIPS, a.shape[2])
    out = dict(w_ff1_in=gw["w_ff1_in"], w_ff1_out=split(gw["w_ff1_out"]), w_in=w_in, w_attn_out=gw["w_attn_out"],
               w_conv_out=gw["w_conv_out"], w_out=split(gw["w_out"]), w_ff2_in=gw["w_ff2_in"],
               w_ff2_out=split(gw["w_ff2_out"]), w_ple_gate=split(gw["w_ple_gate"]), w_ple_proj=gw["w_ple_proj"])
    return [out[n] for n in _BIG]


def kernel(x, p, g_ff1, w_ff1_in, w_ff1_out, g_mix, w_in, b_f, w_attn_out, conv_w, conv_b, g_conv, w_conv_out, w_out, g_ff2, w_ff2_in, w_ff2_out, g_ple, w_ple_gate, w_ple_proj, g_final, loss_target, m_g_ff1, m_w_ff1_in, m_w_ff1_out, m_g_mix, m_w_in, m_b_f, m_w_attn_out, m_conv_w, m_conv_b, m_g_conv, m_w_conv_out, m_w_out, m_g_ff2, m_w_ff2_in, m_w_ff2_out, m_g_ple, m_w_ple_gate, m_w_ple_proj, m_g_final, v_g_ff1, v_w_ff1_in, v_w_ff1_out, v_g_mix, v_w_in, v_b_f, v_w_attn_out, v_conv_w, v_conv_b, v_g_conv, v_w_conv_out, v_w_out, v_g_ff2, v_w_ff2_in, v_w_ff2_out, v_g_ple, v_w_ple_gate, v_w_ple_proj, v_g_final):
    args = dict(locals())
    wts = {n: args[n] for n in _ORDER}
    mom = {n: args["m_" + n] for n in _ORDER}
    var = {n: args["v_" + n] for n in _ORDER}

    nl = g_ff1.shape[0]
    s, d = x.shape[1], x.shape[2]
    nh = b_f.shape[1]
    taps = conv_w.shape[1]
    cc = conv_b.shape[1]
    dm = _Dims(d=d, f=w_ff1_out.shape[1] * N_CHIPS, aw=w_attn_out.shape[1], nh=nh, cc=cc, taps=taps,
               dp=w_ple_proj.shape[1], s=s)
    assert taps - 1 <= HALO

    xi = lax.axis_index("x")
    yi = lax.axis_index("y")
    ci = lax.axis_index("c")
    c_arr = jnp.reshape(ci, (1,)).astype(jnp.int32)
    chip_arr = jnp.reshape(2 * xi + yi, (1,)).astype(jnp.int32)
    place_arr = jnp.stack([2 * xi + yi, ci]).astype(jnp.int32)

    h = x[0]
    target = loss_target[0]
    p_b = p[:, 0].astype(BF16)

    cw_rows = _round_up(nl * taps, 8)
    cw_blk = jnp.pad(conv_w.reshape(nl * taps, -1), ((0, cw_rows - nl * taps), (0, 0)))
    cw_all = _allgather_blocks("gather_conv_w", cw_blk).reshape(N_CHIPS, 2, cw_rows, -1)[:, 0, :nl * taps]
    conv_w_full = jnp.transpose(cw_all.reshape(N_CHIPS, nl, taps, -1), (1, 2, 0, 3)).reshape(nl, taps, cc)
    taps_pad = _round_up(taps, 8)

    def small_of(i):
        row = lambda a: a[i][None, :]
        return dict(g_ff1=row(g_ff1), g_mix=row(g_mix), g_conv=row(g_conv), conv_b=row(conv_b), g_ff2=row(g_ff2),
                    g_ple=row(g_ple), b_f=jnp.pad(b_f[i][None, :], ((0, 0), (0, LANES - nh))),
                    conv_w=jnp.pad(conv_w_full[i], ((0, taps_pad - taps), (0, 0))))

    slots = [_cast_slots(f"cast_{n}", wts[n], chip_arr) for n in _BIG]
    started = {}

    def start_gather(i):
        if i < nl:
            groups = _FIRST_LAYER_GROUPS if i == 0 else (tuple(range(len(_BIG))),)
            started[i] = [(g, _gather_start(f"gather_start_l{i}g{k}", [slots[t][i] for t in g], after=[conv_w_full]))
                          for k, g in enumerate(groups)]

    start_gather(0)
    start_gather(1)
    layer_w, saved = [], []
    for i in range(nl):
        pend = started.pop(i)
        wt = {}

        def arrive(k, h_now, i=i, pend=pend, wt=wt):
            if k < len(pend):
                g, (send, recv, bufs, _) = pend[k]
                bufs = _gather_wait(f"gather_wait_l{i}g{k}", bufs, send, recv, h_now)
                bufs = _gather_forward(f"gather_fwd_l{i}g{k}", bufs)
                wt.update(_unpack_layer({_BIG[t]: b for t, b in zip(g, bufs)}, dm))

        arrive(0, h)
        start_gather(i + 2)
        in_flight = [st[3] for _, st in pend[1:]] + [st[3] for sts in started.values() for _, st in sts]
        sm = small_of(i)
        h, sv1 = _ffn_fwd(f"l{i}_ff1", h, sm["g_ff1"], wt["w_ff1_in"], wt["w_ff1_out"], dm, deps=in_flight)
        arrive(1, h)
        h, sv2 = _mixer_fwd(f"l{i}_mix", h, sm, wt, dm)
        arrive(2, h)
        h, sv3 = _ffn_fwd(f"l{i}_ff2", h, sm["g_ff2"], wt["w_ff2_in"], wt["w_ff2_out"], dm)
        h, sv4 = _ple_fwd(f"l{i}_ple", h, p_b[i], sm, wt, dm)
        layer_w.append((wt, sm))
        saved.append((sv1, sv2, sv3, sv4))

    dh, loss_row, dg_final = _loss_head("loss_head", h, g_final[None, :], target)

    big_grads = [None] * nl
    small_grads = [None] * nl
    leaving = None
    for i in range(nl - 1, -1, -1):
        wt, sm = layer_w[i]
        sv1, sv2, sv3, sv4 = saved[i]
        deps = [leaving[3]] if leaving is not None else []
        dh, dh_b, dg_ple, dw_gate, dw_proj = _ple_bwd(f"l{i}_ple", dh, sv4, p_b[i], sm, wt, dm, deps=deps)
        dh, dh_b, dg_ff2, dw_in2, dw_out2 = _ffn_bwd(f"l{i}_ff2", dh, dh_b, sv3, sm["g_ff2"], wt["w_ff2_in"],
                                                     wt["w_ff2_out"], dm, 1.0)
        dh, dh_b, sg, dw_main, dw_f, dw_a, dw_c, dw_o = _mixer_bwd(f"l{i}_mix", dh, dh_b, sv2, sm, wt, dm)
        dh, dh_b, dg_ff1, dw_in1, dw_out1 = _ffn_bwd(f"l{i}_ff1", dh, dh_b, sv1, sm["g_ff1"], wt["w_ff1_in"],
                                                     wt["w_ff1_out"], dm, 1.0)
        gw = dict(w_ff1_in=dw_in1, w_ff1_out=dw_out1, w_main=dw_main, w_f=dw_f, w_attn_out=dw_a, w_conv_out=dw_c,
                  w_out=dw_o, w_ff2_in=dw_in2, w_ff2_out=dw_out2, w_ple_gate=dw_gate, w_ple_proj=dw_proj)
        if leaving is not None:
            big_grads[i + 1] = _reduce_scatter_end(f"l{i + 1}", leaving, dh, place_arr)
        sg.update(g_ff1=dg_ff1, g_ff2=dg_ff2, g_ple=dg_ple)
        small_grads[i] = sg
        if i > 0:
            leaving = _reduce_scatter_begin(f"l{i}", _pack_grads(gw, dm), c_arr)
    grad_x = dh[None]

    pieces = [small_grads[i][n].reshape(-1) for i in range(nl) for n in _SMALL]
    pieces += [dg_final.reshape(-1), loss_row[0, :1]]
    flat = jnp.concatenate(pieces)
    n_flat = flat.shape[0]
    rows = _round_up(_round_up(n_flat, LANES) // LANES, 8)
    blk = jnp.pad(flat, (0, rows * LANES - n_flat)).reshape(rows, LANES)
    total = _sum_blocks("sum_small", _allgather_blocks("gather_small", blk).reshape(N_DEV, rows, LANES)).reshape(-1)
    small_tot = {n: [] for n in _SMALL}
    pos = 0
    for i in range(nl):
        for n in _SMALL:
            shp = small_grads[i][n].shape
            size = shp[0] * shp[1]
            small_tot[n].append(total[pos:pos + size].reshape(shp))
            pos += size
    g_final_tot = total[pos:pos + d]
    loss = total[pos + d]

    leaving = _reduce_scatter_begin("l0", _pack_grads(gw, dm), c_arr, after=[total])
    stacked = {n: [lax.empty(wts[n].shape, F32) for _ in range(4)] for n in _BIG}

    def adamw_big(i, deps):
        for t, n in enumerate(_BIG):
            stacked[n] = _adamw_layer(f"adamw_{n}_l{i}", i, wts[n], mom[n], var[n], big_grads[i][t], stacked[n], deps)

    for i in range(nl - 1, 0, -1):
        adamw_big(i, [leaving[3]])
    after = stacked[_BIG[-1]][1] if nl > 1 else dh
    big_grads[0] = _reduce_scatter_end("l0", leaving, after, place_arr)
    adamw_big(0, [])
    grads, deltas, new_m, new_v = {}, {}, {}, {}
    for n in _BIG:
        grads[n], deltas[n], new_m[n], new_v[n] = stacked[n]
    chip = 2 * xi + yi
    for n in _SMALL:
        g = jnp.concatenate(small_tot[n], axis=0)
        if n == "conv_w":
            cpc = cc // N_CHIPS
            g = lax.dynamic_slice_in_dim(g.reshape(nl * taps, cc), chip * cpc, cpc, axis=1)
            shape2 = (nl * taps, cpc)
        else:
            shape2 = g.shape
        dl, mm, vv = _adamw_small(f"adamw_{n}", wts[n].reshape(shape2), g, mom[n].reshape(shape2),
                                  var[n].reshape(shape2))
        grads[n] = g.reshape(wts[n].shape)
        deltas[n], new_m[n], new_v[n] = (a.reshape(wts[n].shape) for a in (dl, mm, vv))
    g2 = g_final_tot[None, :]
    dl, mm, vv = _adamw_small("adamw_g_final", g_final[None, :], g2, m_g_final[None, :], v_g_final[None, :])
    grads["g_final"] = g_final_tot
    deltas["g_final"], new_m["g_final"], new_v["g_final"] = dl[0], mm[0], vv[0]

    return (loss, grad_x, *[grads[n] for n in _ORDER], *[deltas[n] for n in _ORDER],
            *[new_m[n] for n in _ORDER], *[new_v[n] for n in _ORDER])
```

```python
import functools

import jax
import jax.numpy as jnp
from jax import lax
from jax.experimental import pallas as pl
from jax.experimental.pallas import tpu as pltpu

F32 = jnp.float32
BF16 = jnp.bfloat16
MESH = pl.DeviceIdType.MESH
ANY = pl.BlockSpec(memory_space=pl.ANY)
HBM_SPEC = pl.BlockSpec(memory_space=pltpu.HBM)
SEM_SPEC = pl.BlockSpec(memory_space=pltpu.SEMAPHORE)
EFFECT = pltpu.SideEffectType.DATAFLOW_SIDE_EFFECTING

RMS_EPS = 1e-6
FFN_RES = 0.5
ADAM_LR = 0.001
ADAM_B1 = 0.9
ADAM_B2 = 0.999
ADAM_EPS = 1e-08
ADAM_WD = 0.01
ADAM_STEP = 10

N_CHIPS = 4
N_DEV = 8
LANES = 128
HALO = 32
VMEM_LIMIT = 56 * 1024 * 1024


def _cparams():
    return pltpu.CompilerParams(vmem_limit_bytes=VMEM_LIMIT)


def _pick(n, prefs):
    for p in prefs:
        if p <= n and n % p == 0:
            return p
    return n


def _half_rows(r):
    return r // 2 if r % 32 == 0 else r


def _sig(x):
    return 1.0 / (1.0 + jnp.exp(-x))


def _rowwise(name, fn, ins, outs, consts=(), reds=(), tm=None, deps=()):
    ins = [a if isinstance(a, tuple) else (a, 0, a.shape[1]) for a in ins]
    m = ins[0][0].shape[0]
    tm = tm or _pick(m, (256, 128, 64, 32, 16, 8))
    n_in, n_c, n_o, n_r = len(ins), len(consts), len(outs), len(reds)
    n_d = len(deps)
    consts = list(consts) + list(deps)

    def body(*refs):
        in_refs = refs[:n_in + n_c]
        o_refs = refs[n_in + n_c + n_d:n_in + n_c + n_d + n_o]
        r_refs = refs[n_in + n_c + n_d + n_o:]
        res = fn(*[r[...] for r in in_refs])
        if not isinstance(res, (tuple, list)):
            res = (res,)
        for r, v in zip(o_refs, res[:n_o]):
            r[...] = v.astype(r.dtype)
        if n_r:
            @pl.when(pl.program_id(0) == 0)
            def _():
                for r in r_refs:
                    r[...] = jnp.zeros(r.shape, r.dtype)
            for r, v in zip(r_refs, res[n_o:]):
                r[...] += v.astype(r.dtype)

    in_specs = [pl.BlockSpec((tm, w), functools.partial(lambda i, cb: (i, cb), cb=cb)) for (_, cb, w) in ins]
    in_specs += [pl.BlockSpec(c.shape, lambda i: (0, 0)) for c in consts]
    out_specs = [pl.BlockSpec((tm, w), lambda i: (i, 0)) for (w, _) in outs]
    out_specs += [pl.BlockSpec(s, lambda i: (0, 0)) for (s, _) in reds]
    out_shape = [jax.ShapeDtypeStruct((m, w), d) for (w, d) in outs]
    out_shape += [jax.ShapeDtypeStruct(s, d) for (s, d) in reds]
    res = pl.pallas_call(
        body, name=name, grid=(m // tm,), in_specs=in_specs, out_specs=out_specs, out_shape=out_shape,
        compiler_params=_cparams(),
    )(*[a for (a, _, _) in ins], *consts)
    return res


def _rms_fwd(name, h, g, deps=()):
    def fn(x, gg):
        r = lax.rsqrt(jnp.mean(x * x, axis=-1, keepdims=True) + RMS_EPS)
        return x * r * gg
    return _rowwise(name, fn, [h], [(h.shape[1], BF16)], consts=[g], deps=deps)[0]


def _rms_bwd_vals(x, g, dn):
    r = lax.rsqrt(jnp.mean(x * x, axis=-1, keepdims=True) + RMS_EPS)
    xh = x * r
    dxh = dn * g
    dx = r * (dxh - xh * jnp.mean(dxh * xh, axis=-1, keepdims=True))
    dg = jnp.sum(dn * xh, axis=0, keepdims=True)
    return dx, dg


def _rms_bwd_res(name, h, g, dns, dres, oscale):
    n_dn = len(dns)

    def fn(x, *rest):
        dn = rest[0].astype(F32)
        for t in rest[1:n_dn]:
            dn = dn + t.astype(F32)
        dr, gg = rest[n_dn], rest[n_dn + 1]
        dx, dg = _rms_bwd_vals(x, gg, dn)
        dh = dr + dx
        return dh, oscale * dh, dg

    d = h.shape[1]
    return _rowwise(name, fn, [h, *dns, dres], [(d, F32), (d, BF16)], consts=[g], reds=[((1, d), F32)])


_TN_PREFS = (1408, 1024, 768, 512, 256, 128)


def _mm_nn(name, a, w, out_dtype, res=None, scale=1.0):
    m, k = a.shape
    j, _, nb = w.shape
    tm = _pick(m, (512, 256, 128))
    tn = _pick(nb, _TN_PREFS)
    tpb = nb // tn
    has_res = res is not None

    def body(a_ref, w_ref, *rest):
        o_ref = rest[-1]
        acc = jnp.dot(a_ref[...], w_ref[...], preferred_element_type=F32)
        if has_res:
            acc = rest[0][...] + scale * acc
        o_ref[...] = acc.astype(o_ref.dtype)

    in_specs = [pl.BlockSpec((tm, k), lambda n, i: (i, 0)),
                pl.BlockSpec((None, k, tn), lambda n, i: (n // tpb, 0, n % tpb))]
    args = [a, w]
    if has_res:
        in_specs.append(pl.BlockSpec((tm, tn), lambda n, i: (i, n)))
        args.append(res)
    return pl.pallas_call(
        body, name=name, grid=(j * tpb, m // tm), in_specs=in_specs,
        out_specs=pl.BlockSpec((tm, tn), lambda n, i: (i, n)),
        out_shape=jax.ShapeDtypeStruct((m, j * nb), out_dtype), compiler_params=_cparams(),
    )(*args)


def _mm_nt(name, dy, w, out_dtype):
    m, n = dy.shape
    j, k, nb = w.shape
    tm = _pick(m, (512, 256, 128))
    to = _pick(k, _TN_PREFS)
    tc = _pick(nb, (1536, 1408, 1024, 512, 256, 128))
    cpb = nb // tc
    n_red = j * cpb

    def body(dy_ref, w_ref, o_ref, acc_ref):
        r = pl.program_id(2)

        @pl.when(r == 0)
        def _():
            acc_ref[...] = jnp.zeros(acc_ref.shape, F32)

        acc_ref[...] += lax.dot_general(dy_ref[...], w_ref[...], (((1,), (1,)), ((), ())),
                                        preferred_element_type=F32)

        @pl.when(r == n_red - 1)
        def _():
            o_ref[...] = acc_ref[...].astype(o_ref.dtype)

    return pl.pallas_call(
        body, name=name, grid=(k // to, m // tm, n_red),
        in_specs=[pl.BlockSpec((tm, tc), lambda ko, i, r: (i, r)),
                  pl.BlockSpec((None, to, tc), lambda ko, i, r: (r // cpb, ko, r % cpb))],
        out_specs=pl.BlockSpec((tm, to), lambda ko, i, r: (i, ko)),
        out_shape=jax.ShapeDtypeStruct((m, k), out_dtype),
        scratch_shapes=[pltpu.VMEM((tm, to), F32)], compiler_params=_cparams(),
    )(dy, w)


def _mm_tn(name, a, dy, j):
    m, k = a.shape
    n = dy.shape[1]
    nb = n // j
    tk = _pick(k, (512, 256, 128))
    tn = _pick(nb, _TN_PREFS)
    tpb = nb // tn

    def body(a_ref, dy_ref, o_ref):
        o_ref[...] = lax.dot_general(a_ref[...], dy_ref[...], (((0,), (0,)), ((), ())),
                                     preferred_element_type=F32).astype(o_ref.dtype)

    return pl.pallas_call(
        body, name=name, grid=(k // tk, j * tpb),
        in_specs=[pl.BlockSpec((m, tk), lambda kb, nn: (0, kb)),
                  pl.BlockSpec((m, tn), lambda kb, nn: (0, nn))],
        out_specs=pl.BlockSpec((None, tk, tn), lambda kb, nn: (nn // tpb, kb, nn % tpb)),
        out_shape=jax.ShapeDtypeStruct((j, k, nb), BF16), compiler_params=_cparams(),
    )(a, dy)


def _cumsum_rows(x_ref, o_ref, blk, reverse):
    s = x_ref.shape[0]
    nblk = s // blk
    ri = lax.broadcasted_iota(jnp.int32, (blk, blk), 0)
    ci = lax.broadcasted_iota(jnp.int32, (blk, blk), 1)
    tri = jnp.where((ci >= ri) if reverse else (ci <= ri), 1.0, 0.0).astype(F32)
    carry = jnp.zeros((1, x_ref.shape[1]), F32)
    order = range(nblk - 1, -1, -1) if reverse else range(nblk)
    for b in order:
        xb = x_ref[b * blk:(b + 1) * blk, :]
        o_ref[b * blk:(b + 1) * blk, :] = jnp.dot(tri, xb, preferred_element_type=F32,
                                                  precision=lax.Precision.HIGHEST) + carry
        carry = carry + jnp.sum(xb, axis=0, keepdims=True)


def _fgate_fwd(name, zf, bf_row):
    s, w = zf.shape
    blk = _pick(s, (256, 128))

    def body(z_ref, b_ref, c_ref, ls_ref):
        v = z_ref[...] + b_ref[...]
        ls_ref[...] = jnp.minimum(v, 0.0) - jnp.log(1.0 + jnp.exp(-jnp.abs(v)))
        _cumsum_rows(ls_ref, c_ref, blk, reverse=False)

    return pl.pallas_call(
        body, name=name, out_shape=jax.ShapeDtypeStruct((s, w), F32),
        scratch_shapes=[pltpu.VMEM((s, w), F32)], compiler_params=_cparams(),
    )(zf, bf_row)


def _fgate_bwd(name, dc_q, dc_k, zf, bf_row):
    s, w = zf.shape
    blk = _pick(s, (256, 128))

    def body(dcq_ref, dck_ref, z_ref, b_ref, dz_ref, dzb_ref, db_ref, dls_ref, dc_ref):
        dc_ref[...] = dcq_ref[...] + dck_ref[...]
        _cumsum_rows(dc_ref, dls_ref, blk, reverse=True)
        dz = dls_ref[...] * _sig(-(z_ref[...] + b_ref[...]))
        dz_ref[...] = dz
        dzb_ref[...] = dz.astype(BF16)
        db_ref[...] = jnp.sum(dz, axis=0, keepdims=True)

    return pl.pallas_call(
        body, name=name,
        out_shape=[jax.ShapeDtypeStruct((s, w), F32), jax.ShapeDtypeStruct((s, w), BF16),
                   jax.ShapeDtypeStruct((1, w), F32)],
        scratch_shapes=[pltpu.VMEM((s, w), F32), pltpu.VMEM((s, w), F32)], compiler_params=_cparams(),
    )(dc_q, dc_k, zf, bf_row)


def _scores(q, k, cq_col, ck_row, scale, row0, col0):
    s = lax.dot_general(q, k, (((1,), (1,)), ((), ())), preferred_element_type=F32) * scale
    s = s + (cq_col - ck_row)
    rows = row0 + lax.broadcasted_iota(jnp.int32, s.shape, 0)
    cols = col0 + lax.broadcasted_iota(jnp.int32, s.shape, 1)
    return jnp.where(cols <= rows, s, -jnp.inf)


def _attn_fwd(name, q, k, v, c_col, c_row, tb):
    h, s, hd = q.shape
    nb = s // tb
    scale = 1.0 / float(hd) ** 0.5

    def body(q_ref, k_ref, v_ref, cq_ref, ck_ref, o_ref, lse_ref):
        i = pl.program_id(1)
        qv = q_ref[...]
        cq = cq_ref[...]

        def step(j, carry):
            m_i, l_i, acc = carry
            k0 = pl.multiple_of(j * tb, tb)
            sc = _scores(qv, k_ref[pl.ds(k0, tb), :], cq, ck_ref[j], scale, i * tb, j * tb)
            m_new = jnp.maximum(m_i, jnp.max(sc, axis=-1, keepdims=True))
            alpha = jnp.exp(m_i - m_new)
            p = jnp.exp(sc - m_new)
            l_new = alpha * l_i + jnp.sum(p, axis=-1, keepdims=True)
            acc = alpha * acc + jnp.dot(p.astype(BF16), v_ref[pl.ds(k0, tb), :], preferred_element_type=F32)
            return m_new, l_new, acc

        init = (jnp.full((tb, 1), -jnp.inf, F32), jnp.zeros((tb, 1), F32), jnp.zeros((tb, hd), F32))
        m_i, l_i, acc = lax.fori_loop(0, i + 1, step, init)
        o_ref[...] = (acc / l_i).astype(o_ref.dtype)
        lse_ref[...] = m_i + jnp.log(l_i)

    return pl.pallas_call(
        body, name=name, grid=(h, nb),
        in_specs=[pl.BlockSpec((None, tb, hd), lambda hh, i: (hh, i, 0)),
                  pl.BlockSpec((None, s, hd), lambda hh, i: (hh, 0, 0)),
                  pl.BlockSpec((None, s, hd), lambda hh, i: (hh, 0, 0)),
                  pl.BlockSpec((None, tb, 1), lambda hh, i: (hh, i, 0)),
                  pl.BlockSpec((None, nb, 1, tb), lambda hh, i: (hh, 0, 0, 0))],
        out_specs=[pl.BlockSpec((None, tb, hd), lambda hh, i: (hh, i, 0)),
                   pl.BlockSpec((None, tb, 1), lambda hh, i: (hh, i, 0))],
        out_shape=[jax.ShapeDtypeStruct((h, s, hd), F32), jax.ShapeDtypeStruct((h, s, 1), F32)],
        compiler_params=_cparams(),
    )(q, k, v, c_col, c_row)


def _attn_bwd_q(name, q, k, v, o, do, lse, c_col, c_row, tb):
    h, s, hd = q.shape
    nb = s // tb
    scale = 1.0 / float(hd) ** 0.5

    def body(q_ref, k_ref, v_ref, o_ref, do_ref, lse_ref, cq_ref, ck_ref, dq_ref, dl_ref, dcq_ref):
        i = pl.program_id(1)
        qv = q_ref[...]
        dov = do_ref[...]
        cq = cq_ref[...]
        lse_v = lse_ref[...]
        delta = jnp.sum(dov.astype(F32) * o_ref[...], axis=-1, keepdims=True)

        def step(j, carry):
            dq, dcq = carry
            k0 = pl.multiple_of(j * tb, tb)
            kj = k_ref[pl.ds(k0, tb), :]
            p = jnp.exp(_scores(qv, kj, cq, ck_ref[j], scale, i * tb, j * tb) - lse_v)
            dp = lax.dot_general(dov, v_ref[pl.ds(k0, tb), :], (((1,), (1,)), ((), ())), preferred_element_type=F32)
            ds = p * (dp - delta)
            return (dq + jnp.dot(ds.astype(BF16), kj, preferred_element_type=F32),
                    dcq + jnp.sum(ds, axis=-1, keepdims=True))

        dq, dcq = lax.fori_loop(0, i + 1, step, (jnp.zeros((tb, hd), F32), jnp.zeros((tb, 1), F32)))
        dq_ref[...] = (dq * scale).astype(dq_ref.dtype)
        dl_ref[...] = delta
        dcq_ref[...] = dcq

    blk = pl.BlockSpec((None, tb, hd), lambda hh, i: (hh, i, 0))
    full = pl.BlockSpec((None, s, hd), lambda hh, i: (hh, 0, 0))
    col = pl.BlockSpec((None, tb, 1), lambda hh, i: (hh, i, 0))
    return pl.pallas_call(
        body, name=name, grid=(h, nb),
        in_specs=[blk, full, full, blk, blk, col, col,
                  pl.BlockSpec((None, nb, 1, tb), lambda hh, i: (hh, 0, 0, 0))],
        out_specs=[blk, col, col],
        out_shape=[jax.ShapeDtypeStruct((h, s, hd), BF16), jax.ShapeDtypeStruct((h, s, 1), F32),
                   jax.ShapeDtypeStruct((h, s, 1), F32)],
        compiler_params=_cparams(),
    )(q, k, v, o, do, lse, c_col, c_row)


def _attn_bwd_kv(name, q, k, v, do, lse_row, delta_row, c_col, c_row, tb):
    h, s, hd = q.shape
    nb = s // tb
    scale = 1.0 / float(hd) ** 0.5

    def body(q_ref, k_ref, v_ref, do_ref, lse_ref, dl_ref, ck_ref, cq_ref, dk_ref, dv_ref, dc_ref):
        j = pl.program_id(1)
        kv = k_ref[...]
        vv = v_ref[...]
        ck = ck_ref[...]

        def step(i, carry):
            dk, dv, dc = carry
            q0 = pl.multiple_of(i * tb, tb)
            qi = q_ref[pl.ds(q0, tb), :]
            doi = do_ref[pl.ds(q0, tb), :]
            st = lax.dot_general(kv, qi, (((1,), (1,)), ((), ())), preferred_element_type=F32) * scale
            st = st + (cq_ref[i] - ck)
            krow = j * tb + lax.broadcasted_iota(jnp.int32, st.shape, 0)
            qcol = i * tb + lax.broadcasted_iota(jnp.int32, st.shape, 1)
            pt = jnp.exp(jnp.where(krow <= qcol, st, -jnp.inf) - lse_ref[i])
            dv = dv + jnp.dot(pt.astype(BF16), doi, preferred_element_type=F32)
            dpt = lax.dot_general(vv, doi, (((1,), (1,)), ((), ())), preferred_element_type=F32)
            dst = pt * (dpt - dl_ref[i])
            dk = dk + jnp.dot(dst.astype(BF16), qi, preferred_element_type=F32)
            dc = dc - jnp.sum(dst, axis=-1, keepdims=True)
            return dk, dv, dc

        init = (jnp.zeros((tb, hd), F32), jnp.zeros((tb, hd), F32), jnp.zeros((tb, 1), F32))
        dk, dv, dc = lax.fori_loop(j, nb, step, init)
        dk_ref[...] = (dk * scale).astype(dk_ref.dtype)
        dv_ref[...] = dv.astype(dv_ref.dtype)
        dc_ref[...] = dc

    blk = pl.BlockSpec((None, tb, hd), lambda hh, jj: (hh, jj, 0))
    full = pl.BlockSpec((None, s, hd), lambda hh, jj: (hh, 0, 0))
    col = pl.BlockSpec((None, tb, 1), lambda hh, jj: (hh, jj, 0))
    rows = pl.BlockSpec((None, nb, 1, tb), lambda hh, jj: (hh, 0, 0, 0))
    return pl.pallas_call(
        body, name=name, grid=(h, nb),
        in_specs=[full, blk, blk, full, rows, rows, col, rows],
        out_specs=[blk, blk, col],
        out_shape=[jax.ShapeDtypeStruct((h, s, hd), BF16), jax.ShapeDtypeStruct((h, s, hd), BF16),
                   jax.ShapeDtypeStruct((h, s, 1), F32)],
        compiler_params=_cparams(),
    )(q, k, v, do, lse_row, delta_row, c_col, c_row)


def _pair_masks(hd):
    lane = lax.broadcasted_iota(jnp.int32, (1, 2 * hd), 1)
    return lane < hd, lane >= hd


def _only(mask, a):
    return jnp.where(mask, a, jnp.zeros_like(a))


def _nt(a, b):
    return lax.dot_general(a, b, (((1,), (1,)), ((), ())), preferred_element_type=F32)


def _causal(sc, rows_are_queries):
    r = lax.broadcasted_iota(jnp.int32, sc.shape, 0)
    c = lax.broadcasted_iota(jnp.int32, sc.shape, 1)
    return jnp.where((c <= r) if rows_are_queries else (r <= c), sc, -jnp.inf)


def _pair_specs(dm, s):
    hd = dm.aw // dm.nh
    pw = 2 * hd
    assert pw == LANES and dm.o_q % pw == 0 and dm.aw % pw == 0
    return hd, pw, dm.o_q // pw, (dm.o_q + dm.aw) // pw, (dm.o_q + 2 * dm.aw) // pw


def _attn2_fwd(name, zm, c_col, c_row, dm):
    s, tb = zm.shape[0], dm.tb
    nb = s // tb
    hd, pw, qb, kb, vb = _pair_specs(dm, s)
    scale = 1.0 / float(hd) ** 0.5

    def body(q_ref, k_ref, v_ref, cq_ref, cr_ref, o_ref, ob_ref, lse_ref):
        i = pl.program_id(1)
        masks = _pair_masks(hd)
        q2 = q_ref[...]
        qe = [_only(m, q2) for m in masks]
        cq = [cq_ref[0], cq_ref[1]]

        def block(j, carry, diag):
            k0 = pl.multiple_of(j * tb, tb)
            kj = k_ref[pl.ds(k0, tb), :]
            vj = v_ref[pl.ds(k0, tb), :]
            out = []
            for e in range(2):
                m_i, l_i, acc = carry[e]
                sc = _nt(qe[e], kj) * scale + (cq[e] - cr_ref[e, j])
                if diag:
                    sc = _causal(sc, True)
                m_new = jnp.maximum(m_i, jnp.max(sc, axis=-1, keepdims=True))
                alpha = jnp.exp(m_i - m_new)
                p = jnp.exp(sc - m_new)
                l_new = alpha * l_i + jnp.sum(p, axis=-1, keepdims=True)
                acc = alpha * acc + jnp.dot(p.astype(BF16), vj, preferred_element_type=F32)
                out.append((m_new, l_new, acc))
            return tuple(out)

        one = (jnp.full((tb, 1), -jnp.inf, F32), jnp.zeros((tb, 1), F32), jnp.zeros((tb, pw), F32))
        carry = lax.fori_loop(0, i, lambda j, c: block(j, c, False), (one, one))
        (m0, l0, a0), (m1, l1, a1) = block(i, carry, True)
        o = jnp.where(masks[0], a0 / l0, a1 / l1)
        o_ref[...] = o
        ob_ref[...] = o.astype(BF16)
        lse_ref[0] = m0 + jnp.log(l0)
        lse_ref[1] = m1 + jnp.log(l1)

    blk = lambda cb: pl.BlockSpec((tb, pw), functools.partial(lambda hp, i, cb: (i, cb + hp), cb=cb))
    full = lambda cb: pl.BlockSpec((s, pw), functools.partial(lambda hp, i, cb: (0, cb + hp), cb=cb))
    col = pl.BlockSpec((2, tb, 1), lambda hp, i: (hp, i, 0))
    rows = pl.BlockSpec((2, nb, 1, tb), lambda hp, i: (hp, 0, 0, 0))
    return pl.pallas_call(
        body, name=name, grid=(dm.nh // 2, nb),
        in_specs=[blk(qb), full(kb), full(vb), col, rows],
        out_specs=[blk(0), blk(0), col],
        out_shape=[jax.ShapeDtypeStruct((s, dm.aw), F32), jax.ShapeDtypeStruct((s, dm.aw), BF16),
                   jax.ShapeDtypeStruct((dm.nh, s, 1), F32)],
        compiler_params=_cparams(),
    )(zm, zm, zm, c_col, c_row)


def _attn2_bwd_q(name, zm, o, do, lse, c_col, c_row, dm):
    s, tb = zm.shape[0], dm.tb
    nb = s // tb
    hd, pw, qb, kb, vb = _pair_specs(dm, s)
    scale = 1.0 / float(hd) ** 0.5

    def body(q_ref, k_ref, v_ref, o_ref, do_ref, lse_ref, cq_ref, cr_ref, dq_ref, dl_ref, dcq_ref):
        i = pl.program_id(1)
        masks = _pair_masks(hd)
        q2 = q_ref[...]
        do2 = do_ref[...]
        prod = do2.astype(F32) * o_ref[...]
        qe = [_only(m, q2) for m in masks]
        doe = [_only(m, do2) for m in masks]
        delta = [jnp.sum(_only(m, prod), axis=-1, keepdims=True) for m in masks]
        cq = [cq_ref[0], cq_ref[1]]
        lse_v = [lse_ref[0], lse_ref[1]]

        def block(j, carry, diag):
            k0 = pl.multiple_of(j * tb, tb)
            kj = k_ref[pl.ds(k0, tb), :]
            vj = v_ref[pl.ds(k0, tb), :]
            out = []
            for e in range(2):
                dq, dcq = carry[e]
                sc = _nt(qe[e], kj) * scale + (cq[e] - cr_ref[e, j])
                if diag:
                    sc = _causal(sc, True)
                p = jnp.exp(sc - lse_v[e])
                ds = p * (_nt(doe[e], vj) - delta[e])
                out.append((dq + jnp.dot(ds.astype(BF16), kj, preferred_element_type=F32),
                            dcq + jnp.sum(ds, axis=-1, keepdims=True)))
            return tuple(out)

        one = (jnp.zeros((tb, pw), F32), jnp.zeros((tb, 1), F32))
        carry = lax.fori_loop(0, i, lambda j, c: block(j, c, False), (one, one))
        (dq0, dc0), (dq1, dc1) = block(i, carry, True)
        dq_ref[...] = (jnp.where(masks[0], dq0, dq1) * scale).astype(dq_ref.dtype)
        dl_ref[0] = delta[0]
        dl_ref[1] = delta[1]
        dcq_ref[0] = dc0
        dcq_ref[1] = dc1

    blk = lambda cb: pl.BlockSpec((tb, pw), functools.partial(lambda hp, i, cb: (i, cb + hp), cb=cb))
    full = lambda cb: pl.BlockSpec((s, pw), functools.partial(lambda hp, i, cb: (0, cb + hp), cb=cb))
    col = pl.BlockSpec((2, tb, 1), lambda hp, i: (hp, i, 0))
    rows = pl.BlockSpec((2, nb, 1, tb), lambda hp, i: (hp, 0, 0, 0))
    return pl.pallas_call(
        body, name=name, grid=(dm.nh // 2, nb),
        in_specs=[blk(qb), full(kb), full(vb), blk(0), blk(0), col, col, rows],
        out_specs=[blk(0), col, col],
        out_shape=[jax.ShapeDtypeStruct((s, dm.aw), BF16), jax.ShapeDtypeStruct((dm.nh, s, 1), F32),
                   jax.ShapeDtypeStruct((dm.nh, s, 1), F32)],
        compiler_params=_cparams(),
    )(zm, zm, zm, o, do, lse, c_col, c_row)


def _attn2_bwd_kv(name, zm, do, lse_row, delta_row, c_col, c_row, dm):
    s, tb = zm.shape[0], dm.tb
    nb = s // tb
    hd, pw, qb, kb, vb = _pair_specs(dm, s)
    scale = 1.0 / float(hd) ** 0.5

    def body(q_ref, k_ref, v_ref, do_ref, lse_ref, dl_ref, ck_ref, cr_ref, dk_ref, dv_ref, dc_ref):
        j = pl.program_id(1)
        masks = _pair_masks(hd)
        k2 = k_ref[...]
        v2 = v_ref[...]
        ke = [_only(m, k2) for m in masks]
        ve = [_only(m, v2) for m in masks]
        ck = [ck_ref[0], ck_ref[1]]

        def block(i, carry, diag):
            q0 = pl.multiple_of(i * tb, tb)
            qi = q_ref[pl.ds(q0, tb), :]
            doi = do_ref[pl.ds(q0, tb), :]
            out = []
            for e in range(2):
                dk, dv, dc = carry[e]
                st = _nt(ke[e], qi) * scale + (cr_ref[e, i] - ck[e])
                if diag:
                    st = _causal(st, False)
                pt = jnp.exp(st - lse_ref[e, i])
                dv = dv + jnp.dot(pt.astype(BF16), doi, preferred_element_type=F32)
                dst = pt * (_nt(ve[e], doi) - dl_ref[e, i])
                dk = dk + jnp.dot(dst.astype(BF16), qi, preferred_element_type=F32)
                out.append((dk, dv, dc - jnp.sum(dst, axis=-1, keepdims=True)))
            return tuple(out)

        one = (jnp.zeros((tb, pw), F32), jnp.zeros((tb, pw), F32), jnp.zeros((tb, 1), F32))
        carry = block(j, (one, one), True)
        (dk0, dv0, dc0), (dk1, dv1, dc1) = lax.fori_loop(j + 1, nb, lambda i, c: block(i, c, False), carry)
        dk_ref[...] = (jnp.where(masks[0], dk0, dk1) * scale).astype(dk_ref.dtype)
        dv_ref[...] = jnp.where(masks[0], dv0, dv1).astype(dv_ref.dtype)
        dc_ref[0] = dc0
        dc_ref[1] = dc1

    blk = lambda cb: pl.BlockSpec((tb, pw), functools.partial(lambda hp, jj, cb: (jj, cb + hp), cb=cb))
    full = lambda cb: pl.BlockSpec((s, pw), functools.partial(lambda hp, jj, cb: (0, cb + hp), cb=cb))
    col = pl.BlockSpec((2, tb, 1), lambda hp, jj: (hp, jj, 0))
    rows = pl.BlockSpec((2, nb, 1, tb), lambda hp, jj: (hp, 0, 0, 0))
    return pl.pallas_call(
        body, name=name, grid=(dm.nh // 2, nb),
        in_specs=[full(qb), blk(kb), blk(vb), full(0), rows, rows, col, rows],
        out_specs=[blk(0), blk(0), col],
        out_shape=[jax.ShapeDtypeStruct((s, dm.aw), BF16), jax.ShapeDtypeStruct((s, dm.aw), BF16),
                   jax.ShapeDtypeStruct((dm.nh, s, 1), F32)],
        compiler_params=_cparams(),
    )(zm, zm, zm, do, lse_row, delta_row, c_col, c_row)


def _glu(cv, cc):
    c1 = cv[:, :cc].astype(F32)
    c2 = cv[:, cc:].astype(F32)
    return c1 * _sig(c2)


def _conv_fwd(name, zm, w_pad, b_row, g_row, cc, taps):
    s = zm.shape[0]
    tr = _pick(s, (256, 128))
    hpb = tr // HALO
    off = HALO - (taps - 1)

    def body(cur_ref, halo_ref, w_ref, b_ref, g_ref, y_ref, cs_ref, apad):
        i = pl.program_id(0)
        apad[0:HALO, :] = _glu(halo_ref[...], cc) * jnp.where(i > 0, 1.0, 0.0)
        apad[HALO:, :] = _glu(cur_ref[...], cc)
        acc = jnp.zeros((tr, cc), F32) + b_ref[...]
        for t in range(taps):
            acc = acc + w_ref[t:t + 1, :] * apad[off + t:off + t + tr, :]
        y_ref[...] = acc
        r = lax.rsqrt(jnp.mean(acc * acc, axis=-1, keepdims=True) + RMS_EPS)
        n = acc * r * g_ref[...]
        cs_ref[...] = (n * _sig(n)).astype(cs_ref.dtype)

    return pl.pallas_call(
        body, name=name, grid=(s // tr,),
        in_specs=[pl.BlockSpec((tr, 2 * cc), lambda i: (i, 0)),
                  pl.BlockSpec((HALO, 2 * cc), lambda i: (jnp.maximum(i * hpb - 1, 0), 0)),
                  pl.BlockSpec(w_pad.shape, lambda i: (0, 0)),
                  pl.BlockSpec(b_row.shape, lambda i: (0, 0)),
                  pl.BlockSpec(g_row.shape, lambda i: (0, 0))],
        out_specs=[pl.BlockSpec((tr, cc), lambda i: (i, 0)), pl.BlockSpec((tr, cc), lambda i: (i, 0))],
        out_shape=[jax.ShapeDtypeStruct((s, cc), F32), jax.ShapeDtypeStruct((s, cc), BF16)],
        scratch_shapes=[pltpu.VMEM((HALO + tr, cc), F32)], compiler_params=_cparams(),
    )(zm, zm, w_pad, b_row, g_row)


def _conv_bwd(name, zm, y, dcs, w_pad, g_row, cc, taps):
    s = zm.shape[0]
    tr = _pick(s, (256, 128))
    hpb = tr // HALO
    nblk = s // tr
    off = HALO - (taps - 1)

    def dy_of(yv, dcsv, g):
        r = lax.rsqrt(jnp.mean(yv * yv, axis=-1, keepdims=True) + RMS_EPS)
        xh = yv * r
        n = xh * g
        sg = _sig(n)
        dn = dcsv.astype(F32) * (sg * (1.0 + n * (1.0 - sg)))
        dxh = dn * g
        dy = r * (dxh - xh * jnp.mean(dxh * xh, axis=-1, keepdims=True))
        return dy, dn * xh

    def body(cur_ref, halo_ref, y_ref, yn_ref, dcs_ref, dcsn_ref, w_ref, g_ref,
             dcv_ref, dw_ref, db_ref, dg_ref, apad, dypad):
        i = pl.program_id(0)

        @pl.when(i == 0)
        def _():
            dw_ref[...] = jnp.zeros(dw_ref.shape, F32)
            db_ref[...] = jnp.zeros(db_ref.shape, F32)
            dg_ref[...] = jnp.zeros(dg_ref.shape, F32)

        g = g_ref[...]
        apad[0:HALO, :] = _glu(halo_ref[...], cc) * jnp.where(i > 0, 1.0, 0.0)
        apad[HALO:, :] = _glu(cur_ref[...], cc)
        dy, dgt = dy_of(y_ref[...], dcs_ref[...], g)
        dyn, _ = dy_of(yn_ref[...], dcsn_ref[...], g)
        dypad[0:tr, :] = dy
        dypad[tr:, :] = dyn * jnp.where(i < nblk - 1, 1.0, 0.0)
        db_ref[...] += jnp.sum(dy, axis=0, keepdims=True)
        dg_ref[...] += jnp.sum(dgt, axis=0, keepdims=True)
        da = jnp.zeros((tr, cc), F32)
        for t in range(taps):
            da = da + w_ref[t:t + 1, :] * dypad[taps - 1 - t:taps - 1 - t + tr, :]
            dw_ref[t:t + 1, :] += jnp.sum(dy * apad[off + t:off + t + tr, :], axis=0, keepdims=True)
        cv = cur_ref[...]
        c1 = cv[:, :cc].astype(F32)
        sg = _sig(cv[:, cc:].astype(F32))
        dcv_ref[:, :cc] = (da * sg).astype(dcv_ref.dtype)
        dcv_ref[:, cc:] = (da * c1 * sg * (1.0 - sg)).astype(dcv_ref.dtype)

    nxt = lambda i: (jnp.minimum((i + 1) * hpb, s // HALO - 1), 0)
    return pl.pallas_call(
        body, name=name, grid=(nblk,),
        in_specs=[pl.BlockSpec((tr, 2 * cc), lambda i: (i, 0)),
                  pl.BlockSpec((HALO, 2 * cc), lambda i: (jnp.maximum(i * hpb - 1, 0), 0)),
                  pl.BlockSpec((tr, cc), lambda i: (i, 0)), pl.BlockSpec((HALO, cc), nxt),
                  pl.BlockSpec((tr, cc), lambda i: (i, 0)), pl.BlockSpec((HALO, cc), nxt),
                  pl.BlockSpec(w_pad.shape, lambda i: (0, 0)), pl.BlockSpec(g_row.shape, lambda i: (0, 0))],
        out_specs=[pl.BlockSpec((tr, 2 * cc), lambda i: (i, 0)),
                   pl.BlockSpec(w_pad.shape, lambda i: (0, 0)),
                   pl.BlockSpec((1, cc), lambda i: (0, 0)), pl.BlockSpec((1, cc), lambda i: (0, 0))],
        out_shape=[jax.ShapeDtypeStruct((s, 2 * cc), BF16), jax.ShapeDtypeStruct(w_pad.shape, F32),
                   jax.ShapeDtypeStruct((1, cc), F32), jax.ShapeDtypeStruct((1, cc), F32)],
        scratch_shapes=[pltpu.VMEM((HALO + tr, cc), F32), pltpu.VMEM((tr + HALO, cc), F32)],
        compiler_params=_cparams(),
    )(zm, zm, y, y, dcs, dcs, w_pad, g_row)


def _place():
    x, y, c = lax.axis_index("x"), lax.axis_index("y"), lax.axis_index("c")
    chips = [(1 - x, y), (x, 1 - y), (1 - x, 1 - y)]
    return x, y, c, chips


def _half(c, rows):
    rh = rows // 2
    return pl.ds(pl.multiple_of(c * rh, 16), rh)


def _remote(src, dst, send, recv, dev):
    return pltpu.make_async_remote_copy(src_ref=src, dst_ref=dst, send_sem=send, recv_sem=recv,
                                        device_id=dev, device_id_type=MESH)


def _cast_slots(name, w, chip_arr):
    nl, r, cdim = w.shape
    tr = _half_rows(r)
    n = r // tr

    def body(q_ref, w_ref, *o_refs):
        layer = pl.program_id(0)
        for j, o_ref in enumerate(o_refs):
            @pl.when(layer == j)
            def _(o_ref=o_ref):
                o_ref[...] = w_ref[...].astype(o_ref.dtype)

    def out_map(j):
        return lambda l, i, q: (q[0], jnp.where(l < j, 0, jnp.where(l == j, i, n - 1)), 0)

    return pl.pallas_call(
        body, name=name,
        grid_spec=pltpu.PrefetchScalarGridSpec(
            num_scalar_prefetch=1, grid=(nl, n),
            in_specs=[pl.BlockSpec((None, tr, cdim), lambda l, i, q: (l, i, 0))],
            out_specs=[pl.BlockSpec((None, tr, cdim), out_map(j)) for j in range(nl)]),
        out_shape=[jax.ShapeDtypeStruct((N_CHIPS, r, cdim), BF16)] * nl, compiler_params=_cparams(),
    )(chip_arr, w)


def _split_start(name, n_sems, bufs, issue, after=()):
    nb = len(bufs)
    n_in = nb + len(after)

    def body(*refs):
        issue(refs[:nb], refs[n_in], refs[n_in + 1])
        refs[-1][...] = jnp.zeros(refs[-1].shape, F32)

    outs = pl.pallas_call(
        body, name=name, in_specs=[HBM_SPEC] * nb + [ANY] * len(after),
        out_shape=(pltpu.SemaphoreType.DMA(n_sems), pltpu.SemaphoreType.DMA(n_sems),
                   *[pltpu.HBM(b.shape, b.dtype) for b in bufs], jax.ShapeDtypeStruct((8, LANES), F32)),
        out_specs=(SEM_SPEC, SEM_SPEC, *[HBM_SPEC] * nb, pl.BlockSpec(memory_space=pltpu.VMEM)),
        input_output_aliases={t: t + 2 for t in range(nb)},
        compiler_params=pltpu.CompilerParams(has_side_effects=EFFECT),
    )(*[pltpu.with_memory_space_constraint(b, pltpu.HBM) for b in bufs], *after)
    return outs[0], outs[1], list(outs[2:2 + nb]), outs[-1]


def _split_wait(name, bufs, send, recv, after, drain):
    nb = len(bufs)

    def body(*refs):
        drain(refs[:nb], refs[nb], refs[nb + 1])

    return list(pl.pallas_call(
        body, name=name, in_specs=[HBM_SPEC] * nb + [SEM_SPEC, SEM_SPEC, ANY],
        out_shape=tuple(pltpu.HBM(b.shape, b.dtype) for b in bufs), out_specs=tuple([HBM_SPEC] * nb),
        input_output_aliases={t: t for t in range(nb)},
        compiler_params=pltpu.CompilerParams(has_side_effects=EFFECT),
    )(*bufs, send, recv, after))


def _gather_start(name, bufs, after=()):
    nt = len(bufs)

    def issue(g, send, recv):
        x, y, c, chips = _place()
        me = 2 * x + y
        for t in range(nt):
            part = g[t].at[me, _half(c, bufs[t].shape[1]), :]
            for j, (qx, qy) in enumerate(chips):
                _remote(part, part, send.at[3 * t + j], recv.at[3 * t + j], (qx, qy, c)).start()

    return _split_start(name, (3 * nt,), bufs, issue, after)


def _gather_wait(name, bufs, send, recv, after):
    nt = len(bufs)

    def drain(g, send, recv):
        x, y, c, _ = _place()
        for t in range(nt):
            part = g[t].at[0, pl.ds(0, bufs[t].shape[1] // 2), :]
            for j in range(3):
                cp = _remote(part, part, send.at[3 * t + j], recv.at[3 * t + j], (x, y, c))
                cp.wait_send()
                cp.wait_recv()

    return _split_wait(name, bufs, send, recv, after, drain)


def _gather_forward(name, bufs):
    nt = len(bufs)

    def body(*refs):
        g = refs[nt:2 * nt]
        send, recv = refs[2 * nt:]
        x, y, c, chips = _place()
        cps = []
        for t in range(nt):
            for j, (qx, qy) in enumerate(chips):
                part = g[t].at[2 * qx + qy, _half(c, bufs[t].shape[1]), :]
                cps.append(_remote(part, part, send.at[t, j], recv.at[t, j], (x, y, 1 - c)))
        for cp in cps:
            cp.start()
        for t in range(nt):
            for j, (qx, qy) in enumerate(chips):
                theirs = g[t].at[2 * qx + qy, _half(1 - c, bufs[t].shape[1]), :]
                _remote(theirs, theirs, send.at[t, j], recv.at[t, j], (x, y, 1 - c)).wait_recv()
        for cp in cps:
            cp.wait_send()

    return list(pl.pallas_call(
        body, name=name, in_specs=[ANY] * nt, out_specs=[ANY] * nt,
        out_shape=[jax.ShapeDtypeStruct(b.shape, b.dtype) for b in bufs],
        input_output_aliases={t: t for t in range(nt)},
        scratch_shapes=[pltpu.SemaphoreType.DMA((nt, 3)), pltpu.SemaphoreType.DMA((nt, 3))],
        compiler_params=pltpu.CompilerParams(has_side_effects=True),
    )(*bufs))


def _swap_halves(name, grads):
    nt = len(grads)

    def body(*refs):
        g_refs, r_refs = refs[:nt], refs[nt:2 * nt]
        send, recv = refs[2 * nt:]
        x, y, c, _ = _place()
        cps = [_remote(g_refs[t].at[:, _half(1 - c, grads[t].shape[1]), :], r_refs[t], send.at[t], recv.at[t],
                       (x, y, 1 - c)) for t in range(nt)]
        for cp in cps:
            cp.start()
        for cp in cps:
            cp.wait()

    return pl.pallas_call(
        body, name=name, in_specs=[ANY] * nt, out_specs=[ANY] * nt,
        out_shape=[jax.ShapeDtypeStruct((N_CHIPS, g.shape[1] // 2, g.shape[2]), g.dtype) for g in grads],
        scratch_shapes=[pltpu.SemaphoreType.DMA((nt,)), pltpu.SemaphoreType.DMA((nt,))],
        compiler_params=pltpu.CompilerParams(has_side_effects=True),
    )(*grads)


def _scatter_start(name, parts, after=()):
    nt = len(parts)
    lands = [lax.empty((3,) + p.shape[1:], p.dtype) for p in parts]

    def issue(refs, send, recv):
        x, y, c, chips = _place()
        for t in range(nt):
            for j, (qx, qy) in enumerate(chips):
                _remote(refs[t].at[2 * qx + qy], refs[nt + t].at[j], send.at[3 * t + j], recv.at[3 * t + j],
                        (qx, qy, c)).start()

    return _split_start(name, (3 * nt,), list(parts) + lands, issue, after)


def _scatter_wait(name, bufs, send, recv, after):
    nt = len(bufs) // 2

    def drain(refs, send, recv):
        x, y, c, _ = _place()
        for t in range(nt):
            for j in range(3):
                cp = _remote(refs[t].at[0], refs[nt + t].at[j], send.at[3 * t + j], recv.at[3 * t + j], (x, y, c))
                cp.wait_send()
                cp.wait_recv()

    return _split_wait(name, bufs, send, recv, after, drain)


def _join_halves(name, fulls):
    nt = len(fulls)

    def body(*refs):
        o_refs = refs[nt:2 * nt]
        send, recv = refs[2 * nt:]
        x, y, c, _ = _place()
        cps = []
        for t in range(nt):
            half = o_refs[t].at[_half(c, fulls[t].shape[0]), :]
            cps.append(_remote(half, half, send.at[t], recv.at[t], (x, y, 1 - c)))
        for cp in cps:
            cp.start()
        for t in range(nt):
            theirs = o_refs[t].at[_half(1 - c, fulls[t].shape[0]), :]
            _remote(theirs, theirs, send.at[t], recv.at[t], (x, y, 1 - c)).wait_recv()
        for cp in cps:
            cp.wait_send()

    return list(pl.pallas_call(
        body, name=name, in_specs=[ANY] * nt, out_specs=[ANY] * nt,
        out_shape=[jax.ShapeDtypeStruct(a.shape, a.dtype) for a in fulls],
        input_output_aliases={t: t for t in range(nt)},
        scratch_shapes=[pltpu.SemaphoreType.DMA((nt,)), pltpu.SemaphoreType.DMA((nt,))],
        compiler_params=pltpu.CompilerParams(has_side_effects=True),
    )(*fulls))


def _add_own_half(name, grad, recv_half, c_arr):
    _, r, cdim = grad.shape
    rh = r // 2
    tr = _half_rows(rh)
    nrb = rh // tr

    def body(c_ref, g_ref, r_ref, o_ref):
        o_ref[...] = (g_ref[...].astype(F32) + r_ref[...].astype(F32)).astype(o_ref.dtype)

    return pl.pallas_call(
        body, name=name,
        grid_spec=pltpu.PrefetchScalarGridSpec(
            num_scalar_prefetch=1, grid=(N_CHIPS, nrb),
            in_specs=[pl.BlockSpec((None, tr, cdim), lambda q, i, cr: (q, cr[0] * nrb + i, 0)),
                      pl.BlockSpec((None, tr, cdim), lambda q, i, cr: (q, i, 0))],
            out_specs=pl.BlockSpec((None, tr, cdim), lambda q, i, cr: (q, i, 0))),
        out_shape=jax.ShapeDtypeStruct((N_CHIPS, rh, cdim), BF16), compiler_params=_cparams(),
    )(c_arr, grad, recv_half)


def _add_chips(name, part, came, place_arr):
    _, rh, cdim = part.shape
    tr = _half_rows(rh)
    nrb = rh // tr

    def body(q_ref, p_ref, r_ref, o_ref):
        acc = p_ref[...].astype(F32)
        for j in range(3):
            acc = acc + r_ref[j].astype(F32)
        o_ref[...] = acc

    return pl.pallas_call(
        body, name=name,
        grid_spec=pltpu.PrefetchScalarGridSpec(
            num_scalar_prefetch=1, grid=(nrb,),
            in_specs=[pl.BlockSpec((None, tr, cdim), lambda i, qr: (qr[0], i, 0)),
                      pl.BlockSpec((3, tr, cdim), lambda i, qr: (0, i, 0))],
            out_specs=pl.BlockSpec((tr, cdim), lambda i, qr: (qr[1] * nrb + i, 0))),
        out_shape=jax.ShapeDtypeStruct((2 * rh, cdim), F32), compiler_params=_cparams(),
    )(place_arr, part, came)


def _reduce_scatter_begin(tag, grads, c_arr, after=()):
    nt = len(grads)
    got = _swap_halves(f"rs_swap_{tag}", grads)
    parts = [_add_own_half(f"rs_add2_{tag}_{t}", grads[t], got[t], c_arr) for t in range(nt)]
    return _scatter_start(f"rs_scatter_start_{tag}", parts, after)


def _reduce_scatter_end(tag, state, after, place_arr):
    send, recv, bufs, _ = state
    nt = len(bufs) // 2
    bufs = _scatter_wait(f"rs_scatter_wait_{tag}", bufs, send, recv, after)
    fulls = [_add_chips(f"rs_add4_{tag}_{t}", bufs[t], bufs[nt + t], place_arr) for t in range(nt)]
    return _join_halves(f"rs_join_{tag}", fulls)


def _allgather_blocks(name, blk):
    m_per, n = blk.shape

    def body(x_ref, out_ref, send_sems, recv_sems, local_sem):
        x, y, c, chips = _place()
        me, sibling = (x, y, c), (x, y, 1 - c)

        def rows(px, py, pc):
            return out_ref.at[pl.ds(pl.multiple_of((4 * px + 2 * py + pc) * m_per, 8), m_per), :]

        def copy(k, block, to, src=None):
            return _remote(rows(*block) if src is None else src, rows(*block), send_sems.at[k], recv_sems.at[k], to)

        mine = pltpu.make_async_copy(x_ref, rows(*me), local_sem)
        mine.start()
        first = [copy(0, me, sibling, src=x_ref)]
        first += [copy(1 + j, me, (*chip, c), src=x_ref) for j, chip in enumerate(chips)]
        for cp in first:
            cp.start()
        passed = [copy(4 + j, (*chip, c), sibling) for j, chip in enumerate(chips)]
        for j, chip in enumerate(chips):
            copy(1 + j, (*chip, c), me).wait_recv()
            passed[j].start()
        copy(0, sibling, me).wait_recv()
        for j, chip in enumerate(chips):
            copy(4 + j, (*chip, 1 - c), me).wait_recv()
        for cp in first + passed:
            cp.wait_send()
        mine.wait()

    return pl.pallas_call(
        body, name=name, out_shape=jax.ShapeDtypeStruct((N_DEV * m_per, n), blk.dtype),
        in_specs=[pl.BlockSpec(memory_space=pltpu.VMEM)], out_specs=pl.BlockSpec(memory_space=pltpu.VMEM),
        scratch_shapes=[pltpu.SemaphoreType.DMA((7,)), pltpu.SemaphoreType.DMA((7,)), pltpu.SemaphoreType.DMA],
        compiler_params=_cparams(),
    )(blk)


def _sum_blocks(name, stacked):
    nd, m, n = stacked.shape
    tr = _pick(m, (336, 256, 168, 128, 64, 32, 16, 8))

    def body(s_ref, o_ref):
        acc = s_ref[0]
        for d in range(1, nd):
            acc = acc + s_ref[d]
        o_ref[...] = acc

    return pl.pallas_call(
        body, name=name, grid=(m // tr,),
        in_specs=[pl.BlockSpec((nd, tr, n), lambda i: (0, i, 0))],
        out_specs=pl.BlockSpec((tr, n), lambda i: (i, 0)),
        out_shape=jax.ShapeDtypeStruct((m, n), F32), compiler_params=_cparams(),
    )(stacked)


def _adamw_vals(w, g, m, v):
    m2 = ADAM_B1 * m + (1.0 - ADAM_B1) * g
    v2 = ADAM_B2 * v + (1.0 - ADAM_B2) * (g * g)
    m_hat = m2 / (1.0 - ADAM_B1 ** ADAM_STEP)
    v_hat = v2 / (1.0 - ADAM_B2 ** ADAM_STEP)
    delta = -ADAM_LR * (m_hat / (jnp.sqrt(v_hat) + ADAM_EPS) + ADAM_WD * w)
    return delta, m2, v2


def _adamw_layer(name, layer, w, m, v, g, stacked, deps=()):
    nl, r, cdim = w.shape
    tr = _pick(r, (128, 64, 32, 16, 8))
    n_d = len(deps)

    def body(w_ref, m_ref, v_ref, g_ref, *rest):
        go_ref, d_ref, mo_ref, vo_ref = rest[4 + n_d:]
        gv = g_ref[...]
        delta, m2, v2 = _adamw_vals(w_ref[...], gv, m_ref[...], v_ref[...])
        go_ref[...] = gv
        d_ref[...] = delta
        mo_ref[...] = m2
        vo_ref[...] = v2

    big = pl.BlockSpec((None, tr, cdim), lambda i: (layer, i, 0))
    return pl.pallas_call(
        body, name=name, grid=(r // tr,),
        in_specs=[big, big, big, pl.BlockSpec((tr, cdim), lambda i: (i, 0))] + [ANY] * (4 + n_d),
        out_specs=[big, big, big, big], out_shape=[jax.ShapeDtypeStruct(w.shape, F32)] * 4,
        input_output_aliases={4 + k: k for k in range(4)}, compiler_params=_cparams(),
    )(w, m, v, g, *stacked, *deps)


def _adamw_small(name, w, g, m, v):
    def body(w_ref, g_ref, m_ref, v_ref, d_ref, mo_ref, vo_ref):
        delta, m2, v2 = _adamw_vals(w_ref[...], g_ref[...], m_ref[...], v_ref[...])
        d_ref[...] = delta
        mo_ref[...] = m2
        vo_ref[...] = v2

    return pl.pallas_call(body, name=name, out_shape=[jax.ShapeDtypeStruct(w.shape, F32)] * 3,
                          compiler_params=_cparams())(w, g, m, v)


def _swiglu_fwd(name, z, f):
    def fn(zz):
        a = zz[:, :f].astype(F32)
        b = zz[:, f:].astype(F32)
        return a * _sig(a) * b
    return _rowwise(name, fn, [z], [(f, BF16)])[0]


def _swiglu_bwd(name, z, ds, f, deps=()):
    def fn(zz, dd):
        a = zz[:, :f].astype(F32)
        b = zz[:, f:].astype(F32)
        d = dd.astype(F32)
        sg = _sig(a)
        da = d * b * (sg * (1.0 + a * (1.0 - sg)))
        db = d * a * sg
        return jnp.concatenate([da, db], axis=1)
    return _rowwise(name, fn, [z, ds], [(2 * f, BF16)], deps=deps)[0]


def _heads(a, nh):
    s, w = a.shape
    return jnp.transpose(a.reshape(s, nh, w // nh), (1, 0, 2))


def _unheads(a):
    nh, s, hd = a.shape
    return jnp.transpose(a, (1, 0, 2)).reshape(s, nh * hd)


def _as_rows(col, tb):
    nh, s, _ = col.shape
    return col.reshape(nh, s // tb, 1, tb)


class _Dims:
    def __init__(self, d, f, aw, nh, cc, taps, dp, s):
        self.d, self.f, self.aw, self.nh, self.cc, self.taps, self.dp, self.s = d, f, aw, nh, cc, taps, dp, s
        self.o_cv, self.o_ga, self.o_gc = 0, 2 * cc, 2 * cc + d
        self.o_q = 2 * cc + 2 * d
        self.n_main = self.o_q + 3 * aw
        self.tb = _pick(s, (256, 128))
        assert self.o_ga % d == 0 and self.o_q % aw == 0 and nh <= LANES


def _ffn_fwd(tag, h, g_row, w_in, w_out, dm, deps=()):
    n = _rms_fwd(f"{tag}_rms", h, g_row, deps)
    z = _mm_nn(f"{tag}_in", n, w_in, BF16)
    s = _swiglu_fwd(f"{tag}_act", z, dm.f)
    h2 = _mm_nn(f"{tag}_out", s, w_out, F32, res=h, scale=FFN_RES)
    return h2, (h, n, z, s)


def _ffn_bwd(tag, dh, dh_half_b, saved, g_row, w_in, w_out, dm, oscale, deps=()):
    h, n, z, s = saved
    ds = _mm_nt(f"{tag}_dout", dh_half_b, w_out, BF16)
    dz = _swiglu_bwd(f"{tag}_dact", z, ds, dm.f, deps)
    dw_out = _mm_tn(f"{tag}_wout", s, dh_half_b, 1)
    dw_in = _mm_tn(f"{tag}_win", n, dz, N_CHIPS)
    dn = _mm_nt(f"{tag}_din", dz, w_in, BF16)
    dh0, dh0_b, dg = _rms_bwd_res(f"{tag}_drms", h, g_row, [dn], dh, oscale)
    return dh0, dh0_b, dg, dw_in, dw_out


def _mixer_fwd(tag, h, sm, wt, dm):
    d, cc, aw, nh, tb = dm.d, dm.cc, dm.aw, dm.nh, dm.tb
    u = _rms_fwd(f"{tag}_rms", h, sm["g_mix"])
    zm = _mm_nn(f"{tag}_in", u, wt["w_main"], BF16)
    zf = _mm_nn(f"{tag}_inf", u, wt["w_f"], F32)
    c = _fgate_fwd(f"{tag}_fgate", zf, sm["b_f"])
    c_col = jnp.transpose(c[:, :nh], (1, 0))[:, :, None]
    c_row = _as_rows(c_col, tb)
    o32, o, lse = _attn2_fwd(f"{tag}_attn", zm, c_col, c_row, dm)
    ya = _mm_nn(f"{tag}_aout", o, wt["w_attn_out"], BF16)
    y, cs = _conv_fwd(f"{tag}_conv", zm, sm["conv_w"], sm["conv_b"], sm["g_conv"], cc, dm.taps)
    yc = _mm_nn(f"{tag}_cout", cs, wt["w_conv_out"], BF16)

    def merge(ga, gc, a, b):
        return _sig(ga.astype(F32)) * a.astype(F32) + _sig(gc.astype(F32)) * b.astype(F32)

    mg = _rowwise(f"{tag}_merge", merge, [(zm, dm.o_ga // d, d), (zm, dm.o_gc // d, d), ya, yc], [(d, BF16)])[0]
    h2 = _mm_nn(f"{tag}_out", mg, wt["w_out"], F32, res=h, scale=1.0)
    return h2, (h, u, zm, zf, c_col, c_row, o32, lse, o, ya, y, cs, yc, mg)


def _mixer_bwd(tag, dh, dh_b, saved, sm, wt, dm, deps=()):
    d, cc, aw, nh, tb = dm.d, dm.cc, dm.aw, dm.nh, dm.tb
    h, u, zm, zf, c_col, c_row, o32, lse, o, ya, y, cs, yc, mg = saved
    dmg = _mm_nt(f"{tag}_dout", dh_b, wt["w_out"], BF16)
    dw_out = _mm_tn(f"{tag}_wout", mg, dh_b, 1)

    def unmerge(dd, ga, gc, a, b):
        dd, a, b = dd.astype(F32), a.astype(F32), b.astype(F32)
        sa, sc = _sig(ga.astype(F32)), _sig(gc.astype(F32))
        return dd * sa, dd * sc, dd * a * sa * (1.0 - sa), dd * b * sc * (1.0 - sc)

    dya, dyc, dga, dgc = _rowwise(f"{tag}_dmerge", unmerge,
                                  [dmg, (zm, dm.o_ga // d, d), (zm, dm.o_gc // d, d), ya, yc], [(d, BF16)] * 4,
                                  deps=deps)
    dw_a = _mm_tn(f"{tag}_waout", o, dya, N_CHIPS)
    do = _mm_nt(f"{tag}_daout", dya, wt["w_attn_out"], BF16)
    dw_c = _mm_tn(f"{tag}_wcout", cs, dyc, N_CHIPS)
    dcs = _mm_nt(f"{tag}_dcout", dyc, wt["w_conv_out"], BF16)
    dcv, dconv_w, dconv_b, dg_conv = _conv_bwd(f"{tag}_dconv", zm, y, dcs, sm["conv_w"], sm["g_conv"], cc, dm.taps)
    dq, delta, dcq_h = _attn2_bwd_q(f"{tag}_dattn_q", zm, o32, do, lse, c_col, c_row, dm)
    dk, dv, dck_h = _attn2_bwd_kv(f"{tag}_dattn_kv", zm, do, _as_rows(lse, tb), _as_rows(delta, tb),
                                  c_col, c_row, dm)
    dc = jnp.pad(jnp.transpose(dcq_h[:, :, 0], (1, 0)), ((0, 0), (0, zf.shape[1] - nh)))
    dc_k = jnp.pad(jnp.transpose(dck_h[:, :, 0], (1, 0)), ((0, 0), (0, zf.shape[1] - nh)))
    dzf, dzf_b, db_f = _fgate_bwd(f"{tag}_dfgate", dc, dc_k, zf, sm["b_f"])
    dzm = jnp.concatenate([dcv, dga, dgc, dq, dk, dv], axis=1)
    dw_main = _mm_tn(f"{tag}_win", u, dzm, 1)
    dw_f = _mm_tn(f"{tag}_winf", u, dzf_b, 1)
    du = _mm_nt(f"{tag}_din", dzm, wt["w_main"], BF16)
    du_f = _mm_nt(f"{tag}_dinf", dzf_b, wt["w_f"], BF16)
    dh0, dh0_b, dg_mix = _rms_bwd_res(f"{tag}_drms", h, sm["g_mix"], [du, du_f], dh, FFN_RES)
    small = dict(g_mix=dg_mix, b_f=db_f[:, :nh], conv_w=dconv_w[:dm.taps], conv_b=dconv_b, g_conv=dg_conv)
    return dh0, dh0_b, small, dw_main, dw_f, dw_a, dw_c, dw_out


def _ple_fwd(tag, h, p_b, sm, wt, dm):
    n = _rms_fwd(f"{tag}_rms", h, sm["g_ple"])
    gp = _mm_nn(f"{tag}_gate", n, wt["w_ple_gate"], BF16)
    pp = _mm_nn(f"{tag}_proj", p_b, wt["w_ple_proj"], BF16)

    def fn(hh, a, b):
        return hh + _sig(a.astype(F32)) * b.astype(F32)

    h2 = _rowwise(f"{tag}_mix", fn, [h, gp, pp], [(dm.d, F32)])[0]
    return h2, (h, n, gp, pp)


def _ple_bwd(tag, dh, saved, p_b, sm, wt, dm, deps=()):
    h, n, gp, pp = saved

    def fn(dd, a, b):
        gate = _sig(a.astype(F32))
        b = b.astype(F32)
        return dd * b * gate * (1.0 - gate), dd * gate

    dgp, dpp = _rowwise(f"{tag}_dmix", fn, [dh, gp, pp], [(dm.d, BF16)] * 2, deps=deps)
    dw_proj = _mm_tn(f"{tag}_wproj", p_b, dpp, N_CHIPS)
    dw_gate = _mm_tn(f"{tag}_wgate", n, dgp, 1)
    dn = _mm_nt(f"{tag}_dgate", dgp, wt["w_ple_gate"], BF16)
    dh0, dh0_b, dg = _rms_bwd_res(f"{tag}_drms", h, sm["g_ple"], [dn], dh, FFN_RES)
    return dh0, dh0_b, dg, dw_gate, dw_proj


def _loss_head(name, h, g_row, target):
    d = h.shape[1]

    def fn(x, tg, g):
        r = lax.rsqrt(jnp.mean(x * x, axis=-1, keepdims=True) + RMS_EPS)
        out = x * r * g
        e = out - tg
        per_row = jnp.sum(e * e, axis=-1, keepdims=True) * (0.5 / d)
        loss = jnp.zeros((1, LANES), F32) + jnp.sum(per_row, axis=0, keepdims=True)
        dx, dg = _rms_bwd_vals(x, g, e * (1.0 / d))
        return dx, loss, dg

    return _rowwise(name, fn, [h, target], [(d, F32)], consts=[g_row], reds=[((1, LANES), F32), ((1, d), F32)])


_BIG = ("w_ff1_in", "w_ff1_out", "w_in", "w_attn_out", "w_conv_out", "w_out", "w_ff2_in", "w_ff2_out",
        "w_ple_gate", "w_ple_proj")
_SMALL = ("g_ff1", "g_mix", "b_f", "conv_w", "conv_b", "g_conv", "g_ff2", "g_ple")
_ORDER = ("g_ff1", "w_ff1_in", "w_ff1_out", "g_mix", "w_in", "b_f", "w_attn_out", "conv_w", "conv_b", "g_conv",
          "w_conv_out", "w_out", "g_ff2", "w_ff2_in", "w_ff2_out", "g_ple", "w_ple_gate", "w_ple_proj", "g_final")


def _round_up(n, k):
    return (n + k - 1) // k * k


_ROW_SHARDED = ("w_ff1_out", "w_out", "w_ff2_out", "w_ple_gate")
_FIRST_LAYER_GROUPS = ((0, 1), (2, 3, 4, 5), (6, 7, 8, 9))


def _columns(pieces, lo, hi):
    out, pos = [], 0
    for a in pieces:
        w = a.shape[1]
        s, e = max(lo, pos), min(hi, pos + w)
        if s < e:
            out.append(a[:, s - pos:e - pos])
        pos += w
    return out


def _unpack_layer(g, dm):
    out = {}
    for n, a in g.items():
        if n == "w_in":
            blocks = [a[j] for j in range(N_CHIPS)]
            o_f = 3 * dm.aw
            o_c = o_f + dm.nh
            total = N_CHIPS * a.shape[2]
            out["w_main"] = jnp.concatenate(_columns(blocks, o_c, total) + _columns(blocks, 0, o_f), axis=1)[None]
            out["w_f"] = jnp.pad(jnp.concatenate(_columns(blocks, o_f, o_c), axis=1),
                                 ((0, 0), (0, LANES - dm.nh)))[None]
        elif n in _ROW_SHARDED:
            out[n] = a.reshape(1, a.shape[0] * a.shape[1], a.shape[2])
        else:
            out[n] = a
    return out


def _pack_grads(gw, names, dm):
    out = []
    for n in names:
        if n == "w_in":
            o_f = 3 * dm.aw
            main, wf = gw["w_main"][0], gw["w_f"][0]
            n_rest = dm.n_main - o_f
            pieces = [main[:, n_rest:], wf[:, :dm.nh], main[:, :n_rest]]
            nb = (dm.n_main + dm.nh) // N_CHIPS
            out.append(jnp.stack([jnp.concatenate(_columns(pieces, j * nb, (j + 1) * nb), axis=1)
                                  for j in range(N_CHIPS)]))
        elif n in _ROW_SHARDED:
            a = gw[n]
            out.append(a.reshape(N_CHIPS, a.shape[1] // N_CHIPS, a.shape[2]))
        else:
            out.append(gw[n])
    return out


def kernel(x, p, g_ff1, w_ff1_in, w_ff1_out, g_mix, w_in, b_f, w_attn_out, conv_w, conv_b, g_conv, w_conv_out, w_out, g_ff2, w_ff2_in, w_ff2_out, g_ple, w_ple_gate, w_ple_proj, g_final, loss_target, m_g_ff1, m_w_ff1_in, m_w_ff1_out, m_g_mix, m_w_in, m_b_f, m_w_attn_out, m_conv_w, m_conv_b, m_g_conv, m_w_conv_out, m_w_out, m_g_ff2, m_w_ff2_in, m_w_ff2_out, m_g_ple, m_w_ple_gate, m_w_ple_proj, m_g_final, v_g_ff1, v_w_ff1_in, v_w_ff1_out, v_g_mix, v_w_in, v_b_f, v_w_attn_out, v_conv_w, v_conv_b, v_g_conv, v_w_conv_out, v_w_out, v_g_ff2, v_w_ff2_in, v_w_ff2_out, v_g_ple, v_w_ple_gate, v_w_ple_proj, v_g_final):
    args = dict(locals())
    wts = {n: args[n] for n in _ORDER}
    mom = {n: args["m_" + n] for n in _ORDER}
    var = {n: args["v_" + n] for n in _ORDER}

    nl = g_ff1.shape[0]
    s, d = x.shape[1], x.shape[2]
    nh = b_f.shape[1]
    taps = conv_w.shape[1]
    cc = conv_b.shape[1]
    dm = _Dims(d=d, f=w_ff1_out.shape[1] * N_CHIPS, aw=w_attn_out.shape[1], nh=nh, cc=cc, taps=taps,
               dp=w_ple_proj.shape[1], s=s)
    assert taps - 1 <= HALO

    xi = lax.axis_index("x")
    yi = lax.axis_index("y")
    ci = lax.axis_index("c")
    c_arr = jnp.reshape(ci, (1,)).astype(jnp.int32)
    chip_arr = jnp.reshape(2 * xi + yi, (1,)).astype(jnp.int32)
    place_arr = jnp.stack([2 * xi + yi, ci]).astype(jnp.int32)

    h = x[0]
    target = loss_target[0]
    p_b = p[:, 0].astype(BF16)

    cw_rows = _round_up(nl * taps, 8)
    cw_blk = jnp.pad(conv_w.reshape(nl * taps, -1), ((0, cw_rows - nl * taps), (0, 0)))
    cw_all = _allgather_blocks("gather_conv_w", cw_blk).reshape(N_CHIPS, 2, cw_rows, -1)[:, 0, :nl * taps]
    conv_w_full = jnp.transpose(cw_all.reshape(N_CHIPS, nl, taps, -1), (1, 2, 0, 3)).reshape(nl, taps, cc)
    taps_pad = _round_up(taps, 8)

    def small_of(i):
        row = lambda a: a[i][None, :]
        return dict(g_ff1=row(g_ff1), g_mix=row(g_mix), g_conv=row(g_conv), conv_b=row(conv_b), g_ff2=row(g_ff2),
                    g_ple=row(g_ple), b_f=jnp.pad(b_f[i][None, :], ((0, 0), (0, LANES - nh))),
                    conv_w=jnp.pad(conv_w_full[i], ((0, taps_pad - taps), (0, 0))))

    slots = [_cast_slots(f"cast_{n}", wts[n], chip_arr) for n in _BIG]
    started = {}

    def start_gather(i):
        if i < nl:
            groups = _FIRST_LAYER_GROUPS if i == 0 else (tuple(range(len(_BIG))),)
            started[i] = [(g, _gather_start(f"gather_start_l{i}g{k}", [slots[t][i] for t in g], after=[conv_w_full]))
                          for k, g in enumerate(groups)]

    start_gather(0)
    start_gather(1)
    layer_w, saved = [], []
    for i in range(nl):
        pend = started.pop(i)
        wt = {}

        def arrive(k, h_now, i=i, pend=pend, wt=wt):
            if k < len(pend):
                g, (send, recv, bufs, _) = pend[k]
                bufs = _gather_wait(f"gather_wait_l{i}g{k}", bufs, send, recv, h_now)
                bufs = _gather_forward(f"gather_fwd_l{i}g{k}", bufs)
                wt.update(_unpack_layer({_BIG[t]: b for t, b in zip(g, bufs)}, dm))

        arrive(0, h)
        start_gather(i + 2)
        in_flight = [st[3] for _, st in pend[1:]] + [st[3] for sts in started.values() for _, st in sts]
        sm = small_of(i)
        h, sv1 = _ffn_fwd(f"l{i}_ff1", h, sm["g_ff1"], wt["w_ff1_in"], wt["w_ff1_out"], dm, deps=in_flight)
        arrive(1, h)
        h, sv2 = _mixer_fwd(f"l{i}_mix", h, sm, wt, dm)
        arrive(2, h)
        h, sv3 = _ffn_fwd(f"l{i}_ff2", h, sm["g_ff2"], wt["w_ff2_in"], wt["w_ff2_out"], dm)
        h, sv4 = _ple_fwd(f"l{i}_ple", h, p_b[i], sm, wt, dm)
        layer_w.append((wt, sm))
        saved.append((sv1, sv2, sv3, sv4))

    dh, loss_row, dg_final = _loss_head("loss_head", h, g_final[None, :], target)

    big_grads = [None] * nl
    small_grads = [None] * nl
    leaving = None

    def begin_first(k, gw, after=()):
        names = [_BIG[t] for t in _FIRST_LAYER_GROUPS[k]]
        return _reduce_scatter_begin(f"l0g{k}", _pack_grads(gw, names, dm), c_arr, after=after)

    for i in range(nl - 1, -1, -1):
        wt, sm = layer_w[i]
        sv1, sv2, sv3, sv4 = saved[i]
        deps = [leaving[3]] if leaving is not None else []
        dh, dh_b, dg_ple, dw_gate, dw_proj = _ple_bwd(f"l{i}_ple", dh, sv4, p_b[i], sm, wt, dm, deps=deps)
        dh, dh_b, dg_ff2, dw_in2, dw_out2 = _ffn_bwd(f"l{i}_ff2", dh, dh_b, sv3, sm["g_ff2"], wt["w_ff2_in"],
                                                     wt["w_ff2_out"], dm, 1.0)
        gw = dict(w_ff2_in=dw_in2, w_ff2_out=dw_out2, w_ple_gate=dw_gate, w_ple_proj=dw_proj)
        first = [begin_first(2, gw)] if i == 0 else []
        dh, dh_b, sg, dw_main, dw_f, dw_a, dw_c, dw_o = _mixer_bwd(f"l{i}_mix", dh, dh_b, sv2, sm, wt, dm,
                                                                   deps=[st[3] for st in first])
        gw.update(w_main=dw_main, w_f=dw_f, w_attn_out=dw_a, w_conv_out=dw_c, w_out=dw_o)
        if i == 0:
            first.append(begin_first(1, gw))
        dh, dh_b, dg_ff1, dw_in1, dw_out1 = _ffn_bwd(f"l{i}_ff1", dh, dh_b, sv1, sm["g_ff1"], wt["w_ff1_in"],
                                                     wt["w_ff1_out"], dm, 1.0, deps=[st[3] for st in first[1:]])
        gw.update(w_ff1_in=dw_in1, w_ff1_out=dw_out1)
        if leaving is not None:
            big_grads[i + 1] = _reduce_scatter_end(f"l{i + 1}", leaving, dh, place_arr)
        sg.update(g_ff1=dg_ff1, g_ff2=dg_ff2, g_ple=dg_ple)
        small_grads[i] = sg
        if i > 0:
            leaving = _reduce_scatter_begin(f"l{i}", _pack_grads(gw, _BIG, dm), c_arr)
    grad_x = dh[None]

    pieces = [small_grads[i][n].reshape(-1) for i in range(nl) for n in _SMALL]
    pieces += [dg_final.reshape(-1), loss_row[0, :1]]
    flat = jnp.concatenate(pieces)
    n_flat = flat.shape[0]
    rows = _round_up(_round_up(n_flat, LANES) // LANES, 8)
    blk = jnp.pad(flat, (0, rows * LANES - n_flat)).reshape(rows, LANES)
    total = _sum_blocks("sum_small", _allgather_blocks("gather_small", blk).reshape(N_DEV, rows, LANES)).reshape(-1)
    small_tot = {n: [] for n in _SMALL}
    pos = 0
    for i in range(nl):
        for n in _SMALL:
            shp = small_grads[i][n].shape
            size = shp[0] * shp[1]
            small_tot[n].append(total[pos:pos + size].reshape(shp))
            pos += size
    g_final_tot = total[pos:pos + d]
    loss = total[pos + d]

    first.append(begin_first(0, gw, after=[total]))
    stacked = {n: [lax.empty(wts[n].shape, F32) for _ in range(4)] for n in _BIG}

    def adamw_big(i, deps):
        for t, n in enumerate(_BIG):
            stacked[n] = _adamw_layer(f"adamw_{n}_l{i}", i, wts[n], mom[n], var[n], big_grads[i][t], stacked[n], deps)

    for i in range(nl - 1, 0, -1):
        adamw_big(i, [first[-1][3]])
    after = stacked[_BIG[-1]][1] if nl > 1 else dh
    big_grads[0] = [None] * len(_BIG)
    for k, state in zip((2, 1, 0), first):
        for t, g in zip(_FIRST_LAYER_GROUPS[k], _reduce_scatter_end(f"l0g{k}", state, after, place_arr)):
            big_grads[0][t] = g
    adamw_big(0, [])
    grads, deltas, new_m, new_v = {}, {}, {}, {}
    for n in _BIG:
        grads[n], deltas[n], new_m[n], new_v[n] = stacked[n]
    chip = 2 * xi + yi
    for n in _SMALL:
        g = jnp.concatenate(small_tot[n], axis=0)
        if n == "conv_w":
            cpc = cc // N_CHIPS
            g = lax.dynamic_slice_in_dim(g.reshape(nl * taps, cc), chip * cpc, cpc, axis=1)
            shape2 = (nl * taps, cpc)
        else:
            shape2 = g.shape
        dl, mm, vv = _adamw_small(f"adamw_{n}", wts[n].reshape(shape2), g, mom[n].reshape(shape2),
                                  var[n].reshape(shape2))
        grads[n] = g.reshape(wts[n].shape)
        deltas[n], new_m[n], new_v[n] = (a.reshape(wts[n].shape) for a in (dl, mm, vv))
    g2 = g_final_tot[None, :]
    dl, mm, vv = _adamw_small("adamw_g_final", g_final[None, :], g2, m_g_final[None, :], v_g_final[None, :])
    grads["g_final"] = g_final_tot
    deltas["g_final"], new_m["g_final"], new_v["g_final"] = dl[0], mm[0], vv[0]

    return (loss, grad_x, *[grads[n] for n in _ORDER], *[deltas[n] for n in _ORDER],
            *[new_m[n] for n in _ORDER], *[new_v[n] for n in _ORDER])
```

```python
import functools

import jax
import jax.numpy as jnp
from jax import lax
from jax.experimental import pallas as pl
from jax.experimental.pallas import tpu as pltpu

F32 = jnp.float32
BF16 = jnp.bfloat16
MESH = pl.DeviceIdType.MESH
ANY = pl.BlockSpec(memory_space=pl.ANY)
HBM_SPEC = pl.BlockSpec(memory_space=pltpu.HBM)
SEM_SPEC = pl.BlockSpec(memory_space=pltpu.SEMAPHORE)
EFFECT = pltpu.SideEffectType.DATAFLOW_SIDE_EFFECTING

RMS_EPS = 1e-6
FFN_RES = 0.5
ADAM_LR = 0.001
ADAM_B1 = 0.9
ADAM_B2 = 0.999
ADAM_EPS = 1e-08
ADAM_WD = 0.01
ADAM_STEP = 10

N_CHIPS = 4
N_DEV = 8
LANES = 128
HALO = 32
VMEM_LIMIT = 56 * 1024 * 1024


def _cparams():
    return pltpu.CompilerParams(vmem_limit_bytes=VMEM_LIMIT)


def _pick(n, prefs):
    for p in prefs:
        if p <= n and n % p == 0:
            return p
    return n


def _half_rows(r):
    return r // 2 if r % 32 == 0 else r


def _sig(x):
    return 1.0 / (1.0 + jnp.exp(-x))


def _rowwise(name, fn, ins, outs, consts=(), reds=(), tm=None, deps=()):
    ins = [a if isinstance(a, tuple) else (a, 0, a.shape[1]) for a in ins]
    m = ins[0][0].shape[0]
    tm = tm or _pick(m, (256, 128, 64, 32, 16, 8))
    n_in, n_c, n_o, n_r = len(ins), len(consts), len(outs), len(reds)
    n_d = len(deps)
    consts = list(consts) + list(deps)

    def body(*refs):
        in_refs = refs[:n_in + n_c]
        o_refs = refs[n_in + n_c + n_d:n_in + n_c + n_d + n_o]
        r_refs = refs[n_in + n_c + n_d + n_o:]
        res = fn(*[r[...] for r in in_refs])
        if not isinstance(res, (tuple, list)):
            res = (res,)
        for r, v in zip(o_refs, res[:n_o]):
            r[...] = v.astype(r.dtype)
        if n_r:
            @pl.when(pl.program_id(0) == 0)
            def _():
                for r in r_refs:
                    r[...] = jnp.zeros(r.shape, r.dtype)
            for r, v in zip(r_refs, res[n_o:]):
                r[...] += v.astype(r.dtype)

    in_specs = [pl.BlockSpec((tm, w), functools.partial(lambda i, cb: (i, cb), cb=cb)) for (_, cb, w) in ins]
    in_specs += [pl.BlockSpec(c.shape, lambda i: (0, 0)) for c in consts]
    out_specs = [pl.BlockSpec((tm, w), lambda i: (i, 0)) for (w, _) in outs]
    out_specs += [pl.BlockSpec(s, lambda i: (0, 0)) for (s, _) in reds]
    out_shape = [jax.ShapeDtypeStruct((m, w), d) for (w, d) in outs]
    out_shape += [jax.ShapeDtypeStruct(s, d) for (s, d) in reds]
    res = pl.pallas_call(
        body, name=name, grid=(m // tm,), in_specs=in_specs, out_specs=out_specs, out_shape=out_shape,
        compiler_params=_cparams(),
    )(*[a for (a, _, _) in ins], *consts)
    return res


def _rms_fwd(name, h, g, deps=()):
    def fn(x, gg):
        r = lax.rsqrt(jnp.mean(x * x, axis=-1, keepdims=True) + RMS_EPS)
        return x * r * gg
    return _rowwise(name, fn, [h], [(h.shape[1], BF16)], consts=[g], deps=deps)[0]


def _rms_bwd_vals(x, g, dn):
    r = lax.rsqrt(jnp.mean(x * x, axis=-1, keepdims=True) + RMS_EPS)
    xh = x * r
    dxh = dn * g
    dx = r * (dxh - xh * jnp.mean(dxh * xh, axis=-1, keepdims=True))
    dg = jnp.sum(dn * xh, axis=0, keepdims=True)
    return dx, dg


def _rms_bwd_res(name, h, g, dns, dres, oscale):
    n_dn = len(dns)

    def fn(x, *rest):
        dn = rest[0].astype(F32)
        for t in rest[1:n_dn]:
            dn = dn + t.astype(F32)
        dr, gg = rest[n_dn], rest[n_dn + 1]
        dx, dg = _rms_bwd_vals(x, gg, dn)
        dh = dr + dx
        return dh, oscale * dh, dg

    d = h.shape[1]
    return _rowwise(name, fn, [h, *dns, dres], [(d, F32), (d, BF16)], consts=[g], reds=[((1, d), F32)])


_TN_PREFS = (1408, 1024, 768, 512, 256, 128)


def _mm_nn(name, a, w, out_dtype, res=None, scale=1.0):
    m, k = a.shape
    j, _, nb = w.shape
    tm = _pick(m, (512, 256, 128))
    tn = _pick(nb, _TN_PREFS)
    tpb = nb // tn
    has_res = res is not None

    def body(a_ref, w_ref, *rest):
        o_ref = rest[-1]
        acc = jnp.dot(a_ref[...], w_ref[...], preferred_element_type=F32)
        if has_res:
            acc = rest[0][...] + scale * acc
        o_ref[...] = acc.astype(o_ref.dtype)

    in_specs = [pl.BlockSpec((tm, k), lambda n, i: (i, 0)),
                pl.BlockSpec((None, k, tn), lambda n, i: (n // tpb, 0, n % tpb))]
    args = [a, w]
    if has_res:
        in_specs.append(pl.BlockSpec((tm, tn), lambda n, i: (i, n)))
        args.append(res)
    return pl.pallas_call(
        body, name=name, grid=(j * tpb, m // tm), in_specs=in_specs,
        out_specs=pl.BlockSpec((tm, tn), lambda n, i: (i, n)),
        out_shape=jax.ShapeDtypeStruct((m, j * nb), out_dtype), compiler_params=_cparams(),
    )(*args)


def _mm_nt(name, dy, w, out_dtype):
    m, n = dy.shape
    j, k, nb = w.shape
    tm = _pick(m, (512, 256, 128))
    to = _pick(k, _TN_PREFS)
    tc = _pick(nb, (1536, 1408, 1024, 512, 256, 128))
    cpb = nb // tc
    n_red = j * cpb

    def body(dy_ref, w_ref, o_ref, acc_ref):
        r = pl.program_id(2)

        @pl.when(r == 0)
        def _():
            acc_ref[...] = jnp.zeros(acc_ref.shape, F32)

        acc_ref[...] += lax.dot_general(dy_ref[...], w_ref[...], (((1,), (1,)), ((), ())),
                                        preferred_element_type=F32)

        @pl.when(r == n_red - 1)
        def _():
            o_ref[...] = acc_ref[...].astype(o_ref.dtype)

    return pl.pallas_call(
        body, name=name, grid=(k // to, m // tm, n_red),
        in_specs=[pl.BlockSpec((tm, tc), lambda ko, i, r: (i, r)),
                  pl.BlockSpec((None, to, tc), lambda ko, i, r: (r // cpb, ko, r % cpb))],
        out_specs=pl.BlockSpec((tm, to), lambda ko, i, r: (i, ko)),
        out_shape=jax.ShapeDtypeStruct((m, k), out_dtype),
        scratch_shapes=[pltpu.VMEM((tm, to), F32)], compiler_params=_cparams(),
    )(dy, w)


def _mm_tn(name, a, dy, j):
    m, k = a.shape
    n = dy.shape[1]
    nb = n // j
    tk = _pick(k, (512, 256, 128))
    tn = _pick(nb, _TN_PREFS)
    tpb = nb // tn

    def body(a_ref, dy_ref, o_ref):
        o_ref[...] = lax.dot_general(a_ref[...], dy_ref[...], (((0,), (0,)), ((), ())),
                                     preferred_element_type=F32).astype(o_ref.dtype)

    return pl.pallas_call(
        body, name=name, grid=(k // tk, j * tpb),
        in_specs=[pl.BlockSpec((m, tk), lambda kb, nn: (0, kb)),
                  pl.BlockSpec((m, tn), lambda kb, nn: (0, nn))],
        out_specs=pl.BlockSpec((None, tk, tn), lambda kb, nn: (nn // tpb, kb, nn % tpb)),
        out_shape=jax.ShapeDtypeStruct((j, k, nb), BF16), compiler_params=_cparams(),
    )(a, dy)


def _cumsum_rows(x_ref, o_ref, blk, reverse):
    s = x_ref.shape[0]
    nblk = s // blk
    ri = lax.broadcasted_iota(jnp.int32, (blk, blk), 0)
    ci = lax.broadcasted_iota(jnp.int32, (blk, blk), 1)
    tri = jnp.where((ci >= ri) if reverse else (ci <= ri), 1.0, 0.0).astype(F32)
    carry = jnp.zeros((1, x_ref.shape[1]), F32)
    order = range(nblk - 1, -1, -1) if reverse else range(nblk)
    for b in order:
        xb = x_ref[b * blk:(b + 1) * blk, :]
        o_ref[b * blk:(b + 1) * blk, :] = jnp.dot(tri, xb, preferred_element_type=F32,
                                                  precision=lax.Precision.HIGHEST) + carry
        carry = carry + jnp.sum(xb, axis=0, keepdims=True)


def _fgate_fwd(name, zf, bf_row):
    s, w = zf.shape
    blk = _pick(s, (256, 128))

    def body(z_ref, b_ref, c_ref, ls_ref):
        v = z_ref[...] + b_ref[...]
        ls_ref[...] = jnp.minimum(v, 0.0) - jnp.log(1.0 + jnp.exp(-jnp.abs(v)))
        _cumsum_rows(ls_ref, c_ref, blk, reverse=False)

    return pl.pallas_call(
        body, name=name, out_shape=jax.ShapeDtypeStruct((s, w), F32),
        scratch_shapes=[pltpu.VMEM((s, w), F32)], compiler_params=_cparams(),
    )(zf, bf_row)


def _fgate_bwd(name, dc_q, dc_k, zf, bf_row):
    s, w = zf.shape
    blk = _pick(s, (256, 128))

    def body(dcq_ref, dck_ref, z_ref, b_ref, dz_ref, dzb_ref, db_ref, dls_ref, dc_ref):
        dc_ref[...] = dcq_ref[...] + dck_ref[...]
        _cumsum_rows(dc_ref, dls_ref, blk, reverse=True)
        dz = dls_ref[...] * _sig(-(z_ref[...] + b_ref[...]))
        dz_ref[...] = dz
        dzb_ref[...] = dz.astype(BF16)
        db_ref[...] = jnp.sum(dz, axis=0, keepdims=True)

    return pl.pallas_call(
        body, name=name,
        out_shape=[jax.ShapeDtypeStruct((s, w), F32), jax.ShapeDtypeStruct((s, w), BF16),
                   jax.ShapeDtypeStruct((1, w), F32)],
        scratch_shapes=[pltpu.VMEM((s, w), F32), pltpu.VMEM((s, w), F32)], compiler_params=_cparams(),
    )(dc_q, dc_k, zf, bf_row)


def _scores(q, k, cq_col, ck_row, scale, row0, col0):
    s = lax.dot_general(q, k, (((1,), (1,)), ((), ())), preferred_element_type=F32) * scale
    s = s + (cq_col - ck_row)
    rows = row0 + lax.broadcasted_iota(jnp.int32, s.shape, 0)
    cols = col0 + lax.broadcasted_iota(jnp.int32, s.shape, 1)
    return jnp.where(cols <= rows, s, -jnp.inf)


def _attn_fwd(name, q, k, v, c_col, c_row, tb):
    h, s, hd = q.shape
    nb = s // tb
    scale = 1.0 / float(hd) ** 0.5

    def body(q_ref, k_ref, v_ref, cq_ref, ck_ref, o_ref, lse_ref):
        i = pl.program_id(1)
        qv = q_ref[...]
        cq = cq_ref[...]

        def step(j, carry):
            m_i, l_i, acc = carry
            k0 = pl.multiple_of(j * tb, tb)
            sc = _scores(qv, k_ref[pl.ds(k0, tb), :], cq, ck_ref[j], scale, i * tb, j * tb)
            m_new = jnp.maximum(m_i, jnp.max(sc, axis=-1, keepdims=True))
            alpha = jnp.exp(m_i - m_new)
            p = jnp.exp(sc - m_new)
            l_new = alpha * l_i + jnp.sum(p, axis=-1, keepdims=True)
            acc = alpha * acc + jnp.dot(p.astype(BF16), v_ref[pl.ds(k0, tb), :], preferred_element_type=F32)
            return m_new, l_new, acc

        init = (jnp.full((tb, 1), -jnp.inf, F32), jnp.zeros((tb, 1), F32), jnp.zeros((tb, hd), F32))
        m_i, l_i, acc = lax.fori_loop(0, i + 1, step, init)
        o_ref[...] = (acc / l_i).astype(o_ref.dtype)
        lse_ref[...] = m_i + jnp.log(l_i)

    return pl.pallas_call(
        body, name=name, grid=(h, nb),
        in_specs=[pl.BlockSpec((None, tb, hd), lambda hh, i: (hh, i, 0)),
                  pl.BlockSpec((None, s, hd), lambda hh, i: (hh, 0, 0)),
                  pl.BlockSpec((None, s, hd), lambda hh, i: (hh, 0, 0)),
                  pl.BlockSpec((None, tb, 1), lambda hh, i: (hh, i, 0)),
                  pl.BlockSpec((None, nb, 1, tb), lambda hh, i: (hh, 0, 0, 0))],
        out_specs=[pl.BlockSpec((None, tb, hd), lambda hh, i: (hh, i, 0)),
                   pl.BlockSpec((None, tb, 1), lambda hh, i: (hh, i, 0))],
        out_shape=[jax.ShapeDtypeStruct((h, s, hd), F32), jax.ShapeDtypeStruct((h, s, 1), F32)],
        compiler_params=_cparams(),
    )(q, k, v, c_col, c_row)


def _attn_bwd_q(name, q, k, v, o, do, lse, c_col, c_row, tb):
    h, s, hd = q.shape
    nb = s // tb
    scale = 1.0 / float(hd) ** 0.5

    def body(q_ref, k_ref, v_ref, o_ref, do_ref, lse_ref, cq_ref, ck_ref, dq_ref, dl_ref, dcq_ref):
        i = pl.program_id(1)
        qv = q_ref[...]
        dov = do_ref[...]
        cq = cq_ref[...]
        lse_v = lse_ref[...]
        delta = jnp.sum(dov.astype(F32) * o_ref[...], axis=-1, keepdims=True)

        def step(j, carry):
            dq, dcq = carry
            k0 = pl.multiple_of(j * tb, tb)
            kj = k_ref[pl.ds(k0, tb), :]
            p = jnp.exp(_scores(qv, kj, cq, ck_ref[j], scale, i * tb, j * tb) - lse_v)
            dp = lax.dot_general(dov, v_ref[pl.ds(k0, tb), :], (((1,), (1,)), ((), ())), preferred_element_type=F32)
            ds = p * (dp - delta)
            return (dq + jnp.dot(ds.astype(BF16), kj, preferred_element_type=F32),
                    dcq + jnp.sum(ds, axis=-1, keepdims=True))

        dq, dcq = lax.fori_loop(0, i + 1, step, (jnp.zeros((tb, hd), F32), jnp.zeros((tb, 1), F32)))
        dq_ref[...] = (dq * scale).astype(dq_ref.dtype)
        dl_ref[...] = delta
        dcq_ref[...] = dcq

    blk = pl.BlockSpec((None, tb, hd), lambda hh, i: (hh, i, 0))
    full = pl.BlockSpec((None, s, hd), lambda hh, i: (hh, 0, 0))
    col = pl.BlockSpec((None, tb, 1), lambda hh, i: (hh, i, 0))
    return pl.pallas_call(
        body, name=name, grid=(h, nb),
        in_specs=[blk, full, full, blk, blk, col, col,
                  pl.BlockSpec((None, nb, 1, tb), lambda hh, i: (hh, 0, 0, 0))],
        out_specs=[blk, col, col],
        out_shape=[jax.ShapeDtypeStruct((h, s, hd), BF16), jax.ShapeDtypeStruct((h, s, 1), F32),
                   jax.ShapeDtypeStruct((h, s, 1), F32)],
        compiler_params=_cparams(),
    )(q, k, v, o, do, lse, c_col, c_row)


def _attn_bwd_kv(name, q, k, v, do, lse_row, delta_row, c_col, c_row, tb):
    h, s, hd = q.shape
    nb = s // tb
    scale = 1.0 / float(hd) ** 0.5

    def body(q_ref, k_ref, v_ref, do_ref, lse_ref, dl_ref, ck_ref, cq_ref, dk_ref, dv_ref, dc_ref):
        j = pl.program_id(1)
        kv = k_ref[...]
        vv = v_ref[...]
        ck = ck_ref[...]

        def step(i, carry):
            dk, dv, dc = carry
            q0 = pl.multiple_of(i * tb, tb)
            qi = q_ref[pl.ds(q0, tb), :]
            doi = do_ref[pl.ds(q0, tb), :]
            st = lax.dot_general(kv, qi, (((1,), (1,)), ((), ())), preferred_element_type=F32) * scale
            st = st + (cq_ref[i] - ck)
            krow = j * tb + lax.broadcasted_iota(jnp.int32, st.shape, 0)
            qcol = i * tb + lax.broadcasted_iota(jnp.int32, st.shape, 1)
            pt = jnp.exp(jnp.where(krow <= qcol, st, -jnp.inf) - lse_ref[i])
            dv = dv + jnp.dot(pt.astype(BF16), doi, preferred_element_type=F32)
            dpt = lax.dot_general(vv, doi, (((1,), (1,)), ((), ())), preferred_element_type=F32)
            dst = pt * (dpt - dl_ref[i])
            dk = dk + jnp.dot(dst.astype(BF16), qi, preferred_element_type=F32)
            dc = dc - jnp.sum(dst, axis=-1, keepdims=True)
            return dk, dv, dc

        init = (jnp.zeros((tb, hd), F32), jnp.zeros((tb, hd), F32), jnp.zeros((tb, 1), F32))
        dk, dv, dc = lax.fori_loop(j, nb, step, init)
        dk_ref[...] = (dk * scale).astype(dk_ref.dtype)
        dv_ref[...] = dv.astype(dv_ref.dtype)
        dc_ref[...] = dc

    blk = pl.BlockSpec((None, tb, hd), lambda hh, jj: (hh, jj, 0))
    full = pl.BlockSpec((None, s, hd), lambda hh, jj: (hh, 0, 0))
    col = pl.BlockSpec((None, tb, 1), lambda hh, jj: (hh, jj, 0))
    rows = pl.BlockSpec((None, nb, 1, tb), lambda hh, jj: (hh, 0, 0, 0))
    return pl.pallas_call(
        body, name=name, grid=(h, nb),
        in_specs=[full, blk, blk, full, rows, rows, col, rows],
        out_specs=[blk, blk, col],
        out_shape=[jax.ShapeDtypeStruct((h, s, hd), BF16), jax.ShapeDtypeStruct((h, s, hd), BF16),
                   jax.ShapeDtypeStruct((h, s, 1), F32)],
        compiler_params=_cparams(),
    )(q, k, v, do, lse_row, delta_row, c_col, c_row)


def _pair_masks(hd):
    lane = lax.broadcasted_iota(jnp.int32, (1, 2 * hd), 1)
    return lane < hd, lane >= hd


def _only(mask, a):
    return jnp.where(mask, a, jnp.zeros_like(a))


def _nt(a, b):
    return lax.dot_general(a, b, (((1,), (1,)), ((), ())), preferred_element_type=F32)


def _causal(sc, rows_are_queries):
    r = lax.broadcasted_iota(jnp.int32, sc.shape, 0)
    c = lax.broadcasted_iota(jnp.int32, sc.shape, 1)
    return jnp.where((c <= r) if rows_are_queries else (r <= c), sc, -jnp.inf)


def _pair_specs(dm, s):
    hd = dm.aw // dm.nh
    pw = 2 * hd
    assert pw == LANES and dm.o_q % pw == 0 and dm.aw % pw == 0
    return hd, pw, dm.o_q // pw, (dm.o_q + dm.aw) // pw, (dm.o_q + 2 * dm.aw) // pw


def _attn2_fwd(name, zm, c_col, c_row, dm):
    s, tb = zm.shape[0], dm.tb
    nb = s // tb
    hd, pw, qb, kb, vb = _pair_specs(dm, s)
    scale = 1.0 / float(hd) ** 0.5

    def body(q_ref, k_ref, v_ref, cq_ref, cr_ref, o_ref, ob_ref, lse_ref):
        i = pl.program_id(1)
        masks = _pair_masks(hd)
        q2 = q_ref[...]
        qe = [_only(m, q2) for m in masks]
        cq = [cq_ref[0], cq_ref[1]]

        def block(j, carry, diag):
            k0 = pl.multiple_of(j * tb, tb)
            kj = k_ref[pl.ds(k0, tb), :]
            vj = v_ref[pl.ds(k0, tb), :]
            out = []
            for e in range(2):
                m_i, l_i, acc = carry[e]
                sc = _nt(qe[e], kj) * scale + (cq[e] - cr_ref[e, j])
                if diag:
                    sc = _causal(sc, True)
                m_new = jnp.maximum(m_i, jnp.max(sc, axis=-1, keepdims=True))
                alpha = jnp.exp(m_i - m_new)
                p = jnp.exp(sc - m_new)
                l_new = alpha * l_i + jnp.sum(p, axis=-1, keepdims=True)
                acc = alpha * acc + jnp.dot(p.astype(BF16), vj, preferred_element_type=F32)
                out.append((m_new, l_new, acc))
            return tuple(out)

        one = (jnp.full((tb, 1), -jnp.inf, F32), jnp.zeros((tb, 1), F32), jnp.zeros((tb, pw), F32))
        carry = lax.fori_loop(0, i, lambda j, c: block(j, c, False), (one, one))
        (m0, l0, a0), (m1, l1, a1) = block(i, carry, True)
        o = jnp.where(masks[0], a0 / l0, a1 / l1)
        o_ref[...] = o
        ob_ref[...] = o.astype(BF16)
        lse_ref[0] = m0 + jnp.log(l0)
        lse_ref[1] = m1 + jnp.log(l1)

    blk = lambda cb: pl.BlockSpec((tb, pw), functools.partial(lambda hp, i, cb: (i, cb + hp), cb=cb))
    full = lambda cb: pl.BlockSpec((s, pw), functools.partial(lambda hp, i, cb: (0, cb + hp), cb=cb))
    col = pl.BlockSpec((2, tb, 1), lambda hp, i: (hp, i, 0))
    rows = pl.BlockSpec((2, nb, 1, tb), lambda hp, i: (hp, 0, 0, 0))
    return pl.pallas_call(
        body, name=name, grid=(dm.nh // 2, nb),
        in_specs=[blk(qb), full(kb), full(vb), col, rows],
        out_specs=[blk(0), blk(0), col],
        out_shape=[jax.ShapeDtypeStruct((s, dm.aw), F32), jax.ShapeDtypeStruct((s, dm.aw), BF16),
                   jax.ShapeDtypeStruct((dm.nh, s, 1), F32)],
        compiler_params=_cparams(),
    )(zm, zm, zm, c_col, c_row)


def _attn2_bwd_q(name, zm, o, do, lse, c_col, c_row, dm):
    s, tb = zm.shape[0], dm.tb
    nb = s // tb
    hd, pw, qb, kb, vb = _pair_specs(dm, s)
    scale = 1.0 / float(hd) ** 0.5

    def body(q_ref, k_ref, v_ref, o_ref, do_ref, lse_ref, cq_ref, cr_ref, dq_ref, dl_ref, dcq_ref):
        i = pl.program_id(1)
        masks = _pair_masks(hd)
        q2 = q_ref[...]
        do2 = do_ref[...]
        prod = do2.astype(F32) * o_ref[...]
        qe = [_only(m, q2) for m in masks]
        doe = [_only(m, do2) for m in masks]
        delta = [jnp.sum(_only(m, prod), axis=-1, keepdims=True) for m in masks]
        cq = [cq_ref[0], cq_ref[1]]
        lse_v = [lse_ref[0], lse_ref[1]]

        def block(j, carry, diag):
            k0 = pl.multiple_of(j * tb, tb)
            kj = k_ref[pl.ds(k0, tb), :]
            vj = v_ref[pl.ds(k0, tb), :]
            out = []
            for e in range(2):
                dq, dcq = carry[e]
                sc = _nt(qe[e], kj) * scale + (cq[e] - cr_ref[e, j])
                if diag:
                    sc = _causal(sc, True)
                p = jnp.exp(sc - lse_v[e])
                ds = p * (_nt(doe[e], vj) - delta[e])
                out.append((dq + jnp.dot(ds.astype(BF16), kj, preferred_element_type=F32),
                            dcq + jnp.sum(ds, axis=-1, keepdims=True)))
            return tuple(out)

        one = (jnp.zeros((tb, pw), F32), jnp.zeros((tb, 1), F32))
        carry = lax.fori_loop(0, i, lambda j, c: block(j, c, False), (one, one))
        (dq0, dc0), (dq1, dc1) = block(i, carry, True)
        dq_ref[...] = (jnp.where(masks[0], dq0, dq1) * scale).astype(dq_ref.dtype)
        dl_ref[0] = delta[0]
        dl_ref[1] = delta[1]
        dcq_ref[0] = dc0
        dcq_ref[1] = dc1

    blk = lambda cb: pl.BlockSpec((tb, pw), functools.partial(lambda hp, i, cb: (i, cb + hp), cb=cb))
    full = lambda cb: pl.BlockSpec((s, pw), functools.partial(lambda hp, i, cb: (0, cb + hp), cb=cb))
    col = pl.BlockSpec((2, tb, 1), lambda hp, i: (hp, i, 0))
    rows = pl.BlockSpec((2, nb, 1, tb), lambda hp, i: (hp, 0, 0, 0))
    return pl.pallas_call(
        body, name=name, grid=(dm.nh // 2, nb),
        in_specs=[blk(qb), full(kb), full(vb), blk(0), blk(0), col, col, rows],
        out_specs=[blk(0), col, col],
        out_shape=[jax.ShapeDtypeStruct((s, dm.aw), BF16), jax.ShapeDtypeStruct((dm.nh, s, 1), F32),
                   jax.ShapeDtypeStruct((dm.nh, s, 1), F32)],
        compiler_params=_cparams(),
    )(zm, zm, zm, o, do, lse, c_col, c_row)


def _attn2_bwd_kv(name, zm, do, lse_row, delta_row, c_col, c_row, dm):
    s, tb = zm.shape[0], dm.tb
    nb = s // tb
    hd, pw, qb, kb, vb = _pair_specs(dm, s)
    scale = 1.0 / float(hd) ** 0.5

    def body(q_ref, k_ref, v_ref, do_ref, lse_ref, dl_ref, ck_ref, cr_ref, dk_ref, dv_ref, dc_ref):
        j = pl.program_id(1)
        masks = _pair_masks(hd)
        k2 = k_ref[...]
        v2 = v_ref[...]
        ke = [_only(m, k2) for m in masks]
        ve = [_only(m, v2) for m in masks]
        ck = [ck_ref[0], ck_ref[1]]

        def block(i, carry, diag):
            q0 = pl.multiple_of(i * tb, tb)
            qi = q_ref[pl.ds(q0, tb), :]
            doi = do_ref[pl.ds(q0, tb), :]
            out = []
            for e in range(2):
                dk, dv, dc = carry[e]
                st = _nt(ke[e], qi) * scale + (cr_ref[e, i] - ck[e])
                if diag:
                    st = _causal(st, False)
                pt = jnp.exp(st - lse_ref[e, i])
                dv = dv + jnp.dot(pt.astype(BF16), doi, preferred_element_type=F32)
                dst = pt * (_nt(ve[e], doi) - dl_ref[e, i])
                dk = dk + jnp.dot(dst.astype(BF16), qi, preferred_element_type=F32)
                out.append((dk, dv, dc - jnp.sum(dst, axis=-1, keepdims=True)))
            return tuple(out)

        one = (jnp.zeros((tb, pw), F32), jnp.zeros((tb, pw), F32), jnp.zeros((tb, 1), F32))
        carry = block(j, (one, one), True)
        (dk0, dv0, dc0), (dk1, dv1, dc1) = lax.fori_loop(j + 1, nb, lambda i, c: block(i, c, False), carry)
        dk_ref[...] = (jnp.where(masks[0], dk0, dk1) * scale).astype(dk_ref.dtype)
        dv_ref[...] = jnp.where(masks[0], dv0, dv1).astype(dv_ref.dtype)
        dc_ref[0] = dc0
        dc_ref[1] = dc1

    blk = lambda cb: pl.BlockSpec((tb, pw), functools.partial(lambda hp, jj, cb: (jj, cb + hp), cb=cb))
    full = lambda cb: pl.BlockSpec((s, pw), functools.partial(lambda hp, jj, cb: (0, cb + hp), cb=cb))
    col = pl.BlockSpec((2, tb, 1), lambda hp, jj: (hp, jj, 0))
    rows = pl.BlockSpec((2, nb, 1, tb), lambda hp, jj: (hp, 0, 0, 0))
    return pl.pallas_call(
        body, name=name, grid=(dm.nh // 2, nb),
        in_specs=[full(qb), blk(kb), blk(vb), full(0), rows, rows, col, rows],
        out_specs=[blk(0), blk(0), col],
        out_shape=[jax.ShapeDtypeStruct((s, dm.aw), BF16), jax.ShapeDtypeStruct((s, dm.aw), BF16),
                   jax.ShapeDtypeStruct((dm.nh, s, 1), F32)],
        compiler_params=_cparams(),
    )(zm, zm, zm, do, lse_row, delta_row, c_col, c_row)


def _glu(cv, cc):
    c1 = cv[:, :cc].astype(F32)
    c2 = cv[:, cc:].astype(F32)
    return c1 * _sig(c2)


def _conv_fwd(name, zm, w_pad, b_row, g_row, cc, taps):
    s = zm.shape[0]
    tr = _pick(s, (256, 128))
    hpb = tr // HALO
    off = HALO - (taps - 1)

    def body(cur_ref, halo_ref, w_ref, b_ref, g_ref, y_ref, cs_ref, apad):
        i = pl.program_id(0)
        apad[0:HALO, :] = _glu(halo_ref[...], cc) * jnp.where(i > 0, 1.0, 0.0)
        apad[HALO:, :] = _glu(cur_ref[...], cc)
        acc = jnp.zeros((tr, cc), F32) + b_ref[...]
        for t in range(taps):
            acc = acc + w_ref[t:t + 1, :] * apad[off + t:off + t + tr, :]
        y_ref[...] = acc
        r = lax.rsqrt(jnp.mean(acc * acc, axis=-1, keepdims=True) + RMS_EPS)
        n = acc * r * g_ref[...]
        cs_ref[...] = (n * _sig(n)).astype(cs_ref.dtype)

    return pl.pallas_call(
        body, name=name, grid=(s // tr,),
        in_specs=[pl.BlockSpec((tr, 2 * cc), lambda i: (i, 0)),
                  pl.BlockSpec((HALO, 2 * cc), lambda i: (jnp.maximum(i * hpb - 1, 0), 0)),
                  pl.BlockSpec(w_pad.shape, lambda i: (0, 0)),
                  pl.BlockSpec(b_row.shape, lambda i: (0, 0)),
                  pl.BlockSpec(g_row.shape, lambda i: (0, 0))],
        out_specs=[pl.BlockSpec((tr, cc), lambda i: (i, 0)), pl.BlockSpec((tr, cc), lambda i: (i, 0))],
        out_shape=[jax.ShapeDtypeStruct((s, cc), F32), jax.ShapeDtypeStruct((s, cc), BF16)],
        scratch_shapes=[pltpu.VMEM((HALO + tr, cc), F32)], compiler_params=_cparams(),
    )(zm, zm, w_pad, b_row, g_row)


def _conv_bwd(name, zm, y, dcs, w_pad, g_row, cc, taps):
    s = zm.shape[0]
    tr = _pick(s, (256, 128))
    hpb = tr // HALO
    nblk = s // tr
    off = HALO - (taps - 1)

    def dy_of(yv, dcsv, g):
        r = lax.rsqrt(jnp.mean(yv * yv, axis=-1, keepdims=True) + RMS_EPS)
        xh = yv * r
        n = xh * g
        sg = _sig(n)
        dn = dcsv.astype(F32) * (sg * (1.0 + n * (1.0 - sg)))
        dxh = dn * g
        dy = r * (dxh - xh * jnp.mean(dxh * xh, axis=-1, keepdims=True))
        return dy, dn * xh

    def body(cur_ref, halo_ref, y_ref, yn_ref, dcs_ref, dcsn_ref, w_ref, g_ref,
             dcv_ref, dw_ref, db_ref, dg_ref, apad, dypad):
        i = pl.program_id(0)

        @pl.when(i == 0)
        def _():
            dw_ref[...] = jnp.zeros(dw_ref.shape, F32)
            db_ref[...] = jnp.zeros(db_ref.shape, F32)
            dg_ref[...] = jnp.zeros(dg_ref.shape, F32)

        g = g_ref[...]
        apad[0:HALO, :] = _glu(halo_ref[...], cc) * jnp.where(i > 0, 1.0, 0.0)
        apad[HALO:, :] = _glu(cur_ref[...], cc)
        dy, dgt = dy_of(y_ref[...], dcs_ref[...], g)
        dyn, _ = dy_of(yn_ref[...], dcsn_ref[...], g)
        dypad[0:tr, :] = dy
        dypad[tr:, :] = dyn * jnp.where(i < nblk - 1, 1.0, 0.0)
        db_ref[...] += jnp.sum(dy, axis=0, keepdims=True)
        dg_ref[...] += jnp.sum(dgt, axis=0, keepdims=True)
        da = jnp.zeros((tr, cc), F32)
        for t in range(taps):
            da = da + w_ref[t:t + 1, :] * dypad[taps - 1 - t:taps - 1 - t + tr, :]
            dw_ref[t:t + 1, :] += jnp.sum(dy * apad[off + t:off + t + tr, :], axis=0, keepdims=True)
        cv = cur_ref[...]
        c1 = cv[:, :cc].astype(F32)
        sg = _sig(cv[:, cc:].astype(F32))
        dcv_ref[:, :cc] = (da * sg).astype(dcv_ref.dtype)
        dcv_ref[:, cc:] = (da * c1 * sg * (1.0 - sg)).astype(dcv_ref.dtype)

    nxt = lambda i: (jnp.minimum((i + 1) * hpb, s // HALO - 1), 0)
    return pl.pallas_call(
        body, name=name, grid=(nblk,),
        in_specs=[pl.BlockSpec((tr, 2 * cc), lambda i: (i, 0)),
                  pl.BlockSpec((HALO, 2 * cc), lambda i: (jnp.maximum(i * hpb - 1, 0), 0)),
                  pl.BlockSpec((tr, cc), lambda i: (i, 0)), pl.BlockSpec((HALO, cc), nxt),
                  pl.BlockSpec((tr, cc), lambda i: (i, 0)), pl.BlockSpec((HALO, cc), nxt),
                  pl.BlockSpec(w_pad.shape, lambda i: (0, 0)), pl.BlockSpec(g_row.shape, lambda i: (0, 0))],
        out_specs=[pl.BlockSpec((tr, 2 * cc), lambda i: (i, 0)),
                   pl.BlockSpec(w_pad.shape, lambda i: (0, 0)),
                   pl.BlockSpec((1, cc), lambda i: (0, 0)), pl.BlockSpec((1, cc), lambda i: (0, 0))],
        out_shape=[jax.ShapeDtypeStruct((s, 2 * cc), BF16), jax.ShapeDtypeStruct(w_pad.shape, F32),
                   jax.ShapeDtypeStruct((1, cc), F32), jax.ShapeDtypeStruct((1, cc), F32)],
        scratch_shapes=[pltpu.VMEM((HALO + tr, cc), F32), pltpu.VMEM((tr + HALO, cc), F32)],
        compiler_params=_cparams(),
    )(zm, zm, y, y, dcs, dcs, w_pad, g_row)


def _place():
    x, y, c = lax.axis_index("x"), lax.axis_index("y"), lax.axis_index("c")
    chips = [(1 - x, y), (x, 1 - y), (1 - x, 1 - y)]
    return x, y, c, chips


def _half(c, rows):
    rh = rows // 2
    return pl.ds(pl.multiple_of(c * rh, 16), rh)


def _remote(src, dst, send, recv, dev):
    return pltpu.make_async_remote_copy(src_ref=src, dst_ref=dst, send_sem=send, recv_sem=recv,
                                        device_id=dev, device_id_type=MESH)


def _cast_slots(name, w, chip_arr):
    nl, r, cdim = w.shape
    tr = _half_rows(r)
    n = r // tr

    def body(q_ref, w_ref, *o_refs):
        layer = pl.program_id(0)
        for j, o_ref in enumerate(o_refs):
            @pl.when(layer == j)
            def _(o_ref=o_ref):
                o_ref[...] = w_ref[...].astype(o_ref.dtype)

    def out_map(j):
        return lambda l, i, q: (q[0], jnp.where(l < j, 0, jnp.where(l == j, i, n - 1)), 0)

    return pl.pallas_call(
        body, name=name,
        grid_spec=pltpu.PrefetchScalarGridSpec(
            num_scalar_prefetch=1, grid=(nl, n),
            in_specs=[pl.BlockSpec((None, tr, cdim), lambda l, i, q: (l, i, 0))],
            out_specs=[pl.BlockSpec((None, tr, cdim), out_map(j)) for j in range(nl)]),
        out_shape=[jax.ShapeDtypeStruct((N_CHIPS, r, cdim), BF16)] * nl, compiler_params=_cparams(),
    )(chip_arr, w)


def _split_start(name, n_sems, bufs, issue, after=()):
    nb = len(bufs)
    n_in = nb + len(after)

    def body(*refs):
        issue(refs[:nb], refs[n_in], refs[n_in + 1])
        refs[-1][...] = jnp.zeros(refs[-1].shape, F32)

    outs = pl.pallas_call(
        body, name=name, in_specs=[HBM_SPEC] * nb + [ANY] * len(after),
        out_shape=(pltpu.SemaphoreType.DMA(n_sems), pltpu.SemaphoreType.DMA(n_sems),
                   *[pltpu.HBM(b.shape, b.dtype) for b in bufs], jax.ShapeDtypeStruct((8, LANES), F32)),
        out_specs=(SEM_SPEC, SEM_SPEC, *[HBM_SPEC] * nb, pl.BlockSpec(memory_space=pltpu.VMEM)),
        input_output_aliases={t: t + 2 for t in range(nb)},
        compiler_params=pltpu.CompilerParams(has_side_effects=EFFECT),
    )(*[pltpu.with_memory_space_constraint(b, pltpu.HBM) for b in bufs], *after)
    return outs[0], outs[1], list(outs[2:2 + nb]), outs[-1]


def _split_wait(name, bufs, send, recv, after, drain):
    nb = len(bufs)

    def body(*refs):
        drain(refs[:nb], refs[nb], refs[nb + 1])

    return list(pl.pallas_call(
        body, name=name, in_specs=[HBM_SPEC] * nb + [SEM_SPEC, SEM_SPEC, ANY],
        out_shape=tuple(pltpu.HBM(b.shape, b.dtype) for b in bufs), out_specs=tuple([HBM_SPEC] * nb),
        input_output_aliases={t: t for t in range(nb)},
        compiler_params=pltpu.CompilerParams(has_side_effects=EFFECT),
    )(*bufs, send, recv, after))


def _gather_start(name, bufs, after=()):
    nt = len(bufs)

    def issue(g, send, recv):
        x, y, c, chips = _place()
        me = 2 * x + y
        for t in range(nt):
            part = g[t].at[me, _half(c, bufs[t].shape[1]), :]
            for j, (qx, qy) in enumerate(chips):
                _remote(part, part, send.at[3 * t + j], recv.at[3 * t + j], (qx, qy, c)).start()

    return _split_start(name, (3 * nt,), bufs, issue, after)


def _gather_wait(name, bufs, send, recv, after):
    nt = len(bufs)

    def drain(g, send, recv):
        x, y, c, _ = _place()
        for t in range(nt):
            part = g[t].at[0, pl.ds(0, bufs[t].shape[1] // 2), :]
            for j in range(3):
                cp = _remote(part, part, send.at[3 * t + j], recv.at[3 * t + j], (x, y, c))
                cp.wait_send()
                cp.wait_recv()

    return _split_wait(name, bufs, send, recv, after, drain)


def _gather_forward(name, bufs):
    nt = len(bufs)

    def body(*refs):
        g = refs[nt:2 * nt]
        send, recv = refs[2 * nt:]
        x, y, c, chips = _place()
        cps = []
        for t in range(nt):
            for j, (qx, qy) in enumerate(chips):
                part = g[t].at[2 * qx + qy, _half(c, bufs[t].shape[1]), :]
                cps.append(_remote(part, part, send.at[t, j], recv.at[t, j], (x, y, 1 - c)))
        for cp in cps:
            cp.start()
        for t in range(nt):
            for j, (qx, qy) in enumerate(chips):
                theirs = g[t].at[2 * qx + qy, _half(1 - c, bufs[t].shape[1]), :]
                _remote(theirs, theirs, send.at[t, j], recv.at[t, j], (x, y, 1 - c)).wait_recv()
        for cp in cps:
            cp.wait_send()

    return list(pl.pallas_call(
        body, name=name, in_specs=[ANY] * nt, out_specs=[ANY] * nt,
        out_shape=[jax.ShapeDtypeStruct(b.shape, b.dtype) for b in bufs],
        input_output_aliases={t: t for t in range(nt)},
        scratch_shapes=[pltpu.SemaphoreType.DMA((nt, 3)), pltpu.SemaphoreType.DMA((nt, 3))],
        compiler_params=pltpu.CompilerParams(has_side_effects=True),
    )(*bufs))


def _swap_halves(name, grads):
    nt = len(grads)

    def body(*refs):
        g_refs, r_refs = refs[:nt], refs[nt:2 * nt]
        send, recv = refs[2 * nt:]
        x, y, c, _ = _place()
        cps = [_remote(g_refs[t].at[:, _half(1 - c, grads[t].shape[1]), :], r_refs[t], send.at[t], recv.at[t],
                       (x, y, 1 - c)) for t in range(nt)]
        for cp in cps:
            cp.start()
        for cp in cps:
            cp.wait()

    return pl.pallas_call(
        body, name=name, in_specs=[ANY] * nt, out_specs=[ANY] * nt,
        out_shape=[jax.ShapeDtypeStruct((N_CHIPS, g.shape[1] // 2, g.shape[2]), g.dtype) for g in grads],
        scratch_shapes=[pltpu.SemaphoreType.DMA((nt,)), pltpu.SemaphoreType.DMA((nt,))],
        compiler_params=pltpu.CompilerParams(has_side_effects=True),
    )(*grads)


def _scatter_start(name, parts, after=()):
    nt = len(parts)
    lands = [lax.empty((3,) + p.shape[1:], p.dtype) for p in parts]

    def issue(refs, send, recv):
        x, y, c, chips = _place()
        for t in range(nt):
            for j, (qx, qy) in enumerate(chips):
                _remote(refs[t].at[2 * qx + qy], refs[nt + t].at[j], send.at[3 * t + j], recv.at[3 * t + j],
                        (qx, qy, c)).start()

    return _split_start(name, (3 * nt,), list(parts) + lands, issue, after)


def _scatter_wait(name, bufs, send, recv, after):
    nt = len(bufs) // 2

    def drain(refs, send, recv):
        x, y, c, _ = _place()
        for t in range(nt):
            for j in range(3):
                cp = _remote(refs[t].at[0], refs[nt + t].at[j], send.at[3 * t + j], recv.at[3 * t + j], (x, y, c))
                cp.wait_send()
                cp.wait_recv()

    return _split_wait(name, bufs, send, recv, after, drain)


def _join_halves(name, fulls):
    nt = len(fulls)

    def body(*refs):
        o_refs = refs[nt:2 * nt]
        send, recv = refs[2 * nt:]
        x, y, c, _ = _place()
        cps = []
        for t in range(nt):
            half = o_refs[t].at[_half(c, fulls[t].shape[0]), :]
            cps.append(_remote(half, half, send.at[t], recv.at[t], (x, y, 1 - c)))
        for cp in cps:
            cp.start()
        for t in range(nt):
            theirs = o_refs[t].at[_half(1 - c, fulls[t].shape[0]), :]
            _remote(theirs, theirs, send.at[t], recv.at[t], (x, y, 1 - c)).wait_recv()
        for cp in cps:
            cp.wait_send()

    return list(pl.pallas_call(
        body, name=name, in_specs=[ANY] * nt, out_specs=[ANY] * nt,
        out_shape=[jax.ShapeDtypeStruct(a.shape, a.dtype) for a in fulls],
        input_output_aliases={t: t for t in range(nt)},
        scratch_shapes=[pltpu.SemaphoreType.DMA((nt,)), pltpu.SemaphoreType.DMA((nt,))],
        compiler_params=pltpu.CompilerParams(has_side_effects=True),
    )(*fulls))


def _add_own_half(name, grad, recv_half, c_arr):
    _, r, cdim = grad.shape
    rh = r // 2
    tr = _half_rows(rh)
    nrb = rh // tr

    def body(c_ref, g_ref, r_ref, o_ref):
        o_ref[...] = (g_ref[...].astype(F32) + r_ref[...].astype(F32)).astype(o_ref.dtype)

    return pl.pallas_call(
        body, name=name,
        grid_spec=pltpu.PrefetchScalarGridSpec(
            num_scalar_prefetch=1, grid=(N_CHIPS, nrb),
            in_specs=[pl.BlockSpec((None, tr, cdim), lambda q, i, cr: (q, cr[0] * nrb + i, 0)),
                      pl.BlockSpec((None, tr, cdim), lambda q, i, cr: (q, i, 0))],
            out_specs=pl.BlockSpec((None, tr, cdim), lambda q, i, cr: (q, i, 0))),
        out_shape=jax.ShapeDtypeStruct((N_CHIPS, rh, cdim), BF16), compiler_params=_cparams(),
    )(c_arr, grad, recv_half)


def _add_chips(name, part, came, place_arr):
    _, rh, cdim = part.shape
    tr = _half_rows(rh)
    nrb = rh // tr

    def body(q_ref, p_ref, r_ref, o_ref):
        acc = p_ref[...].astype(F32)
        for j in range(3):
            acc = acc + r_ref[j].astype(F32)
        o_ref[...] = acc

    return pl.pallas_call(
        body, name=name,
        grid_spec=pltpu.PrefetchScalarGridSpec(
            num_scalar_prefetch=1, grid=(nrb,),
            in_specs=[pl.BlockSpec((None, tr, cdim), lambda i, qr: (qr[0], i, 0)),
                      pl.BlockSpec((3, tr, cdim), lambda i, qr: (0, i, 0))],
            out_specs=pl.BlockSpec((tr, cdim), lambda i, qr: (qr[1] * nrb + i, 0))),
        out_shape=jax.ShapeDtypeStruct((2 * rh, cdim), F32), compiler_params=_cparams(),
    )(place_arr, part, came)


def _reduce_scatter_begin(tag, grads, c_arr, after=()):
    nt = len(grads)
    got = _swap_halves(f"rs_swap_{tag}", grads)
    parts = [_add_own_half(f"rs_add2_{tag}_{t}", grads[t], got[t], c_arr) for t in range(nt)]
    return _scatter_start(f"rs_scatter_start_{tag}", parts, after)


def _reduce_scatter_end(tag, state, after, place_arr):
    send, recv, bufs, _ = state
    nt = len(bufs) // 2
    bufs = _scatter_wait(f"rs_scatter_wait_{tag}", bufs, send, recv, after)
    fulls = [_add_chips(f"rs_add4_{tag}_{t}", bufs[t], bufs[nt + t], place_arr) for t in range(nt)]
    return _join_halves(f"rs_join_{tag}", fulls)


def _allgather_blocks(name, blk):
    m_per, n = blk.shape

    def body(x_ref, out_ref, send_sems, recv_sems, local_sem):
        x, y, c, chips = _place()
        me, sibling = (x, y, c), (x, y, 1 - c)

        def rows(px, py, pc):
            return out_ref.at[pl.ds(pl.multiple_of((4 * px + 2 * py + pc) * m_per, 8), m_per), :]

        def copy(k, block, to, src=None):
            return _remote(rows(*block) if src is None else src, rows(*block), send_sems.at[k], recv_sems.at[k], to)

        mine = pltpu.make_async_copy(x_ref, rows(*me), local_sem)
        mine.start()
        first = [copy(0, me, sibling, src=x_ref)]
        first += [copy(1 + j, me, (*chip, c), src=x_ref) for j, chip in enumerate(chips)]
        for cp in first:
            cp.start()
        passed = [copy(4 + j, (*chip, c), sibling) for j, chip in enumerate(chips)]
        for j, chip in enumerate(chips):
            copy(1 + j, (*chip, c), me).wait_recv()
            passed[j].start()
        copy(0, sibling, me).wait_recv()
        for j, chip in enumerate(chips):
            copy(4 + j, (*chip, 1 - c), me).wait_recv()
        for cp in first + passed:
            cp.wait_send()
        mine.wait()

    return pl.pallas_call(
        body, name=name, out_shape=jax.ShapeDtypeStruct((N_DEV * m_per, n), blk.dtype),
        in_specs=[pl.BlockSpec(memory_space=pltpu.VMEM)], out_specs=pl.BlockSpec(memory_space=pltpu.VMEM),
        scratch_shapes=[pltpu.SemaphoreType.DMA((7,)), pltpu.SemaphoreType.DMA((7,)), pltpu.SemaphoreType.DMA],
        compiler_params=_cparams(),
    )(blk)


def _sum_blocks(name, stacked):
    nd, m, n = stacked.shape
    tr = _pick(m, (336, 256, 168, 128, 64, 32, 16, 8))

    def body(s_ref, o_ref):
        acc = s_ref[0]
        for d in range(1, nd):
            acc = acc + s_ref[d]
        o_ref[...] = acc

    return pl.pallas_call(
        body, name=name, grid=(m // tr,),
        in_specs=[pl.BlockSpec((nd, tr, n), lambda i: (0, i, 0))],
        out_specs=pl.BlockSpec((tr, n), lambda i: (i, 0)),
        out_shape=jax.ShapeDtypeStruct((m, n), F32), compiler_params=_cparams(),
    )(stacked)


def _adamw_vals(w, g, m, v):
    m2 = ADAM_B1 * m + (1.0 - ADAM_B1) * g
    v2 = ADAM_B2 * v + (1.0 - ADAM_B2) * (g * g)
    m_hat = m2 / (1.0 - ADAM_B1 ** ADAM_STEP)
    v_hat = v2 / (1.0 - ADAM_B2 ** ADAM_STEP)
    delta = -ADAM_LR * (m_hat / (jnp.sqrt(v_hat) + ADAM_EPS) + ADAM_WD * w)
    return delta, m2, v2


def _adamw_layer(name, layer, w, m, v, g, stacked, deps=()):
    nl, r, cdim = w.shape
    tr = _pick(r, (128, 64, 32, 16, 8))
    n_d = len(deps)

    def body(w_ref, m_ref, v_ref, g_ref, *rest):
        go_ref, d_ref, mo_ref, vo_ref = rest[4 + n_d:]
        gv = g_ref[...]
        delta, m2, v2 = _adamw_vals(w_ref[...], gv, m_ref[...], v_ref[...])
        go_ref[...] = gv
        d_ref[...] = delta
        mo_ref[...] = m2
        vo_ref[...] = v2

    big = pl.BlockSpec((None, tr, cdim), lambda i: (layer, i, 0))
    return pl.pallas_call(
        body, name=name, grid=(r // tr,),
        in_specs=[big, big, big, pl.BlockSpec((tr, cdim), lambda i: (i, 0))] + [ANY] * (4 + n_d),
        out_specs=[big, big, big, big], out_shape=[jax.ShapeDtypeStruct(w.shape, F32)] * 4,
        input_output_aliases={4 + k: k for k in range(4)}, compiler_params=_cparams(),
    )(w, m, v, g, *stacked, *deps)


def _adamw_small(name, w, g, m, v):
    def body(w_ref, g_ref, m_ref, v_ref, d_ref, mo_ref, vo_ref):
        delta, m2, v2 = _adamw_vals(w_ref[...], g_ref[...], m_ref[...], v_ref[...])
        d_ref[...] = delta
        mo_ref[...] = m2
        vo_ref[...] = v2

    return pl.pallas_call(body, name=name, out_shape=[jax.ShapeDtypeStruct(w.shape, F32)] * 3,
                          compiler_params=_cparams())(w, g, m, v)


def _swiglu_fwd(name, z, f):
    def fn(zz):
        a = zz[:, :f].astype(F32)
        b = zz[:, f:].astype(F32)
        return a * _sig(a) * b
    return _rowwise(name, fn, [z], [(f, BF16)])[0]


def _swiglu_bwd(name, z, ds, f, deps=()):
    def fn(zz, dd):
        a = zz[:, :f].astype(F32)
        b = zz[:, f:].astype(F32)
        d = dd.astype(F32)
        sg = _sig(a)
        da = d * b * (sg * (1.0 + a * (1.0 - sg)))
        db = d * a * sg
        return jnp.concatenate([da, db], axis=1)
    return _rowwise(name, fn, [z, ds], [(2 * f, BF16)], deps=deps)[0]


def _heads(a, nh):
    s, w = a.shape
    return jnp.transpose(a.reshape(s, nh, w // nh), (1, 0, 2))


def _unheads(a):
    nh, s, hd = a.shape
    return jnp.transpose(a, (1, 0, 2)).reshape(s, nh * hd)


def _as_rows(col, tb):
    nh, s, _ = col.shape
    return col.reshape(nh, s // tb, 1, tb)


class _Dims:
    def __init__(self, d, f, aw, nh, cc, taps, dp, s):
        self.d, self.f, self.aw, self.nh, self.cc, self.taps, self.dp, self.s = d, f, aw, nh, cc, taps, dp, s
        self.o_cv, self.o_ga, self.o_gc = 0, 2 * cc, 2 * cc + d
        self.o_q = 2 * cc + 2 * d
        self.n_main = self.o_q + 3 * aw
        self.tb = _pick(s, (256, 128))
        assert self.o_ga % d == 0 and self.o_q % aw == 0 and nh <= LANES


def _ffn_fwd(tag, h, g_row, w_in, w_out, dm, deps=()):
    n = _rms_fwd(f"{tag}_rms", h, g_row, deps)
    z = _mm_nn(f"{tag}_in", n, w_in, BF16)
    s = _swiglu_fwd(f"{tag}_act", z, dm.f)
    h2 = _mm_nn(f"{tag}_out", s, w_out, F32, res=h, scale=FFN_RES)
    return h2, (h, n, z, s)


def _ffn_bwd(tag, dh, dh_half_b, saved, g_row, w_in, w_out, dm, oscale, deps=()):
    h, n, z, s = saved
    ds = _mm_nt(f"{tag}_dout", dh_half_b, w_out, BF16)
    dz = _swiglu_bwd(f"{tag}_dact", z, ds, dm.f, deps)
    dw_out = _mm_tn(f"{tag}_wout", s, dh_half_b, 1)
    dw_in = _mm_tn(f"{tag}_win", n, dz, N_CHIPS)
    dn = _mm_nt(f"{tag}_din", dz, w_in, BF16)
    dh0, dh0_b, dg = _rms_bwd_res(f"{tag}_drms", h, g_row, [dn], dh, oscale)
    return dh0, dh0_b, dg, dw_in, dw_out


def _mixer_fwd(tag, h, sm, wt, dm):
    d, cc, aw, nh, tb = dm.d, dm.cc, dm.aw, dm.nh, dm.tb
    u = _rms_fwd(f"{tag}_rms", h, sm["g_mix"])
    zm = _mm_nn(f"{tag}_in", u, wt["w_main"], BF16)
    zf = _mm_nn(f"{tag}_inf", u, wt["w_f"], F32)
    c = _fgate_fwd(f"{tag}_fgate", zf, sm["b_f"])
    c_col = jnp.transpose(c[:, :nh], (1, 0))[:, :, None]
    c_row = _as_rows(c_col, tb)
    o32, o, lse = _attn2_fwd(f"{tag}_attn", zm, c_col, c_row, dm)
    ya = _mm_nn(f"{tag}_aout", o, wt["w_attn_out"], BF16)
    y, cs = _conv_fwd(f"{tag}_conv", zm, sm["conv_w"], sm["conv_b"], sm["g_conv"], cc, dm.taps)
    yc = _mm_nn(f"{tag}_cout", cs, wt["w_conv_out"], BF16)

    def merge(ga, gc, a, b):
        return _sig(ga.astype(F32)) * a.astype(F32) + _sig(gc.astype(F32)) * b.astype(F32)

    mg = _rowwise(f"{tag}_merge", merge, [(zm, dm.o_ga // d, d), (zm, dm.o_gc // d, d), ya, yc], [(d, BF16)])[0]
    h2 = _mm_nn(f"{tag}_out", mg, wt["w_out"], F32, res=h, scale=1.0)
    return h2, (h, u, zm, zf, c_col, c_row, o32, lse, o, ya, y, cs, yc, mg)


def _mixer_bwd(tag, dh, dh_b, saved, sm, wt, dm, deps=()):
    d, cc, aw, nh, tb = dm.d, dm.cc, dm.aw, dm.nh, dm.tb
    h, u, zm, zf, c_col, c_row, o32, lse, o, ya, y, cs, yc, mg = saved
    dmg = _mm_nt(f"{tag}_dout", dh_b, wt["w_out"], BF16)
    dw_out = _mm_tn(f"{tag}_wout", mg, dh_b, 1)

    def unmerge(dd, ga, gc, a, b):
        dd, a, b = dd.astype(F32), a.astype(F32), b.astype(F32)
        sa, sc = _sig(ga.astype(F32)), _sig(gc.astype(F32))
        return dd * sa, dd * sc, dd * a * sa * (1.0 - sa), dd * b * sc * (1.0 - sc)

    dya, dyc, dga, dgc = _rowwise(f"{tag}_dmerge", unmerge,
                                  [dmg, (zm, dm.o_ga // d, d), (zm, dm.o_gc // d, d), ya, yc], [(d, BF16)] * 4,
                                  deps=deps)
    dw_a = _mm_tn(f"{tag}_waout", o, dya, N_CHIPS)
    do = _mm_nt(f"{tag}_daout", dya, wt["w_attn_out"], BF16)
    dw_c = _mm_tn(f"{tag}_wcout", cs, dyc, N_CHIPS)
    dcs = _mm_nt(f"{tag}_dcout", dyc, wt["w_conv_out"], BF16)
    dcv, dconv_w, dconv_b, dg_conv = _conv_bwd(f"{tag}_dconv", zm, y, dcs, sm["conv_w"], sm["g_conv"], cc, dm.taps)
    dq, delta, dcq_h = _attn2_bwd_q(f"{tag}_dattn_q", zm, o32, do, lse, c_col, c_row, dm)
    dk, dv, dck_h = _attn2_bwd_kv(f"{tag}_dattn_kv", zm, do, _as_rows(lse, tb), _as_rows(delta, tb),
                                  c_col, c_row, dm)
    dc = jnp.pad(jnp.transpose(dcq_h[:, :, 0], (1, 0)), ((0, 0), (0, zf.shape[1] - nh)))
    dc_k = jnp.pad(jnp.transpose(dck_h[:, :, 0], (1, 0)), ((0, 0), (0, zf.shape[1] - nh)))
    dzf, dzf_b, db_f = _fgate_bwd(f"{tag}_dfgate", dc, dc_k, zf, sm["b_f"])
    dzm = jnp.concatenate([dcv, dga, dgc, dq, dk, dv], axis=1)
    dw_main = _mm_tn(f"{tag}_win", u, dzm, 1)
    dw_f = _mm_tn(f"{tag}_winf", u, dzf_b, 1)
    du = _mm_nt(f"{tag}_din", dzm, wt["w_main"], BF16)
    du_f = _mm_nt(f"{tag}_dinf", dzf_b, wt["w_f"], BF16)
    dh0, dh0_b, dg_mix = _rms_bwd_res(f"{tag}_drms", h, sm["g_mix"], [du, du_f], dh, FFN_RES)
    small = dict(g_mix=dg_mix, b_f=db_f[:, :nh], conv_w=dconv_w[:dm.taps], conv_b=dconv_b, g_conv=dg_conv)
    return dh0, dh0_b, small, dw_main, dw_f, dw_a, dw_c, dw_out


def _ple_fwd(tag, h, p_b, sm, wt, dm):
    n = _rms_fwd(f"{tag}_rms", h, sm["g_ple"])
    gp = _mm_nn(f"{tag}_gate", n, wt["w_ple_gate"], BF16)
    pp = _mm_nn(f"{tag}_proj", p_b, wt["w_ple_proj"], BF16)

    def fn(hh, a, b):
        return hh + _sig(a.astype(F32)) * b.astype(F32)

    h2 = _rowwise(f"{tag}_mix", fn, [h, gp, pp], [(dm.d, F32)])[0]
    return h2, (h, n, gp, pp)


def _ple_bwd(tag, dh, saved, p_b, sm, wt, dm, deps=()):
    h, n, gp, pp = saved

    def fn(dd, a, b):
        gate = _sig(a.astype(F32))
        b = b.astype(F32)
        return dd * b * gate * (1.0 - gate), dd * gate

    dgp, dpp = _rowwise(f"{tag}_dmix", fn, [dh, gp, pp], [(dm.d, BF16)] * 2, deps=deps)
    dw_proj = _mm_tn(f"{tag}_wproj", p_b, dpp, N_CHIPS)
    dw_gate = _mm_tn(f"{tag}_wgate", n, dgp, 1)
    dn = _mm_nt(f"{tag}_dgate", dgp, wt["w_ple_gate"], BF16)
    dh0, dh0_b, dg = _rms_bwd_res(f"{tag}_drms", h, sm["g_ple"], [dn], dh, FFN_RES)
    return dh0, dh0_b, dg, dw_gate, dw_proj


def _loss_head(name, h, g_row, target):
    d = h.shape[1]

    def fn(x, tg, g):
        r = lax.rsqrt(jnp.mean(x * x, axis=-1, keepdims=True) + RMS_EPS)
        out = x * r * g
        e = out - tg
        per_row = jnp.sum(e * e, axis=-1, keepdims=True) * (0.5 / d)
        loss = jnp.zeros((1, LANES), F32) + jnp.sum(per_row, axis=0, keepdims=True)
        dx, dg = _rms_bwd_vals(x, g, e * (1.0 / d))
        return dx, loss, dg

    return _rowwise(name, fn, [h, target], [(d, F32)], consts=[g_row], reds=[((1, LANES), F32), ((1, d), F32)])


_BIG = ("w_ff1_in", "w_ff1_out", "w_in", "w_attn_out", "w_conv_out", "w_out", "w_ff2_in", "w_ff2_out",
        "w_ple_gate", "w_ple_proj")
_SMALL = ("g_ff1", "g_mix", "b_f", "conv_w", "conv_b", "g_conv", "g_ff2", "g_ple")
_ORDER = ("g_ff1", "w_ff1_in", "w_ff1_out", "g_mix", "w_in", "b_f", "w_attn_out", "conv_w", "conv_b", "g_conv",
          "w_conv_out", "w_out", "g_ff2", "w_ff2_in", "w_ff2_out", "g_ple", "w_ple_gate", "w_ple_proj", "g_final")


def _round_up(n, k):
    return (n + k - 1) // k * k


_ROW_SHARDED = ("w_ff1_out", "w_out", "w_ff2_out", "w_ple_gate")
_FIRST_LAYER_GROUPS = ((0, 1), (2, 3, 4, 5), (6, 7, 8, 9))


def _columns(pieces, lo, hi):
    out, pos = [], 0
    for a in pieces:
        w = a.shape[1]
        s, e = max(lo, pos), min(hi, pos + w)
        if s < e:
            out.append(a[:, s - pos:e - pos])
        pos += w
    return out


def _unpack_layer(g, dm):
    out = {}
    for n, a in g.items():
        if n == "w_in":
            blocks = [a[j] for j in range(N_CHIPS)]
            o_f = 3 * dm.aw
            o_c = o_f + dm.nh
            total = N_CHIPS * a.shape[2]
            out["w_main"] = jnp.concatenate(_columns(blocks, o_c, total) + _columns(blocks, 0, o_f), axis=1)[None]
            out["w_f"] = jnp.pad(jnp.concatenate(_columns(blocks, o_f, o_c), axis=1),
                                 ((0, 0), (0, LANES - dm.nh)))[None]
        elif n in _ROW_SHARDED:
            out[n] = a.reshape(1, a.shape[0] * a.shape[1], a.shape[2])
        else:
            out[n] = a
    return out


def _pack_grads(gw, names, dm):
    out = []
    for n in names:
        if n == "w_in":
            o_f = 3 * dm.aw
            main, wf = gw["w_main"][0], gw["w_f"][0]
            n_rest = dm.n_main - o_f
            pieces = [main[:, n_rest:], wf[:, :dm.nh], main[:, :n_rest]]
            nb = (dm.n_main + dm.nh) // N_CHIPS
            out.append(jnp.stack([jnp.concatenate(_columns(pieces, j * nb, (j + 1) * nb), axis=1)
                                  for j in range(N_CHIPS)]))
        elif n in _ROW_SHARDED:
            a = gw[n]
            out.append(a.reshape(N_CHIPS, a.shape[1] // N_CHIPS, a.shape[2]))
        else:
            out.append(gw[n])
    return out


def kernel(x, p, g_ff1, w_ff1_in, w_ff1_out, g_mix, w_in, b_f, w_attn_out, conv_w, conv_b, g_conv, w_conv_out, w_out, g_ff2, w_ff2_in, w_ff2_out, g_ple, w_ple_gate, w_ple_proj, g_final, loss_target, m_g_ff1, m_w_ff1_in, m_w_ff1_out, m_g_mix, m_w_in, m_b_f, m_w_attn_out, m_conv_w, m_conv_b, m_g_conv, m_w_conv_out, m_w_out, m_g_ff2, m_w_ff2_in, m_w_ff2_out, m_g_ple, m_w_ple_gate, m_w_ple_proj, m_g_final, v_g_ff1, v_w_ff1_in, v_w_ff1_out, v_g_mix, v_w_in, v_b_f, v_w_attn_out, v_conv_w, v_conv_b, v_g_conv, v_w_conv_out, v_w_out, v_g_ff2, v_w_ff2_in, v_w_ff2_out, v_g_ple, v_w_ple_gate, v_w_ple_proj, v_g_final):
    args = dict(locals())
    wts = {n: args[n] for n in _ORDER}
    mom = {n: args["m_" + n] for n in _ORDER}
    var = {n: args["v_" + n] for n in _ORDER}

    nl = g_ff1.shape[0]
    s, d = x.shape[1], x.shape[2]
    nh = b_f.shape[1]
    taps = conv_w.shape[1]
    cc = conv_b.shape[1]
    dm = _Dims(d=d, f=w_ff1_out.shape[1] * N_CHIPS, aw=w_attn_out.shape[1], nh=nh, cc=cc, taps=taps,
               dp=w_ple_proj.shape[1], s=s)
    assert taps - 1 <= HALO

    xi = lax.axis_index("x")
    yi = lax.axis_index("y")
    ci = lax.axis_index("c")
    c_arr = jnp.reshape(ci, (1,)).astype(jnp.int32)
    chip_arr = jnp.reshape(2 * xi + yi, (1,)).astype(jnp.int32)
    place_arr = jnp.stack([2 * xi + yi, ci]).astype(jnp.int32)

    h = x[0]
    target = loss_target[0]
    p_b = p[:, 0].astype(BF16)

    cw_rows = _round_up(nl * taps, 8)
    cw_blk = jnp.pad(conv_w.reshape(nl * taps, -1), ((0, cw_rows - nl * taps), (0, 0)))
    cw_all = _allgather_blocks("gather_conv_w", cw_blk).reshape(N_CHIPS, 2, cw_rows, -1)[:, 0, :nl * taps]
    conv_w_full = jnp.transpose(cw_all.reshape(N_CHIPS, nl, taps, -1), (1, 2, 0, 3)).reshape(nl, taps, cc)
    taps_pad = _round_up(taps, 8)

    def small_of(i):
        row = lambda a: a[i][None, :]
        return dict(g_ff1=row(g_ff1), g_mix=row(g_mix), g_conv=row(g_conv), conv_b=row(conv_b), g_ff2=row(g_ff2),
                    g_ple=row(g_ple), b_f=jnp.pad(b_f[i][None, :], ((0, 0), (0, LANES - nh))),
                    conv_w=jnp.pad(conv_w_full[i], ((0, taps_pad - taps), (0, 0))))

    slots = [_cast_slots(f"cast_{n}", wts[n], chip_arr) for n in _BIG]
    started = {}
    order = []

    def start_gather(i):
        if i < nl:
            groups = _FIRST_LAYER_GROUPS if i == 0 else (tuple(range(len(_BIG))),)
            started[i] = []
            for k, g in enumerate(groups):
                st = _gather_start(f"gather_start_l{i}g{k}", [slots[t][i] for t in g], after=order[-1:] + [conv_w_full])
                order.append(st[3])
                started[i].append((g, st))

    start_gather(0)
    start_gather(1)
    layer_w, saved = [], []
    for i in range(nl):
        pend = started.pop(i)
        wt = {}

        def arrive(k, h_now, i=i, pend=pend, wt=wt):
            if k < len(pend):
                g, (send, recv, bufs, _) = pend[k]
                bufs = _gather_wait(f"gather_wait_l{i}g{k}", bufs, send, recv, h_now)
                bufs = _gather_forward(f"gather_fwd_l{i}g{k}", bufs)
                wt.update(_unpack_layer({_BIG[t]: b for t, b in zip(g, bufs)}, dm))

        arrive(0, h)
        start_gather(i + 2)
        in_flight = [st[3] for _, st in pend[1:]] + [st[3] for sts in started.values() for _, st in sts]
        sm = small_of(i)
        h, sv1 = _ffn_fwd(f"l{i}_ff1", h, sm["g_ff1"], wt["w_ff1_in"], wt["w_ff1_out"], dm, deps=in_flight)
        arrive(1, h)
        h, sv2 = _mixer_fwd(f"l{i}_mix", h, sm, wt, dm)
        arrive(2, h)
        h, sv3 = _ffn_fwd(f"l{i}_ff2", h, sm["g_ff2"], wt["w_ff2_in"], wt["w_ff2_out"], dm)
        h, sv4 = _ple_fwd(f"l{i}_ple", h, p_b[i], sm, wt, dm)
        layer_w.append((wt, sm))
        saved.append((sv1, sv2, sv3, sv4))

    dh, loss_row, dg_final = _loss_head("loss_head", h, g_final[None, :], target)

    big_grads = [None] * nl
    small_grads = [None] * nl
    leaving = None

    def begin_first(k, gw, after=()):
        names = [_BIG[t] for t in _FIRST_LAYER_GROUPS[k]]
        return _reduce_scatter_begin(f"l0g{k}", _pack_grads(gw, names, dm), c_arr, after=after)

    for i in range(nl - 1, -1, -1):
        wt, sm = layer_w[i]
        sv1, sv2, sv3, sv4 = saved[i]
        deps = [leaving[3]] if leaving is not None else []
        dh, dh_b, dg_ple, dw_gate, dw_proj = _ple_bwd(f"l{i}_ple", dh, sv4, p_b[i], sm, wt, dm, deps=deps)
        dh, dh_b, dg_ff2, dw_in2, dw_out2 = _ffn_bwd(f"l{i}_ff2", dh, dh_b, sv3, sm["g_ff2"], wt["w_ff2_in"],
                                                     wt["w_ff2_out"], dm, 1.0)
        gw = dict(w_ff2_in=dw_in2, w_ff2_out=dw_out2, w_ple_gate=dw_gate, w_ple_proj=dw_proj)
        first = [begin_first(2, gw)] if i == 0 else []
        dh, dh_b, sg, dw_main, dw_f, dw_a, dw_c, dw_o = _mixer_bwd(f"l{i}_mix", dh, dh_b, sv2, sm, wt, dm,
                                                                   deps=[st[3] for st in first])
        gw.update(w_main=dw_main, w_f=dw_f, w_attn_out=dw_a, w_conv_out=dw_c, w_out=dw_o)
        if i == 0:
            first.append(begin_first(1, gw))
        dh, dh_b, dg_ff1, dw_in1, dw_out1 = _ffn_bwd(f"l{i}_ff1", dh, dh_b, sv1, sm["g_ff1"], wt["w_ff1_in"],
                                                     wt["w_ff1_out"], dm, 1.0, deps=[st[3] for st in first[1:]])
        gw.update(w_ff1_in=dw_in1, w_ff1_out=dw_out1)
        if leaving is not None:
            big_grads[i + 1] = _reduce_scatter_end(f"l{i + 1}", leaving, dh, place_arr)
        sg.update(g_ff1=dg_ff1, g_ff2=dg_ff2, g_ple=dg_ple)
        small_grads[i] = sg
        if i > 0:
            leaving = _reduce_scatter_begin(f"l{i}", _pack_grads(gw, _BIG, dm), c_arr)
    grad_x = dh[None]

    pieces = [small_grads[i][n].reshape(-1) for i in range(nl) for n in _SMALL]
    pieces += [dg_final.reshape(-1), loss_row[0, :1]]
    flat = jnp.concatenate(pieces)
    n_flat = flat.shape[0]
    rows = _round_up(_round_up(n_flat, LANES) // LANES, 8)
    blk = jnp.pad(flat, (0, rows * LANES - n_flat)).reshape(rows, LANES)
    total = _sum_blocks("sum_small", _allgather_blocks("gather_small", blk).reshape(N_DEV, rows, LANES)).reshape(-1)
    small_tot = {n: [] for n in _SMALL}
    pos = 0
    for i in range(nl):
        for n in _SMALL:
            shp = small_grads[i][n].shape
            size = shp[0] * shp[1]
            small_tot[n].append(total[pos:pos + size].reshape(shp))
            pos += size
    g_final_tot = total[pos:pos + d]
    loss = total[pos + d]

    first.append(begin_first(0, gw, after=[total]))
    stacked = {n: [lax.empty(wts[n].shape, F32) for _ in range(4)] for n in _BIG}

    def adamw_big(i, deps):
        for t, n in enumerate(_BIG):
            stacked[n] = _adamw_layer(f"adamw_{n}_l{i}", i, wts[n], mom[n], var[n], big_grads[i][t], stacked[n], deps)

    for i in range(nl - 1, 0, -1):
        adamw_big(i, [first[-1][3]])
    after = stacked[_BIG[-1]][1] if nl > 1 else dh
    big_grads[0] = [None] * len(_BIG)
    for k, state in zip((2, 1, 0), first):
        for t, g in zip(_FIRST_LAYER_GROUPS[k], _reduce_scatter_end(f"l0g{k}", state, after, place_arr)):
            big_grads[0][t] = g
    adamw_big(0, [])
    grads, deltas, new_m, new_v = {}, {}, {}, {}
    for n in _BIG:
        grads[n], deltas[n], new_m[n], new_v[n] = stacked[n]
    chip = 2 * xi + yi
    for n in _SMALL:
        g = jnp.concatenate(small_tot[n], axis=0)
        if n == "conv_w":
            cpc = cc // N_CHIPS
            g = lax.dynamic_slice_in_dim(g.reshape(nl * taps, cc), chip * cpc, cpc, axis=1)
            shape2 = (nl * taps, cpc)
        else:
            shape2 = g.shape
        dl, mm, vv = _adamw_small(f"adamw_{n}", wts[n].reshape(shape2), g, mom[n].reshape(shape2),
                                  var[n].reshape(shape2))
        grads[n] = g.reshape(wts[n].shape)
        deltas[n], new_m[n], new_v[n] = (a.reshape(wts[n].shape) for a in (dl, mm, vv))
    g2 = g_final_tot[None, :]
    dl, mm, vv = _adamw_small("adamw_g_final", g_final[None, :], g2, m_g_final[None, :], v_g_final[None, :])
    grads["g_final"] = g_final_tot
    deltas["g_final"], new_m["g_final"], new_v["g_final"] = dl[0], mm[0], vv[0]

    return (loss, grad_x, *[grads[n] for n in _ORDER], *[deltas[n] for n in _ORDER],
            *[new_m[n] for n in _ORDER], *[new_v[n] for n in _ORDER])
```

```python
import functools

import jax
import jax.numpy as jnp
from jax import lax
from jax.experimental import pallas as pl
from jax.experimental.pallas import tpu as pltpu

F32 = jnp.float32
BF16 = jnp.bfloat16
MESH = pl.DeviceIdType.MESH
ANY = pl.BlockSpec(memory_space=pl.ANY)
HBM_SPEC = pl.BlockSpec(memory_space=pltpu.HBM)
SEM_SPEC = pl.BlockSpec(memory_space=pltpu.SEMAPHORE)
EFFECT = pltpu.SideEffectType.DATAFLOW_SIDE_EFFECTING

RMS_EPS = 1e-6
FFN_RES = 0.5
ADAM_LR = 0.001
ADAM_B1 = 0.9
ADAM_B2 = 0.999
ADAM_EPS = 1e-08
ADAM_WD = 0.01
ADAM_STEP = 10

N_CHIPS = 4
N_DEV = 8
LANES = 128
HALO = 32
VMEM_LIMIT = 56 * 1024 * 1024


def _cparams():
    return pltpu.CompilerParams(vmem_limit_bytes=VMEM_LIMIT)


def _pick(n, prefs):
    for p in prefs:
        if p <= n and n % p == 0:
            return p
    return n


def _half_rows(r):
    return r // 2 if r % 32 == 0 else r


def _sig(x):
    return 1.0 / (1.0 + jnp.exp(-x))


def _rowwise(name, fn, ins, outs, consts=(), reds=(), tm=None, deps=()):
    ins = [a if isinstance(a, tuple) else (a, 0, a.shape[1]) for a in ins]
    m = ins[0][0].shape[0]
    tm = tm or _pick(m, (256, 128, 64, 32, 16, 8))
    n_in, n_c, n_o, n_r = len(ins), len(consts), len(outs), len(reds)
    n_d = len(deps)
    consts = list(consts) + list(deps)

    def body(*refs):
        in_refs = refs[:n_in + n_c]
        o_refs = refs[n_in + n_c + n_d:n_in + n_c + n_d + n_o]
        r_refs = refs[n_in + n_c + n_d + n_o:]
        res = fn(*[r[...] for r in in_refs])
        if not isinstance(res, (tuple, list)):
            res = (res,)
        for r, v in zip(o_refs, res[:n_o]):
            r[...] = v.astype(r.dtype)
        if n_r:
            @pl.when(pl.program_id(0) == 0)
            def _():
                for r in r_refs:
                    r[...] = jnp.zeros(r.shape, r.dtype)
            for r, v in zip(r_refs, res[n_o:]):
                r[...] += v.astype(r.dtype)

    in_specs = [pl.BlockSpec((tm, w), functools.partial(lambda i, cb: (i, cb), cb=cb)) for (_, cb, w) in ins]
    in_specs += [pl.BlockSpec(c.shape, lambda i: (0, 0)) for c in consts]
    out_specs = [pl.BlockSpec((tm, w), lambda i: (i, 0)) for (w, _) in outs]
    out_specs += [pl.BlockSpec(s, lambda i: (0, 0)) for (s, _) in reds]
    out_shape = [jax.ShapeDtypeStruct((m, w), d) for (w, d) in outs]
    out_shape += [jax.ShapeDtypeStruct(s, d) for (s, d) in reds]
    res = pl.pallas_call(
        body, name=name, grid=(m // tm,), in_specs=in_specs, out_specs=out_specs, out_shape=out_shape,
        compiler_params=_cparams(),
    )(*[a for (a, _, _) in ins], *consts)
    return res


def _rms_fwd(name, h, g, deps=()):
    def fn(x, gg):
        r = lax.rsqrt(jnp.mean(x * x, axis=-1, keepdims=True) + RMS_EPS)
        return x * r * gg
    return _rowwise(name, fn, [h], [(h.shape[1], BF16)], consts=[g], deps=deps)[0]


def _rms_bwd_vals(x, g, dn):
    r = lax.rsqrt(jnp.mean(x * x, axis=-1, keepdims=True) + RMS_EPS)
    xh = x * r
    dxh = dn * g
    dx = r * (dxh - xh * jnp.mean(dxh * xh, axis=-1, keepdims=True))
    dg = jnp.sum(dn * xh, axis=0, keepdims=True)
    return dx, dg


def _rms_bwd_res(name, h, g, dns, dres, oscale):
    n_dn = len(dns)

    def fn(x, *rest):
        dn = rest[0].astype(F32)
        for t in rest[1:n_dn]:
            dn = dn + t.astype(F32)
        dr, gg = rest[n_dn], rest[n_dn + 1]
        dx, dg = _rms_bwd_vals(x, gg, dn)
        dh = dr + dx
        return dh, oscale * dh, dg

    d = h.shape[1]
    return _rowwise(name, fn, [h, *dns, dres], [(d, F32), (d, BF16)], consts=[g], reds=[((1, d), F32)])


_TN_PREFS = (1408, 1024, 768, 512, 256, 128)


def _mm_nn(name, a, w, out_dtype, res=None, scale=1.0):
    m, k = a.shape
    j, _, nb = w.shape
    tm = _pick(m, (512, 256, 128))
    tn = _pick(nb, _TN_PREFS)
    tpb = nb // tn
    has_res = res is not None

    def body(a_ref, w_ref, *rest):
        o_ref = rest[-1]
        acc = jnp.dot(a_ref[...], w_ref[...], preferred_element_type=F32)
        if has_res:
            acc = rest[0][...] + scale * acc
        o_ref[...] = acc.astype(o_ref.dtype)

    in_specs = [pl.BlockSpec((tm, k), lambda n, i: (i, 0)),
                pl.BlockSpec((None, k, tn), lambda n, i: (n // tpb, 0, n % tpb))]
    args = [a, w]
    if has_res:
        in_specs.append(pl.BlockSpec((tm, tn), lambda n, i: (i, n)))
        args.append(res)
    return pl.pallas_call(
        body, name=name, grid=(j * tpb, m // tm), in_specs=in_specs,
        out_specs=pl.BlockSpec((tm, tn), lambda n, i: (i, n)),
        out_shape=jax.ShapeDtypeStruct((m, j * nb), out_dtype), compiler_params=_cparams(),
    )(*args)


def _mm_nt(name, dy, w, out_dtype):
    m, n = dy.shape
    j, k, nb = w.shape
    tm = _pick(m, (512, 256, 128))
    to = _pick(k, _TN_PREFS)
    tc = _pick(nb, (1536, 1408, 1024, 512, 256, 128))
    cpb = nb // tc
    n_red = j * cpb

    def body(dy_ref, w_ref, o_ref, acc_ref):
        r = pl.program_id(2)

        @pl.when(r == 0)
        def _():
            acc_ref[...] = jnp.zeros(acc_ref.shape, F32)

        acc_ref[...] += lax.dot_general(dy_ref[...], w_ref[...], (((1,), (1,)), ((), ())),
                                        preferred_element_type=F32)

        @pl.when(r == n_red - 1)
        def _():
            o_ref[...] = acc_ref[...].astype(o_ref.dtype)

    return pl.pallas_call(
        body, name=name, grid=(k // to, m // tm, n_red),
        in_specs=[pl.BlockSpec((tm, tc), lambda ko, i, r: (i, r)),
                  pl.BlockSpec((None, to, tc), lambda ko, i, r: (r // cpb, ko, r % cpb))],
        out_specs=pl.BlockSpec((tm, to), lambda ko, i, r: (i, ko)),
        out_shape=jax.ShapeDtypeStruct((m, k), out_dtype),
        scratch_shapes=[pltpu.VMEM((tm, to), F32)], compiler_params=_cparams(),
    )(dy, w)


def _mm_tn(name, a, dy, j):
    m, k = a.shape
    n = dy.shape[1]
    nb = n // j
    tk = _pick(k, (512, 256, 128))
    tn = _pick(nb, _TN_PREFS)
    tpb = nb // tn

    def body(a_ref, dy_ref, o_ref):
        o_ref[...] = lax.dot_general(a_ref[...], dy_ref[...], (((0,), (0,)), ((), ())),
                                     preferred_element_type=F32).astype(o_ref.dtype)

    return pl.pallas_call(
        body, name=name, grid=(k // tk, j * tpb),
        in_specs=[pl.BlockSpec((m, tk), lambda kb, nn: (0, kb)),
                  pl.BlockSpec((m, tn), lambda kb, nn: (0, nn))],
        out_specs=pl.BlockSpec((None, tk, tn), lambda kb, nn: (nn // tpb, kb, nn % tpb)),
        out_shape=jax.ShapeDtypeStruct((j, k, nb), BF16), compiler_params=_cparams(),
    )(a, dy)


def _cumsum_rows(x_ref, o_ref, blk, reverse):
    s = x_ref.shape[0]
    nblk = s // blk
    ri = lax.broadcasted_iota(jnp.int32, (blk, blk), 0)
    ci = lax.broadcasted_iota(jnp.int32, (blk, blk), 1)
    tri = jnp.where((ci >= ri) if reverse else (ci <= ri), 1.0, 0.0).astype(F32)
    carry = jnp.zeros((1, x_ref.shape[1]), F32)
    order = range(nblk - 1, -1, -1) if reverse else range(nblk)
    for b in order:
        xb = x_ref[b * blk:(b + 1) * blk, :]
        o_ref[b * blk:(b + 1) * blk, :] = jnp.dot(tri, xb, preferred_element_type=F32,
                                                  precision=lax.Precision.HIGHEST) + carry
        carry = carry + jnp.sum(xb, axis=0, keepdims=True)


def _fgate_fwd(name, zf, bf_row):
    s, w = zf.shape
    blk = _pick(s, (256, 128))

    def body(z_ref, b_ref, c_ref, ls_ref):
        v = z_ref[...] + b_ref[...]
        ls_ref[...] = jnp.minimum(v, 0.0) - jnp.log(1.0 + jnp.exp(-jnp.abs(v)))
        _cumsum_rows(ls_ref, c_ref, blk, reverse=False)

    return pl.pallas_call(
        body, name=name, out_shape=jax.ShapeDtypeStruct((s, w), F32),
        scratch_shapes=[pltpu.VMEM((s, w), F32)], compiler_params=_cparams(),
    )(zf, bf_row)


def _fgate_bwd(name, dc_q, dc_k, zf, bf_row):
    s, w = zf.shape
    blk = _pick(s, (256, 128))

    def body(dcq_ref, dck_ref, z_ref, b_ref, dz_ref, dzb_ref, db_ref, dls_ref, dc_ref):
        acc = dcq_ref[0] + dck_ref[0]
        for pr in range(1, dc_q.shape[0]):
            acc = acc + (dcq_ref[pr] + dck_ref[pr])
        dc_ref[...] = acc
        _cumsum_rows(dc_ref, dls_ref, blk, reverse=True)
        dz = dls_ref[...] * _sig(-(z_ref[...] + b_ref[...]))
        dz_ref[...] = dz
        dzb_ref[...] = dz.astype(BF16)
        db_ref[...] = jnp.sum(dz, axis=0, keepdims=True)

    return pl.pallas_call(
        body, name=name,
        out_shape=[jax.ShapeDtypeStruct((s, w), F32), jax.ShapeDtypeStruct((s, w), BF16),
                   jax.ShapeDtypeStruct((1, w), F32)],
        scratch_shapes=[pltpu.VMEM((s, w), F32), pltpu.VMEM((s, w), F32)], compiler_params=_cparams(),
    )(dc_q, dc_k, zf, bf_row)


def _scores(q, k, cq_col, ck_row, scale, row0, col0):
    s = lax.dot_general(q, k, (((1,), (1,)), ((), ())), preferred_element_type=F32) * scale
    s = s + (cq_col - ck_row)
    rows = row0 + lax.broadcasted_iota(jnp.int32, s.shape, 0)
    cols = col0 + lax.broadcasted_iota(jnp.int32, s.shape, 1)
    return jnp.where(cols <= rows, s, -jnp.inf)


def _attn_fwd(name, q, k, v, c_col, c_row, tb):
    h, s, hd = q.shape
    nb = s // tb
    scale = 1.0 / float(hd) ** 0.5

    def body(q_ref, k_ref, v_ref, cq_ref, ck_ref, o_ref, lse_ref):
        i = pl.program_id(1)
        qv = q_ref[...]
        cq = cq_ref[...]

        def step(j, carry):
            m_i, l_i, acc = carry
            k0 = pl.multiple_of(j * tb, tb)
            sc = _scores(qv, k_ref[pl.ds(k0, tb), :], cq, ck_ref[j], scale, i * tb, j * tb)
            m_new = jnp.maximum(m_i, jnp.max(sc, axis=-1, keepdims=True))
            alpha = jnp.exp(m_i - m_new)
            p = jnp.exp(sc - m_new)
            l_new = alpha * l_i + jnp.sum(p, axis=-1, keepdims=True)
            acc = alpha * acc + jnp.dot(p.astype(BF16), v_ref[pl.ds(k0, tb), :], preferred_element_type=F32)
            return m_new, l_new, acc

        init = (jnp.full((tb, 1), -jnp.inf, F32), jnp.zeros((tb, 1), F32), jnp.zeros((tb, hd), F32))
        m_i, l_i, acc = lax.fori_loop(0, i + 1, step, init)
        o_ref[...] = (acc / l_i).astype(o_ref.dtype)
        lse_ref[...] = m_i + jnp.log(l_i)

    return pl.pallas_call(
        body, name=name, grid=(h, nb),
        in_specs=[pl.BlockSpec((None, tb, hd), lambda hh, i: (hh, i, 0)),
                  pl.BlockSpec((None, s, hd), lambda hh, i: (hh, 0, 0)),
                  pl.BlockSpec((None, s, hd), lambda hh, i: (hh, 0, 0)),
                  pl.BlockSpec((None, tb, 1), lambda hh, i: (hh, i, 0)),
                  pl.BlockSpec((None, nb, 1, tb), lambda hh, i: (hh, 0, 0, 0))],
        out_specs=[pl.BlockSpec((None, tb, hd), lambda hh, i: (hh, i, 0)),
                   pl.BlockSpec((None, tb, 1), lambda hh, i: (hh, i, 0))],
        out_shape=[jax.ShapeDtypeStruct((h, s, hd), F32), jax.ShapeDtypeStruct((h, s, 1), F32)],
        compiler_params=_cparams(),
    )(q, k, v, c_col, c_row)


def _attn_bwd_q(name, q, k, v, o, do, lse, c_col, c_row, tb):
    h, s, hd = q.shape
    nb = s // tb
    scale = 1.0 / float(hd) ** 0.5

    def body(q_ref, k_ref, v_ref, o_ref, do_ref, lse_ref, cq_ref, ck_ref, dq_ref, dl_ref, dcq_ref):
        i = pl.program_id(1)
        qv = q_ref[...]
        dov = do_ref[...]
        cq = cq_ref[...]
        lse_v = lse_ref[...]
        delta = jnp.sum(dov.astype(F32) * o_ref[...], axis=-1, keepdims=True)

        def step(j, carry):
            dq, dcq = carry
            k0 = pl.multiple_of(j * tb, tb)
            kj = k_ref[pl.ds(k0, tb), :]
            p = jnp.exp(_scores(qv, kj, cq, ck_ref[j], scale, i * tb, j * tb) - lse_v)
            dp = lax.dot_general(dov, v_ref[pl.ds(k0, tb), :], (((1,), (1,)), ((), ())), preferred_element_type=F32)
            ds = p * (dp - delta)
            return (dq + jnp.dot(ds.astype(BF16), kj, preferred_element_type=F32),
                    dcq + jnp.sum(ds, axis=-1, keepdims=True))

        dq, dcq = lax.fori_loop(0, i + 1, step, (jnp.zeros((tb, hd), F32), jnp.zeros((tb, 1), F32)))
        dq_ref[...] = (dq * scale).astype(dq_ref.dtype)
        dl_ref[...] = delta
        dcq_ref[...] = dcq

    blk = pl.BlockSpec((None, tb, hd), lambda hh, i: (hh, i, 0))
    full = pl.BlockSpec((None, s, hd), lambda hh, i: (hh, 0, 0))
    col = pl.BlockSpec((None, tb, 1), lambda hh, i: (hh, i, 0))
    return pl.pallas_call(
        body, name=name, grid=(h, nb),
        in_specs=[blk, full, full, blk, blk, col, col,
                  pl.BlockSpec((None, nb, 1, tb), lambda hh, i: (hh, 0, 0, 0))],
        out_specs=[blk, col, col],
        out_shape=[jax.ShapeDtypeStruct((h, s, hd), BF16), jax.ShapeDtypeStruct((h, s, 1), F32),
                   jax.ShapeDtypeStruct((h, s, 1), F32)],
        compiler_params=_cparams(),
    )(q, k, v, o, do, lse, c_col, c_row)


def _attn_bwd_kv(name, q, k, v, do, lse_row, delta_row, c_col, c_row, tb):
    h, s, hd = q.shape
    nb = s // tb
    scale = 1.0 / float(hd) ** 0.5

    def body(q_ref, k_ref, v_ref, do_ref, lse_ref, dl_ref, ck_ref, cq_ref, dk_ref, dv_ref, dc_ref):
        j = pl.program_id(1)
        kv = k_ref[...]
        vv = v_ref[...]
        ck = ck_ref[...]

        def step(i, carry):
            dk, dv, dc = carry
            q0 = pl.multiple_of(i * tb, tb)
            qi = q_ref[pl.ds(q0, tb), :]
            doi = do_ref[pl.ds(q0, tb), :]
            st = lax.dot_general(kv, qi, (((1,), (1,)), ((), ())), preferred_element_type=F32) * scale
            st = st + (cq_ref[i] - ck)
            krow = j * tb + lax.broadcasted_iota(jnp.int32, st.shape, 0)
            qcol = i * tb + lax.broadcasted_iota(jnp.int32, st.shape, 1)
            pt = jnp.exp(jnp.where(krow <= qcol, st, -jnp.inf) - lse_ref[i])
            dv = dv + jnp.dot(pt.astype(BF16), doi, preferred_element_type=F32)
            dpt = lax.dot_general(vv, doi, (((1,), (1,)), ((), ())), preferred_element_type=F32)
            dst = pt * (dpt - dl_ref[i])
            dk = dk + jnp.dot(dst.astype(BF16), qi, preferred_element_type=F32)
            dc = dc - jnp.sum(dst, axis=-1, keepdims=True)
            return dk, dv, dc

        init = (jnp.zeros((tb, hd), F32), jnp.zeros((tb, hd), F32), jnp.zeros((tb, 1), F32))
        dk, dv, dc = lax.fori_loop(j, nb, step, init)
        dk_ref[...] = (dk * scale).astype(dk_ref.dtype)
        dv_ref[...] = dv.astype(dv_ref.dtype)
        dc_ref[...] = dc

    blk = pl.BlockSpec((None, tb, hd), lambda hh, jj: (hh, jj, 0))
    full = pl.BlockSpec((None, s, hd), lambda hh, jj: (hh, 0, 0))
    col = pl.BlockSpec((None, tb, 1), lambda hh, jj: (hh, jj, 0))
    rows = pl.BlockSpec((None, nb, 1, tb), lambda hh, jj: (hh, 0, 0, 0))
    return pl.pallas_call(
        body, name=name, grid=(h, nb),
        in_specs=[full, blk, blk, full, rows, rows, col, rows],
        out_specs=[blk, blk, col],
        out_shape=[jax.ShapeDtypeStruct((h, s, hd), BF16), jax.ShapeDtypeStruct((h, s, hd), BF16),
                   jax.ShapeDtypeStruct((h, s, 1), F32)],
        compiler_params=_cparams(),
    )(q, k, v, do, lse_row, delta_row, c_col, c_row)


def _pair_masks(hd):
    lane = lax.broadcasted_iota(jnp.int32, (1, 2 * hd), 1)
    return lane < hd, lane >= hd


def _only(mask, a):
    return jnp.where(mask, a, jnp.zeros_like(a))


def _on_head_lanes(col0, col1):
    lane = lax.broadcasted_iota(jnp.int32, (1, LANES), 1)
    h0 = 2 * pl.program_id(0)
    return jnp.where(lane == h0, col0, 0.0) + jnp.where(lane == h0 + 1, col1, 0.0)


def _nt(a, b):
    return lax.dot_general(a, b, (((1,), (1,)), ((), ())), preferred_element_type=F32)


def _causal(sc, rows_are_queries):
    r = lax.broadcasted_iota(jnp.int32, sc.shape, 0)
    c = lax.broadcasted_iota(jnp.int32, sc.shape, 1)
    return jnp.where((c <= r) if rows_are_queries else (r <= c), sc, -jnp.inf)


def _pair_specs(dm, s):
    hd = dm.aw // dm.nh
    pw = 2 * hd
    assert pw == LANES and dm.o_q % pw == 0 and dm.aw % pw == 0
    return hd, pw, dm.o_q // pw, (dm.o_q + dm.aw) // pw, (dm.o_q + 2 * dm.aw) // pw


def _attn2_fwd(name, zm, c_col, c_row, dm):
    s, tb = zm.shape[0], dm.tb
    nb = s // tb
    hd, pw, qb, kb, vb = _pair_specs(dm, s)
    scale = 1.0 / float(hd) ** 0.5

    def body(q_ref, k_ref, v_ref, cq_ref, cr_ref, o_ref, ob_ref, lse_ref):
        i = pl.program_id(1)
        masks = _pair_masks(hd)
        q2 = q_ref[...]
        qe = [_only(m, q2) for m in masks]
        cq = [cq_ref[0], cq_ref[1]]

        def block(j, carry, diag):
            k0 = pl.multiple_of(j * tb, tb)
            kj = k_ref[pl.ds(k0, tb), :]
            vj = v_ref[pl.ds(k0, tb), :]
            out = []
            for e in range(2):
                m_i, l_i, acc = carry[e]
                sc = _nt(qe[e], kj) * scale + (cq[e] - cr_ref[e, j])
                if diag:
                    sc = _causal(sc, True)
                m_new = jnp.maximum(m_i, jnp.max(sc, axis=-1, keepdims=True))
                alpha = jnp.exp(m_i - m_new)
                p = jnp.exp(sc - m_new)
                l_new = alpha * l_i + jnp.sum(p, axis=-1, keepdims=True)
                acc = alpha * acc + jnp.dot(p.astype(BF16), vj, preferred_element_type=F32)
                out.append((m_new, l_new, acc))
            return tuple(out)

        one = (jnp.full((tb, 1), -jnp.inf, F32), jnp.zeros((tb, 1), F32), jnp.zeros((tb, pw), F32))
        carry = lax.fori_loop(0, i, lambda j, c: block(j, c, False), (one, one))
        (m0, l0, a0), (m1, l1, a1) = block(i, carry, True)
        o = jnp.where(masks[0], a0 / l0, a1 / l1)
        o_ref[...] = o
        ob_ref[...] = o.astype(BF16)
        lse_ref[0] = m0 + jnp.log(l0)
        lse_ref[1] = m1 + jnp.log(l1)

    blk = lambda cb: pl.BlockSpec((tb, pw), functools.partial(lambda hp, i, cb: (i, cb + hp), cb=cb))
    full = lambda cb: pl.BlockSpec((s, pw), functools.partial(lambda hp, i, cb: (0, cb + hp), cb=cb))
    col = pl.BlockSpec((2, tb, 1), lambda hp, i: (hp, i, 0))
    rows = pl.BlockSpec((2, nb, 1, tb), lambda hp, i: (hp, 0, 0, 0))
    return pl.pallas_call(
        body, name=name, grid=(dm.nh // 2, nb),
        in_specs=[blk(qb), full(kb), full(vb), col, rows],
        out_specs=[blk(0), blk(0), col],
        out_shape=[jax.ShapeDtypeStruct((s, dm.aw), F32), jax.ShapeDtypeStruct((s, dm.aw), BF16),
                   jax.ShapeDtypeStruct((dm.nh, s, 1), F32)],
        compiler_params=_cparams(),
    )(zm, zm, zm, c_col, c_row)


def _attn2_bwd_q(name, zm, o, do, lse, c_col, c_row, dm):
    s, tb = zm.shape[0], dm.tb
    nb = s // tb
    hd, pw, qb, kb, vb = _pair_specs(dm, s)
    scale = 1.0 / float(hd) ** 0.5

    def body(q_ref, k_ref, v_ref, o_ref, do_ref, lse_ref, cq_ref, cr_ref, dq_ref, dl_ref, dcq_ref):
        i = pl.program_id(1)
        masks = _pair_masks(hd)
        q2 = q_ref[...]
        do2 = do_ref[...]
        prod = do2.astype(F32) * o_ref[...]
        qe = [_only(m, q2) for m in masks]
        doe = [_only(m, do2) for m in masks]
        delta = [jnp.sum(_only(m, prod), axis=-1, keepdims=True) for m in masks]
        cq = [cq_ref[0], cq_ref[1]]
        lse_v = [lse_ref[0], lse_ref[1]]

        def block(j, carry, diag):
            k0 = pl.multiple_of(j * tb, tb)
            kj = k_ref[pl.ds(k0, tb), :]
            vj = v_ref[pl.ds(k0, tb), :]
            out = []
            for e in range(2):
                dq, dcq = carry[e]
                sc = _nt(qe[e], kj) * scale + (cq[e] - cr_ref[e, j])
                if diag:
                    sc = _causal(sc, True)
                p = jnp.exp(sc - lse_v[e])
                ds = p * (_nt(doe[e], vj) - delta[e])
                out.append((dq + jnp.dot(ds.astype(BF16), kj, preferred_element_type=F32),
                            dcq + jnp.sum(ds, axis=-1, keepdims=True)))
            return tuple(out)

        one = (jnp.zeros((tb, pw), F32), jnp.zeros((tb, 1), F32))
        carry = lax.fori_loop(0, i, lambda j, c: block(j, c, False), (one, one))
        (dq0, dc0), (dq1, dc1) = block(i, carry, True)
        dq_ref[...] = (jnp.where(masks[0], dq0, dq1) * scale).astype(dq_ref.dtype)
        dl_ref[0] = delta[0]
        dl_ref[1] = delta[1]
        dcq_ref[...] = _on_head_lanes(dc0, dc1)

    blk = lambda cb: pl.BlockSpec((tb, pw), functools.partial(lambda hp, i, cb: (i, cb + hp), cb=cb))
    full = lambda cb: pl.BlockSpec((s, pw), functools.partial(lambda hp, i, cb: (0, cb + hp), cb=cb))
    col = pl.BlockSpec((2, tb, 1), lambda hp, i: (hp, i, 0))
    rows = pl.BlockSpec((2, nb, 1, tb), lambda hp, i: (hp, 0, 0, 0))
    return pl.pallas_call(
        body, name=name, grid=(dm.nh // 2, nb),
        in_specs=[blk(qb), full(kb), full(vb), blk(0), blk(0), col, col, rows],
        out_specs=[blk(0), col, pl.BlockSpec((None, tb, LANES), lambda hp, i: (hp, i, 0))],
        out_shape=[jax.ShapeDtypeStruct((s, dm.aw), BF16), jax.ShapeDtypeStruct((dm.nh, s, 1), F32),
                   jax.ShapeDtypeStruct((dm.nh // 2, s, LANES), F32)],
        compiler_params=_cparams(),
    )(zm, zm, zm, o, do, lse, c_col, c_row)


def _attn2_bwd_kv(name, zm, do, lse_row, delta_row, c_col, c_row, dm):
    s, tb = zm.shape[0], dm.tb
    nb = s // tb
    hd, pw, qb, kb, vb = _pair_specs(dm, s)
    scale = 1.0 / float(hd) ** 0.5

    def body(q_ref, k_ref, v_ref, do_ref, lse_ref, dl_ref, ck_ref, cr_ref, dk_ref, dv_ref, dc_ref):
        j = pl.program_id(1)
        masks = _pair_masks(hd)
        k2 = k_ref[...]
        v2 = v_ref[...]
        ke = [_only(m, k2) for m in masks]
        ve = [_only(m, v2) for m in masks]
        ck = [ck_ref[0], ck_ref[1]]

        def block(i, carry, diag):
            q0 = pl.multiple_of(i * tb, tb)
            qi = q_ref[pl.ds(q0, tb), :]
            doi = do_ref[pl.ds(q0, tb), :]
            out = []
            for e in range(2):
                dk, dv, dc = carry[e]
                st = _nt(ke[e], qi) * scale + (cr_ref[e, i] - ck[e])
                if diag:
                    st = _causal(st, False)
                pt = jnp.exp(st - lse_ref[e, i])
                dv = dv + jnp.dot(pt.astype(BF16), doi, preferred_element_type=F32)
                dst = pt * (_nt(ve[e], doi) - dl_ref[e, i])
                dk = dk + jnp.dot(dst.astype(BF16), qi, preferred_element_type=F32)
                out.append((dk, dv, dc - jnp.sum(dst, axis=-1, keepdims=True)))
            return tuple(out)

        one = (jnp.zeros((tb, pw), F32), jnp.zeros((tb, pw), F32), jnp.zeros((tb, 1), F32))
        carry = block(j, (one, one), True)
        (dk0, dv0, dc0), (dk1, dv1, dc1) = lax.fori_loop(j + 1, nb, lambda i, c: block(i, c, False), carry)
        dk_ref[...] = (jnp.where(masks[0], dk0, dk1) * scale).astype(dk_ref.dtype)
        dv_ref[...] = jnp.where(masks[0], dv0, dv1).astype(dv_ref.dtype)
        dc_ref[...] = _on_head_lanes(dc0, dc1)

    blk = lambda cb: pl.BlockSpec((tb, pw), functools.partial(lambda hp, jj, cb: (jj, cb + hp), cb=cb))
    full = lambda cb: pl.BlockSpec((s, pw), functools.partial(lambda hp, jj, cb: (0, cb + hp), cb=cb))
    col = pl.BlockSpec((2, tb, 1), lambda hp, jj: (hp, jj, 0))
    rows = pl.BlockSpec((2, nb, 1, tb), lambda hp, jj: (hp, 0, 0, 0))
    return pl.pallas_call(
        body, name=name, grid=(dm.nh // 2, nb),
        in_specs=[full(qb), blk(kb), blk(vb), full(0), rows, rows, col, rows],
        out_specs=[blk(0), blk(0), pl.BlockSpec((None, tb, LANES), lambda hp, jj: (hp, jj, 0))],
        out_shape=[jax.ShapeDtypeStruct((s, dm.aw), BF16), jax.ShapeDtypeStruct((s, dm.aw), BF16),
                   jax.ShapeDtypeStruct((dm.nh // 2, s, LANES), F32)],
        compiler_params=_cparams(),
    )(zm, zm, zm, do, lse_row, delta_row, c_col, c_row)


def _glu(cv, cc):
    c1 = cv[:, :cc].astype(F32)
    c2 = cv[:, cc:].astype(F32)
    return c1 * _sig(c2)


def _conv_fwd(name, zm, w_pad, b_row, g_row, cc, taps):
    s = zm.shape[0]
    tr = _pick(s, (256, 128))
    hpb = tr // HALO
    off = HALO - (taps - 1)

    def body(cur_ref, halo_ref, w_ref, b_ref, g_ref, y_ref, cs_ref, apad):
        i = pl.program_id(0)
        apad[0:HALO, :] = _glu(halo_ref[...], cc) * jnp.where(i > 0, 1.0, 0.0)
        apad[HALO:, :] = _glu(cur_ref[...], cc)
        acc = jnp.zeros((tr, cc), F32) + b_ref[...]
        for t in range(taps):
            acc = acc + w_ref[t:t + 1, :] * apad[off + t:off + t + tr, :]
        y_ref[...] = acc
        r = lax.rsqrt(jnp.mean(acc * acc, axis=-1, keepdims=True) + RMS_EPS)
        n = acc * r * g_ref[...]
        cs_ref[...] = (n * _sig(n)).astype(cs_ref.dtype)

    return pl.pallas_call(
        body, name=name, grid=(s // tr,),
        in_specs=[pl.BlockSpec((tr, 2 * cc), lambda i: (i, 0)),
                  pl.BlockSpec((HALO, 2 * cc), lambda i: (jnp.maximum(i * hpb - 1, 0), 0)),
                  pl.BlockSpec(w_pad.shape, lambda i: (0, 0)),
                  pl.BlockSpec(b_row.shape, lambda i: (0, 0)),
                  pl.BlockSpec(g_row.shape, lambda i: (0, 0))],
        out_specs=[pl.BlockSpec((tr, cc), lambda i: (i, 0)), pl.BlockSpec((tr, cc), lambda i: (i, 0))],
        out_shape=[jax.ShapeDtypeStruct((s, cc), F32), jax.ShapeDtypeStruct((s, cc), BF16)],
        scratch_shapes=[pltpu.VMEM((HALO + tr, cc), F32)], compiler_params=_cparams(),
    )(zm, zm, w_pad, b_row, g_row)


def _conv_bwd(name, zm, y, dcs, w_pad, g_row, cc, taps):
    s = zm.shape[0]
    tr = _pick(s, (256, 128))
    hpb = tr // HALO
    nblk = s // tr
    off = HALO - (taps - 1)

    def dy_of(yv, dcsv, g):
        r = lax.rsqrt(jnp.mean(yv * yv, axis=-1, keepdims=True) + RMS_EPS)
        xh = yv * r
        n = xh * g
        sg = _sig(n)
        dn = dcsv.astype(F32) * (sg * (1.0 + n * (1.0 - sg)))
        dxh = dn * g
        dy = r * (dxh - xh * jnp.mean(dxh * xh, axis=-1, keepdims=True))
        return dy, dn * xh

    def body(cur_ref, halo_ref, y_ref, yn_ref, dcs_ref, dcsn_ref, w_ref, g_ref,
             dcv_ref, dw_ref, db_ref, dg_ref, apad, dypad):
        i = pl.program_id(0)

        @pl.when(i == 0)
        def _():
            dw_ref[...] = jnp.zeros(dw_ref.shape, F32)
            db_ref[...] = jnp.zeros(db_ref.shape, F32)
            dg_ref[...] = jnp.zeros(dg_ref.shape, F32)

        g = g_ref[...]
        apad[0:HALO, :] = _glu(halo_ref[...], cc) * jnp.where(i > 0, 1.0, 0.0)
        apad[HALO:, :] = _glu(cur_ref[...], cc)
        dy, dgt = dy_of(y_ref[...], dcs_ref[...], g)
        dyn, _ = dy_of(yn_ref[...], dcsn_ref[...], g)
        dypad[0:tr, :] = dy
        dypad[tr:, :] = dyn * jnp.where(i < nblk - 1, 1.0, 0.0)
        db_ref[...] += jnp.sum(dy, axis=0, keepdims=True)
        dg_ref[...] += jnp.sum(dgt, axis=0, keepdims=True)
        da = jnp.zeros((tr, cc), F32)
        for t in range(taps):
            da = da + w_ref[t:t + 1, :] * dypad[taps - 1 - t:taps - 1 - t + tr, :]
            dw_ref[t:t + 1, :] += jnp.sum(dy * apad[off + t:off + t + tr, :], axis=0, keepdims=True)
        cv = cur_ref[...]
        c1 = cv[:, :cc].astype(F32)
        sg = _sig(cv[:, cc:].astype(F32))
        dcv_ref[:, :cc] = (da * sg).astype(dcv_ref.dtype)
        dcv_ref[:, cc:] = (da * c1 * sg * (1.0 - sg)).astype(dcv_ref.dtype)

    nxt = lambda i: (jnp.minimum((i + 1) * hpb, s // HALO - 1), 0)
    return pl.pallas_call(
        body, name=name, grid=(nblk,),
        in_specs=[pl.BlockSpec((tr, 2 * cc), lambda i: (i, 0)),
                  pl.BlockSpec((HALO, 2 * cc), lambda i: (jnp.maximum(i * hpb - 1, 0), 0)),
                  pl.BlockSpec((tr, cc), lambda i: (i, 0)), pl.BlockSpec((HALO, cc), nxt),
                  pl.BlockSpec((tr, cc), lambda i: (i, 0)), pl.BlockSpec((HALO, cc), nxt),
                  pl.BlockSpec(w_pad.shape, lambda i: (0, 0)), pl.BlockSpec(g_row.shape, lambda i: (0, 0))],
        out_specs=[pl.BlockSpec((tr, 2 * cc), lambda i: (i, 0)),
                   pl.BlockSpec(w_pad.shape, lambda i: (0, 0)),
                   pl.BlockSpec((1, cc), lambda i: (0, 0)), pl.BlockSpec((1, cc), lambda i: (0, 0))],
        out_shape=[jax.ShapeDtypeStruct((s, 2 * cc), BF16), jax.ShapeDtypeStruct(w_pad.shape, F32),
                   jax.ShapeDtypeStruct((1, cc), F32), jax.ShapeDtypeStruct((1, cc), F32)],
        scratch_shapes=[pltpu.VMEM((HALO + tr, cc), F32), pltpu.VMEM((tr + HALO, cc), F32)],
        compiler_params=_cparams(),
    )(zm, zm, y, y, dcs, dcs, w_pad, g_row)


def _place():
    x, y, c = lax.axis_index("x"), lax.axis_index("y"), lax.axis_index("c")
    chips = [(1 - x, y), (x, 1 - y), (1 - x, 1 - y)]
    return x, y, c, chips


def _half(c, rows):
    rh = rows // 2
    return pl.ds(pl.multiple_of(c * rh, 16), rh)


def _remote(src, dst, send, recv, dev):
    return pltpu.make_async_remote_copy(src_ref=src, dst_ref=dst, send_sem=send, recv_sem=recv,
                                        device_id=dev, device_id_type=MESH)


def _cast_slots(name, w, chip_arr):
    nl, r, cdim = w.shape
    tr = _half_rows(r)
    n = r // tr

    def body(q_ref, w_ref, *o_refs):
        layer = pl.program_id(0)
        for j, o_ref in enumerate(o_refs):
            @pl.when(layer == j)
            def _(o_ref=o_ref):
                o_ref[...] = w_ref[...].astype(o_ref.dtype)

    def out_map(j):
        return lambda l, i, q: (q[0], jnp.where(l < j, 0, jnp.where(l == j, i, n - 1)), 0)

    return pl.pallas_call(
        body, name=name,
        grid_spec=pltpu.PrefetchScalarGridSpec(
            num_scalar_prefetch=1, grid=(nl, n),
            in_specs=[pl.BlockSpec((None, tr, cdim), lambda l, i, q: (l, i, 0))],
            out_specs=[pl.BlockSpec((None, tr, cdim), out_map(j)) for j in range(nl)]),
        out_shape=[jax.ShapeDtypeStruct((N_CHIPS, r, cdim), BF16)] * nl, compiler_params=_cparams(),
    )(chip_arr, w)


def _split_start(name, n_sems, bufs, issue, after=()):
    nb = len(bufs)
    n_in = nb + len(after)

    def body(*refs):
        issue(refs[:nb], refs[n_in], refs[n_in + 1])
        refs[-1][...] = jnp.zeros(refs[-1].shape, F32)

    outs = pl.pallas_call(
        body, name=name, in_specs=[HBM_SPEC] * nb + [ANY] * len(after),
        out_shape=(pltpu.SemaphoreType.DMA(n_sems), pltpu.SemaphoreType.DMA(n_sems),
                   *[pltpu.HBM(b.shape, b.dtype) for b in bufs], jax.ShapeDtypeStruct((8, LANES), F32)),
        out_specs=(SEM_SPEC, SEM_SPEC, *[HBM_SPEC] * nb, pl.BlockSpec(memory_space=pltpu.VMEM)),
        input_output_aliases={t: t + 2 for t in range(nb)},
        compiler_params=pltpu.CompilerParams(has_side_effects=EFFECT),
    )(*[pltpu.with_memory_space_constraint(b, pltpu.HBM) for b in bufs], *after)
    return outs[0], outs[1], list(outs[2:2 + nb]), outs[-1]


def _split_wait(name, bufs, send, recv, after, drain):
    nb = len(bufs)

    def body(*refs):
        drain(refs[:nb], refs[nb], refs[nb + 1])

    return list(pl.pallas_call(
        body, name=name, in_specs=[HBM_SPEC] * nb + [SEM_SPEC, SEM_SPEC, ANY],
        out_shape=tuple(pltpu.HBM(b.shape, b.dtype) for b in bufs), out_specs=tuple([HBM_SPEC] * nb),
        input_output_aliases={t: t for t in range(nb)},
        compiler_params=pltpu.CompilerParams(has_side_effects=EFFECT),
    )(*bufs, send, recv, after))


def _gather_start(name, bufs, after=()):
    nt = len(bufs)

    def issue(g, send, recv):
        x, y, c, chips = _place()
        me = 2 * x + y
        for t in range(nt):
            part = g[t].at[me, _half(c, bufs[t].shape[1]), :]
            for j, (qx, qy) in enumerate(chips):
                _remote(part, part, send.at[3 * t + j], recv.at[3 * t + j], (qx, qy, c)).start()

    return _split_start(name, (3 * nt,), bufs, issue, after)


def _gather_wait(name, bufs, send, recv, after):
    nt = len(bufs)

    def drain(g, send, recv):
        x, y, c, _ = _place()
        for t in range(nt):
            part = g[t].at[0, pl.ds(0, bufs[t].shape[1] // 2), :]
            for j in range(3):
                cp = _remote(part, part, send.at[3 * t + j], recv.at[3 * t + j], (x, y, c))
                cp.wait_send()
                cp.wait_recv()

    return _split_wait(name, bufs, send, recv, after, drain)


def _gather_forward(name, bufs):
    nt = len(bufs)

    def body(*refs):
        g = refs[nt:2 * nt]
        send, recv = refs[2 * nt:]
        x, y, c, chips = _place()
        cps = []
        for t in range(nt):
            for j, (qx, qy) in enumerate(chips):
                part = g[t].at[2 * qx + qy, _half(c, bufs[t].shape[1]), :]
                cps.append(_remote(part, part, send.at[t, j], recv.at[t, j], (x, y, 1 - c)))
        for cp in cps:
            cp.start()
        for t in range(nt):
            for j, (qx, qy) in enumerate(chips):
                theirs = g[t].at[2 * qx + qy, _half(1 - c, bufs[t].shape[1]), :]
                _remote(theirs, theirs, send.at[t, j], recv.at[t, j], (x, y, 1 - c)).wait_recv()
        for cp in cps:
            cp.wait_send()

    return list(pl.pallas_call(
        body, name=name, in_specs=[ANY] * nt, out_specs=[ANY] * nt,
        out_shape=[jax.ShapeDtypeStruct(b.shape, b.dtype) for b in bufs],
        input_output_aliases={t: t for t in range(nt)},
        scratch_shapes=[pltpu.SemaphoreType.DMA((nt, 3)), pltpu.SemaphoreType.DMA((nt, 3))],
        compiler_params=pltpu.CompilerParams(has_side_effects=True),
    )(*bufs))


def _swap_halves(name, grads):
    nt = len(grads)

    def body(*refs):
        g_refs, r_refs = refs[:nt], refs[nt:2 * nt]
        send, recv = refs[2 * nt:]
        x, y, c, _ = _place()
        cps = [_remote(g_refs[t].at[:, _half(1 - c, grads[t].shape[1]), :], r_refs[t], send.at[t], recv.at[t],
                       (x, y, 1 - c)) for t in range(nt)]
        for cp in cps:
            cp.start()
        for cp in cps:
            cp.wait()

    return pl.pallas_call(
        body, name=name, in_specs=[ANY] * nt, out_specs=[ANY] * nt,
        out_shape=[jax.ShapeDtypeStruct((N_CHIPS, g.shape[1] // 2, g.shape[2]), g.dtype) for g in grads],
        scratch_shapes=[pltpu.SemaphoreType.DMA((nt,)), pltpu.SemaphoreType.DMA((nt,))],
        compiler_params=pltpu.CompilerParams(has_side_effects=True),
    )(*grads)


def _scatter_start(name, parts, after=()):
    nt = len(parts)
    lands = [lax.empty((3,) + p.shape[1:], p.dtype) for p in parts]

    def issue(refs, send, recv):
        x, y, c, chips = _place()
        for t in range(nt):
            for j, (qx, qy) in enumerate(chips):
                _remote(refs[t].at[2 * qx + qy], refs[nt + t].at[j], send.at[3 * t + j], recv.at[3 * t + j],
                        (qx, qy, c)).start()

    return _split_start(name, (3 * nt,), list(parts) + lands, issue, after)


def _scatter_wait(name, bufs, send, recv, after):
    nt = len(bufs) // 2

    def drain(refs, send, recv):
        x, y, c, _ = _place()
        for t in range(nt):
            for j in range(3):
                cp = _remote(refs[t].at[0], refs[nt + t].at[j], send.at[3 * t + j], recv.at[3 * t + j], (x, y, c))
                cp.wait_send()
                cp.wait_recv()

    return _split_wait(name, bufs, send, recv, after, drain)


def _join_halves(name, fulls):
    nt = len(fulls)

    def body(*refs):
        o_refs = refs[nt:2 * nt]
        send, recv = refs[2 * nt:]
        x, y, c, _ = _place()
        cps = []
        for t in range(nt):
            half = o_refs[t].at[_half(c, fulls[t].shape[0]), :]
            cps.append(_remote(half, half, send.at[t], recv.at[t], (x, y, 1 - c)))
        for cp in cps:
            cp.start()
        for t in range(nt):
            theirs = o_refs[t].at[_half(1 - c, fulls[t].shape[0]), :]
            _remote(theirs, theirs, send.at[t], recv.at[t], (x, y, 1 - c)).wait_recv()
        for cp in cps:
            cp.wait_send()

    return list(pl.pallas_call(
        body, name=name, in_specs=[ANY] * nt, out_specs=[ANY] * nt,
        out_shape=[jax.ShapeDtypeStruct(a.shape, a.dtype) for a in fulls],
        input_output_aliases={t: t for t in range(nt)},
        scratch_shapes=[pltpu.SemaphoreType.DMA((nt,)), pltpu.SemaphoreType.DMA((nt,))],
        compiler_params=pltpu.CompilerParams(has_side_effects=True),
    )(*fulls))


def _add_own_half(name, grad, recv_half, c_arr):
    _, r, cdim = grad.shape
    rh = r // 2
    tr = _half_rows(rh)
    nrb = rh // tr

    def body(c_ref, g_ref, r_ref, o_ref):
        o_ref[...] = (g_ref[...].astype(F32) + r_ref[...].astype(F32)).astype(o_ref.dtype)

    return pl.pallas_call(
        body, name=name,
        grid_spec=pltpu.PrefetchScalarGridSpec(
            num_scalar_prefetch=1, grid=(N_CHIPS, nrb),
            in_specs=[pl.BlockSpec((None, tr, cdim), lambda q, i, cr: (q, cr[0] * nrb + i, 0)),
                      pl.BlockSpec((None, tr, cdim), lambda q, i, cr: (q, i, 0))],
            out_specs=pl.BlockSpec((None, tr, cdim), lambda q, i, cr: (q, i, 0))),
        out_shape=jax.ShapeDtypeStruct((N_CHIPS, rh, cdim), BF16), compiler_params=_cparams(),
    )(c_arr, grad, recv_half)


def _add_chips(name, part, came, place_arr):
    _, rh, cdim = part.shape
    tr = _half_rows(rh)
    nrb = rh // tr

    def body(q_ref, p_ref, r_ref, o_ref):
        acc = p_ref[...].astype(F32)
        for j in range(3):
            acc = acc + r_ref[j].astype(F32)
        o_ref[...] = acc

    return pl.pallas_call(
        body, name=name,
        grid_spec=pltpu.PrefetchScalarGridSpec(
            num_scalar_prefetch=1, grid=(nrb,),
            in_specs=[pl.BlockSpec((None, tr, cdim), lambda i, qr: (qr[0], i, 0)),
                      pl.BlockSpec((3, tr, cdim), lambda i, qr: (0, i, 0))],
            out_specs=pl.BlockSpec((tr, cdim), lambda i, qr: (qr[1] * nrb + i, 0))),
        out_shape=jax.ShapeDtypeStruct((2 * rh, cdim), F32), compiler_params=_cparams(),
    )(place_arr, part, came)


def _reduce_scatter_begin(tag, grads, c_arr, after=()):
    nt = len(grads)
    got = _swap_halves(f"rs_swap_{tag}", grads)
    parts = [_add_own_half(f"rs_add2_{tag}_{t}", grads[t], got[t], c_arr) for t in range(nt)]
    return _scatter_start(f"rs_scatter_start_{tag}", parts, after)


def _reduce_scatter_end(tag, state, after, place_arr):
    send, recv, bufs, _ = state
    nt = len(bufs) // 2
    bufs = _scatter_wait(f"rs_scatter_wait_{tag}", bufs, send, recv, after)
    fulls = [_add_chips(f"rs_add4_{tag}_{t}", bufs[t], bufs[nt + t], place_arr) for t in range(nt)]
    return _join_halves(f"rs_join_{tag}", fulls)


def _allgather_blocks(name, blk):
    m_per, n = blk.shape

    def body(x_ref, out_ref, send_sems, recv_sems, local_sem):
        x, y, c, chips = _place()
        me, sibling = (x, y, c), (x, y, 1 - c)

        def rows(px, py, pc):
            return out_ref.at[pl.ds(pl.multiple_of((4 * px + 2 * py + pc) * m_per, 8), m_per), :]

        def copy(k, block, to, src=None):
            return _remote(rows(*block) if src is None else src, rows(*block), send_sems.at[k], recv_sems.at[k], to)

        mine = pltpu.make_async_copy(x_ref, rows(*me), local_sem)
        mine.start()
        first = [copy(0, me, sibling, src=x_ref)]
        first += [copy(1 + j, me, (*chip, c), src=x_ref) for j, chip in enumerate(chips)]
        for cp in first:
            cp.start()
        passed = [copy(4 + j, (*chip, c), sibling) for j, chip in enumerate(chips)]
        for j, chip in enumerate(chips):
            copy(1 + j, (*chip, c), me).wait_recv()
            passed[j].start()
        copy(0, sibling, me).wait_recv()
        for j, chip in enumerate(chips):
            copy(4 + j, (*chip, 1 - c), me).wait_recv()
        for cp in first + passed:
            cp.wait_send()
        mine.wait()

    return pl.pallas_call(
        body, name=name, out_shape=jax.ShapeDtypeStruct((N_DEV * m_per, n), blk.dtype),
        in_specs=[pl.BlockSpec(memory_space=pltpu.VMEM)], out_specs=pl.BlockSpec(memory_space=pltpu.VMEM),
        scratch_shapes=[pltpu.SemaphoreType.DMA((7,)), pltpu.SemaphoreType.DMA((7,)), pltpu.SemaphoreType.DMA],
        compiler_params=_cparams(),
    )(blk)


def _sum_blocks(name, stacked):
    nd, m, n = stacked.shape
    tr = _pick(m, (336, 256, 168, 128, 64, 32, 16, 8))

    def body(s_ref, o_ref):
        acc = s_ref[0]
        for d in range(1, nd):
            acc = acc + s_ref[d]
        o_ref[...] = acc

    return pl.pallas_call(
        body, name=name, grid=(m // tr,),
        in_specs=[pl.BlockSpec((nd, tr, n), lambda i: (0, i, 0))],
        out_specs=pl.BlockSpec((tr, n), lambda i: (i, 0)),
        out_shape=jax.ShapeDtypeStruct((m, n), F32), compiler_params=_cparams(),
    )(stacked)


def _adamw_vals(w, g, m, v):
    m2 = ADAM_B1 * m + (1.0 - ADAM_B1) * g
    v2 = ADAM_B2 * v + (1.0 - ADAM_B2) * (g * g)
    m_hat = m2 / (1.0 - ADAM_B1 ** ADAM_STEP)
    v_hat = v2 / (1.0 - ADAM_B2 ** ADAM_STEP)
    delta = -ADAM_LR * (m_hat / (jnp.sqrt(v_hat) + ADAM_EPS) + ADAM_WD * w)
    return delta, m2, v2


def _adamw_layer(name, layer, w, m, v, g, stacked, deps=()):
    nl, r, cdim = w.shape
    tr = _pick(r, (128, 64, 32, 16, 8))
    n_d = len(deps)

    def body(w_ref, m_ref, v_ref, g_ref, *rest):
        go_ref, d_ref, mo_ref, vo_ref = rest[4 + n_d:]
        gv = g_ref[...]
        delta, m2, v2 = _adamw_vals(w_ref[...], gv, m_ref[...], v_ref[...])
        go_ref[...] = gv
        d_ref[...] = delta
        mo_ref[...] = m2
        vo_ref[...] = v2

    big = pl.BlockSpec((None, tr, cdim), lambda i: (layer, i, 0))
    return pl.pallas_call(
        body, name=name, grid=(r // tr,),
        in_specs=[big, big, big, pl.BlockSpec((tr, cdim), lambda i: (i, 0))] + [ANY] * (4 + n_d),
        out_specs=[big, big, big, big], out_shape=[jax.ShapeDtypeStruct(w.shape, F32)] * 4,
        input_output_aliases={4 + k: k for k in range(4)}, compiler_params=_cparams(),
    )(w, m, v, g, *stacked, *deps)


def _adamw_small(name, w, g, m, v):
    def body(w_ref, g_ref, m_ref, v_ref, d_ref, mo_ref, vo_ref):
        delta, m2, v2 = _adamw_vals(w_ref[...], g_ref[...], m_ref[...], v_ref[...])
        d_ref[...] = delta
        mo_ref[...] = m2
        vo_ref[...] = v2

    return pl.pallas_call(body, name=name, out_shape=[jax.ShapeDtypeStruct(w.shape, F32)] * 3,
                          compiler_params=_cparams())(w, g, m, v)


def _swiglu_fwd(name, z, f):
    def fn(zz):
        a = zz[:, :f].astype(F32)
        b = zz[:, f:].astype(F32)
        return a * _sig(a) * b
    return _rowwise(name, fn, [z], [(f, BF16)])[0]


def _swiglu_bwd(name, z, ds, f, deps=()):
    def fn(zz, dd):
        a = zz[:, :f].astype(F32)
        b = zz[:, f:].astype(F32)
        d = dd.astype(F32)
        sg = _sig(a)
        da = d * b * (sg * (1.0 + a * (1.0 - sg)))
        db = d * a * sg
        return jnp.concatenate([da, db], axis=1)
    return _rowwise(name, fn, [z, ds], [(2 * f, BF16)], deps=deps)[0]


def _heads(a, nh):
    s, w = a.shape
    return jnp.transpose(a.reshape(s, nh, w // nh), (1, 0, 2))


def _unheads(a):
    nh, s, hd = a.shape
    return jnp.transpose(a, (1, 0, 2)).reshape(s, nh * hd)


def _as_rows(col, tb):
    nh, s, _ = col.shape
    return col.reshape(nh, s // tb, 1, tb)


class _Dims:
    def __init__(self, d, f, aw, nh, cc, taps, dp, s):
        self.d, self.f, self.aw, self.nh, self.cc, self.taps, self.dp, self.s = d, f, aw, nh, cc, taps, dp, s
        self.o_cv, self.o_ga, self.o_gc = 0, 2 * cc, 2 * cc + d
        self.o_q = 2 * cc + 2 * d
        self.n_main = self.o_q + 3 * aw
        self.tb = _pick(s, (256, 128))
        assert self.o_ga % d == 0 and self.o_q % aw == 0 and nh <= LANES


def _ffn_fwd(tag, h, g_row, w_in, w_out, dm, deps=()):
    n = _rms_fwd(f"{tag}_rms", h, g_row, deps)
    z = _mm_nn(f"{tag}_in", n, w_in, BF16)
    s = _swiglu_fwd(f"{tag}_act", z, dm.f)
    h2 = _mm_nn(f"{tag}_out", s, w_out, F32, res=h, scale=FFN_RES)
    return h2, (h, n, z, s)


def _ffn_bwd(tag, dh, dh_half_b, saved, g_row, w_in, w_out, dm, oscale, deps=()):
    h, n, z, s = saved
    ds = _mm_nt(f"{tag}_dout", dh_half_b, w_out, BF16)
    dz = _swiglu_bwd(f"{tag}_dact", z, ds, dm.f, deps)
    dw_out = _mm_tn(f"{tag}_wout", s, dh_half_b, 1)
    dw_in = _mm_tn(f"{tag}_win", n, dz, N_CHIPS)
    dn = _mm_nt(f"{tag}_din", dz, w_in, BF16)
    dh0, dh0_b, dg = _rms_bwd_res(f"{tag}_drms", h, g_row, [dn], dh, oscale)
    return dh0, dh0_b, dg, dw_in, dw_out


def _mixer_fwd(tag, h, sm, wt, dm):
    d, cc, aw, nh, tb = dm.d, dm.cc, dm.aw, dm.nh, dm.tb
    u = _rms_fwd(f"{tag}_rms", h, sm["g_mix"])
    zm = _mm_nn(f"{tag}_in", u, wt["w_main"], BF16)
    zf = _mm_nn(f"{tag}_inf", u, wt["w_f"], F32)
    c = _fgate_fwd(f"{tag}_fgate", zf, sm["b_f"])
    c_col = jnp.transpose(c[:, :nh], (1, 0))[:, :, None]
    c_row = _as_rows(c_col, tb)
    o32, o, lse = _attn2_fwd(f"{tag}_attn", zm, c_col, c_row, dm)
    ya = _mm_nn(f"{tag}_aout", o, wt["w_attn_out"], BF16)
    y, cs = _conv_fwd(f"{tag}_conv", zm, sm["conv_w"], sm["conv_b"], sm["g_conv"], cc, dm.taps)
    yc = _mm_nn(f"{tag}_cout", cs, wt["w_conv_out"], BF16)

    def merge(ga, gc, a, b):
        return _sig(ga.astype(F32)) * a.astype(F32) + _sig(gc.astype(F32)) * b.astype(F32)

    mg = _rowwise(f"{tag}_merge", merge, [(zm, dm.o_ga // d, d), (zm, dm.o_gc // d, d), ya, yc], [(d, BF16)])[0]
    h2 = _mm_nn(f"{tag}_out", mg, wt["w_out"], F32, res=h, scale=1.0)
    return h2, (h, u, zm, zf, c_col, c_row, o32, lse, o, ya, y, cs, yc, mg)


def _mixer_bwd(tag, dh, dh_b, saved, sm, wt, dm, deps=()):
    d, cc, aw, nh, tb = dm.d, dm.cc, dm.aw, dm.nh, dm.tb
    h, u, zm, zf, c_col, c_row, o32, lse, o, ya, y, cs, yc, mg = saved
    dmg = _mm_nt(f"{tag}_dout", dh_b, wt["w_out"], BF16)
    dw_out = _mm_tn(f"{tag}_wout", mg, dh_b, 1)

    def unmerge(dd, ga, gc, a, b):
        dd, a, b = dd.astype(F32), a.astype(F32), b.astype(F32)
        sa, sc = _sig(ga.astype(F32)), _sig(gc.astype(F32))
        return dd * sa, dd * sc, dd * a * sa * (1.0 - sa), dd * b * sc * (1.0 - sc)

    dya, dyc, dga, dgc = _rowwise(f"{tag}_dmerge", unmerge,
                                  [dmg, (zm, dm.o_ga // d, d), (zm, dm.o_gc // d, d), ya, yc], [(d, BF16)] * 4,
                                  deps=deps)
    dw_a = _mm_tn(f"{tag}_waout", o, dya, N_CHIPS)
    do = _mm_nt(f"{tag}_daout", dya, wt["w_attn_out"], BF16)
    dw_c = _mm_tn(f"{tag}_wcout", cs, dyc, N_CHIPS)
    dcs = _mm_nt(f"{tag}_dcout", dyc, wt["w_conv_out"], BF16)
    dcv, dconv_w, dconv_b, dg_conv = _conv_bwd(f"{tag}_dconv", zm, y, dcs, sm["conv_w"], sm["g_conv"], cc, dm.taps)
    dq, delta, dcq_h = _attn2_bwd_q(f"{tag}_dattn_q", zm, o32, do, lse, c_col, c_row, dm)
    dk, dv, dck_h = _attn2_bwd_kv(f"{tag}_dattn_kv", zm, do, _as_rows(lse, tb), _as_rows(delta, tb),
                                  c_col, c_row, dm)
    dzf, dzf_b, db_f = _fgate_bwd(f"{tag}_dfgate", dcq_h, dck_h, zf, sm["b_f"])
    dzm = jnp.concatenate([dcv, dga, dgc, dq, dk, dv], axis=1)
    dw_main = _mm_tn(f"{tag}_win", u, dzm, 1)
    dw_f = _mm_tn(f"{tag}_winf", u, dzf_b, 1)
    du = _mm_nt(f"{tag}_din", dzm, wt["w_main"], BF16)
    du_f = _mm_nt(f"{tag}_dinf", dzf_b, wt["w_f"], BF16)
    dh0, dh0_b, dg_mix = _rms_bwd_res(f"{tag}_drms", h, sm["g_mix"], [du, du_f], dh, FFN_RES)
    small = dict(g_mix=dg_mix, b_f=db_f[:, :nh], conv_w=dconv_w[:dm.taps], conv_b=dconv_b, g_conv=dg_conv)
    return dh0, dh0_b, small, dw_main, dw_f, dw_a, dw_c, dw_out


def _ple_fwd(tag, h, p_b, sm, wt, dm):
    n = _rms_fwd(f"{tag}_rms", h, sm["g_ple"])
    gp = _mm_nn(f"{tag}_gate", n, wt["w_ple_gate"], BF16)
    pp = _mm_nn(f"{tag}_proj", p_b, wt["w_ple_proj"], BF16)

    def fn(hh, a, b):
        return hh + _sig(a.astype(F32)) * b.astype(F32)

    h2 = _rowwise(f"{tag}_mix", fn, [h, gp, pp], [(dm.d, F32)])[0]
    return h2, (h, n, gp, pp)


def _ple_bwd(tag, dh, saved, p_b, sm, wt, dm, deps=()):
    h, n, gp, pp = saved

    def fn(dd, a, b):
        gate = _sig(a.astype(F32))
        b = b.astype(F32)
        return dd * b * gate * (1.0 - gate), dd * gate

    dgp, dpp = _rowwise(f"{tag}_dmix", fn, [dh, gp, pp], [(dm.d, BF16)] * 2, deps=deps)
    dw_proj = _mm_tn(f"{tag}_wproj", p_b, dpp, N_CHIPS)
    dw_gate = _mm_tn(f"{tag}_wgate", n, dgp, 1)
    dn = _mm_nt(f"{tag}_dgate", dgp, wt["w_ple_gate"], BF16)
    dh0, dh0_b, dg = _rms_bwd_res(f"{tag}_drms", h, sm["g_ple"], [dn], dh, FFN_RES)
    return dh0, dh0_b, dg, dw_gate, dw_proj


def _loss_head(name, h, g_row, target):
    d = h.shape[1]

    def fn(x, tg, g):
        r = lax.rsqrt(jnp.mean(x * x, axis=-1, keepdims=True) + RMS_EPS)
        out = x * r * g
        e = out - tg
        per_row = jnp.sum(e * e, axis=-1, keepdims=True) * (0.5 / d)
        loss = jnp.zeros((1, LANES), F32) + jnp.sum(per_row, axis=0, keepdims=True)
        dx, dg = _rms_bwd_vals(x, g, e * (1.0 / d))
        return dx, loss, dg

    return _rowwise(name, fn, [h, target], [(d, F32)], consts=[g_row], reds=[((1, LANES), F32), ((1, d), F32)])


_BIG = ("w_ff1_in", "w_ff1_out", "w_in", "w_attn_out", "w_conv_out", "w_out", "w_ff2_in", "w_ff2_out",
        "w_ple_gate", "w_ple_proj")
_SMALL = ("g_ff1", "g_mix", "b_f", "conv_w", "conv_b", "g_conv", "g_ff2", "g_ple")
_ORDER = ("g_ff1", "w_ff1_in", "w_ff1_out", "g_mix", "w_in", "b_f", "w_attn_out", "conv_w", "conv_b", "g_conv",
          "w_conv_out", "w_out", "g_ff2", "w_ff2_in", "w_ff2_out", "g_ple", "w_ple_gate", "w_ple_proj", "g_final")


def _round_up(n, k):
    return (n + k - 1) // k * k


_ROW_SHARDED = ("w_ff1_out", "w_out", "w_ff2_out", "w_ple_gate")
_FIRST_LAYER_GROUPS = ((0, 1), (2, 3, 4, 5), (6, 7, 8, 9))


def _columns(pieces, lo, hi):
    out, pos = [], 0
    for a in pieces:
        w = a.shape[1]
        s, e = max(lo, pos), min(hi, pos + w)
        if s < e:
            out.append(a[:, s - pos:e - pos])
        pos += w
    return out


def _unpack_layer(g, dm):
    out = {}
    for n, a in g.items():
        if n == "w_in":
            blocks = [a[j] for j in range(N_CHIPS)]
            o_f = 3 * dm.aw
            o_c = o_f + dm.nh
            total = N_CHIPS * a.shape[2]
            out["w_main"] = jnp.concatenate(_columns(blocks, o_c, total) + _columns(blocks, 0, o_f), axis=1)[None]
            out["w_f"] = jnp.pad(jnp.concatenate(_columns(blocks, o_f, o_c), axis=1),
                                 ((0, 0), (0, LANES - dm.nh)))[None]
        elif n in _ROW_SHARDED:
            out[n] = a.reshape(1, a.shape[0] * a.shape[1], a.shape[2])
        else:
            out[n] = a
    return out


def _pack_grads(gw, names, dm):
    out = []
    for n in names:
        if n == "w_in":
            o_f = 3 * dm.aw
            main, wf = gw["w_main"][0], gw["w_f"][0]
            n_rest = dm.n_main - o_f
            pieces = [main[:, n_rest:], wf[:, :dm.nh], main[:, :n_rest]]
            nb = (dm.n_main + dm.nh) // N_CHIPS
            out.append(jnp.stack([jnp.concatenate(_columns(pieces, j * nb, (j + 1) * nb), axis=1)
                                  for j in range(N_CHIPS)]))
        elif n in _ROW_SHARDED:
            a = gw[n]
            out.append(a.reshape(N_CHIPS, a.shape[1] // N_CHIPS, a.shape[2]))
        else:
            out.append(gw[n])
    return out


def kernel(x, p, g_ff1, w_ff1_in, w_ff1_out, g_mix, w_in, b_f, w_attn_out, conv_w, conv_b, g_conv, w_conv_out, w_out, g_ff2, w_ff2_in, w_ff2_out, g_ple, w_ple_gate, w_ple_proj, g_final, loss_target, m_g_ff1, m_w_ff1_in, m_w_ff1_out, m_g_mix, m_w_in, m_b_f, m_w_attn_out, m_conv_w, m_conv_b, m_g_conv, m_w_conv_out, m_w_out, m_g_ff2, m_w_ff2_in, m_w_ff2_out, m_g_ple, m_w_ple_gate, m_w_ple_proj, m_g_final, v_g_ff1, v_w_ff1_in, v_w_ff1_out, v_g_mix, v_w_in, v_b_f, v_w_attn_out, v_conv_w, v_conv_b, v_g_conv, v_w_conv_out, v_w_out, v_g_ff2, v_w_ff2_in, v_w_ff2_out, v_g_ple, v_w_ple_gate, v_w_ple_proj, v_g_final):
    args = dict(locals())
    wts = {n: args[n] for n in _ORDER}
    mom = {n: args["m_" + n] for n in _ORDER}
    var = {n: args["v_" + n] for n in _ORDER}

    nl = g_ff1.shape[0]
    s, d = x.shape[1], x.shape[2]
    nh = b_f.shape[1]
    taps = conv_w.shape[1]
    cc = conv_b.shape[1]
    dm = _Dims(d=d, f=w_ff1_out.shape[1] * N_CHIPS, aw=w_attn_out.shape[1], nh=nh, cc=cc, taps=taps,
               dp=w_ple_proj.shape[1], s=s)
    assert taps - 1 <= HALO

    xi = lax.axis_index("x")
    yi = lax.axis_index("y")
    ci = lax.axis_index("c")
    c_arr = jnp.reshape(ci, (1,)).astype(jnp.int32)
    chip_arr = jnp.reshape(2 * xi + yi, (1,)).astype(jnp.int32)
    place_arr = jnp.stack([2 * xi + yi, ci]).astype(jnp.int32)

    h = x[0]
    target = loss_target[0]
    p_b = p[:, 0].astype(BF16)

    cw_rows = _round_up(nl * taps, 8)
    cw_blk = jnp.pad(conv_w.reshape(nl * taps, -1), ((0, cw_rows - nl * taps), (0, 0)))
    cw_all = _allgather_blocks("gather_conv_w", cw_blk).reshape(N_CHIPS, 2, cw_rows, -1)[:, 0, :nl * taps]
    conv_w_full = jnp.transpose(cw_all.reshape(N_CHIPS, nl, taps, -1), (1, 2, 0, 3)).reshape(nl, taps, cc)
    taps_pad = _round_up(taps, 8)

    def small_of(i):
        row = lambda a: a[i][None, :]
        return dict(g_ff1=row(g_ff1), g_mix=row(g_mix), g_conv=row(g_conv), conv_b=row(conv_b), g_ff2=row(g_ff2),
                    g_ple=row(g_ple), b_f=jnp.pad(b_f[i][None, :], ((0, 0), (0, LANES - nh))),
                    conv_w=jnp.pad(conv_w_full[i], ((0, taps_pad - taps), (0, 0))))

    slots = [_cast_slots(f"cast_{n}", wts[n], chip_arr) for n in _BIG]
    started = {}
    order = []

    def start_gather(i):
        if i < nl:
            groups = _FIRST_LAYER_GROUPS if i == 0 else (tuple(range(len(_BIG))),)
            started[i] = []
            for k, g in enumerate(groups):
                st = _gather_start(f"gather_start_l{i}g{k}", [slots[t][i] for t in g], after=order[-1:] + [conv_w_full])
                order.append(st[3])
                started[i].append((g, st))

    start_gather(0)
    start_gather(1)
    layer_w, saved = [], []
    for i in range(nl):
        pend = started.pop(i)
        wt = {}

        def arrive(k, h_now, i=i, pend=pend, wt=wt):
            if k < len(pend):
                g, (send, recv, bufs, _) = pend[k]
                bufs = _gather_wait(f"gather_wait_l{i}g{k}", bufs, send, recv, h_now)
                bufs = _gather_forward(f"gather_fwd_l{i}g{k}", bufs)
                wt.update(_unpack_layer({_BIG[t]: b for t, b in zip(g, bufs)}, dm))

        arrive(0, h)
        start_gather(i + 2)
        in_flight = [st[3] for _, st in pend[1:]] + [st[3] for sts in started.values() for _, st in sts]
        sm = small_of(i)
        h, sv1 = _ffn_fwd(f"l{i}_ff1", h, sm["g_ff1"], wt["w_ff1_in"], wt["w_ff1_out"], dm, deps=in_flight)
        arrive(1, h)
        h, sv2 = _mixer_fwd(f"l{i}_mix", h, sm, wt, dm)
        arrive(2, h)
        h, sv3 = _ffn_fwd(f"l{i}_ff2", h, sm["g_ff2"], wt["w_ff2_in"], wt["w_ff2_out"], dm)
        h, sv4 = _ple_fwd(f"l{i}_ple", h, p_b[i], sm, wt, dm)
        layer_w.append((wt, sm))
        saved.append((sv1, sv2, sv3, sv4))

    dh, loss_row, dg_final = _loss_head("loss_head", h, g_final[None, :], target)

    big_grads = [None] * nl
    small_grads = [None] * nl
    leaving = None

    def begin_first(k, gw, after=()):
        names = [_BIG[t] for t in _FIRST_LAYER_GROUPS[k]]
        return _reduce_scatter_begin(f"l0g{k}", _pack_grads(gw, names, dm), c_arr, after=after)

    for i in range(nl - 1, -1, -1):
        wt, sm = layer_w[i]
        sv1, sv2, sv3, sv4 = saved[i]
        deps = [leaving[3]] if leaving is not None else []
        dh, dh_b, dg_ple, dw_gate, dw_proj = _ple_bwd(f"l{i}_ple", dh, sv4, p_b[i], sm, wt, dm, deps=deps)
        dh, dh_b, dg_ff2, dw_in2, dw_out2 = _ffn_bwd(f"l{i}_ff2", dh, dh_b, sv3, sm["g_ff2"], wt["w_ff2_in"],
                                                     wt["w_ff2_out"], dm, 1.0)
        gw = dict(w_ff2_in=dw_in2, w_ff2_out=dw_out2, w_ple_gate=dw_gate, w_ple_proj=dw_proj)
        first = [begin_first(2, gw)] if i == 0 else []
        dh, dh_b, sg, dw_main, dw_f, dw_a, dw_c, dw_o = _mixer_bwd(f"l{i}_mix", dh, dh_b, sv2, sm, wt, dm,
                                                                   deps=[st[3] for st in first])
        gw.update(w_main=dw_main, w_f=dw_f, w_attn_out=dw_a, w_conv_out=dw_c, w_out=dw_o)
        if i == 0:
            first.append(begin_first(1, gw))
        dh, dh_b, dg_ff1, dw_in1, dw_out1 = _ffn_bwd(f"l{i}_ff1", dh, dh_b, sv1, sm["g_ff1"], wt["w_ff1_in"],
                                                     wt["w_ff1_out"], dm, 1.0, deps=[st[3] for st in first[1:]])
        gw.update(w_ff1_in=dw_in1, w_ff1_out=dw_out1)
        if leaving is not None:
            big_grads[i + 1] = _reduce_scatter_end(f"l{i + 1}", leaving, dh, place_arr)
        sg.update(g_ff1=dg_ff1, g_ff2=dg_ff2, g_ple=dg_ple)
        small_grads[i] = sg
        if i > 0:
            leaving = _reduce_scatter_begin(f"l{i}", _pack_grads(gw, _BIG, dm), c_arr)
    grad_x = dh[None]

    pieces = [small_grads[i][n].reshape(-1) for i in range(nl) for n in _SMALL]
    pieces += [dg_final.reshape(-1), loss_row[0, :1]]
    flat = jnp.concatenate(pieces)
    n_flat = flat.shape[0]
    rows = _round_up(_round_up(n_flat, LANES) // LANES, 8)
    blk = jnp.pad(flat, (0, rows * LANES - n_flat)).reshape(rows, LANES)
    total = _sum_blocks("sum_small", _allgather_blocks("gather_small", blk).reshape(N_DEV, rows, LANES)).reshape(-1)
    small_tot = {n: [] for n in _SMALL}
    pos = 0
    for i in range(nl):
        for n in _SMALL:
            shp = small_grads[i][n].shape
            size = shp[0] * shp[1]
            small_tot[n].append(total[pos:pos + size].reshape(shp))
            pos += size
    g_final_tot = total[pos:pos + d]
    loss = total[pos + d]

    first.append(begin_first(0, gw, after=[total]))
    stacked = {n: [lax.empty(wts[n].shape, F32) for _ in range(4)] for n in _BIG}

    def adamw_big(i, deps):
        for t, n in enumerate(_BIG):
            stacked[n] = _adamw_layer(f"adamw_{n}_l{i}", i, wts[n], mom[n], var[n], big_grads[i][t], stacked[n], deps)

    for i in range(nl - 1, 0, -1):
        adamw_big(i, [first[-1][3]])
    after = stacked[_BIG[-1]][1] if nl > 1 else dh
    big_grads[0] = [None] * len(_BIG)
    for k, state in zip((2, 1, 0), first):
        for t, g in zip(_FIRST_LAYER_GROUPS[k], _reduce_scatter_end(f"l0g{k}", state, after, place_arr)):
            big_grads[0][t] = g
    adamw_big(0, [])
    grads, deltas, new_m, new_v = {}, {}, {}, {}
    for n in _BIG:
        grads[n], deltas[n], new_m[n], new_v[n] = stacked[n]
    chip = 2 * xi + yi
    for n in _SMALL:
        g = jnp.concatenate(small_tot[n], axis=0)
        if n == "conv_w":
            cpc = cc // N_CHIPS
            g = lax.dynamic_slice_in_dim(g.reshape(nl * taps, cc), chip * cpc, cpc, axis=1)
            shape2 = (nl * taps, cpc)
        else:
            shape2 = g.shape
        dl, mm, vv = _adamw_small(f"adamw_{n}", wts[n].reshape(shape2), g, mom[n].reshape(shape2),
                                  var[n].reshape(shape2))
        grads[n] = g.reshape(wts[n].shape)
        deltas[n], new_m[n], new_v[n] = (a.reshape(wts[n].shape) for a in (dl, mm, vv))
    g2 = g_final_tot[None, :]
    dl, mm, vv = _adamw_small("adamw_g_final", g_final[None, :], g2, m_g_final[None, :], v_g_final[None, :])
    grads["g_final"] = g_final_tot
    deltas["g_final"], new_m["g_final"], new_v["g_final"] = dl[0], mm[0], vv[0]

    return (loss, grad_x, *[grads[n] for n in _ORDER], *[deltas[n] for n in _ORDER],
            *[new_m[n] for n in _ORDER], *[new_v[n] for n in _ORDER])
```

```python
import functools

import jax
import jax.numpy as jnp
from jax import lax
from jax.experimental import pallas as pl
from jax.experimental.pallas import tpu as pltpu

F32 = jnp.float32
BF16 = jnp.bfloat16
MESH = pl.DeviceIdType.MESH
ANY = pl.BlockSpec(memory_space=pl.ANY)
HBM_SPEC = pl.BlockSpec(memory_space=pltpu.HBM)
SEM_SPEC = pl.BlockSpec(memory_space=pltpu.SEMAPHORE)
EFFECT = pltpu.SideEffectType.DATAFLOW_SIDE_EFFECTING

RMS_EPS = 1e-6
FFN_RES = 0.5
ADAM_LR = 0.001
ADAM_B1 = 0.9
ADAM_B2 = 0.999
ADAM_EPS = 1e-08
ADAM_WD = 0.01
ADAM_STEP = 10

N_CHIPS = 4
N_DEV = 8
LANES = 128
HALO = 32
VMEM_LIMIT = 56 * 1024 * 1024


def _cparams():
    return pltpu.CompilerParams(vmem_limit_bytes=VMEM_LIMIT)


def _pick(n, prefs):
    for p in prefs:
        if p <= n and n % p == 0:
            return p
    return n


def _half_rows(r):
    return r // 2 if r % 32 == 0 else r


def _sig(x):
    return 1.0 / (1.0 + jnp.exp(-x))


def _rowwise(name, fn, ins, outs, consts=(), reds=(), tm=None, deps=()):
    ins = [a if isinstance(a, tuple) else (a, 0, a.shape[1]) for a in ins]
    m = ins[0][0].shape[0]
    tm = tm or _pick(m, (256, 128, 64, 32, 16, 8))
    n_in, n_c, n_o, n_r = len(ins), len(consts), len(outs), len(reds)
    n_d = len(deps)
    consts = list(consts) + list(deps)

    def body(*refs):
        in_refs = refs[:n_in + n_c]
        o_refs = refs[n_in + n_c + n_d:n_in + n_c + n_d + n_o]
        r_refs = refs[n_in + n_c + n_d + n_o:]
        res = fn(*[r[...] for r in in_refs])
        if not isinstance(res, (tuple, list)):
            res = (res,)
        for r, v in zip(o_refs, res[:n_o]):
            r[...] = v.astype(r.dtype)
        if n_r:
            @pl.when(pl.program_id(0) == 0)
            def _():
                for r in r_refs:
                    r[...] = jnp.zeros(r.shape, r.dtype)
            for r, v in zip(r_refs, res[n_o:]):
                r[...] += v.astype(r.dtype)

    in_specs = [pl.BlockSpec((tm, w), functools.partial(lambda i, cb: (i, cb), cb=cb)) for (_, cb, w) in ins]
    in_specs += [pl.BlockSpec(c.shape, lambda i: (0, 0)) for c in consts]
    out_specs = [pl.BlockSpec((tm, w), lambda i: (i, 0)) for (w, _) in outs]
    out_specs += [pl.BlockSpec(s, lambda i: (0, 0)) for (s, _) in reds]
    out_shape = [jax.ShapeDtypeStruct((m, w), d) for (w, d) in outs]
    out_shape += [jax.ShapeDtypeStruct(s, d) for (s, d) in reds]
    res = pl.pallas_call(
        body, name=name, grid=(m // tm,), in_specs=in_specs, out_specs=out_specs, out_shape=out_shape,
        compiler_params=_cparams(),
    )(*[a for (a, _, _) in ins], *consts)
    return res


def _rms_fwd(name, h, g, deps=()):
    def fn(x, gg):
        r = lax.rsqrt(jnp.mean(x * x, axis=-1, keepdims=True) + RMS_EPS)
        return x * r * gg
    return _rowwise(name, fn, [h], [(h.shape[1], BF16)], consts=[g], deps=deps)[0]


def _rms_bwd_vals(x, g, dn):
    r = lax.rsqrt(jnp.mean(x * x, axis=-1, keepdims=True) + RMS_EPS)
    xh = x * r
    dxh = dn * g
    dx = r * (dxh - xh * jnp.mean(dxh * xh, axis=-1, keepdims=True))
    dg = jnp.sum(dn * xh, axis=0, keepdims=True)
    return dx, dg


def _rms_bwd_res(name, h, g, dns, dres, oscale):
    n_dn = len(dns)

    def fn(x, *rest):
        dn = rest[0].astype(F32)
        for t in rest[1:n_dn]:
            dn = dn + t.astype(F32)
        dr, gg = rest[n_dn], rest[n_dn + 1]
        dx, dg = _rms_bwd_vals(x, gg, dn)
        dh = dr + dx
        return dh, oscale * dh, dg

    d = h.shape[1]
    return _rowwise(name, fn, [h, *dns, dres], [(d, F32), (d, BF16)], consts=[g], reds=[((1, d), F32)])


_TN_PREFS = (1408, 1024, 768, 512, 256, 128)


def _mm_nn(name, a, w, out_dtype, res=None, scale=1.0):
    m, k = a.shape
    j, _, nb = w.shape
    tm = _pick(m, (512, 256, 128))
    tn = _pick(nb, _TN_PREFS)
    tpb = nb // tn
    has_res = res is not None

    def body(a_ref, w_ref, *rest):
        o_ref = rest[-1]
        acc = jnp.dot(a_ref[...], w_ref[...], preferred_element_type=F32)
        if has_res:
            acc = rest[0][...] + scale * acc
        o_ref[...] = acc.astype(o_ref.dtype)

    in_specs = [pl.BlockSpec((tm, k), lambda n, i: (i, 0)),
                pl.BlockSpec((None, k, tn), lambda n, i: (n // tpb, 0, n % tpb))]
    args = [a, w]
    if has_res:
        in_specs.append(pl.BlockSpec((tm, tn), lambda n, i: (i, n)))
        args.append(res)
    return pl.pallas_call(
        body, name=name, grid=(j * tpb, m // tm), in_specs=in_specs,
        out_specs=pl.BlockSpec((tm, tn), lambda n, i: (i, n)),
        out_shape=jax.ShapeDtypeStruct((m, j * nb), out_dtype), compiler_params=_cparams(),
    )(*args)


def _mm_nt(name, dy, w, out_dtype):
    m, n = dy.shape
    j, k, nb = w.shape
    tm = _pick(m, (512, 256, 128))
    to = _pick(k, _TN_PREFS)
    tc = _pick(nb, (1536, 1408, 1024, 512, 256, 128))
    cpb = nb // tc
    n_red = j * cpb

    def body(dy_ref, w_ref, o_ref, acc_ref):
        r = pl.program_id(2)

        @pl.when(r == 0)
        def _():
            acc_ref[...] = jnp.zeros(acc_ref.shape, F32)

        acc_ref[...] += lax.dot_general(dy_ref[...], w_ref[...], (((1,), (1,)), ((), ())),
                                        preferred_element_type=F32)

        @pl.when(r == n_red - 1)
        def _():
            o_ref[...] = acc_ref[...].astype(o_ref.dtype)

    return pl.pallas_call(
        body, name=name, grid=(k // to, m // tm, n_red),
        in_specs=[pl.BlockSpec((tm, tc), lambda ko, i, r: (i, r)),
                  pl.BlockSpec((None, to, tc), lambda ko, i, r: (r // cpb, ko, r % cpb))],
        out_specs=pl.BlockSpec((tm, to), lambda ko, i, r: (i, ko)),
        out_shape=jax.ShapeDtypeStruct((m, k), out_dtype),
        scratch_shapes=[pltpu.VMEM((tm, to), F32)], compiler_params=_cparams(),
    )(dy, w)


def _mm_tn(name, a, dy, j):
    m, k = a.shape
    n = dy.shape[1]
    nb = n // j
    tk = _pick(k, (512, 256, 128))
    tn = _pick(nb, _TN_PREFS)
    tpb = nb // tn

    def body(a_ref, dy_ref, o_ref):
        o_ref[...] = lax.dot_general(a_ref[...], dy_ref[...], (((0,), (0,)), ((), ())),
                                     preferred_element_type=F32).astype(o_ref.dtype)

    return pl.pallas_call(
        body, name=name, grid=(k // tk, j * tpb),
        in_specs=[pl.BlockSpec((m, tk), lambda kb, nn: (0, kb)),
                  pl.BlockSpec((m, tn), lambda kb, nn: (0, nn))],
        out_specs=pl.BlockSpec((None, tk, tn), lambda kb, nn: (nn // tpb, kb, nn % tpb)),
        out_shape=jax.ShapeDtypeStruct((j, k, nb), BF16), compiler_params=_cparams(),
    )(a, dy)


def _cumsum_rows(x_ref, o_ref, blk, reverse):
    s = x_ref.shape[0]
    nblk = s // blk
    ri = lax.broadcasted_iota(jnp.int32, (blk, blk), 0)
    ci = lax.broadcasted_iota(jnp.int32, (blk, blk), 1)
    tri = jnp.where((ci >= ri) if reverse else (ci <= ri), 1.0, 0.0).astype(F32)
    carry = jnp.zeros((1, x_ref.shape[1]), F32)
    order = range(nblk - 1, -1, -1) if reverse else range(nblk)
    for b in order:
        xb = x_ref[b * blk:(b + 1) * blk, :]
        o_ref[b * blk:(b + 1) * blk, :] = jnp.dot(tri, xb, preferred_element_type=F32,
                                                  precision=lax.Precision.HIGHEST) + carry
        carry = carry + jnp.sum(xb, axis=0, keepdims=True)


def _fgate_fwd(name, zf, bf_row):
    s, w = zf.shape
    blk = _pick(s, (256, 128))

    def body(z_ref, b_ref, c_ref, ls_ref):
        v = z_ref[...] + b_ref[...]
        ls_ref[...] = jnp.minimum(v, 0.0) - jnp.log(1.0 + jnp.exp(-jnp.abs(v)))
        _cumsum_rows(ls_ref, c_ref, blk, reverse=False)

    return pl.pallas_call(
        body, name=name, out_shape=jax.ShapeDtypeStruct((s, w), F32),
        scratch_shapes=[pltpu.VMEM((s, w), F32)], compiler_params=_cparams(),
    )(zf, bf_row)


def _fgate_bwd(name, dc_q, dc_k, zf, bf_row):
    s, w = zf.shape
    blk = _pick(s, (256, 128))

    def body(dcq_ref, dck_ref, z_ref, b_ref, dz_ref, dzb_ref, db_ref, dls_ref, dc_ref):
        acc = dcq_ref[0] + dck_ref[0]
        for pr in range(1, dc_q.shape[0]):
            acc = acc + (dcq_ref[pr] + dck_ref[pr])
        dc_ref[...] = acc
        _cumsum_rows(dc_ref, dls_ref, blk, reverse=True)
        dz = dls_ref[...] * _sig(-(z_ref[...] + b_ref[...]))
        dz_ref[...] = dz
        dzb_ref[...] = dz.astype(BF16)
        db_ref[...] = jnp.sum(dz, axis=0, keepdims=True)

    return pl.pallas_call(
        body, name=name,
        out_shape=[jax.ShapeDtypeStruct((s, w), F32), jax.ShapeDtypeStruct((s, w), BF16),
                   jax.ShapeDtypeStruct((1, w), F32)],
        scratch_shapes=[pltpu.VMEM((s, w), F32), pltpu.VMEM((s, w), F32)], compiler_params=_cparams(),
    )(dc_q, dc_k, zf, bf_row)


def _scores(q, k, cq_col, ck_row, scale, row0, col0):
    s = lax.dot_general(q, k, (((1,), (1,)), ((), ())), preferred_element_type=F32) * scale
    s = s + (cq_col - ck_row)
    rows = row0 + lax.broadcasted_iota(jnp.int32, s.shape, 0)
    cols = col0 + lax.broadcasted_iota(jnp.int32, s.shape, 1)
    return jnp.where(cols <= rows, s, -jnp.inf)


def _attn_fwd(name, q, k, v, c_col, c_row, tb):
    h, s, hd = q.shape
    nb = s // tb
    scale = 1.0 / float(hd) ** 0.5

    def body(q_ref, k_ref, v_ref, cq_ref, ck_ref, o_ref, lse_ref):
        i = pl.program_id(1)
        qv = q_ref[...]
        cq = cq_ref[...]

        def step(j, carry):
            m_i, l_i, acc = carry
            k0 = pl.multiple_of(j * tb, tb)
            sc = _scores(qv, k_ref[pl.ds(k0, tb), :], cq, ck_ref[j], scale, i * tb, j * tb)
            m_new = jnp.maximum(m_i, jnp.max(sc, axis=-1, keepdims=True))
            alpha = jnp.exp(m_i - m_new)
            p = jnp.exp(sc - m_new)
            l_new = alpha * l_i + jnp.sum(p, axis=-1, keepdims=True)
            acc = alpha * acc + jnp.dot(p.astype(BF16), v_ref[pl.ds(k0, tb), :], preferred_element_type=F32)
            return m_new, l_new, acc

        init = (jnp.full((tb, 1), -jnp.inf, F32), jnp.zeros((tb, 1), F32), jnp.zeros((tb, hd), F32))
        m_i, l_i, acc = lax.fori_loop(0, i + 1, step, init)
        o_ref[...] = (acc / l_i).astype(o_ref.dtype)
        lse_ref[...] = m_i + jnp.log(l_i)

    return pl.pallas_call(
        body, name=name, grid=(h, nb),
        in_specs=[pl.BlockSpec((None, tb, hd), lambda hh, i: (hh, i, 0)),
                  pl.BlockSpec((None, s, hd), lambda hh, i: (hh, 0, 0)),
                  pl.BlockSpec((None, s, hd), lambda hh, i: (hh, 0, 0)),
                  pl.BlockSpec((None, tb, 1), lambda hh, i: (hh, i, 0)),
                  pl.BlockSpec((None, nb, 1, tb), lambda hh, i: (hh, 0, 0, 0))],
        out_specs=[pl.BlockSpec((None, tb, hd), lambda hh, i: (hh, i, 0)),
                   pl.BlockSpec((None, tb, 1), lambda hh, i: (hh, i, 0))],
        out_shape=[jax.ShapeDtypeStruct((h, s, hd), F32), jax.ShapeDtypeStruct((h, s, 1), F32)],
        compiler_params=_cparams(),
    )(q, k, v, c_col, c_row)


def _attn_bwd_q(name, q, k, v, o, do, lse, c_col, c_row, tb):
    h, s, hd = q.shape
    nb = s // tb
    scale = 1.0 / float(hd) ** 0.5

    def body(q_ref, k_ref, v_ref, o_ref, do_ref, lse_ref, cq_ref, ck_ref, dq_ref, dl_ref, dcq_ref):
        i = pl.program_id(1)
        qv = q_ref[...]
        dov = do_ref[...]
        cq = cq_ref[...]
        lse_v = lse_ref[...]
        delta = jnp.sum(dov.astype(F32) * o_ref[...], axis=-1, keepdims=True)

        def step(j, carry):
            dq, dcq = carry
            k0 = pl.multiple_of(j * tb, tb)
            kj = k_ref[pl.ds(k0, tb), :]
            p = jnp.exp(_scores(qv, kj, cq, ck_ref[j], scale, i * tb, j * tb) - lse_v)
            dp = lax.dot_general(dov, v_ref[pl.ds(k0, tb), :], (((1,), (1,)), ((), ())), preferred_element_type=F32)
            ds = p * (dp - delta)
            return (dq + jnp.dot(ds.astype(BF16), kj, preferred_element_type=F32),
                    dcq + jnp.sum(ds, axis=-1, keepdims=True))

        dq, dcq = lax.fori_loop(0, i + 1, step, (jnp.zeros((tb, hd), F32), jnp.zeros((tb, 1), F32)))
        dq_ref[...] = (dq * scale).astype(dq_ref.dtype)
        dl_ref[...] = delta
        dcq_ref[...] = dcq

    blk = pl.BlockSpec((None, tb, hd), lambda hh, i: (hh, i, 0))
    full = pl.BlockSpec((None, s, hd), lambda hh, i: (hh, 0, 0))
    col = pl.BlockSpec((None, tb, 1), lambda hh, i: (hh, i, 0))
    return pl.pallas_call(
        body, name=name, grid=(h, nb),
        in_specs=[blk, full, full, blk, blk, col, col,
                  pl.BlockSpec((None, nb, 1, tb), lambda hh, i: (hh, 0, 0, 0))],
        out_specs=[blk, col, col],
        out_shape=[jax.ShapeDtypeStruct((h, s, hd), BF16), jax.ShapeDtypeStruct((h, s, 1), F32),
                   jax.ShapeDtypeStruct((h, s, 1), F32)],
        compiler_params=_cparams(),
    )(q, k, v, o, do, lse, c_col, c_row)


def _attn_bwd_kv(name, q, k, v, do, lse_row, delta_row, c_col, c_row, tb):
    h, s, hd = q.shape
    nb = s // tb
    scale = 1.0 / float(hd) ** 0.5

    def body(q_ref, k_ref, v_ref, do_ref, lse_ref, dl_ref, ck_ref, cq_ref, dk_ref, dv_ref, dc_ref):
        j = pl.program_id(1)
        kv = k_ref[...]
        vv = v_ref[...]
        ck = ck_ref[...]

        def step(i, carry):
            dk, dv, dc = carry
            q0 = pl.multiple_of(i * tb, tb)
            qi = q_ref[pl.ds(q0, tb), :]
            doi = do_ref[pl.ds(q0, tb), :]
            st = lax.dot_general(kv, qi, (((1,), (1,)), ((), ())), preferred_element_type=F32) * scale
            st = st + (cq_ref[i] - ck)
            krow = j * tb + lax.broadcasted_iota(jnp.int32, st.shape, 0)
            qcol = i * tb + lax.broadcasted_iota(jnp.int32, st.shape, 1)
            pt = jnp.exp(jnp.where(krow <= qcol, st, -jnp.inf) - lse_ref[i])
            dv = dv + jnp.dot(pt.astype(BF16), doi, preferred_element_type=F32)
            dpt = lax.dot_general(vv, doi, (((1,), (1,)), ((), ())), preferred_element_type=F32)
            dst = pt * (dpt - dl_ref[i])
            dk = dk + jnp.dot(dst.astype(BF16), qi, preferred_element_type=F32)
            dc = dc - jnp.sum(dst, axis=-1, keepdims=True)
            return dk, dv, dc

        init = (jnp.zeros((tb, hd), F32), jnp.zeros((tb, hd), F32), jnp.zeros((tb, 1), F32))
        dk, dv, dc = lax.fori_loop(j, nb, step, init)
        dk_ref[...] = (dk * scale).astype(dk_ref.dtype)
        dv_ref[...] = dv.astype(dv_ref.dtype)
        dc_ref[...] = dc

    blk = pl.BlockSpec((None, tb, hd), lambda hh, jj: (hh, jj, 0))
    full = pl.BlockSpec((None, s, hd), lambda hh, jj: (hh, 0, 0))
    col = pl.BlockSpec((None, tb, 1), lambda hh, jj: (hh, jj, 0))
    rows = pl.BlockSpec((None, nb, 1, tb), lambda hh, jj: (hh, 0, 0, 0))
    return pl.pallas_call(
        body, name=name, grid=(h, nb),
        in_specs=[full, blk, blk, full, rows, rows, col, rows],
        out_specs=[blk, blk, col],
        out_shape=[jax.ShapeDtypeStruct((h, s, hd), BF16), jax.ShapeDtypeStruct((h, s, hd), BF16),
                   jax.ShapeDtypeStruct((h, s, 1), F32)],
        compiler_params=_cparams(),
    )(q, k, v, do, lse_row, delta_row, c_col, c_row)


def _pair_masks(hd):
    lane = lax.broadcasted_iota(jnp.int32, (1, 2 * hd), 1)
    return lane < hd, lane >= hd


def _only(mask, a):
    return jnp.where(mask, a, jnp.zeros_like(a))


def _on_head_lanes(col0, col1):
    lane = lax.broadcasted_iota(jnp.int32, (1, LANES), 1)
    h0 = 2 * pl.program_id(0)
    return jnp.where(lane == h0, col0, 0.0) + jnp.where(lane == h0 + 1, col1, 0.0)


def _nt(a, b):
    return lax.dot_general(a, b, (((1,), (1,)), ((), ())), preferred_element_type=F32)


def _causal(sc, rows_are_queries):
    r = lax.broadcasted_iota(jnp.int32, sc.shape, 0)
    c = lax.broadcasted_iota(jnp.int32, sc.shape, 1)
    return jnp.where((c <= r) if rows_are_queries else (r <= c), sc, -jnp.inf)


def _pair_specs(dm, s):
    hd = dm.aw // dm.nh
    pw = 2 * hd
    assert pw == LANES and dm.o_q % pw == 0 and dm.aw % pw == 0
    return hd, pw, dm.o_q // pw, (dm.o_q + dm.aw) // pw, (dm.o_q + 2 * dm.aw) // pw


def _attn2_fwd(name, zm, c_col, c_row, dm):
    s, tb = zm.shape[0], dm.tb
    nb = s // tb
    hd, pw, qb, kb, vb = _pair_specs(dm, s)
    scale = 1.0 / float(hd) ** 0.5

    def body(q_ref, k_ref, v_ref, cq_ref, cr_ref, o_ref, ob_ref, lse_ref):
        i = pl.program_id(1)
        masks = _pair_masks(hd)
        q2 = q_ref[...]
        qe = [_only(m, q2) for m in masks]
        cq = [cq_ref[0], cq_ref[1]]

        def block(j, carry, diag):
            k0 = pl.multiple_of(j * tb, tb)
            kj = k_ref[pl.ds(k0, tb), :]
            vj = v_ref[pl.ds(k0, tb), :]
            out = []
            for e in range(2):
                m_i, l_i, acc = carry[e]
                sc = _nt(qe[e], kj) * scale + (cq[e] - cr_ref[e, j])
                if diag:
                    sc = _causal(sc, True)
                m_new = jnp.maximum(m_i, jnp.max(sc, axis=-1, keepdims=True))
                alpha = jnp.exp(m_i - m_new)
                p = jnp.exp(sc - m_new)
                l_new = alpha * l_i + jnp.sum(p, axis=-1, keepdims=True)
                acc = alpha * acc + jnp.dot(p.astype(BF16), vj, preferred_element_type=F32)
                out.append((m_new, l_new, acc))
            return tuple(out)

        one = (jnp.full((tb, 1), -jnp.inf, F32), jnp.zeros((tb, 1), F32), jnp.zeros((tb, pw), F32))
        carry = lax.fori_loop(0, i, lambda j, c: block(j, c, False), (one, one))
        (m0, l0, a0), (m1, l1, a1) = block(i, carry, True)
        o = jnp.where(masks[0], a0 / l0, a1 / l1)
        o_ref[...] = o
        ob_ref[...] = o.astype(BF16)
        lse_ref[0] = m0 + jnp.log(l0)
        lse_ref[1] = m1 + jnp.log(l1)

    blk = lambda cb: pl.BlockSpec((tb, pw), functools.partial(lambda hp, i, cb: (i, cb + hp), cb=cb))
    full = lambda cb: pl.BlockSpec((s, pw), functools.partial(lambda hp, i, cb: (0, cb + hp), cb=cb))
    col = pl.BlockSpec((2, tb, 1), lambda hp, i: (hp, i, 0))
    rows = pl.BlockSpec((2, nb, 1, tb), lambda hp, i: (hp, 0, 0, 0))
    return pl.pallas_call(
        body, name=name, grid=(dm.nh // 2, nb),
        in_specs=[blk(qb), full(kb), full(vb), col, rows],
        out_specs=[blk(0), blk(0), col],
        out_shape=[jax.ShapeDtypeStruct((s, dm.aw), F32), jax.ShapeDtypeStruct((s, dm.aw), BF16),
                   jax.ShapeDtypeStruct((dm.nh, s, 1), F32)],
        compiler_params=_cparams(),
    )(zm, zm, zm, c_col, c_row)


def _attn2_bwd_q(name, zm, o, do, lse, c_col, c_row, dm):
    s, tb = zm.shape[0], dm.tb
    nb = s // tb
    hd, pw, qb, kb, vb = _pair_specs(dm, s)
    scale = 1.0 / float(hd) ** 0.5

    def body(q_ref, k_ref, v_ref, o_ref, do_ref, lse_ref, cq_ref, cr_ref, dq_ref, dl_ref, dcq_ref):
        i = pl.program_id(1)
        masks = _pair_masks(hd)
        q2 = q_ref[...]
        do2 = do_ref[...]
        prod = do2.astype(F32) * o_ref[...]
        qe = [_only(m, q2) for m in masks]
        doe = [_only(m, do2) for m in masks]
        delta = [jnp.sum(_only(m, prod), axis=-1, keepdims=True) for m in masks]
        cq = [cq_ref[0], cq_ref[1]]
        lse_v = [lse_ref[0], lse_ref[1]]

        def block(j, carry, diag):
            k0 = pl.multiple_of(j * tb, tb)
            kj = k_ref[pl.ds(k0, tb), :]
            vj = v_ref[pl.ds(k0, tb), :]
            out = []
            for e in range(2):
                dq, dcq = carry[e]
                sc = _nt(qe[e], kj) * scale + (cq[e] - cr_ref[e, j])
                if diag:
                    sc = _causal(sc, True)
                p = jnp.exp(sc - lse_v[e])
                ds = p * (_nt(doe[e], vj) - delta[e])
                out.append((dq + jnp.dot(ds.astype(BF16), kj, preferred_element_type=F32),
                            dcq + jnp.sum(ds, axis=-1, keepdims=True)))
            return tuple(out)

        one = (jnp.zeros((tb, pw), F32), jnp.zeros((tb, 1), F32))
        carry = lax.fori_loop(0, i, lambda j, c: block(j, c, False), (one, one))
        (dq0, dc0), (dq1, dc1) = block(i, carry, True)
        dq_ref[...] = (jnp.where(masks[0], dq0, dq1) * scale).astype(dq_ref.dtype)
        dl_ref[0] = delta[0]
        dl_ref[1] = delta[1]
        dcq_ref[...] = _on_head_lanes(dc0, dc1)

    blk = lambda cb: pl.BlockSpec((tb, pw), functools.partial(lambda hp, i, cb: (i, cb + hp), cb=cb))
    full = lambda cb: pl.BlockSpec((s, pw), functools.partial(lambda hp, i, cb: (0, cb + hp), cb=cb))
    col = pl.BlockSpec((2, tb, 1), lambda hp, i: (hp, i, 0))
    rows = pl.BlockSpec((2, nb, 1, tb), lambda hp, i: (hp, 0, 0, 0))
    return pl.pallas_call(
        body, name=name, grid=(dm.nh // 2, nb),
        in_specs=[blk(qb), full(kb), full(vb), blk(0), blk(0), col, col, rows],
        out_specs=[blk(0), col, pl.BlockSpec((None, tb, LANES), lambda hp, i: (hp, i, 0))],
        out_shape=[jax.ShapeDtypeStruct((s, dm.aw), BF16), jax.ShapeDtypeStruct((dm.nh, s, 1), F32),
                   jax.ShapeDtypeStruct((dm.nh // 2, s, LANES), F32)],
        compiler_params=_cparams(),
    )(zm, zm, zm, o, do, lse, c_col, c_row)


def _attn2_bwd_kv(name, zm, do, lse_row, delta_row, c_col, c_row, dm):
    s, tb = zm.shape[0], dm.tb
    nb = s // tb
    hd, pw, qb, kb, vb = _pair_specs(dm, s)
    scale = 1.0 / float(hd) ** 0.5

    def body(q_ref, k_ref, v_ref, do_ref, lse_ref, dl_ref, ck_ref, cr_ref, dk_ref, dv_ref, dc_ref):
        j = pl.program_id(1)
        masks = _pair_masks(hd)
        k2 = k_ref[...]
        v2 = v_ref[...]
        ke = [_only(m, k2) for m in masks]
        ve = [_only(m, v2) for m in masks]
        ck = [ck_ref[0], ck_ref[1]]

        def block(i, carry, diag):
            q0 = pl.multiple_of(i * tb, tb)
            qi = q_ref[pl.ds(q0, tb), :]
            doi = do_ref[pl.ds(q0, tb), :]
            out = []
            for e in range(2):
                dk, dv, dc = carry[e]
                st = _nt(ke[e], qi) * scale + (cr_ref[e, i] - ck[e])
                if diag:
                    st = _causal(st, False)
                pt = jnp.exp(st - lse_ref[e, i])
                dv = dv + jnp.dot(pt.astype(BF16), doi, preferred_element_type=F32)
                dst = pt * (_nt(ve[e], doi) - dl_ref[e, i])
                dk = dk + jnp.dot(dst.astype(BF16), qi, preferred_element_type=F32)
                out.append((dk, dv, dc - jnp.sum(dst, axis=-1, keepdims=True)))
            return tuple(out)

        one = (jnp.zeros((tb, pw), F32), jnp.zeros((tb, pw), F32), jnp.zeros((tb, 1), F32))
        carry = block(j, (one, one), True)
        (dk0, dv0, dc0), (dk1, dv1, dc1) = lax.fori_loop(j + 1, nb, lambda i, c: block(i, c, False), carry)
        dk_ref[...] = (jnp.where(masks[0], dk0, dk1) * scale).astype(dk_ref.dtype)
        dv_ref[...] = jnp.where(masks[0], dv0, dv1).astype(dv_ref.dtype)
        dc_ref[...] = _on_head_lanes(dc0, dc1)

    blk = lambda cb: pl.BlockSpec((tb, pw), functools.partial(lambda hp, jj, cb: (jj, cb + hp), cb=cb))
    full = lambda cb: pl.BlockSpec((s, pw), functools.partial(lambda hp, jj, cb: (0, cb + hp), cb=cb))
    col = pl.BlockSpec((2, tb, 1), lambda hp, jj: (hp, jj, 0))
    rows = pl.BlockSpec((2, nb, 1, tb), lambda hp, jj: (hp, 0, 0, 0))
    return pl.pallas_call(
        body, name=name, grid=(dm.nh // 2, nb),
        in_specs=[full(qb), blk(kb), blk(vb), full(0), rows, rows, col, rows],
        out_specs=[blk(0), blk(0), pl.BlockSpec((None, tb, LANES), lambda hp, jj: (hp, jj, 0))],
        out_shape=[jax.ShapeDtypeStruct((s, dm.aw), BF16), jax.ShapeDtypeStruct((s, dm.aw), BF16),
                   jax.ShapeDtypeStruct((dm.nh // 2, s, LANES), F32)],
        compiler_params=_cparams(),
    )(zm, zm, zm, do, lse_row, delta_row, c_col, c_row)


def _glu(cv, cc):
    c1 = cv[:, :cc].astype(F32)
    c2 = cv[:, cc:].astype(F32)
    return c1 * _sig(c2)


def _conv_fwd(name, zm, w_pad, b_row, g_row, cc, taps):
    s = zm.shape[0]
    tr = _pick(s, (256, 128))
    hpb = tr // HALO
    off = HALO - (taps - 1)

    def body(cur_ref, halo_ref, w_ref, b_ref, g_ref, y_ref, cs_ref, apad):
        i = pl.program_id(0)
        apad[0:HALO, :] = _glu(halo_ref[...], cc) * jnp.where(i > 0, 1.0, 0.0)
        apad[HALO:, :] = _glu(cur_ref[...], cc)
        acc = jnp.zeros((tr, cc), F32) + b_ref[...]
        for t in range(taps):
            acc = acc + w_ref[t:t + 1, :] * apad[off + t:off + t + tr, :]
        y_ref[...] = acc
        r = lax.rsqrt(jnp.mean(acc * acc, axis=-1, keepdims=True) + RMS_EPS)
        n = acc * r * g_ref[...]
        cs_ref[...] = (n * _sig(n)).astype(cs_ref.dtype)

    return pl.pallas_call(
        body, name=name, grid=(s // tr,),
        in_specs=[pl.BlockSpec((tr, 2 * cc), lambda i: (i, 0)),
                  pl.BlockSpec((HALO, 2 * cc), lambda i: (jnp.maximum(i * hpb - 1, 0), 0)),
                  pl.BlockSpec(w_pad.shape, lambda i: (0, 0)),
                  pl.BlockSpec(b_row.shape, lambda i: (0, 0)),
                  pl.BlockSpec(g_row.shape, lambda i: (0, 0))],
        out_specs=[pl.BlockSpec((tr, cc), lambda i: (i, 0)), pl.BlockSpec((tr, cc), lambda i: (i, 0))],
        out_shape=[jax.ShapeDtypeStruct((s, cc), F32), jax.ShapeDtypeStruct((s, cc), BF16)],
        scratch_shapes=[pltpu.VMEM((HALO + tr, cc), F32)], compiler_params=_cparams(),
    )(zm, zm, w_pad, b_row, g_row)


def _conv_bwd(name, zm, y, dcs, w_pad, g_row, cc, taps):
    s = zm.shape[0]
    tr = _pick(s, (256, 128))
    hpb = tr // HALO
    nblk = s // tr
    off = HALO - (taps - 1)

    def dy_of(yv, dcsv, g):
        r = lax.rsqrt(jnp.mean(yv * yv, axis=-1, keepdims=True) + RMS_EPS)
        xh = yv * r
        n = xh * g
        sg = _sig(n)
        dn = dcsv.astype(F32) * (sg * (1.0 + n * (1.0 - sg)))
        dxh = dn * g
        dy = r * (dxh - xh * jnp.mean(dxh * xh, axis=-1, keepdims=True))
        return dy, dn * xh

    def body(cur_ref, halo_ref, y_ref, yn_ref, dcs_ref, dcsn_ref, w_ref, g_ref,
             dcv_ref, dw_ref, db_ref, dg_ref, apad, dypad):
        i = pl.program_id(0)

        @pl.when(i == 0)
        def _():
            dw_ref[...] = jnp.zeros(dw_ref.shape, F32)
            db_ref[...] = jnp.zeros(db_ref.shape, F32)
            dg_ref[...] = jnp.zeros(dg_ref.shape, F32)

        g = g_ref[...]
        apad[0:HALO, :] = _glu(halo_ref[...], cc) * jnp.where(i > 0, 1.0, 0.0)
        apad[HALO:, :] = _glu(cur_ref[...], cc)
        dy, dgt = dy_of(y_ref[...], dcs_ref[...], g)
        dyn, _ = dy_of(yn_ref[...], dcsn_ref[...], g)
        dypad[0:tr, :] = dy
        dypad[tr:, :] = dyn * jnp.where(i < nblk - 1, 1.0, 0.0)
        db_ref[...] += jnp.sum(dy, axis=0, keepdims=True)
        dg_ref[...] += jnp.sum(dgt, axis=0, keepdims=True)
        da = jnp.zeros((tr, cc), F32)
        for t in range(taps):
            da = da + w_ref[t:t + 1, :] * dypad[taps - 1 - t:taps - 1 - t + tr, :]
            dw_ref[t:t + 1, :] += jnp.sum(dy * apad[off + t:off + t + tr, :], axis=0, keepdims=True)
        cv = cur_ref[...]
        c1 = cv[:, :cc].astype(F32)
        sg = _sig(cv[:, cc:].astype(F32))
        dcv_ref[:, :cc] = (da * sg).astype(dcv_ref.dtype)
        dcv_ref[:, cc:] = (da * c1 * sg * (1.0 - sg)).astype(dcv_ref.dtype)

    nxt = lambda i: (jnp.minimum((i + 1) * hpb, s // HALO - 1), 0)
    return pl.pallas_call(
        body, name=name, grid=(nblk,),
        in_specs=[pl.BlockSpec((tr, 2 * cc), lambda i: (i, 0)),
                  pl.BlockSpec((HALO, 2 * cc), lambda i: (jnp.maximum(i * hpb - 1, 0), 0)),
                  pl.BlockSpec((tr, cc), lambda i: (i, 0)), pl.BlockSpec((HALO, cc), nxt),
                  pl.BlockSpec((tr, cc), lambda i: (i, 0)), pl.BlockSpec((HALO, cc), nxt),
                  pl.BlockSpec(w_pad.shape, lambda i: (0, 0)), pl.BlockSpec(g_row.shape, lambda i: (0, 0))],
        out_specs=[pl.BlockSpec((tr, 2 * cc), lambda i: (i, 0)),
                   pl.BlockSpec(w_pad.shape, lambda i: (0, 0)),
                   pl.BlockSpec((1, cc), lambda i: (0, 0)), pl.BlockSpec((1, cc), lambda i: (0, 0))],
        out_shape=[jax.ShapeDtypeStruct((s, 2 * cc), BF16), jax.ShapeDtypeStruct(w_pad.shape, F32),
                   jax.ShapeDtypeStruct((1, cc), F32), jax.ShapeDtypeStruct((1, cc), F32)],
        scratch_shapes=[pltpu.VMEM((HALO + tr, cc), F32), pltpu.VMEM((tr + HALO, cc), F32)],
        compiler_params=_cparams(),
    )(zm, zm, y, y, dcs, dcs, w_pad, g_row)


def _place():
    x, y, c = lax.axis_index("x"), lax.axis_index("y"), lax.axis_index("c")
    chips = [(1 - x, y), (x, 1 - y), (1 - x, 1 - y)]
    return x, y, c, chips


def _half(c, rows):
    rh = rows // 2
    return pl.ds(pl.multiple_of(c * rh, 16), rh)


def _remote(src, dst, send, recv, dev):
    return pltpu.make_async_remote_copy(src_ref=src, dst_ref=dst, send_sem=send, recv_sem=recv,
                                        device_id=dev, device_id_type=MESH)


def _cast_slots(name, w, chip_arr, deps=()):
    nl, r, cdim = w.shape
    tr = _half_rows(r)
    n = r // tr
    n_d = len(deps)

    def body(q_ref, w_ref, *rest):
        o_refs = rest[n_d:]
        layer = pl.program_id(0)
        for j, o_ref in enumerate(o_refs):
            @pl.when(layer == j)
            def _(o_ref=o_ref):
                o_ref[...] = w_ref[...].astype(o_ref.dtype)

    def out_map(j):
        return lambda l, i, q: (q[0], jnp.where(l < j, 0, jnp.where(l == j, i, n - 1)), 0)

    return pl.pallas_call(
        body, name=name,
        grid_spec=pltpu.PrefetchScalarGridSpec(
            num_scalar_prefetch=1, grid=(nl, n),
            in_specs=[pl.BlockSpec((None, tr, cdim), lambda l, i, q: (l, i, 0))] + [ANY] * n_d,
            out_specs=[pl.BlockSpec((None, tr, cdim), out_map(j)) for j in range(nl)]),
        out_shape=[jax.ShapeDtypeStruct((N_CHIPS, r, cdim), BF16)] * nl, compiler_params=_cparams(),
    )(chip_arr, w, *deps)


def _split_start(name, n_sems, bufs, issue, after=()):
    nb = len(bufs)
    n_in = nb + len(after)

    def body(*refs):
        issue(refs[:nb], refs[n_in], refs[n_in + 1])
        refs[-1][...] = jnp.zeros(refs[-1].shape, F32)

    outs = pl.pallas_call(
        body, name=name, in_specs=[HBM_SPEC] * nb + [ANY] * len(after),
        out_shape=(pltpu.SemaphoreType.DMA(n_sems), pltpu.SemaphoreType.DMA(n_sems),
                   *[pltpu.HBM(b.shape, b.dtype) for b in bufs], jax.ShapeDtypeStruct((8, LANES), F32)),
        out_specs=(SEM_SPEC, SEM_SPEC, *[HBM_SPEC] * nb, pl.BlockSpec(memory_space=pltpu.VMEM)),
        input_output_aliases={t: t + 2 for t in range(nb)},
        compiler_params=pltpu.CompilerParams(has_side_effects=EFFECT),
    )(*[pltpu.with_memory_space_constraint(b, pltpu.HBM) for b in bufs], *after)
    return outs[0], outs[1], list(outs[2:2 + nb]), outs[-1]


def _split_wait(name, bufs, send, recv, after, drain):
    nb = len(bufs)

    def body(*refs):
        drain(refs[:nb], refs[nb], refs[nb + 1])

    return list(pl.pallas_call(
        body, name=name, in_specs=[HBM_SPEC] * nb + [SEM_SPEC, SEM_SPEC, ANY],
        out_shape=tuple(pltpu.HBM(b.shape, b.dtype) for b in bufs), out_specs=tuple([HBM_SPEC] * nb),
        input_output_aliases={t: t for t in range(nb)},
        compiler_params=pltpu.CompilerParams(has_side_effects=EFFECT),
    )(*bufs, send, recv, after))


def _gather_start(name, bufs, after=()):
    nt = len(bufs)

    def issue(g, send, recv):
        x, y, c, chips = _place()
        me = 2 * x + y
        for t in range(nt):
            part = g[t].at[me, _half(c, bufs[t].shape[1]), :]
            for j, (qx, qy) in enumerate(chips):
                _remote(part, part, send.at[3 * t + j], recv.at[3 * t + j], (qx, qy, c)).start()

    return _split_start(name, (3 * nt,), bufs, issue, after)


def _gather_wait(name, bufs, send, recv, after):
    nt = len(bufs)

    def drain(g, send, recv):
        x, y, c, _ = _place()
        for t in range(nt):
            part = g[t].at[0, pl.ds(0, bufs[t].shape[1] // 2), :]
            for j in range(3):
                cp = _remote(part, part, send.at[3 * t + j], recv.at[3 * t + j], (x, y, c))
                cp.wait_send()
                cp.wait_recv()

    return _split_wait(name, bufs, send, recv, after, drain)


def _gather_forward(name, bufs):
    nt = len(bufs)

    def body(*refs):
        g = refs[nt:2 * nt]
        send, recv = refs[2 * nt:]
        x, y, c, chips = _place()
        cps = []
        for t in range(nt):
            for j, (qx, qy) in enumerate(chips):
                part = g[t].at[2 * qx + qy, _half(c, bufs[t].shape[1]), :]
                cps.append(_remote(part, part, send.at[t, j], recv.at[t, j], (x, y, 1 - c)))
        for cp in cps:
            cp.start()
        for t in range(nt):
            for j, (qx, qy) in enumerate(chips):
                theirs = g[t].at[2 * qx + qy, _half(1 - c, bufs[t].shape[1]), :]
                _remote(theirs, theirs, send.at[t, j], recv.at[t, j], (x, y, 1 - c)).wait_recv()
        for cp in cps:
            cp.wait_send()

    return list(pl.pallas_call(
        body, name=name, in_specs=[ANY] * nt, out_specs=[ANY] * nt,
        out_shape=[jax.ShapeDtypeStruct(b.shape, b.dtype) for b in bufs],
        input_output_aliases={t: t for t in range(nt)},
        scratch_shapes=[pltpu.SemaphoreType.DMA((nt, 3)), pltpu.SemaphoreType.DMA((nt, 3))],
        compiler_params=pltpu.CompilerParams(has_side_effects=True),
    )(*bufs))


def _swap_halves(name, grads):
    nt = len(grads)

    def body(*refs):
        g_refs, r_refs = refs[:nt], refs[nt:2 * nt]
        send, recv = refs[2 * nt:]
        x, y, c, _ = _place()
        cps = [_remote(g_refs[t].at[:, _half(1 - c, grads[t].shape[1]), :], r_refs[t], send.at[t], recv.at[t],
                       (x, y, 1 - c)) for t in range(nt)]
        for cp in cps:
            cp.start()
        for cp in cps:
            cp.wait()

    return pl.pallas_call(
        body, name=name, in_specs=[ANY] * nt, out_specs=[ANY] * nt,
        out_shape=[jax.ShapeDtypeStruct((N_CHIPS, g.shape[1] // 2, g.shape[2]), g.dtype) for g in grads],
        scratch_shapes=[pltpu.SemaphoreType.DMA((nt,)), pltpu.SemaphoreType.DMA((nt,))],
        compiler_params=pltpu.CompilerParams(has_side_effects=True),
    )(*grads)


def _scatter_start(name, parts, after=()):
    nt = len(parts)
    lands = [lax.empty((3,) + p.shape[1:], p.dtype) for p in parts]

    def issue(refs, send, recv):
        x, y, c, chips = _place()
        for t in range(nt):
            for j, (qx, qy) in enumerate(chips):
                _remote(refs[t].at[2 * qx + qy], refs[nt + t].at[j], send.at[3 * t + j], recv.at[3 * t + j],
                        (qx, qy, c)).start()

    return _split_start(name, (3 * nt,), list(parts) + lands, issue, after)


def _scatter_wait(name, bufs, send, recv, after):
    nt = len(bufs) // 2

    def drain(refs, send, recv):
        x, y, c, _ = _place()
        for t in range(nt):
            for j in range(3):
                cp = _remote(refs[t].at[0], refs[nt + t].at[j], send.at[3 * t + j], recv.at[3 * t + j], (x, y, c))
                cp.wait_send()
                cp.wait_recv()

    return _split_wait(name, bufs, send, recv, after, drain)


def _join_halves(name, fulls):
    nt = len(fulls)

    def body(*refs):
        o_refs = refs[nt:2 * nt]
        send, recv = refs[2 * nt:]
        x, y, c, _ = _place()
        cps = []
        for t in range(nt):
            half = o_refs[t].at[_half(c, fulls[t].shape[0]), :]
            cps.append(_remote(half, half, send.at[t], recv.at[t], (x, y, 1 - c)))
        for cp in cps:
            cp.start()
        for t in range(nt):
            theirs = o_refs[t].at[_half(1 - c, fulls[t].shape[0]), :]
            _remote(theirs, theirs, send.at[t], recv.at[t], (x, y, 1 - c)).wait_recv()
        for cp in cps:
            cp.wait_send()

    return list(pl.pallas_call(
        body, name=name, in_specs=[ANY] * nt, out_specs=[ANY] * nt,
        out_shape=[jax.ShapeDtypeStruct(a.shape, a.dtype) for a in fulls],
        input_output_aliases={t: t for t in range(nt)},
        scratch_shapes=[pltpu.SemaphoreType.DMA((nt,)), pltpu.SemaphoreType.DMA((nt,))],
        compiler_params=pltpu.CompilerParams(has_side_effects=True),
    )(*fulls))


def _add_own_half(name, grad, recv_half, c_arr):
    _, r, cdim = grad.shape
    rh = r // 2
    tr = _half_rows(rh)
    nrb = rh // tr

    def body(c_ref, g_ref, r_ref, o_ref):
        o_ref[...] = (g_ref[...].astype(F32) + r_ref[...].astype(F32)).astype(o_ref.dtype)

    return pl.pallas_call(
        body, name=name,
        grid_spec=pltpu.PrefetchScalarGridSpec(
            num_scalar_prefetch=1, grid=(N_CHIPS, nrb),
            in_specs=[pl.BlockSpec((None, tr, cdim), lambda q, i, cr: (q, cr[0] * nrb + i, 0)),
                      pl.BlockSpec((None, tr, cdim), lambda q, i, cr: (q, i, 0))],
            out_specs=pl.BlockSpec((None, tr, cdim), lambda q, i, cr: (q, i, 0))),
        out_shape=jax.ShapeDtypeStruct((N_CHIPS, rh, cdim), BF16), compiler_params=_cparams(),
    )(c_arr, grad, recv_half)


def _add_chips(name, part, came, place_arr):
    _, rh, cdim = part.shape
    tr = _half_rows(rh)
    nrb = rh // tr

    def body(q_ref, p_ref, r_ref, o_ref):
        acc = p_ref[...].astype(F32)
        for j in range(3):
            acc = acc + r_ref[j].astype(F32)
        o_ref[...] = acc

    return pl.pallas_call(
        body, name=name,
        grid_spec=pltpu.PrefetchScalarGridSpec(
            num_scalar_prefetch=1, grid=(nrb,),
            in_specs=[pl.BlockSpec((None, tr, cdim), lambda i, qr: (qr[0], i, 0)),
                      pl.BlockSpec((3, tr, cdim), lambda i, qr: (0, i, 0))],
            out_specs=pl.BlockSpec((tr, cdim), lambda i, qr: (qr[1] * nrb + i, 0))),
        out_shape=jax.ShapeDtypeStruct((2 * rh, cdim), F32), compiler_params=_cparams(),
    )(place_arr, part, came)


def _reduce_scatter_begin(tag, grads, c_arr, after=()):
    nt = len(grads)
    got = _swap_halves(f"rs_swap_{tag}", grads)
    parts = [_add_own_half(f"rs_add2_{tag}_{t}", grads[t], got[t], c_arr) for t in range(nt)]
    return _scatter_start(f"rs_scatter_start_{tag}", parts, after)


def _reduce_scatter_end(tag, state, after, place_arr):
    send, recv, bufs, _ = state
    nt = len(bufs) // 2
    bufs = _scatter_wait(f"rs_scatter_wait_{tag}", bufs, send, recv, after)
    fulls = [_add_chips(f"rs_add4_{tag}_{t}", bufs[t], bufs[nt + t], place_arr) for t in range(nt)]
    return _join_halves(f"rs_join_{tag}", fulls)


def _allgather_blocks(name, blk):
    m_per, n = blk.shape

    def body(x_ref, out_ref, send_sems, recv_sems, local_sem):
        x, y, c, chips = _place()
        me, sibling = (x, y, c), (x, y, 1 - c)

        def rows(px, py, pc):
            return out_ref.at[pl.ds(pl.multiple_of((4 * px + 2 * py + pc) * m_per, 8), m_per), :]

        def copy(k, block, to, src=None):
            return _remote(rows(*block) if src is None else src, rows(*block), send_sems.at[k], recv_sems.at[k], to)

        mine = pltpu.make_async_copy(x_ref, rows(*me), local_sem)
        mine.start()
        first = [copy(0, me, sibling, src=x_ref)]
        first += [copy(1 + j, me, (*chip, c), src=x_ref) for j, chip in enumerate(chips)]
        for cp in first:
            cp.start()
        passed = [copy(4 + j, (*chip, c), sibling) for j, chip in enumerate(chips)]
        for j, chip in enumerate(chips):
            copy(1 + j, (*chip, c), me).wait_recv()
            passed[j].start()
        copy(0, sibling, me).wait_recv()
        for j, chip in enumerate(chips):
            copy(4 + j, (*chip, 1 - c), me).wait_recv()
        for cp in first + passed:
            cp.wait_send()
        mine.wait()

    return pl.pallas_call(
        body, name=name, out_shape=jax.ShapeDtypeStruct((N_DEV * m_per, n), blk.dtype),
        in_specs=[pl.BlockSpec(memory_space=pltpu.VMEM)], out_specs=pl.BlockSpec(memory_space=pltpu.VMEM),
        scratch_shapes=[pltpu.SemaphoreType.DMA((7,)), pltpu.SemaphoreType.DMA((7,)), pltpu.SemaphoreType.DMA],
        compiler_params=_cparams(),
    )(blk)


def _sum_blocks(name, stacked):
    nd, m, n = stacked.shape
    tr = _pick(m, (336, 256, 168, 128, 64, 32, 16, 8))

    def body(s_ref, o_ref):
        acc = s_ref[0]
        for d in range(1, nd):
            acc = acc + s_ref[d]
        o_ref[...] = acc

    return pl.pallas_call(
        body, name=name, grid=(m // tr,),
        in_specs=[pl.BlockSpec((nd, tr, n), lambda i: (0, i, 0))],
        out_specs=pl.BlockSpec((tr, n), lambda i: (i, 0)),
        out_shape=jax.ShapeDtypeStruct((m, n), F32), compiler_params=_cparams(),
    )(stacked)


def _adamw_vals(w, g, m, v):
    m2 = ADAM_B1 * m + (1.0 - ADAM_B1) * g
    v2 = ADAM_B2 * v + (1.0 - ADAM_B2) * (g * g)
    m_hat = m2 / (1.0 - ADAM_B1 ** ADAM_STEP)
    v_hat = v2 / (1.0 - ADAM_B2 ** ADAM_STEP)
    delta = -ADAM_LR * (m_hat / (jnp.sqrt(v_hat) + ADAM_EPS) + ADAM_WD * w)
    return delta, m2, v2


def _adamw_layer(name, layer, w, m, v, g, stacked, deps=()):
    nl, r, cdim = w.shape
    tr = _pick(r, (128, 64, 32, 16, 8))
    n_d = len(deps)

    def body(w_ref, m_ref, v_ref, g_ref, *rest):
        go_ref, d_ref, mo_ref, vo_ref = rest[4 + n_d:]
        gv = g_ref[...]
        delta, m2, v2 = _adamw_vals(w_ref[...], gv, m_ref[...], v_ref[...])
        go_ref[...] = gv
        d_ref[...] = delta
        mo_ref[...] = m2
        vo_ref[...] = v2

    big = pl.BlockSpec((None, tr, cdim), lambda i: (layer, i, 0))
    return pl.pallas_call(
        body, name=name, grid=(r // tr,),
        in_specs=[big, big, big, pl.BlockSpec((tr, cdim), lambda i: (i, 0))] + [ANY] * (4 + n_d),
        out_specs=[big, big, big, big], out_shape=[jax.ShapeDtypeStruct(w.shape, F32)] * 4,
        input_output_aliases={4 + k: k for k in range(4)}, compiler_params=_cparams(),
    )(w, m, v, g, *stacked, *deps)


def _adamw_small(name, w, g, m, v):
    def body(w_ref, g_ref, m_ref, v_ref, d_ref, mo_ref, vo_ref):
        delta, m2, v2 = _adamw_vals(w_ref[...], g_ref[...], m_ref[...], v_ref[...])
        d_ref[...] = delta
        mo_ref[...] = m2
        vo_ref[...] = v2

    return pl.pallas_call(body, name=name, out_shape=[jax.ShapeDtypeStruct(w.shape, F32)] * 3,
                          compiler_params=_cparams())(w, g, m, v)


def _swiglu_fwd(name, z, f):
    def fn(zz):
        a = zz[:, :f].astype(F32)
        b = zz[:, f:].astype(F32)
        return a * _sig(a) * b
    return _rowwise(name, fn, [z], [(f, BF16)])[0]


def _swiglu_bwd(name, z, ds, f, deps=()):
    def fn(zz, dd):
        a = zz[:, :f].astype(F32)
        b = zz[:, f:].astype(F32)
        d = dd.astype(F32)
        sg = _sig(a)
        da = d * b * (sg * (1.0 + a * (1.0 - sg)))
        db = d * a * sg
        return jnp.concatenate([da, db], axis=1)
    return _rowwise(name, fn, [z, ds], [(2 * f, BF16)], deps=deps)[0]


def _heads(a, nh):
    s, w = a.shape
    return jnp.transpose(a.reshape(s, nh, w // nh), (1, 0, 2))


def _unheads(a):
    nh, s, hd = a.shape
    return jnp.transpose(a, (1, 0, 2)).reshape(s, nh * hd)


def _as_rows(col, tb):
    nh, s, _ = col.shape
    return col.reshape(nh, s // tb, 1, tb)


class _Dims:
    def __init__(self, d, f, aw, nh, cc, taps, dp, s):
        self.d, self.f, self.aw, self.nh, self.cc, self.taps, self.dp, self.s = d, f, aw, nh, cc, taps, dp, s
        self.o_cv, self.o_ga, self.o_gc = 0, 2 * cc, 2 * cc + d
        self.o_q = 2 * cc + 2 * d
        self.n_main = self.o_q + 3 * aw
        self.tb = _pick(s, (256, 128))
        assert self.o_ga % d == 0 and self.o_q % aw == 0 and nh <= LANES


def _ffn_fwd(tag, h, g_row, w_in, w_out, dm, deps=()):
    n = _rms_fwd(f"{tag}_rms", h, g_row, deps)
    z = _mm_nn(f"{tag}_in", n, w_in, BF16)
    s = _swiglu_fwd(f"{tag}_act", z, dm.f)
    h2 = _mm_nn(f"{tag}_out", s, w_out, F32, res=h, scale=FFN_RES)
    return h2, (h, n, z, s)


def _ffn_bwd(tag, dh, dh_half_b, saved, g_row, w_in, w_out, dm, oscale, deps=()):
    h, n, z, s = saved
    ds = _mm_nt(f"{tag}_dout", dh_half_b, w_out, BF16)
    dz = _swiglu_bwd(f"{tag}_dact", z, ds, dm.f, deps)
    dw_out = _mm_tn(f"{tag}_wout", s, dh_half_b, 1)
    dw_in = _mm_tn(f"{tag}_win", n, dz, N_CHIPS)
    dn = _mm_nt(f"{tag}_din", dz, w_in, BF16)
    dh0, dh0_b, dg = _rms_bwd_res(f"{tag}_drms", h, g_row, [dn], dh, oscale)
    return dh0, dh0_b, dg, dw_in, dw_out


def _mixer_fwd(tag, h, sm, wt, dm):
    d, cc, aw, nh, tb = dm.d, dm.cc, dm.aw, dm.nh, dm.tb
    u = _rms_fwd(f"{tag}_rms", h, sm["g_mix"])
    zm = _mm_nn(f"{tag}_in", u, wt["w_main"], BF16)
    zf = _mm_nn(f"{tag}_inf", u, wt["w_f"], F32)
    c = _fgate_fwd(f"{tag}_fgate", zf, sm["b_f"])
    c_col = jnp.transpose(c[:, :nh], (1, 0))[:, :, None]
    c_row = _as_rows(c_col, tb)
    o32, o, lse = _attn2_fwd(f"{tag}_attn", zm, c_col, c_row, dm)
    ya = _mm_nn(f"{tag}_aout", o, wt["w_attn_out"], BF16)
    y, cs = _conv_fwd(f"{tag}_conv", zm, sm["conv_w"], sm["conv_b"], sm["g_conv"], cc, dm.taps)
    yc = _mm_nn(f"{tag}_cout", cs, wt["w_conv_out"], BF16)

    def merge(ga, gc, a, b):
        return _sig(ga.astype(F32)) * a.astype(F32) + _sig(gc.astype(F32)) * b.astype(F32)

    mg = _rowwise(f"{tag}_merge", merge, [(zm, dm.o_ga // d, d), (zm, dm.o_gc // d, d), ya, yc], [(d, BF16)])[0]
    h2 = _mm_nn(f"{tag}_out", mg, wt["w_out"], F32, res=h, scale=1.0)
    return h2, (h, u, zm, zf, c_col, c_row, o32, lse, o, ya, y, cs, yc, mg)


def _mixer_bwd(tag, dh, dh_b, saved, sm, wt, dm, deps=()):
    d, cc, aw, nh, tb = dm.d, dm.cc, dm.aw, dm.nh, dm.tb
    h, u, zm, zf, c_col, c_row, o32, lse, o, ya, y, cs, yc, mg = saved
    dmg = _mm_nt(f"{tag}_dout", dh_b, wt["w_out"], BF16)
    dw_out = _mm_tn(f"{tag}_wout", mg, dh_b, 1)

    def unmerge(dd, ga, gc, a, b):
        dd, a, b = dd.astype(F32), a.astype(F32), b.astype(F32)
        sa, sc = _sig(ga.astype(F32)), _sig(gc.astype(F32))
        return dd * sa, dd * sc, dd * a * sa * (1.0 - sa), dd * b * sc * (1.0 - sc)

    dya, dyc, dga, dgc = _rowwise(f"{tag}_dmerge", unmerge,
                                  [dmg, (zm, dm.o_ga // d, d), (zm, dm.o_gc // d, d), ya, yc], [(d, BF16)] * 4,
                                  deps=deps)
    dw_a = _mm_tn(f"{tag}_waout", o, dya, N_CHIPS)
    do = _mm_nt(f"{tag}_daout", dya, wt["w_attn_out"], BF16)
    dw_c = _mm_tn(f"{tag}_wcout", cs, dyc, N_CHIPS)
    dcs = _mm_nt(f"{tag}_dcout", dyc, wt["w_conv_out"], BF16)
    dcv, dconv_w, dconv_b, dg_conv = _conv_bwd(f"{tag}_dconv", zm, y, dcs, sm["conv_w"], sm["g_conv"], cc, dm.taps)
    dq, delta, dcq_h = _attn2_bwd_q(f"{tag}_dattn_q", zm, o32, do, lse, c_col, c_row, dm)
    dk, dv, dck_h = _attn2_bwd_kv(f"{tag}_dattn_kv", zm, do, _as_rows(lse, tb), _as_rows(delta, tb),
                                  c_col, c_row, dm)
    dzf, dzf_b, db_f = _fgate_bwd(f"{tag}_dfgate", dcq_h, dck_h, zf, sm["b_f"])
    dzm = jnp.concatenate([dcv, dga, dgc, dq, dk, dv], axis=1)
    dw_main = _mm_tn(f"{tag}_win", u, dzm, 1)
    dw_f = _mm_tn(f"{tag}_winf", u, dzf_b, 1)
    du = _mm_nt(f"{tag}_din", dzm, wt["w_main"], BF16)
    du_f = _mm_nt(f"{tag}_dinf", dzf_b, wt["w_f"], BF16)
    dh0, dh0_b, dg_mix = _rms_bwd_res(f"{tag}_drms", h, sm["g_mix"], [du, du_f], dh, FFN_RES)
    small = dict(g_mix=dg_mix, b_f=db_f[:, :nh], conv_w=dconv_w[:dm.taps], conv_b=dconv_b, g_conv=dg_conv)
    return dh0, dh0_b, small, dw_main, dw_f, dw_a, dw_c, dw_out


def _ple_fwd(tag, h, p_b, sm, wt, dm):
    n = _rms_fwd(f"{tag}_rms", h, sm["g_ple"])
    gp = _mm_nn(f"{tag}_gate", n, wt["w_ple_gate"], BF16)
    pp = _mm_nn(f"{tag}_proj", p_b, wt["w_ple_proj"], BF16)

    def fn(hh, a, b):
        return hh + _sig(a.astype(F32)) * b.astype(F32)

    h2 = _rowwise(f"{tag}_mix", fn, [h, gp, pp], [(dm.d, F32)])[0]
    return h2, (h, n, gp, pp)


def _ple_bwd(tag, dh, saved, p_b, sm, wt, dm, deps=()):
    h, n, gp, pp = saved

    def fn(dd, a, b):
        gate = _sig(a.astype(F32))
        b = b.astype(F32)
        return dd * b * gate * (1.0 - gate), dd * gate

    dgp, dpp = _rowwise(f"{tag}_dmix", fn, [dh, gp, pp], [(dm.d, BF16)] * 2, deps=deps)
    dw_proj = _mm_tn(f"{tag}_wproj", p_b, dpp, N_CHIPS)
    dw_gate = _mm_tn(f"{tag}_wgate", n, dgp, 1)
    dn = _mm_nt(f"{tag}_dgate", dgp, wt["w_ple_gate"], BF16)
    dh0, dh0_b, dg = _rms_bwd_res(f"{tag}_drms", h, sm["g_ple"], [dn], dh, FFN_RES)
    return dh0, dh0_b, dg, dw_gate, dw_proj


def _loss_head(name, h, g_row, target):
    d = h.shape[1]

    def fn(x, tg, g):
        r = lax.rsqrt(jnp.mean(x * x, axis=-1, keepdims=True) + RMS_EPS)
        out = x * r * g
        e = out - tg
        per_row = jnp.sum(e * e, axis=-1, keepdims=True) * (0.5 / d)
        loss = jnp.zeros((1, LANES), F32) + jnp.sum(per_row, axis=0, keepdims=True)
        dx, dg = _rms_bwd_vals(x, g, e * (1.0 / d))
        return dx, loss, dg

    return _rowwise(name, fn, [h, target], [(d, F32)], consts=[g_row], reds=[((1, LANES), F32), ((1, d), F32)])


_BIG = ("w_ff1_in", "w_ff1_out", "w_in", "w_attn_out", "w_conv_out", "w_out", "w_ff2_in", "w_ff2_out",
        "w_ple_gate", "w_ple_proj")
_SMALL = ("g_ff1", "g_mix", "b_f", "conv_w", "conv_b", "g_conv", "g_ff2", "g_ple")
_ORDER = ("g_ff1", "w_ff1_in", "w_ff1_out", "g_mix", "w_in", "b_f", "w_attn_out", "conv_w", "conv_b", "g_conv",
          "w_conv_out", "w_out", "g_ff2", "w_ff2_in", "w_ff2_out", "g_ple", "w_ple_gate", "w_ple_proj", "g_final")


def _round_up(n, k):
    return (n + k - 1) // k * k


_ROW_SHARDED = ("w_ff1_out", "w_out", "w_ff2_out", "w_ple_gate")
_FIRST_LAYER_GROUPS = ((0, 1), (2, 3, 4, 5), (6, 7, 8, 9))


def _columns(pieces, lo, hi):
    out, pos = [], 0
    for a in pieces:
        w = a.shape[1]
        s, e = max(lo, pos), min(hi, pos + w)
        if s < e:
            out.append(a[:, s - pos:e - pos])
        pos += w
    return out


def _unpack_layer(g, dm):
    out = {}
    for n, a in g.items():
        if n == "w_in":
            blocks = [a[j] for j in range(N_CHIPS)]
            o_f = 3 * dm.aw
            o_c = o_f + dm.nh
            total = N_CHIPS * a.shape[2]
            out["w_main"] = jnp.concatenate(_columns(blocks, o_c, total) + _columns(blocks, 0, o_f), axis=1)[None]
            out["w_f"] = jnp.pad(jnp.concatenate(_columns(blocks, o_f, o_c), axis=1),
                                 ((0, 0), (0, LANES - dm.nh)))[None]
        elif n in _ROW_SHARDED:
            out[n] = a.reshape(1, a.shape[0] * a.shape[1], a.shape[2])
        else:
            out[n] = a
    return out


def _pack_grads(gw, names, dm):
    out = []
    for n in names:
        if n == "w_in":
            o_f = 3 * dm.aw
            main, wf = gw["w_main"][0], gw["w_f"][0]
            n_rest = dm.n_main - o_f
            pieces = [main[:, n_rest:], wf[:, :dm.nh], main[:, :n_rest]]
            nb = (dm.n_main + dm.nh) // N_CHIPS
            out.append(jnp.stack([jnp.concatenate(_columns(pieces, j * nb, (j + 1) * nb), axis=1)
                                  for j in range(N_CHIPS)]))
        elif n in _ROW_SHARDED:
            a = gw[n]
            out.append(a.reshape(N_CHIPS, a.shape[1] // N_CHIPS, a.shape[2]))
        else:
            out.append(gw[n])
    return out


def kernel(x, p, g_ff1, w_ff1_in, w_ff1_out, g_mix, w_in, b_f, w_attn_out, conv_w, conv_b, g_conv, w_conv_out, w_out, g_ff2, w_ff2_in, w_ff2_out, g_ple, w_ple_gate, w_ple_proj, g_final, loss_target, m_g_ff1, m_w_ff1_in, m_w_ff1_out, m_g_mix, m_w_in, m_b_f, m_w_attn_out, m_conv_w, m_conv_b, m_g_conv, m_w_conv_out, m_w_out, m_g_ff2, m_w_ff2_in, m_w_ff2_out, m_g_ple, m_w_ple_gate, m_w_ple_proj, m_g_final, v_g_ff1, v_w_ff1_in, v_w_ff1_out, v_g_mix, v_w_in, v_b_f, v_w_attn_out, v_conv_w, v_conv_b, v_g_conv, v_w_conv_out, v_w_out, v_g_ff2, v_w_ff2_in, v_w_ff2_out, v_g_ple, v_w_ple_gate, v_w_ple_proj, v_g_final):
    args = dict(locals())
    wts = {n: args[n] for n in _ORDER}
    mom = {n: args["m_" + n] for n in _ORDER}
    var = {n: args["v_" + n] for n in _ORDER}

    nl = g_ff1.shape[0]
    s, d = x.shape[1], x.shape[2]
    nh = b_f.shape[1]
    taps = conv_w.shape[1]
    cc = conv_b.shape[1]
    dm = _Dims(d=d, f=w_ff1_out.shape[1] * N_CHIPS, aw=w_attn_out.shape[1], nh=nh, cc=cc, taps=taps,
               dp=w_ple_proj.shape[1], s=s)
    assert taps - 1 <= HALO

    xi = lax.axis_index("x")
    yi = lax.axis_index("y")
    ci = lax.axis_index("c")
    c_arr = jnp.reshape(ci, (1,)).astype(jnp.int32)
    chip_arr = jnp.reshape(2 * xi + yi, (1,)).astype(jnp.int32)
    place_arr = jnp.stack([2 * xi + yi, ci]).astype(jnp.int32)

    h = x[0]
    target = loss_target[0]
    p_b = p[:, 0].astype(BF16)

    cw_rows = _round_up(nl * taps, 8)
    cw_blk = jnp.pad(conv_w.reshape(nl * taps, -1), ((0, cw_rows - nl * taps), (0, 0)))
    cw_all = _allgather_blocks("gather_conv_w", cw_blk).reshape(N_CHIPS, 2, cw_rows, -1)[:, 0, :nl * taps]
    conv_w_full = jnp.transpose(cw_all.reshape(N_CHIPS, nl, taps, -1), (1, 2, 0, 3)).reshape(nl, taps, cc)
    taps_pad = _round_up(taps, 8)

    def small_of(i):
        row = lambda a: a[i][None, :]
        return dict(g_ff1=row(g_ff1), g_mix=row(g_mix), g_conv=row(g_conv), conv_b=row(conv_b), g_ff2=row(g_ff2),
                    g_ple=row(g_ple), b_f=jnp.pad(b_f[i][None, :], ((0, 0), (0, LANES - nh))),
                    conv_w=jnp.pad(conv_w_full[i], ((0, taps_pad - taps), (0, 0))))

    slots = {}
    started = {}
    order = []

    def start_gather(i):
        if i < nl:
            groups = _FIRST_LAYER_GROUPS if i == 0 else (tuple(range(len(_BIG))),)
            started[i] = []
            for k, g in enumerate(groups):
                for t in g:
                    if t not in slots:
                        slots[t] = _cast_slots(f"cast_{_BIG[t]}", wts[_BIG[t]], chip_arr, deps=order[-1:])
                st = _gather_start(f"gather_start_l{i}g{k}", [slots[t][i] for t in g], after=order[-1:] + [conv_w_full])
                order.append(st[3])
                started[i].append((g, st))

    start_gather(0)
    start_gather(1)
    layer_w, saved = [], []
    for i in range(nl):
        pend = started.pop(i)
        wt = {}

        def arrive(k, h_now, i=i, pend=pend, wt=wt):
            if k < len(pend):
                g, (send, recv, bufs, _) = pend[k]
                bufs = _gather_wait(f"gather_wait_l{i}g{k}", bufs, send, recv, h_now)
                bufs = _gather_forward(f"gather_fwd_l{i}g{k}", bufs)
                wt.update(_unpack_layer({_BIG[t]: b for t, b in zip(g, bufs)}, dm))

        arrive(0, h)
        start_gather(i + 2)
        in_flight = [st[3] for _, st in pend[1:]] + [st[3] for sts in started.values() for _, st in sts]
        sm = small_of(i)
        h, sv1 = _ffn_fwd(f"l{i}_ff1", h, sm["g_ff1"], wt["w_ff1_in"], wt["w_ff1_out"], dm, deps=in_flight)
        arrive(1, h)
        h, sv2 = _mixer_fwd(f"l{i}_mix", h, sm, wt, dm)
        arrive(2, h)
        h, sv3 = _ffn_fwd(f"l{i}_ff2", h, sm["g_ff2"], wt["w_ff2_in"], wt["w_ff2_out"], dm)
        h, sv4 = _ple_fwd(f"l{i}_ple", h, p_b[i], sm, wt, dm)
        layer_w.append((wt, sm))
        saved.append((sv1, sv2, sv3, sv4))

    dh, loss_row, dg_final = _loss_head("loss_head", h, g_final[None, :], target)

    big_grads = [None] * nl
    small_grads = [None] * nl
    leaving = None

    def begin_first(k, gw, after=()):
        names = [_BIG[t] for t in _FIRST_LAYER_GROUPS[k]]
        return _reduce_scatter_begin(f"l0g{k}", _pack_grads(gw, names, dm), c_arr, after=after)

    for i in range(nl - 1, -1, -1):
        wt, sm = layer_w[i]
        sv1, sv2, sv3, sv4 = saved[i]
        deps = [leaving[3]] if leaving is not None else []
        dh, dh_b, dg_ple, dw_gate, dw_proj = _ple_bwd(f"l{i}_ple", dh, sv4, p_b[i], sm, wt, dm, deps=deps)
        dh, dh_b, dg_ff2, dw_in2, dw_out2 = _ffn_bwd(f"l{i}_ff2", dh, dh_b, sv3, sm["g_ff2"], wt["w_ff2_in"],
                                                     wt["w_ff2_out"], dm, 1.0)
        gw = dict(w_ff2_in=dw_in2, w_ff2_out=dw_out2, w_ple_gate=dw_gate, w_ple_proj=dw_proj)
        first = [begin_first(2, gw)] if i == 0 else []
        dh, dh_b, sg, dw_main, dw_f, dw_a, dw_c, dw_o = _mixer_bwd(f"l{i}_mix", dh, dh_b, sv2, sm, wt, dm,
                                                                   deps=[st[3] for st in first])
        gw.update(w_main=dw_main, w_f=dw_f, w_attn_out=dw_a, w_conv_out=dw_c, w_out=dw_o)
        if i == 0:
            first.append(begin_first(1, gw))
        dh, dh_b, dg_ff1, dw_in1, dw_out1 = _ffn_bwd(f"l{i}_ff1", dh, dh_b, sv1, sm["g_ff1"], wt["w_ff1_in"],
                                                     wt["w_ff1_out"], dm, 1.0, deps=[st[3] for st in first[1:]])
        gw.update(w_ff1_in=dw_in1, w_ff1_out=dw_out1)
        if leaving is not None:
            big_grads[i + 1] = _reduce_scatter_end(f"l{i + 1}", leaving, dh, place_arr)
        sg.update(g_ff1=dg_ff1, g_ff2=dg_ff2, g_ple=dg_ple)
        small_grads[i] = sg
        if i > 0:
            leaving = _reduce_scatter_begin(f"l{i}", _pack_grads(gw, _BIG, dm), c_arr)
    grad_x = dh[None]

    pieces = [small_grads[i][n].reshape(-1) for i in range(nl) for n in _SMALL]
    pieces += [dg_final.reshape(-1), loss_row[0, :1]]
    flat = jnp.concatenate(pieces)
    n_flat = flat.shape[0]
    rows = _round_up(_round_up(n_flat, LANES) // LANES, 8)
    blk = jnp.pad(flat, (0, rows * LANES - n_flat)).reshape(rows, LANES)
    total = _sum_blocks("sum_small", _allgather_blocks("gather_small", blk).reshape(N_DEV, rows, LANES)).reshape(-1)
    small_tot = {n: [] for n in _SMALL}
    pos = 0
    for i in range(nl):
        for n in _SMALL:
            shp = small_grads[i][n].shape
            size = shp[0] * shp[1]
            small_tot[n].append(total[pos:pos + size].reshape(shp))
            pos += size
    g_final_tot = total[pos:pos + d]
    loss = total[pos + d]

    first.append(begin_first(0, gw, after=[total]))
    stacked = {n: [lax.empty(wts[n].shape, F32) for _ in range(4)] for n in _BIG}

    def adamw_big(i, deps):
        for t, n in enumerate(_BIG):
            stacked[n] = _adamw_layer(f"adamw_{n}_l{i}", i, wts[n], mom[n], var[n], big_grads[i][t], stacked[n], deps)

    for i in range(nl - 1, 0, -1):
        adamw_big(i, [first[-1][3]])
    after = stacked[_BIG[-1]][1] if nl > 1 else dh
    big_grads[0] = [None] * len(_BIG)
    for k, state in zip((2, 1, 0), first):
        for t, g in zip(_FIRST_LAYER_GROUPS[k], _reduce_scatter_end(f"l0g{k}", state, after, place_arr)):
            big_grads[0][t] = g
    adamw_big(0, [])
    grads, deltas, new_m, new_v = {}, {}, {}, {}
    for n in _BIG:
        grads[n], deltas[n], new_m[n], new_v[n] = stacked[n]
    chip = 2 * xi + yi
    for n in _SMALL:
        g = jnp.concatenate(small_tot[n], axis=0)
        if n == "conv_w":
            cpc = cc // N_CHIPS
            g = lax.dynamic_slice_in_dim(g.reshape(nl * taps, cc), chip * cpc, cpc, axis=1)
            shape2 = (nl * taps, cpc)
        else:
            shape2 = g.shape
        dl, mm, vv = _adamw_small(f"adamw_{n}", wts[n].reshape(shape2), g, mom[n].reshape(shape2),
                                  var[n].reshape(shape2))
        grads[n] = g.reshape(wts[n].shape)
        deltas[n], new_m[n], new_v[n] = (a.reshape(wts[n].shape) for a in (dl, mm, vv))
    g2 = g_final_tot[None, :]
    dl, mm, vv = _adamw_small("adamw_g_final", g_final[None, :], g2, m_g_final[None, :], v_g_final[None, :])
    grads["g_final"] = g_final_tot
    deltas["g_final"], new_m["g_final"], new_v["g_final"] = dl[0], mm[0], vv[0]

    return (loss, grad_x, *[grads[n] for n in _ORDER], *[deltas[n] for n in _ORDER],
            *[new_m[n] for n in _ORDER], *[new_v[n] for n in _ORDER])
```
